```python
import math
import jax, jax.numpy as jnp
from jax import lax
import numpy as np

D_MODEL = 2048
BATCH = 8
SEQ = 4096
DEPTH = 4

CHUNK = 64
N_MIXERS = 2
BRANCH = D_MODEL
CONV_K = 3
SB_HEADS = 16
SB_HEAD_DIM = BRANCH // SB_HEADS
Q_BLOCK = 128
RMS_EPS = 1e-6

kernel_name = "hybrid_shortconv_stickbreaking_trunk"


def rmsnorm(x, g):
    xf = x.astype(jnp.float32)
    y = xf * lax.rsqrt(jnp.mean(xf * xf, axis=-1, keepdims=True) + RMS_EPS)
    return (y * g.astype(jnp.float32)).astype(x.dtype)


def causal_depthwise_conv(u, w):
    rhs = w[:, None, :]
    return lax.conv_general_dilated(
        u, rhs.astype(u.dtype), window_strides=(1,), padding=[(CONV_K - 1, 0)],
        dimension_numbers=("NWC", "WIO", "NWC"), feature_group_count=u.shape[-1])


def short_conv_mixer(h, w_in, conv_w, w_out):
    proj = jnp.einsum("bsd,de->bse", h, w_in)
    b_gate, c_gate, xt, z = jnp.split(proj, 4, axis=-1)
    y = b_gate * causal_depthwise_conv(c_gate * xt, conv_w)
    return jnp.einsum("bse,ed->bsd", jax.nn.silu(z) * y, w_out)


def stick_breaking_mixer(h, w_in, w_out):
    bsz, seq, _ = h.shape
    proj = jnp.einsum("bsd,de->bse", h, w_in)
    q, k, v, z = jnp.split(proj, 4, axis=-1)
    to_heads = lambda t: t.reshape(bsz, seq, SB_HEADS, SB_HEAD_DIM).transpose(0, 2, 1, 3)
    q, k, v = to_heads(q), to_heads(k), to_heads(v)
    scale = 1.0 / math.sqrt(SB_HEAD_DIM)
    outs = []
    for blk in range(seq // Q_BLOCK):
        s0 = blk * Q_BLOCK
        end = s0 + Q_BLOCK
        qb = q[:, :, s0:end]
        kb = k[:, :, :end]
        vb = v[:, :, :end]
        logits = jnp.einsum("bhqd,bhkd->bhqk", qb, kb).astype(jnp.float32) * scale
        t_idx = s0 + jnp.arange(Q_BLOCK)[:, None]
        s_idx = jnp.arange(end)[None, :]
        mask = s_idx < t_idx
        log_keep = jnp.where(mask, jax.nn.log_sigmoid(-logits), 0.0)
        tail = lax.cumsum(log_keep, axis=3, reverse=True) - log_keep
        weights = jnp.where(mask, jnp.exp(jax.nn.log_sigmoid(logits) + tail), 0.0)
        outs.append(jnp.einsum("bhqk,bhkd->bhqd", weights.astype(vb.dtype), vb))
    o = jnp.concatenate(outs, axis=2)
    o = o.transpose(0, 2, 1, 3).reshape(bsz, seq, BRANCH)
    return jnp.einsum("bse,ed->bsd", jax.nn.silu(z) * o, w_out)


def _fwd_setup_inputs(seed: int = 0) -> dict:
    key = jax.random.key(seed)
    keys = iter(jax.random.split(key, 64))
    nrm = lambda shape, s: jax.random.normal(next(keys), shape, jnp.float32) * s
    gain = lambda: 1.0 + nrm((D_MODEL,), 0.02)
    inp = {"x": nrm((BATCH, SEQ, D_MODEL), 1.0)}
    for i in range(DEPTH):
        inp[f"ln_pre_{i}"] = gain()
        if i % N_MIXERS == 0:
            inp[f"conv_w_in_{i}"] = nrm((D_MODEL, 4 * BRANCH), D_MODEL ** -0.5)
            inp[f"conv_w_{i}"] = nrm((CONV_K, BRANCH), CONV_K ** -0.5)
            inp[f"conv_w_out_{i}"] = nrm((BRANCH, D_MODEL), BRANCH ** -0.5)
        else:
            inp[f"sb_w_in_{i}"] = nrm((D_MODEL, 4 * BRANCH), D_MODEL ** -0.5)
            inp[f"sb_w_out_{i}"] = nrm((BRANCH, D_MODEL), BRANCH ** -0.5)
        inp[f"ln_post_{i}"] = gain()
    return inp


def _fwd_reference(x,
              ln_pre_0, conv_w_in_0, conv_w_0, conv_w_out_0, ln_post_0,
              ln_pre_1, sb_w_in_1, sb_w_out_1, ln_post_1,
              ln_pre_2, conv_w_in_2, conv_w_2, conv_w_out_2, ln_post_2,
              ln_pre_3, sb_w_in_3, sb_w_out_3, ln_post_3):
    layers = [
        (ln_pre_0, (conv_w_in_0, conv_w_0, conv_w_out_0), ln_post_0),
        (ln_pre_1, (sb_w_in_1, sb_w_out_1), ln_post_1),
        (ln_pre_2, (conv_w_in_2, conv_w_2, conv_w_out_2), ln_post_2),
        (ln_pre_3, (sb_w_in_3, sb_w_out_3), ln_post_3),
    ]
    h = x
    for i in range(DEPTH):
        g_pre, params, g_post = layers[i]
        u = rmsnorm(h, g_pre)
        if i % N_MIXERS == 0:
            m = short_conv_mixer(u, *params)
        else:
            m = stick_breaking_mixer(u, *params)
        h = h + rmsnorm(m, g_post)
    return h


import jax as _jax
import jax.numpy as _jnp

TWIN_FORMAT = 'train_step'
FWD_PARAMS = ['x', 'ln_pre_0', 'conv_w_in_0', 'conv_w_0', 'conv_w_out_0', 'ln_post_0', 'ln_pre_1', 'sb_w_in_1', 'sb_w_out_1', 'ln_post_1', 'ln_pre_2', 'conv_w_in_2', 'conv_w_2', 'conv_w_out_2', 'ln_post_2', 'ln_pre_3', 'sb_w_in_3', 'sb_w_out_3', 'ln_post_3']
TWIN_WEIGHTS = ['ln_pre_0', 'conv_w_in_0', 'conv_w_0', 'conv_w_out_0', 'ln_post_0', 'ln_pre_1', 'sb_w_in_1', 'sb_w_out_1', 'ln_post_1', 'ln_pre_2', 'conv_w_in_2', 'conv_w_2', 'conv_w_out_2', 'ln_post_2', 'ln_pre_3', 'sb_w_in_3', 'sb_w_out_3', 'ln_post_3']
TWIN_DIFF_INPUT = 'x'
TWIN_INPUTS = ['x', 'ln_pre_0', 'conv_w_in_0', 'conv_w_0', 'conv_w_out_0', 'ln_post_0', 'ln_pre_1', 'sb_w_in_1', 'sb_w_out_1', 'ln_post_1', 'ln_pre_2', 'conv_w_in_2', 'conv_w_2', 'conv_w_out_2', 'ln_post_2', 'ln_pre_3', 'sb_w_in_3', 'sb_w_out_3', 'ln_post_3', 'loss_target', 'm_ln_pre_0', 'm_conv_w_in_0', 'm_conv_w_0', 'm_conv_w_out_0', 'm_ln_post_0', 'm_ln_pre_1', 'm_sb_w_in_1', 'm_sb_w_out_1', 'm_ln_post_1', 'm_ln_pre_2', 'm_conv_w_in_2', 'm_conv_w_2', 'm_conv_w_out_2', 'm_ln_post_2', 'm_ln_pre_3', 'm_sb_w_in_3', 'm_sb_w_out_3', 'm_ln_post_3', 'v_ln_pre_0', 'v_conv_w_in_0', 'v_conv_w_0', 'v_conv_w_out_0', 'v_ln_post_0', 'v_ln_pre_1', 'v_sb_w_in_1', 'v_sb_w_out_1', 'v_ln_post_1', 'v_ln_pre_2', 'v_conv_w_in_2', 'v_conv_w_2', 'v_conv_w_out_2', 'v_ln_post_2', 'v_ln_pre_3', 'v_sb_w_in_3', 'v_sb_w_out_3', 'v_ln_post_3']
TWIN_OUTPUTS = ['loss', 'grad_x', 'grad_ln_pre_0', 'grad_conv_w_in_0', 'grad_conv_w_0', 'grad_conv_w_out_0', 'grad_ln_post_0', 'grad_ln_pre_1', 'grad_sb_w_in_1', 'grad_sb_w_out_1', 'grad_ln_post_1', 'grad_ln_pre_2', 'grad_conv_w_in_2', 'grad_conv_w_2', 'grad_conv_w_out_2', 'grad_ln_post_2', 'grad_ln_pre_3', 'grad_sb_w_in_3', 'grad_sb_w_out_3', 'grad_ln_post_3', 'delta_ln_pre_0', 'delta_conv_w_in_0', 'delta_conv_w_0', 'delta_conv_w_out_0', 'delta_ln_post_0', 'delta_ln_pre_1', 'delta_sb_w_in_1', 'delta_sb_w_out_1', 'delta_ln_post_1', 'delta_ln_pre_2', 'delta_conv_w_in_2', 'delta_conv_w_2', 'delta_conv_w_out_2', 'delta_ln_post_2', 'delta_ln_pre_3', 'delta_sb_w_in_3', 'delta_sb_w_out_3', 'delta_ln_post_3', 'new_m_ln_pre_0', 'new_m_conv_w_in_0', 'new_m_conv_w_0', 'new_m_conv_w_out_0', 'new_m_ln_post_0', 'new_m_ln_pre_1', 'new_m_sb_w_in_1', 'new_m_sb_w_out_1', 'new_m_ln_post_1', 'new_m_ln_pre_2', 'new_m_conv_w_in_2', 'new_m_conv_w_2', 'new_m_conv_w_out_2', 'new_m_ln_post_2', 'new_m_ln_pre_3', 'new_m_sb_w_in_3', 'new_m_sb_w_out_3', 'new_m_ln_post_3', 'new_v_ln_pre_0', 'new_v_conv_w_in_0', 'new_v_conv_w_0', 'new_v_conv_w_out_0', 'new_v_ln_post_0', 'new_v_ln_pre_1', 'new_v_sb_w_in_1', 'new_v_sb_w_out_1', 'new_v_ln_post_1', 'new_v_ln_pre_2', 'new_v_conv_w_in_2', 'new_v_conv_w_2', 'new_v_conv_w_out_2', 'new_v_ln_post_2', 'new_v_ln_pre_3', 'new_v_sb_w_in_3', 'new_v_sb_w_out_3', 'new_v_ln_post_3']
TWIN_LEAF_KINDS = {'loss': 'loss', 'grad_x': 'grad_x', 'grad_ln_pre_0': 'grad_w', 'grad_conv_w_in_0': 'grad_w', 'grad_conv_w_0': 'grad_w', 'grad_conv_w_out_0': 'grad_w', 'grad_ln_post_0': 'grad_w', 'grad_ln_pre_1': 'grad_w', 'grad_sb_w_in_1': 'grad_w', 'grad_sb_w_out_1': 'grad_w', 'grad_ln_post_1': 'grad_w', 'grad_ln_pre_2': 'grad_w', 'grad_conv_w_in_2': 'grad_w', 'grad_conv_w_2': 'grad_w', 'grad_conv_w_out_2': 'grad_w', 'grad_ln_post_2': 'grad_w', 'grad_ln_pre_3': 'grad_w', 'grad_sb_w_in_3': 'grad_w', 'grad_sb_w_out_3': 'grad_w', 'grad_ln_post_3': 'grad_w', 'delta_ln_pre_0': 'delta_w', 'delta_conv_w_in_0': 'delta_w', 'delta_conv_w_0': 'delta_w', 'delta_conv_w_out_0': 'delta_w', 'delta_ln_post_0': 'delta_w', 'delta_ln_pre_1': 'delta_w', 'delta_sb_w_in_1': 'delta_w', 'delta_sb_w_out_1': 'delta_w', 'delta_ln_post_1': 'delta_w', 'delta_ln_pre_2': 'delta_w', 'delta_conv_w_in_2': 'delta_w', 'delta_conv_w_2': 'delta_w', 'delta_conv_w_out_2': 'delta_w', 'delta_ln_post_2': 'delta_w', 'delta_ln_pre_3': 'delta_w', 'delta_sb_w_in_3': 'delta_w', 'delta_sb_w_out_3': 'delta_w', 'delta_ln_post_3': 'delta_w', 'new_m_ln_pre_0': 'new_m', 'new_m_conv_w_in_0': 'new_m', 'new_m_conv_w_0': 'new_m', 'new_m_conv_w_out_0': 'new_m', 'new_m_ln_post_0': 'new_m', 'new_m_ln_pre_1': 'new_m', 'new_m_sb_w_in_1': 'new_m', 'new_m_sb_w_out_1': 'new_m', 'new_m_ln_post_1': 'new_m', 'new_m_ln_pre_2': 'new_m', 'new_m_conv_w_in_2': 'new_m', 'new_m_conv_w_2': 'new_m', 'new_m_conv_w_out_2': 'new_m', 'new_m_ln_post_2': 'new_m', 'new_m_ln_pre_3': 'new_m', 'new_m_sb_w_in_3': 'new_m', 'new_m_sb_w_out_3': 'new_m', 'new_m_ln_post_3': 'new_m', 'new_v_ln_pre_0': 'new_v', 'new_v_conv_w_in_0': 'new_v', 'new_v_conv_w_0': 'new_v', 'new_v_conv_w_out_0': 'new_v', 'new_v_ln_post_0': 'new_v', 'new_v_ln_pre_1': 'new_v', 'new_v_sb_w_in_1': 'new_v', 'new_v_sb_w_out_1': 'new_v', 'new_v_ln_post_1': 'new_v', 'new_v_ln_pre_2': 'new_v', 'new_v_conv_w_in_2': 'new_v', 'new_v_conv_w_2': 'new_v', 'new_v_conv_w_out_2': 'new_v', 'new_v_ln_post_2': 'new_v', 'new_v_ln_pre_3': 'new_v', 'new_v_sb_w_in_3': 'new_v', 'new_v_sb_w_out_3': 'new_v', 'new_v_ln_post_3': 'new_v'}


def _forward(args):
    return _fwd_reference(*[args[k] for k in FWD_PARAMS])


def _output_shape():
    def fwd():
        inp = _fwd_setup_inputs(0)
        return _fwd_reference(*[inp[k] for k in FWD_PARAMS])
    out = _jax.eval_shape(fwd)
    return out.shape, out.dtype

N_MICROBATCH = 1
ADAM_LR = 0.001
ADAM_B1 = 0.9
ADAM_B2 = 0.999
ADAM_EPS = 1e-08
ADAM_WD = 0.01
ADAM_STEP = 10
PER_EXAMPLE_BATCH_AXIS = {'x': 0, 'loss_target': 0}
SHARED_INPUTS = []
_WEIGHT_DTYPES = {'ln_pre_0': _jnp.float32, 'conv_w_in_0': _jnp.float32, 'conv_w_0': _jnp.float32, 'conv_w_out_0': _jnp.float32, 'ln_post_0': _jnp.float32, 'ln_pre_1': _jnp.float32, 'sb_w_in_1': _jnp.float32, 'sb_w_out_1': _jnp.float32, 'ln_post_1': _jnp.float32, 'ln_pre_2': _jnp.float32, 'conv_w_in_2': _jnp.float32, 'conv_w_2': _jnp.float32, 'conv_w_out_2': _jnp.float32, 'ln_post_2': _jnp.float32, 'ln_pre_3': _jnp.float32, 'sb_w_in_3': _jnp.float32, 'sb_w_out_3': _jnp.float32, 'ln_post_3': _jnp.float32}
MOMENT_SCALE = {'ln_pre_0': 1.223152e+00, 'conv_w_in_0': 5.994837e-01, 'conv_w_0': 5.966832e-01, 'conv_w_out_0': 5.866138e-01, 'ln_post_0': 1.599537e+01, 'ln_pre_1': 6.157242e-01, 'sb_w_in_1': 3.092669e-01, 'sb_w_out_1': 3.922823e-01, 'ln_post_1': 1.600795e+01, 'ln_pre_2': 5.144668e-01, 'conv_w_in_2': 2.579140e-01, 'conv_w_2': 2.552037e-01, 'conv_w_out_2': 2.524457e-01, 'ln_post_2': 1.598788e+01, 'ln_pre_3': 3.177730e-01, 'sb_w_in_3': 1.573888e-01, 'sb_w_out_3': 1.997452e-01, 'ln_post_3': 1.597171e+01}


def _to_microbatches(a, axis):
    t = _jnp.moveaxis(a, axis, 0)
    t = t.reshape((N_MICROBATCH, t.shape[0] // N_MICROBATCH) + t.shape[1:])
    return _jnp.moveaxis(t, 1, axis + 1)


def setup_inputs(seed: int = 0) -> dict:
    inp = _fwd_setup_inputs(seed)
    key = _jax.random.fold_in(_jax.random.key(seed), 7919)
    shape, _ = _output_shape()
    out = dict(inp)
    out["loss_target"] = _jax.random.normal(_jax.random.fold_in(key, 0), shape, _jnp.float32)
    for i, name in enumerate(TWIN_WEIGHTS):
        w = inp[name].astype(_jnp.float32)
        if MOMENT_SCALE is None:
            s = _jnp.sqrt(_jnp.mean(_jnp.square(w)) + 1e-30)
        else:
            s = MOMENT_SCALE[name]
        km, kv = _jax.random.split(_jax.random.fold_in(key, i + 1))
        out[name] = w
        out["m_" + name] = s * _jax.random.normal(km, w.shape, _jnp.float32)
        out["v_" + name] = (s * s) * _jax.random.uniform(kv, w.shape, _jnp.float32, 0.5, 1.5)
    if N_MICROBATCH > 1:
        for name, axis in PER_EXAMPLE_BATCH_AXIS.items():
            out[name] = _to_microbatches(out[name], axis)
    return {'x': out['x'], 'ln_pre_0': out['ln_pre_0'], 'conv_w_in_0': out['conv_w_in_0'], 'conv_w_0': out['conv_w_0'], 'conv_w_out_0': out['conv_w_out_0'], 'ln_post_0': out['ln_post_0'], 'ln_pre_1': out['ln_pre_1'], 'sb_w_in_1': out['sb_w_in_1'], 'sb_w_out_1': out['sb_w_out_1'], 'ln_post_1': out['ln_post_1'], 'ln_pre_2': out['ln_pre_2'], 'conv_w_in_2': out['conv_w_in_2'], 'conv_w_2': out['conv_w_2'], 'conv_w_out_2': out['conv_w_out_2'], 'ln_post_2': out['ln_post_2'], 'ln_pre_3': out['ln_pre_3'], 'sb_w_in_3': out['sb_w_in_3'], 'sb_w_out_3': out['sb_w_out_3'], 'ln_post_3': out['ln_post_3'], 'loss_target': out['loss_target'], 'm_ln_pre_0': out['m_ln_pre_0'], 'm_conv_w_in_0': out['m_conv_w_in_0'], 'm_conv_w_0': out['m_conv_w_0'], 'm_conv_w_out_0': out['m_conv_w_out_0'], 'm_ln_post_0': out['m_ln_post_0'], 'm_ln_pre_1': out['m_ln_pre_1'], 'm_sb_w_in_1': out['m_sb_w_in_1'], 'm_sb_w_out_1': out['m_sb_w_out_1'], 'm_ln_post_1': out['m_ln_post_1'], 'm_ln_pre_2': out['m_ln_pre_2'], 'm_conv_w_in_2': out['m_conv_w_in_2'], 'm_conv_w_2': out['m_conv_w_2'], 'm_conv_w_out_2': out['m_conv_w_out_2'], 'm_ln_post_2': out['m_ln_post_2'], 'm_ln_pre_3': out['m_ln_pre_3'], 'm_sb_w_in_3': out['m_sb_w_in_3'], 'm_sb_w_out_3': out['m_sb_w_out_3'], 'm_ln_post_3': out['m_ln_post_3'], 'v_ln_pre_0': out['v_ln_pre_0'], 'v_conv_w_in_0': out['v_conv_w_in_0'], 'v_conv_w_0': out['v_conv_w_0'], 'v_conv_w_out_0': out['v_conv_w_out_0'], 'v_ln_post_0': out['v_ln_post_0'], 'v_ln_pre_1': out['v_ln_pre_1'], 'v_sb_w_in_1': out['v_sb_w_in_1'], 'v_sb_w_out_1': out['v_sb_w_out_1'], 'v_ln_post_1': out['v_ln_post_1'], 'v_ln_pre_2': out['v_ln_pre_2'], 'v_conv_w_in_2': out['v_conv_w_in_2'], 'v_conv_w_2': out['v_conv_w_2'], 'v_conv_w_out_2': out['v_conv_w_out_2'], 'v_ln_post_2': out['v_ln_post_2'], 'v_ln_pre_3': out['v_ln_pre_3'], 'v_sb_w_in_3': out['v_sb_w_in_3'], 'v_sb_w_out_3': out['v_sb_w_out_3'], 'v_ln_post_3': out['v_ln_post_3']}


def _loss(weights, diff, rest, loss_target):
    with _jax.named_scope("forward"):
        args = {**rest, TWIN_DIFF_INPUT: diff, **{k: w.astype(_WEIGHT_DTYPES[k]) for k, w in weights.items()}}
        y = _forward(args)
    with _jax.named_scope("loss_head"):
        err = _jnp.square(y.astype(_jnp.float32) - loss_target)
        return 0.5 * _jnp.sum(_jnp.mean(err, axis=-1)) if err.ndim else 0.5 * err


def _adamw(w, g, m, v):
    m = ADAM_B1 * m + (1.0 - ADAM_B1) * g
    v = ADAM_B2 * v + (1.0 - ADAM_B2) * _jnp.square(g)
    m_hat = m / (1.0 - ADAM_B1 ** ADAM_STEP)
    v_hat = v / (1.0 - ADAM_B2 ** ADAM_STEP)
    delta = -ADAM_LR * (m_hat / (_jnp.sqrt(v_hat) + ADAM_EPS) + ADAM_WD * w)
    return delta, m, v


def reference(x, ln_pre_0, conv_w_in_0, conv_w_0, conv_w_out_0, ln_post_0, ln_pre_1, sb_w_in_1, sb_w_out_1, ln_post_1, ln_pre_2, conv_w_in_2, conv_w_2, conv_w_out_2, ln_post_2, ln_pre_3, sb_w_in_3, sb_w_out_3, ln_post_3, loss_target, m_ln_pre_0, m_conv_w_in_0, m_conv_w_0, m_conv_w_out_0, m_ln_post_0, m_ln_pre_1, m_sb_w_in_1, m_sb_w_out_1, m_ln_post_1, m_ln_pre_2, m_conv_w_in_2, m_conv_w_2, m_conv_w_out_2, m_ln_post_2, m_ln_pre_3, m_sb_w_in_3, m_sb_w_out_3, m_ln_post_3, v_ln_pre_0, v_conv_w_in_0, v_conv_w_0, v_conv_w_out_0, v_ln_post_0, v_ln_pre_1, v_sb_w_in_1, v_sb_w_out_1, v_ln_post_1, v_ln_pre_2, v_conv_w_in_2, v_conv_w_2, v_conv_w_out_2, v_ln_post_2, v_ln_pre_3, v_sb_w_in_3, v_sb_w_out_3, v_ln_post_3):
    given = dict(x=x, ln_pre_0=ln_pre_0, conv_w_in_0=conv_w_in_0, conv_w_0=conv_w_0, conv_w_out_0=conv_w_out_0, ln_post_0=ln_post_0, ln_pre_1=ln_pre_1, sb_w_in_1=sb_w_in_1, sb_w_out_1=sb_w_out_1, ln_post_1=ln_post_1, ln_pre_2=ln_pre_2, conv_w_in_2=conv_w_in_2, conv_w_2=conv_w_2, conv_w_out_2=conv_w_out_2, ln_post_2=ln_post_2, ln_pre_3=ln_pre_3, sb_w_in_3=sb_w_in_3, sb_w_out_3=sb_w_out_3, ln_post_3=ln_post_3, loss_target=loss_target, m_ln_pre_0=m_ln_pre_0, m_conv_w_in_0=m_conv_w_in_0, m_conv_w_0=m_conv_w_0, m_conv_w_out_0=m_conv_w_out_0, m_ln_post_0=m_ln_post_0, m_ln_pre_1=m_ln_pre_1, m_sb_w_in_1=m_sb_w_in_1, m_sb_w_out_1=m_sb_w_out_1, m_ln_post_1=m_ln_post_1, m_ln_pre_2=m_ln_pre_2, m_conv_w_in_2=m_conv_w_in_2, m_conv_w_2=m_conv_w_2, m_conv_w_out_2=m_conv_w_out_2, m_ln_post_2=m_ln_post_2, m_ln_pre_3=m_ln_pre_3, m_sb_w_in_3=m_sb_w_in_3, m_sb_w_out_3=m_sb_w_out_3, m_ln_post_3=m_ln_post_3, v_ln_pre_0=v_ln_pre_0, v_conv_w_in_0=v_conv_w_in_0, v_conv_w_0=v_conv_w_0, v_conv_w_out_0=v_conv_w_out_0, v_ln_post_0=v_ln_post_0, v_ln_pre_1=v_ln_pre_1, v_sb_w_in_1=v_sb_w_in_1, v_sb_w_out_1=v_sb_w_out_1, v_ln_post_1=v_ln_post_1, v_ln_pre_2=v_ln_pre_2, v_conv_w_in_2=v_conv_w_in_2, v_conv_w_2=v_conv_w_2, v_conv_w_out_2=v_conv_w_out_2, v_ln_post_2=v_ln_post_2, v_ln_pre_3=v_ln_pre_3, v_sb_w_in_3=v_sb_w_in_3, v_sb_w_out_3=v_sb_w_out_3, v_ln_post_3=v_ln_post_3)
    weights = {n: given[n] for n in TWIN_WEIGHTS}
    shared = {n: given[n] for n in SHARED_INPUTS}
    per_example = {n: given[n] for n in ['x']}
    grad_fn = _jax.value_and_grad(_loss, argnums=(0, 1))

    def one_microbatch(ex, loss_target):
        ex = dict(ex)
        diff = ex.pop(TWIN_DIFF_INPUT)
        return grad_fn(weights, diff, {**shared, **ex}, loss_target)

    if N_MICROBATCH == 1:
        loss, (grad_w, grad_x) = one_microbatch(per_example, given["loss_target"])
    else:
        def body(carry, xs):
            loss_sum, grad_sum = carry
            l_k, (gw_k, gx_k) = one_microbatch(xs[0], xs[1])
            with _jax.named_scope("update"):
                return (loss_sum + l_k, _jax.tree.map(_jnp.add, grad_sum, gw_k)), gx_k

        init = (_jnp.zeros((), _jnp.float32), _jax.tree.map(_jnp.zeros_like, weights))
        (loss, grad_w), grad_x = _jax.lax.scan(body, init, (per_example, given["loss_target"]))
    with _jax.named_scope("update"):
        delta_w, new_m, new_v = {}, {}, {}
        for n in TWIN_WEIGHTS:
            delta_w[n], new_m[n], new_v[n] = _adamw(weights[n], grad_w[n], given["m_" + n], given["v_" + n])
    return (loss, grad_x, *[grad_w[n] for n in TWIN_WEIGHTS], *[delta_w[n] for n in TWIN_WEIGHTS],
            *[new_m[n] for n in TWIN_WEIGHTS], *[new_v[n] for n in TWIN_WEIGHTS])
```

```python
import functools
import math

import jax
import jax.numpy as jnp
from jax import lax
from jax.experimental import pallas as pl
from jax.experimental.pallas import tpu as pltpu

F32 = jnp.float32
BF16 = jnp.bfloat16
MESH = pl.DeviceIdType.MESH
ANY = pl.BlockSpec(memory_space=pl.ANY)

N_DEV = 8
N_CHIP = 4
DEPTH = 4
HEAD_DIM = 128
CONV_K = 3
RMS_EPS = 1e-6
ADAM_LR = 0.001
ADAM_B1 = 0.9
ADAM_B2 = 0.999
ADAM_EPS = 1e-08
ADAM_WD = 0.01
ADAM_STEP = 10

V7X_VMEM_BYTES = 64 * 1024 * 1024
VMEM_LIMIT = V7X_VMEM_BYTES * 3 // 4
LANES = 128
SUBLANES = 8


def _params(*sem):
    return pltpu.CompilerParams(dimension_semantics=sem, vmem_limit_bytes=VMEM_LIMIT)


def _silu_parts(z):
    sig = jax.nn.sigmoid(z)
    return z * sig, sig


NN = (((1,), (0,)), ((), ()))
NT = (((1,), (1,)), ((), ()))
TN = (((0,), (0,)), ((), ()))


def _mm(a, b, *, dims, grid, a_spec, b_spec, o_spec, out_shape, acc_shape, name):
    nk = grid[2]

    def body(a_ref, b_ref, o_ref, *scratch):
        p = lax.dot_general(a_ref[...], b_ref[...], dims, preferred_element_type=F32)
        if nk == 1:
            o_ref[...] = p.astype(o_ref.dtype)
        else:
            acc_ref = scratch[0]
            k = pl.program_id(2)

            @pl.when(k == 0)
            def _():
                acc_ref[...] = p

            @pl.when(k > 0)
            def _():
                acc_ref[...] += p

            @pl.when(k == nk - 1)
            def _():
                o_ref[...] = acc_ref[...].astype(o_ref.dtype)

    scratch = [] if nk == 1 else [pltpu.VMEM(acc_shape, F32)]
    return pl.pallas_call(
        body, grid=grid, in_specs=[a_spec, b_spec], out_specs=o_spec, out_shape=out_shape,
        scratch_shapes=scratch, compiler_params=_params("parallel", "parallel", "arbitrary"), name=name,
    )(a, b)


def _proj(u, w_in_all, layer, shard0, n_shard, out_dtype, name):
    s, d = u.shape
    ws = w_in_all.shape[-1]
    tm, tn = min(s, 512), min(ws, 1024)
    nj = ws // tn
    return _mm(
        u, w_in_all, dims=NN, grid=(s // tm, n_shard * nj, 1),
        a_spec=pl.BlockSpec((tm, d), lambda i, j, k: (i, 0)),
        b_spec=pl.BlockSpec((None, None, d, tn), lambda i, j, k: (layer, shard0 + j // nj, 0, j % nj)),
        o_spec=pl.BlockSpec((tm, tn), lambda i, j, k: (i, j)),
        out_shape=jax.ShapeDtypeStruct((s, n_shard * ws), out_dtype), acc_shape=(tm, tn), name=name,
    )


def _out_proj(a, w_out_all, layer, name):
    s, bdim = a.shape
    d = w_out_all.shape[-1]
    tm, tn = min(s, 512), min(d, 1024)
    return _mm(
        a, w_out_all, dims=NN, grid=(s // tm, d // tn, 1),
        a_spec=pl.BlockSpec((tm, bdim), lambda i, j, k: (i, 0)),
        b_spec=pl.BlockSpec((None, bdim, tn), lambda i, j, k: (layer, 0, j)),
        o_spec=pl.BlockSpec((tm, tn), lambda i, j, k: (i, j)),
        out_shape=jax.ShapeDtypeStruct((s, d), F32), acc_shape=(tm, tn), name=name,
    )


def _out_proj_bwd_act(dm, w_out_all, layer, name):
    s, d = dm.shape
    bdim = w_out_all.shape[-2]
    tm, tn = min(s, 512), min(bdim, 1024)
    return _mm(
        dm, w_out_all, dims=NT, grid=(s // tm, bdim // tn, 1),
        a_spec=pl.BlockSpec((tm, d), lambda i, j, k: (i, 0)),
        b_spec=pl.BlockSpec((None, tn, d), lambda i, j, k: (layer, j, 0)),
        o_spec=pl.BlockSpec((tm, tn), lambda i, j, k: (i, j)),
        out_shape=jax.ShapeDtypeStruct((s, bdim), F32), acc_shape=(tm, tn), name=name,
    )


def _weight_grad(act, dout, n_blocks, name):
    s, din = act.shape
    w = dout.shape[1] // n_blocks
    tm, tk = min(din, 512), min(s, 512)
    return _mm(
        act, dout, dims=TN, grid=(din // tm, n_blocks, s // tk),
        a_spec=pl.BlockSpec((tk, tm), lambda i, j, k: (k, i)),
        b_spec=pl.BlockSpec((tk, w), lambda i, j, k: (k, j)),
        o_spec=pl.BlockSpec((None, tm, w), lambda i, j, k: (j, i, 0)),
        out_shape=jax.ShapeDtypeStruct((n_blocks, din, w), BF16), acc_shape=(tm, w), name=name,
    )


def _proj_bwd_act(dproj, w_in_all, layer, name):
    s = dproj.shape[0]
    d, ws = w_in_all.shape[-2:]
    tm, tn = min(s, 512), min(d, 1024)
    return _mm(
        dproj, w_in_all, dims=NT, grid=(s // tm, d // tn, N_DEV),
        a_spec=pl.BlockSpec((tm, ws), lambda i, j, k: (i, k)),
        b_spec=pl.BlockSpec((None, None, tn, ws), lambda i, j, k: (layer, k, j, 0)),
        o_spec=pl.BlockSpec((tm, tn), lambda i, j, k: (i, j)),
        out_shape=jax.ShapeDtypeStruct((s, d), F32), acc_shape=(tm, tn), name=name,
    )


def _row_spec(tm, d):
    return pl.BlockSpec((tm, d), lambda i: (i, 0))


def _gain_spec(d):
    return pl.BlockSpec((1, d), lambda i: (0, 0))


def _rstd(x):
    return lax.rsqrt(jnp.mean(x * x, axis=-1, keepdims=True) + RMS_EPS)


def _rmsnorm_fwd(h, gain, name):
    s, d = h.shape
    tm = min(s, 512)

    def body(h_ref, g_ref, u_ref):
        x = h_ref[...]
        u_ref[...] = (x * _rstd(x) * g_ref[...]).astype(u_ref.dtype)

    return pl.pallas_call(
        body, grid=(s // tm,), in_specs=[_row_spec(tm, d), _gain_spec(d)], out_specs=_row_spec(tm, d),
        out_shape=jax.ShapeDtypeStruct((s, d), BF16), compiler_params=_params("parallel"), name=name,
    )(h, gain)


def _post_norm_residual(h, m, gain, name):
    s, d = h.shape
    tm = min(s, 512)

    def body(h_ref, m_ref, g_ref, o_ref):
        x = m_ref[...]
        o_ref[...] = h_ref[...] + x * _rstd(x) * g_ref[...]

    return pl.pallas_call(
        body, grid=(s // tm,), in_specs=[_row_spec(tm, d), _row_spec(tm, d), _gain_spec(d)],
        out_specs=_row_spec(tm, d), out_shape=jax.ShapeDtypeStruct((s, d), F32),
        compiler_params=_params("parallel"), name=name,
    )(h, m, gain)


def _sum_rows_into(acc_ref, x):
    tm, d = x.shape
    acc_ref[...] += jnp.sum(x.reshape(tm // SUBLANES, SUBLANES, d), axis=0)


def _norm_bwd_body(n_steps, with_residual):
    def body(*refs):
        if with_residual:
            dy_ref, x_ref, g_ref, dres_ref, dx_ref, dg_ref, acc_ref = refs
        else:
            dy_ref, x_ref, g_ref, dx_ref, dg_ref, acc_ref = refs
        i = pl.program_id(0)

        @pl.when(i == 0)
        def _():
            acc_ref[...] = jnp.zeros_like(acc_ref)

        x = x_ref[...]
        dy = dy_ref[...]
        rstd = _rstd(x)
        n = x * rstd
        dn = dy * g_ref[...]
        dx = rstd * (dn - n * jnp.mean(dn * n, axis=-1, keepdims=True))
        if with_residual:
            dx = dres_ref[...] + dx
        dx_ref[...] = dx.astype(dx_ref.dtype)
        _sum_rows_into(acc_ref, dy * n)

        @pl.when(i == n_steps - 1)
        def _():
            dg_ref[...] = jnp.sum(acc_ref[...], axis=0, keepdims=True)

    return body


def _post_norm_bwd(dh, m, gain, name):
    s, d = m.shape
    tm = min(s, 512)
    n_steps = s // tm
    return pl.pallas_call(
        _norm_bwd_body(n_steps, False), grid=(n_steps,),
        in_specs=[_row_spec(tm, d), _row_spec(tm, d), _gain_spec(d)],
        out_specs=[_row_spec(tm, d), _gain_spec(d)],
        out_shape=[jax.ShapeDtypeStruct((s, d), BF16), jax.ShapeDtypeStruct((1, d), F32)],
        scratch_shapes=[pltpu.VMEM((SUBLANES, d), F32)], compiler_params=_params("arbitrary"), name=name,
    )(dh, m, gain)


def _pre_norm_bwd(du, h, gain, dh, name):
    s, d = h.shape
    tm = min(s, 512)
    n_steps = s // tm
    return pl.pallas_call(
        _norm_bwd_body(n_steps, True), grid=(n_steps,),
        in_specs=[_row_spec(tm, d), _row_spec(tm, d), _gain_spec(d), _row_spec(tm, d)],
        out_specs=[_row_spec(tm, d), _gain_spec(d)],
        out_shape=[jax.ShapeDtypeStruct((s, d), F32), jax.ShapeDtypeStruct((1, d), F32)],
        scratch_shapes=[pltpu.VMEM((SUBLANES, d), F32)], compiler_params=_params("arbitrary"), name=name,
    )(du, h, gain, dh)


def _loss_head(y, target, name):
    s, d = y.shape
    tm = min(s, 512)
    n_steps = s // tm

    def body(y_ref, t_ref, dy_ref, loss_ref, acc_ref):
        i = pl.program_id(0)

        @pl.when(i == 0)
        def _():
            acc_ref[...] = jnp.zeros_like(acc_ref)

        err = y_ref[...] - t_ref[...]
        dy_ref[...] = err / d
        _sum_rows_into(acc_ref, err * err)

        @pl.when(i == n_steps - 1)
        def _():
            total = jnp.sum(jnp.sum(acc_ref[...], axis=0, keepdims=True), axis=1, keepdims=True)
            loss_ref[...] = 0.5 * total / d

    return pl.pallas_call(
        body, grid=(n_steps,), in_specs=[_row_spec(tm, d), _row_spec(tm, d)],
        out_specs=[_row_spec(tm, d), pl.BlockSpec((1, 1), lambda i: (0, 0))],
        out_shape=[jax.ShapeDtypeStruct((s, d), F32), jax.ShapeDtypeStruct((1, 1), F32)],
        scratch_shapes=[pltpu.VMEM((SUBLANES, d), F32)], compiler_params=_params("arbitrary"), name=name,
    )(y, target)


def _shift_down(p, halo, row, n):
    out = jnp.where(row == 0, halo[SUBLANES - n:SUBLANES - n + 1], pltpu.roll(p, n, 0))
    if n == 2:
        out = jnp.where(row == 1, halo[SUBLANES - 1:SUBLANES], out)
    return out


def _shift_up(p, halo, row, n):
    tm = p.shape[0]
    out = jnp.where(row == tm - 1, halo[n - 1:n], pltpu.roll(p, tm - n, 0))
    if n == 2:
        out = jnp.where(row == tm - 2, halo[0:1], out)
    return out


def _conv_specs(tm, tc, nb, n_row_blocks):
    hb = tm // SUBLANES
    cur = lambda part: pl.BlockSpec((tm, tc), lambda i, j: (i, part * nb + j))
    prev = lambda part: pl.BlockSpec((SUBLANES, tc), lambda i, j: (jnp.maximum(i * hb - 1, 0), part * nb + j))
    nxt = lambda part: pl.BlockSpec(
        (SUBLANES, tc), lambda i, j: (jnp.minimum((i + 1) * hb, n_row_blocks * hb - 1), part * nb + j))
    return cur, prev, nxt


def _conv_gate_fwd(proj, conv_w, name):
    s, b4 = proj.shape
    bdim = b4 // 4
    tm, tc = min(s, 512), min(bdim, 512)
    nb = bdim // tc
    cur, prev, _ = _conv_specs(tm, tc, nb, s // tm)

    def body(b_ref, c_ref, x_ref, z_ref, cp_ref, xp_ref, w_ref, a_ref):
        i = pl.program_id(0)
        row = lax.broadcasted_iota(jnp.int32, (tm, tc), 0)
        p = c_ref[...] * x_ref[...]
        halo = jnp.where(i > 0, cp_ref[...] * xp_ref[...], 0.0)
        w = w_ref[...]
        cv = w[0:1] * _shift_down(p, halo, row, 2) + w[1:2] * _shift_down(p, halo, row, 1) + w[2:3] * p
        silu, _ = _silu_parts(z_ref[...])
        a_ref[...] = (silu * (b_ref[...] * cv)).astype(a_ref.dtype)

    return pl.pallas_call(
        body, grid=(s // tm, nb),
        in_specs=[cur(0), cur(1), cur(2), cur(3), prev(1), prev(2), pl.BlockSpec((CONV_K, tc), lambda i, j: (0, j))],
        out_specs=pl.BlockSpec((tm, tc), lambda i, j: (i, j)),
        out_shape=jax.ShapeDtypeStruct((s, bdim), BF16), compiler_params=_params("parallel", "parallel"), name=name,
    )(proj, proj, proj, proj, proj, proj, conv_w)


def _conv_gate_bwd(proj, da, conv_w, name):
    s, b4 = proj.shape
    bdim = b4 // 4
    tm, tc = min(s, 512), min(bdim, 512)
    nb = bdim // tc
    n_rows = s // tm
    cur, prev, nxt = _conv_specs(tm, tc, nb, n_rows)
    da_cur = pl.BlockSpec((tm, tc), lambda j, i: (i, j))
    hb = tm // SUBLANES
    da_nxt = pl.BlockSpec((SUBLANES, tc), lambda j, i: (jnp.minimum((i + 1) * hb, n_rows * hb - 1), j))
    swap = lambda spec: pl.BlockSpec(spec.block_shape, lambda j, i, f=spec.index_map: f(i, j))

    def body(b_ref, c_ref, x_ref, z_ref, cp_ref, xp_ref, bn_ref, zn_ref, da_ref, dan_ref, w_ref,
             db_ref, dc_ref, dx_ref, dz_ref, dw_ref, acc_ref):
        i = pl.program_id(1)

        @pl.when(i == 0)
        def _():
            acc_ref[...] = jnp.zeros_like(acc_ref)

        row = lax.broadcasted_iota(jnp.int32, (tm, tc), 0)
        w = w_ref[...]
        b, c, x = b_ref[...], c_ref[...], x_ref[...]
        p = c * x
        halo_p = jnp.where(i > 0, cp_ref[...] * xp_ref[...], 0.0)
        p1, p2 = _shift_down(p, halo_p, row, 1), _shift_down(p, halo_p, row, 2)
        cv = w[0:1] * p2 + w[1:2] * p1 + w[2:3] * p
        z = z_ref[...]
        silu, sig = _silu_parts(z)
        da = da_ref[...]
        dy = da * silu
        dcv = dy * b
        silu_n, _ = _silu_parts(zn_ref[...])
        halo_d = jnp.where(i < n_rows - 1, dan_ref[...] * silu_n * bn_ref[...], 0.0)
        dp = w[2:3] * dcv + w[1:2] * _shift_up(dcv, halo_d, row, 1) + w[0:1] * _shift_up(dcv, halo_d, row, 2)
        db_ref[...] = (dy * cv).astype(db_ref.dtype)
        dc_ref[...] = (dp * x).astype(dc_ref.dtype)
        dx_ref[...] = (dp * c).astype(dx_ref.dtype)
        dz_ref[...] = (da * (b * cv) * (sig * (1.0 + z * (1.0 - sig)))).astype(dz_ref.dtype)
        for k, pk in enumerate((p2, p1, p)):
            _sum_rows_into(acc_ref.at[k], dcv * pk)

        @pl.when(i == n_rows - 1)
        def _():
            for k in range(CONV_K):
                dw_ref[k:k + 1, :] = jnp.sum(acc_ref[k], axis=0, keepdims=True)

    out = pl.BlockSpec((tm, tc), lambda j, i: (i, j))
    act = jax.ShapeDtypeStruct((s, bdim), BF16)
    return pl.pallas_call(
        body, grid=(nb, n_rows),
        in_specs=[swap(cur(0)), swap(cur(1)), swap(cur(2)), swap(cur(3)), swap(prev(1)), swap(prev(2)),
                  swap(nxt(0)), swap(nxt(3)), da_cur, da_nxt, pl.BlockSpec((CONV_K, tc), lambda j, i: (0, j))],
        out_specs=[out, out, out, out, pl.BlockSpec((CONV_K, tc), lambda j, i: (0, j))],
        out_shape=[act, act, act, act, jax.ShapeDtypeStruct((CONV_K, bdim), F32)],
        scratch_shapes=[pltpu.VMEM((CONV_K, SUBLANES, tc), F32)],
        compiler_params=_params("parallel", "arbitrary"), name=name,
    )(proj, proj, proj, proj, proj, proj, proj, proj, da, da, conv_w)


def _split_dot(x, u):
    hi = x.astype(BF16)
    lo = (x - hi.astype(F32)).astype(BF16)
    return (lax.dot_general(hi, u, NN, preferred_element_type=F32)
            + lax.dot_general(lo, u, NN, preferred_element_type=F32))


def _sb_tile(q, k, carry, suffix_ones, mask):
    logits = lax.dot_general(q, k, NT, preferred_element_type=F32) * (1.0 / math.sqrt(HEAD_DIM))
    e = jnp.exp(-jnp.abs(logits))
    log_keep = -(jnp.maximum(logits, 0.0) + jnp.log1p(e))
    if mask is not None:
        log_keep = jnp.where(mask, log_keep, 0.0)
    tail = _split_dot(log_keep, suffix_ones)
    w = jnp.exp(logits + tail + carry)
    if mask is not None:
        w = jnp.where(mask, w, 0.0)
    return logits, e, tail, w


def _tri(n, upper):
    r = lax.broadcasted_iota(jnp.int32, (n, n), 0)
    c = lax.broadcasted_iota(jnp.int32, (n, n), 1)
    return jnp.where(r <= c if upper else r >= c, 1.0, 0.0).astype(BF16)


def _head_spec(s, part, n_heads):
    return pl.BlockSpec((s, HEAD_DIM), lambda h: (0, part * n_heads + h))


def _sb_attn_fwd(qkv, name):
    s, b3 = qkv.shape
    bdim = b3 // 3
    n_heads = bdim // HEAD_DIM
    blk = min(s, 256)
    n_blk = s // blk

    def body(q_ref, k_ref, v_ref, o_ref, car_ref):
        suffix_ones = _tri(blk, upper=False)
        r = lax.broadcasted_iota(jnp.int32, (blk, blk), 0)
        c = lax.broadcasted_iota(jnp.int32, (blk, blk), 1)
        diag_mask = c < r
        lane = lax.broadcasted_iota(jnp.int32, (blk, LANES), 1)

        def q_block(qi, _):
            q0 = pl.multiple_of(qi * blk, blk)
            q = q_ref[pl.ds(q0, blk), :]

            def step(j, mask, state):
                carry, acc, saved = state
                k0 = pl.multiple_of(j * blk, blk)
                _, _, tail, w = _sb_tile(q, k_ref[pl.ds(k0, blk), :], carry, suffix_ones, mask)
                acc = acc + lax.dot_general(w.astype(BF16), v_ref[pl.ds(k0, blk), :], NN, preferred_element_type=F32)
                saved = jnp.where(lane == j, carry, saved)
                return carry + tail[:, 0:1], acc, saved

            state = (jnp.zeros((blk, 1), F32), jnp.zeros((blk, HEAD_DIM), F32), jnp.zeros((blk, LANES), F32))
            state = step(qi, diag_mask, state)
            state = lax.fori_loop(0, qi, lambda jj, st: step(qi - 1 - jj, None, st), state)
            o_ref[pl.ds(q0, blk), :] = state[1]
            car_ref[pl.ds(q0, blk), :] = state[2]
            return 0

        lax.fori_loop(0, n_blk, q_block, 0)

    out = pl.BlockSpec((s, HEAD_DIM), lambda h: (0, h))
    shape = jax.ShapeDtypeStruct((s, bdim), F32)
    return pl.pallas_call(
        body, grid=(n_heads,),
        in_specs=[_head_spec(s, 0, n_heads), _head_spec(s, 1, n_heads), _head_spec(s, 2, n_heads)],
        out_specs=[out, out], out_shape=[shape, shape], compiler_params=_params("parallel"), name=name,
    )(qkv, qkv, qkv)


def _sb_attn_bwd(qkv, do, carries, name):
    s, b3 = qkv.shape
    bdim = b3 // 3
    n_heads = bdim // HEAD_DIM
    blk = min(s, 256)
    n_blk = s // blk
    scale = 1.0 / math.sqrt(HEAD_DIM)

    def body(q_ref, k_ref, v_ref, do_ref, car_ref, dq_ref, dk_ref, dv_ref, dk_acc, dv_acc):
        suffix_ones = _tri(blk, upper=False)
        prefix_ones = _tri(blk, upper=True)
        r = lax.broadcasted_iota(jnp.int32, (blk, blk), 0)
        c = lax.broadcasted_iota(jnp.int32, (blk, blk), 1)
        diag_mask = c < r
        lane = lax.broadcasted_iota(jnp.int32, (blk, LANES), 1)
        dk_acc[...] = jnp.zeros_like(dk_acc)
        dv_acc[...] = jnp.zeros_like(dv_acc)

        def q_block(qi, _):
            q0 = pl.multiple_of(qi * blk, blk)
            q = q_ref[pl.ds(q0, blk), :]
            do_q = do_ref[pl.ds(q0, blk), :]
            saved = car_ref[pl.ds(q0, blk), :]

            def step(j, mask, state):
                g_before, dq = state
                k0 = pl.multiple_of(j * blk, blk)
                k = k_ref[pl.ds(k0, blk), :]
                v = v_ref[pl.ds(k0, blk), :]
                carry = jnp.sum(jnp.where(lane == j, saved, 0.0), axis=1, keepdims=True)
                logits, e, _, w = _sb_tile(q, k, carry, suffix_ones, mask)
                dw = lax.dot_general(do_q, v, NT, preferred_element_type=F32)
                g = dw * w
                g_upto = _split_dot(g, prefix_ones)
                sig = jnp.where(logits >= 0.0, 1.0, e) / (1.0 + e)
                dlogits = g - sig * (g_before + g_upto)
                if mask is not None:
                    dlogits = jnp.where(mask, dlogits, 0.0)
                ds = (dlogits * scale).astype(BF16)
                dq = dq + lax.dot_general(ds, k, NN, preferred_element_type=F32)
                dk_acc[pl.ds(k0, blk), :] += lax.dot_general(ds, q, TN, preferred_element_type=F32)
                dv_acc[pl.ds(k0, blk), :] += lax.dot_general(w.astype(BF16), do_q, TN, preferred_element_type=F32)
                return g_before + g_upto[:, blk - 1:blk], dq

            state = (jnp.zeros((blk, 1), F32), jnp.zeros((blk, HEAD_DIM), F32))
            state = lax.fori_loop(0, qi, lambda j, st: step(j, None, st), state)
            state = step(qi, diag_mask, state)
            dq_ref[pl.ds(q0, blk), :] = state[1].astype(dq_ref.dtype)
            return 0

        lax.fori_loop(0, n_blk, q_block, 0)
        dk_ref[...] = dk_acc[...].astype(dk_ref.dtype)
        dv_ref[...] = dv_acc[...].astype(dv_ref.dtype)

    head = pl.BlockSpec((s, HEAD_DIM), lambda h: (0, h))
    shape = jax.ShapeDtypeStruct((s, bdim), BF16)
    return pl.pallas_call(
        body, grid=(n_heads,),
        in_specs=[_head_spec(s, 0, n_heads), _head_spec(s, 1, n_heads), _head_spec(s, 2, n_heads), head, head],
        out_specs=[head, head, head], out_shape=[shape, shape, shape],
        scratch_shapes=[pltpu.VMEM((s, HEAD_DIM), F32), pltpu.VMEM((s, HEAD_DIM), F32)],
        compiler_params=_params("parallel"), name=name,
    )(qkv, qkv, qkv, do, carries)


def _sb_gate_fwd(z, o, name):
    s, bdim = z.shape
    tm = min(s, 512)

    def body(z_ref, o_ref, a_ref):
        silu, _ = _silu_parts(z_ref[...])
        a_ref[...] = (silu * o_ref[...]).astype(a_ref.dtype)

    return pl.pallas_call(
        body, grid=(s // tm,), in_specs=[_row_spec(tm, bdim), _row_spec(tm, bdim)], out_specs=_row_spec(tm, bdim),
        out_shape=jax.ShapeDtypeStruct((s, bdim), BF16), compiler_params=_params("parallel"), name=name,
    )(z, o)


def _sb_gate_bwd(da, z, o, name):
    s, bdim = z.shape
    tm = min(s, 512)

    def body(da_ref, z_ref, o_ref, do_ref, dz_ref):
        z = z_ref[...]
        da = da_ref[...]
        silu, sig = _silu_parts(z)
        do_ref[...] = (da * silu).astype(do_ref.dtype)
        dz_ref[...] = (da * o_ref[...] * (sig * (1.0 + z * (1.0 - sig)))).astype(dz_ref.dtype)

    spec = _row_spec(tm, bdim)
    shape = jax.ShapeDtypeStruct((s, bdim), BF16)
    return pl.pallas_call(
        body, grid=(s // tm,), in_specs=[spec, spec, spec], out_specs=[spec, spec], out_shape=[shape, shape],
        compiler_params=_params("parallel"), name=name,
    )(da, z, o)


def _stack_cast(arrays, name):
    n = len(arrays)
    r, c = arrays[0].shape
    tr = min(r, 256)

    def body(*refs):
        out_ref = refs[n]
        for k in range(n):
            out_ref[k] = refs[k][...].astype(out_ref.dtype)

    return pl.pallas_call(
        body, grid=(r // tr,), in_specs=[pl.BlockSpec((tr, c), lambda i: (i, 0))] * n,
        out_specs=pl.BlockSpec((n, tr, c), lambda i: (0, i, 0)),
        out_shape=jax.ShapeDtypeStruct((n, r, c), BF16), compiler_params=_params("parallel"), name=name,
    )(*arrays)


def _add_core_pair(grads, received, core, name):
    _, _, r, c = grads.shape
    tr = min(r, 256)

    def body(core_ref, g_ref, r_ref, o_ref):
        o_ref[...] = (g_ref[...].astype(F32) + r_ref[...].astype(F32)).astype(o_ref.dtype)

    grid_spec = pltpu.PrefetchScalarGridSpec(
        num_scalar_prefetch=1, grid=(N_CHIP, r // tr),
        in_specs=[pl.BlockSpec((None, None, tr, c), lambda q, i, core_ref: (q, core_ref[0], i, 0)),
                  pl.BlockSpec((None, tr, c), lambda q, i, core_ref: (q, i, 0))],
        out_specs=pl.BlockSpec((None, tr, c), lambda q, i, core_ref: (q, i, 0)),
    )
    return pl.pallas_call(
        body, grid_spec=grid_spec, out_shape=jax.ShapeDtypeStruct((N_CHIP, r, c), BF16),
        compiler_params=_params("parallel", "parallel"), name=name,
    )(core, grads, received)


def _adamw(w, parts, m, v, name):
    r, c = w.shape
    n_parts = parts.shape[0]
    tr = min(r, 256)

    def body(w_ref, p_ref, m_ref, v_ref, g_ref, d_ref, nm_ref, nv_ref):
        g = p_ref[0].astype(F32)
        for k in range(1, n_parts):
            g = g + p_ref[k].astype(F32)
        new_m = ADAM_B1 * m_ref[...] + (1.0 - ADAM_B1) * g
        new_v = ADAM_B2 * v_ref[...] + (1.0 - ADAM_B2) * (g * g)
        m_hat = new_m / (1.0 - ADAM_B1 ** ADAM_STEP)
        v_hat = new_v / (1.0 - ADAM_B2 ** ADAM_STEP)
        g_ref[...] = g
        d_ref[...] = -ADAM_LR * (m_hat / (jnp.sqrt(v_hat) + ADAM_EPS) + ADAM_WD * w_ref[...])
        nm_ref[...] = new_m
        nv_ref[...] = new_v

    spec = pl.BlockSpec((tr, c), lambda i: (i, 0))
    shape = jax.ShapeDtypeStruct((r, c), F32)
    return pl.pallas_call(
        body, grid=(r // tr,), in_specs=[spec, pl.BlockSpec((n_parts, tr, c), lambda i: (0, i, 0)), spec, spec],
        out_specs=[spec] * 4, out_shape=[shape] * 4, compiler_params=_params("parallel"), name=name,
    )(w, parts, m, v)


def _place():
    x, y, c = lax.axis_index("x"), lax.axis_index("y"), lax.axis_index("c")
    other_chips = [(1 - x, y), (x, 1 - y), (1 - x, 1 - y)]
    return x, y, c, other_chips


def _all_gather(blocks, name):
    n_arr = len(blocks)
    items = [(a, i) for a, blk in enumerate(blocks) for i in range(blk.shape[0])]
    n_items = len(items)

    def body(*refs):
        srcs, outs = refs[:n_arr], refs[n_arr:2 * n_arr]
        send_sems, recv_sems, local_sems = refs[2 * n_arr:]
        x, y, c, other_chips = _place()
        me, sibling = (x, y, c), (x, y, 1 - c)

        def slot(it, dev):
            a, i = items[it]
            return outs[a].at[i, 4 * dev[0] + 2 * dev[1] + dev[2]]

        def copy(it, k, block_of, to, from_src=False):
            a, i = items[it]
            return pltpu.make_async_remote_copy(
                src_ref=srcs[a].at[i] if from_src else slot(it, block_of), dst_ref=slot(it, block_of),
                send_sem=send_sems.at[it * 7 + k], recv_sem=recv_sems.at[it * 7 + k],
                device_id=to, device_id_type=MESH)

        own = [pltpu.make_async_copy(srcs[items[it][0]].at[items[it][1]], slot(it, me), local_sems.at[it])
               for it in range(n_items)]
        for cp in own:
            cp.start()
        first = []
        for it in range(n_items):
            first.append(copy(it, 0, me, sibling, from_src=True))
            first += [copy(it, 1 + j, me, (*chip, c), from_src=True) for j, chip in enumerate(other_chips)]
        for cp in first:
            cp.start()
        passed = []
        for it in range(n_items):
            for j, chip in enumerate(other_chips):
                copy(it, 1 + j, (*chip, c), me).wait_recv()
                passed.append(copy(it, 4 + j, (*chip, c), sibling))
                passed[-1].start()
        for it in range(n_items):
            copy(it, 0, sibling, me).wait_recv()
            for j, chip in enumerate(other_chips):
                copy(it, 4 + j, (*chip, 1 - c), me).wait_recv()
        for cp in first + passed:
            cp.wait_send()
        for cp in own:
            cp.wait()

    return pl.pallas_call(
        body, in_specs=[ANY] * n_arr, out_specs=[ANY] * n_arr,
        out_shape=[jax.ShapeDtypeStruct((b.shape[0], N_DEV) + b.shape[1:], b.dtype) for b in blocks],
        scratch_shapes=[pltpu.SemaphoreType.DMA((7 * n_items,)), pltpu.SemaphoreType.DMA((7 * n_items,)),
                        pltpu.SemaphoreType.DMA((n_items,))],
        name=name,
    )(*blocks)


def _exchange_core_pair(grads, name):
    n_arr = len(grads)

    def body(*refs):
        srcs, outs = refs[:n_arr], refs[n_arr:2 * n_arr]
        send_sems, recv_sems = refs[2 * n_arr:]
        x, y, c, _ = _place()
        copies = [
            pltpu.make_async_remote_copy(
                src_ref=srcs[a].at[q, 1 - c], dst_ref=outs[a].at[q],
                send_sem=send_sems.at[a * N_CHIP + q], recv_sem=recv_sems.at[a * N_CHIP + q],
                device_id=(x, y, 1 - c), device_id_type=MESH)
            for a in range(n_arr) for q in range(N_CHIP)]
        for cp in copies:
            cp.start()
        for cp in copies:
            cp.wait_recv()
        for cp in copies:
            cp.wait_send()

    return pl.pallas_call(
        body, in_specs=[ANY] * n_arr, out_specs=[ANY] * n_arr,
        out_shape=[jax.ShapeDtypeStruct((N_CHIP,) + g.shape[2:], g.dtype) for g in grads],
        scratch_shapes=[pltpu.SemaphoreType.DMA((N_CHIP * n_arr,)), pltpu.SemaphoreType.DMA((N_CHIP * n_arr,))],
        name=name,
    )(*grads)


def _exchange_chips(partials, name):
    n_arr = len(partials)

    def body(*refs):
        srcs, outs = refs[:n_arr], refs[n_arr:2 * n_arr]
        send_sems, recv_sems, local_sems = refs[2 * n_arr:]
        x, y, c, other_chips = _place()
        my_chip = 2 * x + y
        own = [pltpu.make_async_copy(srcs[a].at[my_chip], outs[a].at[my_chip], local_sems.at[a]) for a in range(n_arr)]
        for cp in own:
            cp.start()
        copies = [
            pltpu.make_async_remote_copy(
                src_ref=srcs[a].at[2 * chip[0] + chip[1]], dst_ref=outs[a].at[my_chip],
                send_sem=send_sems.at[a * 3 + j], recv_sem=recv_sems.at[a * 3 + j],
                device_id=(*chip, c), device_id_type=MESH)
            for a in range(n_arr) for j, chip in enumerate(other_chips)]
        for cp in copies:
            cp.start()
        for cp in copies:
            cp.wait_recv()
        for cp in copies:
            cp.wait_send()
        for cp in own:
            cp.wait()

    return pl.pallas_call(
        body, in_specs=[ANY] * n_arr, out_specs=[ANY] * n_arr,
        out_shape=[jax.ShapeDtypeStruct(p.shape, p.dtype) for p in partials],
        scratch_shapes=[pltpu.SemaphoreType.DMA((3 * n_arr,)), pltpu.SemaphoreType.DMA((3 * n_arr,)),
                        pltpu.SemaphoreType.DMA((n_arr,))],
        name=name,
    )(*partials)


def kernel(x, ln_pre_0, conv_w_in_0, conv_w_0, conv_w_out_0, ln_post_0, ln_pre_1, sb_w_in_1, sb_w_out_1, ln_post_1, ln_pre_2, conv_w_in_2, conv_w_2, conv_w_out_2, ln_post_2, ln_pre_3, sb_w_in_3, sb_w_out_3, ln_post_3, loss_target, m_ln_pre_0, m_conv_w_in_0, m_conv_w_0, m_conv_w_out_0, m_ln_post_0, m_ln_pre_1, m_sb_w_in_1, m_sb_w_out_1, m_ln_post_1, m_ln_pre_2, m_conv_w_in_2, m_conv_w_2, m_conv_w_out_2, m_ln_post_2, m_ln_pre_3, m_sb_w_in_3, m_sb_w_out_3, m_ln_post_3, v_ln_pre_0, v_conv_w_in_0, v_conv_w_0, v_conv_w_out_0, v_ln_post_0, v_ln_pre_1, v_sb_w_in_1, v_sb_w_out_1, v_ln_post_1, v_ln_pre_2, v_conv_w_in_2, v_conv_w_2, v_conv_w_out_2, v_ln_post_2, v_ln_pre_3, v_sb_w_in_3, v_sb_w_out_3, v_ln_post_3):
    names = ['ln_pre_0', 'conv_w_in_0', 'conv_w_0', 'conv_w_out_0', 'ln_post_0', 'ln_pre_1', 'sb_w_in_1', 'sb_w_out_1',
             'ln_post_1', 'ln_pre_2', 'conv_w_in_2', 'conv_w_2', 'conv_w_out_2', 'ln_post_2', 'ln_pre_3', 'sb_w_in_3',
             'sb_w_out_3', 'ln_post_3']
    given = dict(locals())
    w = {n: given[n] for n in names}
    mom = {n: given["m_" + n] for n in names}
    var = {n: given["v_" + n] for n in names}
    conv_layers = [i for i in range(DEPTH) if i % 2 == 0]
    w_in_names = [("conv_w_in_%d" if i % 2 == 0 else "sb_w_in_%d") % i for i in range(DEPTH)]
    w_out_names = [("conv_w_out_%d" if i % 2 == 0 else "sb_w_out_%d") % i for i in range(DEPTH)]

    s, d = x.shape[1:]
    h = x.reshape(s, d)
    target = loss_target.reshape(s, d)
    gains = {n: w[n].reshape(1, d) for n in names if n.startswith("ln_")}
    place = 4 * lax.axis_index("x") + 2 * lax.axis_index("y") + lax.axis_index("c")
    core = lax.axis_index("c").astype(jnp.int32).reshape(1)

    w_in_all, w_out_all, conv_all = _all_gather(
        [_stack_cast([w[n] for n in w_in_names], "cast_w_in"), _stack_cast([w[n] for n in w_out_names], "cast_w_out"),
         jnp.stack([w["conv_w_%d" % i] for i in conv_layers])], "gather_weights")
    bdim = w_out_all.shape[1] * w_out_all.shape[2]
    w_out_all = w_out_all.reshape(DEPTH, bdim, d)
    conv_all = jnp.swapaxes(conv_all, 1, 2).reshape(len(conv_layers), CONV_K, bdim)
    conv_full = {i: conv_all[n] for n, i in enumerate(conv_layers)}

    saved = []
    for i in range(DEPTH):
        u = _rmsnorm_fwd(h, gains["ln_pre_%d" % i], "pre_norm_%d" % i)
        if i % 2 == 0:
            proj = _proj(u, w_in_all, i, 0, N_DEV, F32, "proj_%d" % i)
            a = _conv_gate_fwd(proj, conv_full[i], "conv_gate_%d" % i)
            extra = (proj,)
        else:
            qkv = _proj(u, w_in_all, i, 0, 6, BF16, "proj_qkv_%d" % i)
            z = _proj(u, w_in_all, i, 6, 2, F32, "proj_z_%d" % i)
            o, carries = _sb_attn_fwd(qkv, "sb_attn_%d" % i)
            a = _sb_gate_fwd(z, o, "sb_gate_%d" % i)
            extra = (qkv, z, o, carries)
        m = _out_proj(a, w_out_all, i, "out_proj_%d" % i)
        saved.append((h, u, a, m, extra))
        h = _post_norm_residual(h, m, gains["ln_post_%d" % i], "post_norm_%d" % i)

    dh, loss = _loss_head(h, target, "loss_head")
    loss = lax.psum(loss[0, 0], ("x", "y", "c"))

    small = {}
    big = {}
    for i in reversed(range(DEPTH)):
        h_in, u, a, m, extra = saved[i]
        dm, small["ln_post_%d" % i] = _post_norm_bwd(dh, m, gains["ln_post_%d" % i], "post_norm_bwd_%d" % i)
        big[w_out_names[i]] = _weight_grad(a, dm, 1, "grad_w_out_%d" % i).reshape(N_CHIP, 2, bdim // N_DEV, d)
        da = _out_proj_bwd_act(dm, w_out_all, i, "out_proj_bwd_%d" % i)
        if i % 2 == 0:
            (proj,) = extra
            db, dc, dxt, dz, small["conv_w_%d" % i] = _conv_gate_bwd(proj, da, conv_full[i], "conv_gate_bwd_%d" % i)
            dproj = jnp.concatenate([db, dc, dxt, dz], axis=1)
        else:
            qkv, z, o, carries = extra
            do, dz = _sb_gate_bwd(da, z, o, "sb_gate_bwd_%d" % i)
            dq, dk, dv = _sb_attn_bwd(qkv, do, carries, "sb_attn_bwd_%d" % i)
            dproj = jnp.concatenate([dq, dk, dv, dz], axis=1)
        g_in = _weight_grad(u, dproj, N_DEV, "grad_w_in_%d" % i)
        big[w_in_names[i]] = g_in.reshape((N_CHIP, 2) + g_in.shape[1:])
        du = _proj_bwd_act(dproj, w_in_all, i, "proj_bwd_%d" % i)
        dh, small["ln_pre_%d" % i] = _pre_norm_bwd(du, h_in, gains["ln_pre_%d" % i], dh, "pre_norm_bwd_%d" % i)

    big_names = w_in_names + w_out_names
    from_sibling = _exchange_core_pair([big[n] for n in big_names], "reduce_core_pair")
    pair_sums = [_add_core_pair(big[n], r, core, "add_core_pair_" + n) for n, r in zip(big_names, from_sibling)]
    chip_parts = dict(zip(big_names, _exchange_chips(pair_sums, "reduce_chips")))

    gain_names = [n for n in names if n.startswith("ln_")]
    conv_names = ["conv_w_%d" % i for i in conv_layers]
    rows = [small[n] for n in gain_names] + [small[n] for n in conv_names]
    n_rows = len(gain_names) + CONV_K * len(conv_names)
    pad = -n_rows % SUBLANES
    stacked = jnp.concatenate(rows + [jnp.zeros((pad, d), F32)], axis=0)
    (small_all,) = _all_gather([stacked[None]], "gather_small_grads")
    small_all = small_all[0]

    out_g, out_d, out_m, out_v = {}, {}, {}, {}

    def update(n, w2, parts, m2, v2, shape):
        g2, d2, nm2, nv2 = _adamw(w2, parts, m2, v2, "adamw_" + n)
        out_g[n], out_d[n], out_m[n], out_v[n] = (t.reshape(shape) for t in (g2, d2, nm2, nv2))

    for n in big_names:
        update(n, w[n], chip_parts[n], mom[n], var[n], w[n].shape)
    n_gain = len(gain_names)
    stack = lambda src: jnp.stack([src[n] for n in gain_names])
    g2, d2, nm2, nv2 = _adamw(stack(w), small_all[:, :n_gain], stack(mom), stack(var), "adamw_gains")
    for k, n in enumerate(gain_names):
        out_g[n], out_d[n], out_m[n], out_v[n] = g2[k], d2[k], nm2[k], nv2[k]
    wc = bdim // N_DEV
    for k, n in enumerate(conv_names):
        rows_k = small_all[:, n_gain + CONV_K * k:n_gain + CONV_K * (k + 1)]
        parts = lax.dynamic_slice_in_dim(rows_k, place * wc, wc, axis=2)
        update(n, w[n], parts, mom[n], var[n], w[n].shape)

    grad_x = dh.reshape(x.shape)
    return (loss, grad_x, *[out_g[n] for n in names], *[out_d[n] for n in names],
            *[out_m[n] for n in names], *[out_v[n] for n in names])
```

```python
import functools
import math

import jax
import jax.numpy as jnp
from jax import lax
from jax.experimental import pallas as pl
from jax.experimental.pallas import tpu as pltpu

F32 = jnp.float32
BF16 = jnp.bfloat16
MESH = pl.DeviceIdType.MESH
ANY = pl.BlockSpec(memory_space=pl.ANY)

N_DEV = 8
N_CHIP = 4
DEPTH = 4
HEAD_DIM = 128
CONV_K = 3
RMS_EPS = 1e-6
ADAM_LR = 0.001
ADAM_B1 = 0.9
ADAM_B2 = 0.999
ADAM_EPS = 1e-08
ADAM_WD = 0.01
ADAM_STEP = 10

V7X_VMEM_BYTES = 64 * 1024 * 1024
VMEM_LIMIT = V7X_VMEM_BYTES * 3 // 4
LANES = 128
SUBLANES = 8
HEADS_PER_STEP = 2


def _params(*sem):
    return pltpu.CompilerParams(dimension_semantics=sem, vmem_limit_bytes=VMEM_LIMIT)


def _silu_parts(z):
    sig = jax.nn.sigmoid(z)
    return z * sig, sig


NN = (((1,), (0,)), ((), ()))
NT = (((1,), (1,)), ((), ()))
TN = (((0,), (0,)), ((), ()))


def _mm(a, b, *, dims, grid, a_spec, b_spec, o_spec, out_shape, acc_shape, name):
    nk = grid[2]

    def body(a_ref, b_ref, o_ref, *scratch):
        p = lax.dot_general(a_ref[...], b_ref[...], dims, preferred_element_type=F32)
        if nk == 1:
            o_ref[...] = p.astype(o_ref.dtype)
        else:
            acc_ref = scratch[0]
            k = pl.program_id(2)

            @pl.when(k == 0)
            def _():
                acc_ref[...] = p

            @pl.when(k > 0)
            def _():
                acc_ref[...] += p

            @pl.when(k == nk - 1)
            def _():
                o_ref[...] = acc_ref[...].astype(o_ref.dtype)

    scratch = [] if nk == 1 else [pltpu.VMEM(acc_shape, F32)]
    return pl.pallas_call(
        body, grid=grid, in_specs=[a_spec, b_spec], out_specs=o_spec, out_shape=out_shape,
        scratch_shapes=scratch, compiler_params=_params("parallel", "parallel", "arbitrary"), name=name,
    )(a, b)


def _proj(u, w_in_all, layer, shard0, n_shard, out_dtype, name):
    s, d = u.shape
    ws = w_in_all.shape[-1]
    tm, tn = min(s, 512), min(ws, 1024)
    nj = ws // tn
    return _mm(
        u, w_in_all, dims=NN, grid=(s // tm, n_shard * nj, 1),
        a_spec=pl.BlockSpec((tm, d), lambda i, j, k: (i, 0)),
        b_spec=pl.BlockSpec((None, None, d, tn), lambda i, j, k: (layer, shard0 + j // nj, 0, j % nj)),
        o_spec=pl.BlockSpec((tm, tn), lambda i, j, k: (i, j)),
        out_shape=jax.ShapeDtypeStruct((s, n_shard * ws), out_dtype), acc_shape=(tm, tn), name=name,
    )


def _out_proj(a, w_out_all, layer, name):
    s, bdim = a.shape
    d = w_out_all.shape[-1]
    tm, tn = min(s, 512), min(d, 1024)
    return _mm(
        a, w_out_all, dims=NN, grid=(s // tm, d // tn, 1),
        a_spec=pl.BlockSpec((tm, bdim), lambda i, j, k: (i, 0)),
        b_spec=pl.BlockSpec((None, bdim, tn), lambda i, j, k: (layer, 0, j)),
        o_spec=pl.BlockSpec((tm, tn), lambda i, j, k: (i, j)),
        out_shape=jax.ShapeDtypeStruct((s, d), F32), acc_shape=(tm, tn), name=name,
    )


def _out_proj_bwd_act(dm, w_out_all, layer, name):
    s, d = dm.shape
    bdim = w_out_all.shape[-2]
    tm, tn = min(s, 512), min(bdim, 1024)
    return _mm(
        dm, w_out_all, dims=NT, grid=(s // tm, bdim // tn, 1),
        a_spec=pl.BlockSpec((tm, d), lambda i, j, k: (i, 0)),
        b_spec=pl.BlockSpec((None, tn, d), lambda i, j, k: (layer, j, 0)),
        o_spec=pl.BlockSpec((tm, tn), lambda i, j, k: (i, j)),
        out_shape=jax.ShapeDtypeStruct((s, bdim), F32), acc_shape=(tm, tn), name=name,
    )


def _weight_grad(act_t, dout, n_blocks, name):
    din, s = act_t.shape
    w = dout.shape[1] // n_blocks
    tm, tn = min(din, 512), min(w, 1024)
    nj = w // tn
    return _mm(
        act_t, dout, dims=NN, grid=(din // tm, n_blocks * nj, 1),
        a_spec=pl.BlockSpec((tm, s), lambda i, j, k: (i, 0)),
        b_spec=pl.BlockSpec((s, tn), lambda i, j, k: (0, j)),
        o_spec=pl.BlockSpec((None, tm, tn), lambda i, j, k: (j // nj, i, j % nj)),
        out_shape=jax.ShapeDtypeStruct((n_blocks, din, w), BF16), acc_shape=(tm, tn), name=name,
    )


def _proj_bwd_act(dproj, w_in_all, layer, name):
    s = dproj.shape[0]
    n_shards, d, ws = w_in_all.shape[-3:]
    tm, tn = min(s, 512), min(d, 512)

    def body(a_ref, b_ref, o_ref):
        acc = None
        for k in range(n_shards):
            p = lax.dot_general(a_ref[:, k * ws:(k + 1) * ws], b_ref[k], NT, preferred_element_type=F32)
            acc = p if acc is None else acc + p
        o_ref[...] = acc

    return pl.pallas_call(
        body, grid=(s // tm, d // tn),
        in_specs=[pl.BlockSpec((tm, n_shards * ws), lambda i, j: (i, 0)),
                  pl.BlockSpec((None, n_shards, tn, ws), lambda i, j: (layer, 0, j, 0))],
        out_specs=pl.BlockSpec((tm, tn), lambda i, j: (i, j)), out_shape=jax.ShapeDtypeStruct((s, d), F32),
        compiler_params=_params("parallel", "parallel"), name=name,
    )(dproj, w_in_all)


def _row_spec(tm, d):
    return pl.BlockSpec((tm, d), lambda i: (i, 0))


def _gain_spec(d):
    return pl.BlockSpec((1, d), lambda i: (0, 0))


def _rstd(x):
    return lax.rsqrt(jnp.mean(x * x, axis=-1, keepdims=True) + RMS_EPS)


def _rmsnorm_fwd(h, gain, name):
    s, d = h.shape
    tm = min(s, 512)

    def body(h_ref, g_ref, u_ref, ut_ref):
        x = h_ref[...]
        u = (x * _rstd(x) * g_ref[...]).astype(u_ref.dtype)
        u_ref[...] = u
        ut_ref[...] = u.T

    return pl.pallas_call(
        body, grid=(s // tm,), in_specs=[_row_spec(tm, d), _gain_spec(d)],
        out_specs=[_row_spec(tm, d), pl.BlockSpec((d, tm), lambda i: (0, i))],
        out_shape=[jax.ShapeDtypeStruct((s, d), BF16), jax.ShapeDtypeStruct((d, s), BF16)],
        compiler_params=_params("parallel"), name=name,
    )(h, gain)


def _post_norm_residual(h, m, gain, name):
    s, d = h.shape
    tm = min(s, 512)

    def body(h_ref, m_ref, g_ref, o_ref):
        x = m_ref[...]
        o_ref[...] = h_ref[...] + x * _rstd(x) * g_ref[...]

    return pl.pallas_call(
        body, grid=(s // tm,), in_specs=[_row_spec(tm, d), _row_spec(tm, d), _gain_spec(d)],
        out_specs=_row_spec(tm, d), out_shape=jax.ShapeDtypeStruct((s, d), F32),
        compiler_params=_params("parallel"), name=name,
    )(h, m, gain)


def _sum_rows_into(acc_ref, x):
    tm, d = x.shape
    acc_ref[...] += jnp.sum(x.reshape(tm // SUBLANES, SUBLANES, d), axis=0)


def _norm_bwd_body(n_steps, with_residual):
    def body(*refs):
        if with_residual:
            dy_ref, x_ref, g_ref, dres_ref, dx_ref, dg_ref, acc_ref = refs
        else:
            dy_ref, x_ref, g_ref, dx_ref, dg_ref, acc_ref = refs
        i = pl.program_id(0)

        @pl.when(i == 0)
        def _():
            acc_ref[...] = jnp.zeros_like(acc_ref)

        x = x_ref[...]
        dy = dy_ref[...]
        rstd = _rstd(x)
        n = x * rstd
        dn = dy * g_ref[...]
        dx = rstd * (dn - n * jnp.mean(dn * n, axis=-1, keepdims=True))
        if with_residual:
            dx = dres_ref[...] + dx
        dx_ref[...] = dx.astype(dx_ref.dtype)
        _sum_rows_into(acc_ref, dy * n)

        @pl.when(i == n_steps - 1)
        def _():
            dg_ref[...] = jnp.sum(acc_ref[...], axis=0, keepdims=True)

    return body


def _post_norm_bwd(dh, m, gain, name):
    s, d = m.shape
    tm = min(s, 512)
    n_steps = s // tm
    return pl.pallas_call(
        _norm_bwd_body(n_steps, False), grid=(n_steps,),
        in_specs=[_row_spec(tm, d), _row_spec(tm, d), _gain_spec(d)],
        out_specs=[_row_spec(tm, d), _gain_spec(d)],
        out_shape=[jax.ShapeDtypeStruct((s, d), BF16), jax.ShapeDtypeStruct((1, d), F32)],
        scratch_shapes=[pltpu.VMEM((SUBLANES, d), F32)], compiler_params=_params("arbitrary"), name=name,
    )(dh, m, gain)


def _pre_norm_bwd(du, h, gain, dh, name):
    s, d = h.shape
    tm = min(s, 512)
    n_steps = s // tm
    return pl.pallas_call(
        _norm_bwd_body(n_steps, True), grid=(n_steps,),
        in_specs=[_row_spec(tm, d), _row_spec(tm, d), _gain_spec(d), _row_spec(tm, d)],
        out_specs=[_row_spec(tm, d), _gain_spec(d)],
        out_shape=[jax.ShapeDtypeStruct((s, d), F32), jax.ShapeDtypeStruct((1, d), F32)],
        scratch_shapes=[pltpu.VMEM((SUBLANES, d), F32)], compiler_params=_params("arbitrary"), name=name,
    )(du, h, gain, dh)


def _loss_head(y, target, name):
    s, d = y.shape
    tm = min(s, 512)
    n_steps = s // tm

    def body(y_ref, t_ref, dy_ref, loss_ref, acc_ref):
        i = pl.program_id(0)

        @pl.when(i == 0)
        def _():
            acc_ref[...] = jnp.zeros_like(acc_ref)

        err = y_ref[...] - t_ref[...]
        dy_ref[...] = err / d
        _sum_rows_into(acc_ref, err * err)

        @pl.when(i == n_steps - 1)
        def _():
            total = jnp.sum(jnp.sum(acc_ref[...], axis=0, keepdims=True), axis=1, keepdims=True)
            loss_ref[...] = 0.5 * total / d

    return pl.pallas_call(
        body, grid=(n_steps,), in_specs=[_row_spec(tm, d), _row_spec(tm, d)],
        out_specs=[_row_spec(tm, d), pl.BlockSpec((1, 1), lambda i: (0, 0))],
        out_shape=[jax.ShapeDtypeStruct((s, d), F32), jax.ShapeDtypeStruct((1, 1), F32)],
        scratch_shapes=[pltpu.VMEM((SUBLANES, d), F32)], compiler_params=_params("arbitrary"), name=name,
    )(y, target)


def _shift_down(p, halo, row, n):
    out = jnp.where(row == 0, halo[SUBLANES - n:SUBLANES - n + 1], pltpu.roll(p, n, 0))
    if n == 2:
        out = jnp.where(row == 1, halo[SUBLANES - 1:SUBLANES], out)
    return out


def _shift_up(p, halo, row, n):
    tm = p.shape[0]
    out = jnp.where(row == tm - 1, halo[n - 1:n], pltpu.roll(p, tm - n, 0))
    if n == 2:
        out = jnp.where(row == tm - 2, halo[0:1], out)
    return out


def _conv_specs(tm, tc, nb, n_row_blocks):
    hb = tm // SUBLANES
    cur = lambda part: pl.BlockSpec((tm, tc), lambda i, j: (i, part * nb + j))
    prev = lambda part: pl.BlockSpec((SUBLANES, tc), lambda i, j: (jnp.maximum(i * hb - 1, 0), part * nb + j))
    nxt = lambda part: pl.BlockSpec(
        (SUBLANES, tc), lambda i, j: (jnp.minimum((i + 1) * hb, n_row_blocks * hb - 1), part * nb + j))
    return cur, prev, nxt


def _conv_gate_fwd(proj, conv_w, name):
    s, b4 = proj.shape
    bdim = b4 // 4
    tm, tc = min(s, 512), min(bdim, 512)
    nb = bdim // tc
    cur, prev, _ = _conv_specs(tm, tc, nb, s // tm)

    def body(b_ref, c_ref, x_ref, z_ref, cp_ref, xp_ref, w_ref, a_ref, at_ref):
        i = pl.program_id(0)
        row = lax.broadcasted_iota(jnp.int32, (tm, tc), 0)
        p = c_ref[...] * x_ref[...]
        halo = jnp.where(i > 0, cp_ref[...] * xp_ref[...], 0.0)
        w = w_ref[...]
        cv = w[0:1] * _shift_down(p, halo, row, 2) + w[1:2] * _shift_down(p, halo, row, 1) + w[2:3] * p
        silu, _ = _silu_parts(z_ref[...])
        a = (silu * (b_ref[...] * cv)).astype(a_ref.dtype)
        a_ref[...] = a
        at_ref[...] = a.T

    return pl.pallas_call(
        body, grid=(s // tm, nb),
        in_specs=[cur(0), cur(1), cur(2), cur(3), prev(1), prev(2), pl.BlockSpec((CONV_K, tc), lambda i, j: (0, j))],
        out_specs=[pl.BlockSpec((tm, tc), lambda i, j: (i, j)), pl.BlockSpec((tc, tm), lambda i, j: (j, i))],
        out_shape=[jax.ShapeDtypeStruct((s, bdim), BF16), jax.ShapeDtypeStruct((bdim, s), BF16)],
        compiler_params=_params("parallel", "parallel"), name=name,
    )(proj, proj, proj, proj, proj, proj, conv_w)


def _conv_gate_bwd(proj, da, conv_w, name):
    s, b4 = proj.shape
    bdim = b4 // 4
    tm, tc = min(s, 512), min(bdim, 512)
    nb = bdim // tc
    n_rows = s // tm
    cur, prev, nxt = _conv_specs(tm, tc, nb, n_rows)
    da_cur = pl.BlockSpec((tm, tc), lambda j, i: (i, j))
    hb = tm // SUBLANES
    da_nxt = pl.BlockSpec((SUBLANES, tc), lambda j, i: (jnp.minimum((i + 1) * hb, n_rows * hb - 1), j))
    swap = lambda spec: pl.BlockSpec(spec.block_shape, lambda j, i, f=spec.index_map: f(i, j))

    def body(b_ref, c_ref, x_ref, z_ref, cp_ref, xp_ref, bn_ref, zn_ref, da_ref, dan_ref, w_ref,
             db_ref, dc_ref, dx_ref, dz_ref, dw_ref, acc_ref):
        i = pl.program_id(1)

        @pl.when(i == 0)
        def _():
            acc_ref[...] = jnp.zeros_like(acc_ref)

        row = lax.broadcasted_iota(jnp.int32, (tm, tc), 0)
        w = w_ref[...]
        b, c, x = b_ref[...], c_ref[...], x_ref[...]
        p = c * x
        halo_p = jnp.where(i > 0, cp_ref[...] * xp_ref[...], 0.0)
        p1, p2 = _shift_down(p, halo_p, row, 1), _shift_down(p, halo_p, row, 2)
        cv = w[0:1] * p2 + w[1:2] * p1 + w[2:3] * p
        z = z_ref[...]
        silu, sig = _silu_parts(z)
        da = da_ref[...]
        dy = da * silu
        dcv = dy * b
        silu_n, _ = _silu_parts(zn_ref[...])
        halo_d = jnp.where(i < n_rows - 1, dan_ref[...] * silu_n * bn_ref[...], 0.0)
        dp = w[2:3] * dcv + w[1:2] * _shift_up(dcv, halo_d, row, 1) + w[0:1] * _shift_up(dcv, halo_d, row, 2)
        db_ref[...] = (dy * cv).astype(db_ref.dtype)
        dc_ref[...] = (dp * x).astype(dc_ref.dtype)
        dx_ref[...] = (dp * c).astype(dx_ref.dtype)
        dz_ref[...] = (da * (b * cv) * (sig * (1.0 + z * (1.0 - sig)))).astype(dz_ref.dtype)
        for k, pk in enumerate((p2, p1, p)):
            _sum_rows_into(acc_ref.at[k], dcv * pk)

        @pl.when(i == n_rows - 1)
        def _():
            for k in range(CONV_K):
                dw_ref[k:k + 1, :] = jnp.sum(acc_ref[k], axis=0, keepdims=True)

    out = pl.BlockSpec((tm, tc), lambda j, i: (i, j))
    act = jax.ShapeDtypeStruct((s, bdim), BF16)
    return pl.pallas_call(
        body, grid=(nb, n_rows),
        in_specs=[swap(cur(0)), swap(cur(1)), swap(cur(2)), swap(cur(3)), swap(prev(1)), swap(prev(2)),
                  swap(nxt(0)), swap(nxt(3)), da_cur, da_nxt, pl.BlockSpec((CONV_K, tc), lambda j, i: (0, j))],
        out_specs=[out, out, out, out, pl.BlockSpec((CONV_K, tc), lambda j, i: (0, j))],
        out_shape=[act, act, act, act, jax.ShapeDtypeStruct((CONV_K, bdim), F32)],
        scratch_shapes=[pltpu.VMEM((CONV_K, SUBLANES, tc), F32)],
        compiler_params=_params("parallel", "arbitrary"), name=name,
    )(proj, proj, proj, proj, proj, proj, proj, proj, da, da, conv_w)


def _split(x):
    hi = x.astype(BF16)
    lo = (x - hi.astype(F32)).astype(BF16)
    return jnp.concatenate([hi, lo], axis=1)


def _sb_tiles(qs, ks, carries, suffix_ones, mask):
    heads = range(len(qs))
    scale = 1.0 / math.sqrt(HEAD_DIM)
    logits = [lax.dot_general(qs[g], ks[g], NT, preferred_element_type=F32) * scale for g in heads]
    es = [jnp.exp(-jnp.abs(logits[g])) for g in heads]
    keeps = []
    for g in heads:
        log_keep = -(jnp.maximum(logits[g], 0.0) + jnp.log(1.0 + es[g]))
        if mask is not None:
            log_keep = jnp.where(mask, log_keep, 0.0)
        keeps.append(_split(log_keep))
    tails = [lax.dot_general(keeps[g], suffix_ones, NN, preferred_element_type=F32) for g in heads]
    ws = []
    for g in heads:
        w = jnp.exp(logits[g] + tails[g] + carries[g])
        if mask is not None:
            w = jnp.where(mask, w, 0.0)
        ws.append(w)
    return logits, es, tails, ws


def _tri_twice(n, upper):
    r = lax.broadcasted_iota(jnp.int32, (2 * n, n), 0)
    r = jnp.where(r >= n, r - n, r)
    c = lax.broadcasted_iota(jnp.int32, (2 * n, n), 1)
    return jnp.where(r <= c if upper else r >= c, 1.0, 0.0).astype(BF16)


def _group_spec(s, width, part, n_groups):
    return pl.BlockSpec((s, width), lambda h: (0, part * n_groups + h))


def _head_cols(g):
    return slice(g * HEAD_DIM, (g + 1) * HEAD_DIM)


def _lane_tile(x, n):
    return x if n == LANES else jnp.concatenate([x] * (n // LANES), axis=1)


def _sb_attn_fwd(qkv, name):
    s, b3 = qkv.shape
    bdim = b3 // 3
    hps = min(HEADS_PER_STEP, bdim // HEAD_DIM)
    width = hps * HEAD_DIM
    n_groups = bdim // width
    blk = min(s, 256)
    n_blk = s // blk

    def body(q_ref, k_ref, v_ref, o_ref, car_ref, carry_ref):
        suffix_ones = _tri_twice(blk, upper=False)
        r = lax.broadcasted_iota(jnp.int32, (blk, blk), 0)
        c = lax.broadcasted_iota(jnp.int32, (blk, blk), 1)
        diag_mask = c < r
        lane = lax.broadcasted_iota(jnp.int32, (blk, LANES), 1)

        def q_block(qi, _):
            q0 = pl.multiple_of(qi * blk, blk)
            rows = pl.ds(q0, blk)
            qs = [q_ref[rows, _head_cols(g)] for g in range(hps)]
            o_ref[rows, :] = jnp.zeros((blk, width), F32)
            car_ref[rows, :] = jnp.zeros((blk, width), F32)
            carry_ref[...] = jnp.zeros_like(carry_ref)

            def step(j, mask):
                k0 = pl.multiple_of(j * blk, blk)
                ks = [k_ref[pl.ds(k0, blk), _head_cols(g)] for g in range(hps)]
                carries = [carry_ref[g] for g in range(hps)]
                _, _, tails, ws = _sb_tiles(qs, ks, [_lane_tile(cy, blk) for cy in carries], suffix_ones, mask)
                for g in range(hps):
                    v = v_ref[pl.ds(k0, blk), _head_cols(g)]
                    o_ref[rows, _head_cols(g)] += lax.dot_general(ws[g].astype(BF16), v, NN, preferred_element_type=F32)
                    car_ref[rows, _head_cols(g)] = jnp.where(lane == j, carries[g], car_ref[rows, _head_cols(g)])
                    carry_ref[g] = carries[g] + jnp.broadcast_to(tails[g][:, 0:1], (blk, LANES))

            step(qi, diag_mask)

            def off_diagonal(jj, _):
                step(qi - 1 - jj, None)
                return 0

            lax.fori_loop(0, qi, off_diagonal, 0)
            return 0

        lax.fori_loop(0, n_blk, q_block, 0)

    out = pl.BlockSpec((s, width), lambda h: (0, h))
    shape = jax.ShapeDtypeStruct((s, bdim), F32)
    return pl.pallas_call(
        body, grid=(n_groups,),
        in_specs=[_group_spec(s, width, part, n_groups) for part in range(3)],
        out_specs=[out, out], out_shape=[shape, shape], scratch_shapes=[pltpu.VMEM((hps, blk, LANES), F32)],
        compiler_params=_params("parallel"), name=name,
    )(qkv, qkv, qkv)


def _sb_attn_bwd(qkv, do, carries, name):
    s, b3 = qkv.shape
    bdim = b3 // 3
    hps = min(HEADS_PER_STEP, bdim // HEAD_DIM)
    width = hps * HEAD_DIM
    n_groups = bdim // width
    blk = min(s, 256)
    n_blk = s // blk
    scale = 1.0 / math.sqrt(HEAD_DIM)

    def body(q_ref, k_ref, v_ref, do_ref, car_ref, dq_ref, dk_ref, dv_ref, dk_acc, dv_acc, dq_acc, before_ref):
        suffix_ones = _tri_twice(blk, upper=False)
        prefix_ones = _tri_twice(blk, upper=True)
        r = lax.broadcasted_iota(jnp.int32, (blk, blk), 0)
        c = lax.broadcasted_iota(jnp.int32, (blk, blk), 1)
        diag_mask = c < r
        lane = lax.broadcasted_iota(jnp.int32, (blk, LANES), 1)
        dk_acc[...] = jnp.zeros_like(dk_acc)
        dv_acc[...] = jnp.zeros_like(dv_acc)

        def q_block(qi, _):
            q0 = pl.multiple_of(qi * blk, blk)
            rows = pl.ds(q0, blk)
            qs = [q_ref[rows, _head_cols(g)] for g in range(hps)]
            dos = [do_ref[rows, _head_cols(g)] for g in range(hps)]
            dq_acc[...] = jnp.zeros_like(dq_acc)
            before_ref[...] = jnp.zeros_like(before_ref)

            def step(j, mask):
                k0 = pl.multiple_of(j * blk, blk)
                heads = range(hps)
                ks = [k_ref[pl.ds(k0, blk), _head_cols(g)] for g in heads]
                dws = [lax.dot_general(dos[g], v_ref[pl.ds(k0, blk), _head_cols(g)], NT, preferred_element_type=F32)
                       for g in heads]
                carries = [jnp.sum(jnp.where(lane == j, car_ref[rows, _head_cols(g)], 0.0), axis=1, keepdims=True)
                           for g in heads]
                logits, es, _, ws = _sb_tiles(qs, ks, carries, suffix_ones, mask)
                gws = [dws[g] * ws[g] for g in heads]
                g_upto = [lax.dot_general(_split(gws[g]), prefix_ones, NN, preferred_element_type=F32) for g in heads]
                dss = []
                for g in heads:
                    sig = jnp.where(logits[g] >= 0.0, 1.0, es[g]) / (1.0 + es[g])
                    dlogits = gws[g] - sig * (_lane_tile(before_ref[g], blk) + g_upto[g])
                    if mask is not None:
                        dlogits = jnp.where(mask, dlogits, 0.0)
                    dss.append((dlogits * scale).astype(BF16))
                for g in heads:
                    dq_acc[:, _head_cols(g)] += lax.dot_general(dss[g], ks[g], NN, preferred_element_type=F32)
                    dk_acc[pl.ds(k0, blk), _head_cols(g)] += lax.dot_general(
                        dss[g], qs[g], TN, preferred_element_type=F32)
                    dv_acc[pl.ds(k0, blk), _head_cols(g)] += lax.dot_general(
                        ws[g].astype(BF16), dos[g], TN, preferred_element_type=F32)
                    before_ref[g] += jnp.broadcast_to(g_upto[g][:, blk - 1:blk], (blk, LANES))

            def off_diagonal(j, _):
                step(j, None)
                return 0

            lax.fori_loop(0, qi, off_diagonal, 0)
            step(qi, diag_mask)
            dq_ref[rows, :] = dq_acc[...].astype(dq_ref.dtype)
            return 0

        lax.fori_loop(0, n_blk, q_block, 0)
        dk_ref[...] = dk_acc[...].astype(dk_ref.dtype)
        dv_ref[...] = dv_acc[...].astype(dv_ref.dtype)

    group = pl.BlockSpec((s, width), lambda h: (0, h))
    once = pl.BlockSpec((s, width), lambda h: (0, h), pipeline_mode=pl.Buffered(1))
    shape = jax.ShapeDtypeStruct((s, bdim), BF16)
    return pl.pallas_call(
        body, grid=(n_groups,),
        in_specs=[_group_spec(s, width, part, n_groups) for part in range(3)] + [once, once],
        out_specs=[group, group, group], out_shape=[shape, shape, shape],
        scratch_shapes=[pltpu.VMEM((s, width), F32), pltpu.VMEM((s, width), F32), pltpu.VMEM((blk, width), F32),
                        pltpu.VMEM((hps, blk, LANES), F32)],
        compiler_params=_params("parallel"), name=name,
    )(qkv, qkv, qkv, do, carries)


def _sb_gate_fwd(z, o, name):
    s, bdim = z.shape
    tm = min(s, 512)

    def body(z_ref, o_ref, a_ref, at_ref):
        silu, _ = _silu_parts(z_ref[...])
        a = (silu * o_ref[...]).astype(a_ref.dtype)
        a_ref[...] = a
        at_ref[...] = a.T

    return pl.pallas_call(
        body, grid=(s // tm,), in_specs=[_row_spec(tm, bdim), _row_spec(tm, bdim)],
        out_specs=[_row_spec(tm, bdim), pl.BlockSpec((bdim, tm), lambda i: (0, i))],
        out_shape=[jax.ShapeDtypeStruct((s, bdim), BF16), jax.ShapeDtypeStruct((bdim, s), BF16)],
        compiler_params=_params("parallel"), name=name,
    )(z, o)


def _sb_gate_bwd(da, z, o, name):
    s, bdim = z.shape
    tm = min(s, 512)

    def body(da_ref, z_ref, o_ref, do_ref, dz_ref):
        z = z_ref[...]
        da = da_ref[...]
        silu, sig = _silu_parts(z)
        do_ref[...] = (da * silu).astype(do_ref.dtype)
        dz_ref[...] = (da * o_ref[...] * (sig * (1.0 + z * (1.0 - sig)))).astype(dz_ref.dtype)

    spec = _row_spec(tm, bdim)
    shape = jax.ShapeDtypeStruct((s, bdim), BF16)
    return pl.pallas_call(
        body, grid=(s // tm,), in_specs=[spec, spec, spec], out_specs=[spec, spec], out_shape=[shape, shape],
        compiler_params=_params("parallel"), name=name,
    )(da, z, o)


def _stack_cast(arrays, name):
    n = len(arrays)
    r, c = arrays[0].shape
    tr = min(r, 256)

    def body(*refs):
        out_ref = refs[n]
        for k in range(n):
            out_ref[k] = refs[k][...].astype(out_ref.dtype)

    return pl.pallas_call(
        body, grid=(r // tr,), in_specs=[pl.BlockSpec((tr, c), lambda i: (i, 0))] * n,
        out_specs=pl.BlockSpec((n, tr, c), lambda i: (0, i, 0)),
        out_shape=jax.ShapeDtypeStruct((n, r, c), BF16), compiler_params=_params("parallel"), name=name,
    )(*arrays)


def _add_core_pair(grads, received, core, name):
    _, _, r, c = grads.shape
    tr = min(r, 256)

    def body(core_ref, g_ref, r_ref, o_ref):
        o_ref[...] = (g_ref[...].astype(F32) + r_ref[...].astype(F32)).astype(o_ref.dtype)

    grid_spec = pltpu.PrefetchScalarGridSpec(
        num_scalar_prefetch=1, grid=(N_CHIP, r // tr),
        in_specs=[pl.BlockSpec((None, None, tr, c), lambda q, i, core_ref: (q, core_ref[0], i, 0)),
                  pl.BlockSpec((None, tr, c), lambda q, i, core_ref: (q, i, 0))],
        out_specs=pl.BlockSpec((None, tr, c), lambda q, i, core_ref: (q, i, 0)),
    )
    return pl.pallas_call(
        body, grid_spec=grid_spec, out_shape=jax.ShapeDtypeStruct((N_CHIP, r, c), BF16),
        compiler_params=_params("parallel", "parallel"), name=name,
    )(core, grads, received)


def _adamw(w, parts, m, v, name):
    r, c = w.shape
    n_parts = parts.shape[0]
    tr = min(r, 256)

    def body(w_ref, p_ref, m_ref, v_ref, g_ref, d_ref, nm_ref, nv_ref):
        g = p_ref[0].astype(F32)
        for k in range(1, n_parts):
            g = g + p_ref[k].astype(F32)
        new_m = ADAM_B1 * m_ref[...] + (1.0 - ADAM_B1) * g
        new_v = ADAM_B2 * v_ref[...] + (1.0 - ADAM_B2) * (g * g)
        m_hat = new_m / (1.0 - ADAM_B1 ** ADAM_STEP)
        v_hat = new_v / (1.0 - ADAM_B2 ** ADAM_STEP)
        g_ref[...] = g
        d_ref[...] = -ADAM_LR * (m_hat / (jnp.sqrt(v_hat) + ADAM_EPS) + ADAM_WD * w_ref[...])
        nm_ref[...] = new_m
        nv_ref[...] = new_v

    spec = pl.BlockSpec((tr, c), lambda i: (i, 0))
    shape = jax.ShapeDtypeStruct((r, c), F32)
    return pl.pallas_call(
        body, grid=(r // tr,), in_specs=[spec, pl.BlockSpec((n_parts, tr, c), lambda i: (0, i, 0)), spec, spec],
        out_specs=[spec] * 4, out_shape=[shape] * 4, compiler_params=_params("parallel"), name=name,
    )(w, parts, m, v)


def _place():
    x, y, c = lax.axis_index("x"), lax.axis_index("y"), lax.axis_index("c")
    other_chips = [(1 - x, y), (x, 1 - y), (1 - x, 1 - y)]
    return x, y, c, other_chips


def _all_gather(blocks, name):
    n_arr = len(blocks)
    items = [(a, i) for a, blk in enumerate(blocks) for i in range(blk.shape[0])]
    n_items = len(items)

    def body(*refs):
        srcs, outs = refs[:n_arr], refs[n_arr:2 * n_arr]
        send_sems, recv_sems, local_sems = refs[2 * n_arr:]
        x, y, c, other_chips = _place()
        me, sibling = (x, y, c), (x, y, 1 - c)

        def slot(it, dev):
            a, i = items[it]
            return outs[a].at[i, 4 * dev[0] + 2 * dev[1] + dev[2]]

        def copy(it, k, block_of, to, from_src=False):
            a, i = items[it]
            return pltpu.make_async_remote_copy(
                src_ref=srcs[a].at[i] if from_src else slot(it, block_of), dst_ref=slot(it, block_of),
                send_sem=send_sems.at[it * 7 + k], recv_sem=recv_sems.at[it * 7 + k],
                device_id=to, device_id_type=MESH)

        own = [pltpu.make_async_copy(srcs[items[it][0]].at[items[it][1]], slot(it, me), local_sems.at[it])
               for it in range(n_items)]
        for cp in own:
            cp.start()
        first = []
        for it in range(n_items):
            first.append(copy(it, 0, me, sibling, from_src=True))
            first += [copy(it, 1 + j, me, (*chip, c), from_src=True) for j, chip in enumerate(other_chips)]
        for cp in first:
            cp.start()
        passed = []
        for it in range(n_items):
            for j, chip in enumerate(other_chips):
                copy(it, 1 + j, (*chip, c), me).wait_recv()
                passed.append(copy(it, 4 + j, (*chip, c), sibling))
                passed[-1].start()
        for it in range(n_items):
            copy(it, 0, sibling, me).wait_recv()
            for j, chip in enumerate(other_chips):
                copy(it, 4 + j, (*chip, 1 - c), me).wait_recv()
        for cp in first + passed:
            cp.wait_send()
        for cp in own:
            cp.wait()

    return pl.pallas_call(
        body, in_specs=[ANY] * n_arr, out_specs=[ANY] * n_arr,
        out_shape=[jax.ShapeDtypeStruct((b.shape[0], N_DEV) + b.shape[1:], b.dtype) for b in blocks],
        scratch_shapes=[pltpu.SemaphoreType.DMA((7 * n_items,)), pltpu.SemaphoreType.DMA((7 * n_items,)),
                        pltpu.SemaphoreType.DMA((n_items,))],
        name=name,
    )(*blocks)


def _exchange_core_pair(grads, name):
    n_arr = len(grads)

    def body(*refs):
        srcs, outs = refs[:n_arr], refs[n_arr:2 * n_arr]
        send_sems, recv_sems = refs[2 * n_arr:]
        x, y, c, _ = _place()
        copies = [
            pltpu.make_async_remote_copy(
                src_ref=srcs[a].at[q, 1 - c], dst_ref=outs[a].at[q],
                send_sem=send_sems.at[a * N_CHIP + q], recv_sem=recv_sems.at[a * N_CHIP + q],
                device_id=(x, y, 1 - c), device_id_type=MESH)
            for a in range(n_arr) for q in range(N_CHIP)]
        for cp in copies:
            cp.start()
        for cp in copies:
            cp.wait_recv()
        for cp in copies:
            cp.wait_send()

    return pl.pallas_call(
        body, in_specs=[ANY] * n_arr, out_specs=[ANY] * n_arr,
        out_shape=[jax.ShapeDtypeStruct((N_CHIP,) + g.shape[2:], g.dtype) for g in grads],
        scratch_shapes=[pltpu.SemaphoreType.DMA((N_CHIP * n_arr,)), pltpu.SemaphoreType.DMA((N_CHIP * n_arr,))],
        name=name,
    )(*grads)


def _exchange_chips(partials, name):
    n_arr = len(partials)

    def body(*refs):
        srcs, outs = refs[:n_arr], refs[n_arr:2 * n_arr]
        send_sems, recv_sems, local_sems = refs[2 * n_arr:]
        x, y, c, other_chips = _place()
        my_chip = 2 * x + y
        own = [pltpu.make_async_copy(srcs[a].at[my_chip], outs[a].at[my_chip], local_sems.at[a]) for a in range(n_arr)]
        for cp in own:
            cp.start()
        copies = [
            pltpu.make_async_remote_copy(
                src_ref=srcs[a].at[2 * chip[0] + chip[1]], dst_ref=outs[a].at[my_chip],
                send_sem=send_sems.at[a * 3 + j], recv_sem=recv_sems.at[a * 3 + j],
                device_id=(*chip, c), device_id_type=MESH)
            for a in range(n_arr) for j, chip in enumerate(other_chips)]
        for cp in copies:
            cp.start()
        for cp in copies:
            cp.wait_recv()
        for cp in copies:
            cp.wait_send()
        for cp in own:
            cp.wait()

    return pl.pallas_call(
        body, in_specs=[ANY] * n_arr, out_specs=[ANY] * n_arr,
        out_shape=[jax.ShapeDtypeStruct(p.shape, p.dtype) for p in partials],
        scratch_shapes=[pltpu.SemaphoreType.DMA((3 * n_arr,)), pltpu.SemaphoreType.DMA((3 * n_arr,)),
                        pltpu.SemaphoreType.DMA((n_arr,))],
        name=name,
    )(*partials)


def kernel(x, ln_pre_0, conv_w_in_0, conv_w_0, conv_w_out_0, ln_post_0, ln_pre_1, sb_w_in_1, sb_w_out_1, ln_post_1, ln_pre_2, conv_w_in_2, conv_w_2, conv_w_out_2, ln_post_2, ln_pre_3, sb_w_in_3, sb_w_out_3, ln_post_3, loss_target, m_ln_pre_0, m_conv_w_in_0, m_conv_w_0, m_conv_w_out_0, m_ln_post_0, m_ln_pre_1, m_sb_w_in_1, m_sb_w_out_1, m_ln_post_1, m_ln_pre_2, m_conv_w_in_2, m_conv_w_2, m_conv_w_out_2, m_ln_post_2, m_ln_pre_3, m_sb_w_in_3, m_sb_w_out_3, m_ln_post_3, v_ln_pre_0, v_conv_w_in_0, v_conv_w_0, v_conv_w_out_0, v_ln_post_0, v_ln_pre_1, v_sb_w_in_1, v_sb_w_out_1, v_ln_post_1, v_ln_pre_2, v_conv_w_in_2, v_conv_w_2, v_conv_w_out_2, v_ln_post_2, v_ln_pre_3, v_sb_w_in_3, v_sb_w_out_3, v_ln_post_3):
    names = ['ln_pre_0', 'conv_w_in_0', 'conv_w_0', 'conv_w_out_0', 'ln_post_0', 'ln_pre_1', 'sb_w_in_1', 'sb_w_out_1',
             'ln_post_1', 'ln_pre_2', 'conv_w_in_2', 'conv_w_2', 'conv_w_out_2', 'ln_post_2', 'ln_pre_3', 'sb_w_in_3',
             'sb_w_out_3', 'ln_post_3']
    given = dict(locals())
    w = {n: given[n] for n in names}
    mom = {n: given["m_" + n] for n in names}
    var = {n: given["v_" + n] for n in names}
    conv_layers = [i for i in range(DEPTH) if i % 2 == 0]
    w_in_names = [("conv_w_in_%d" if i % 2 == 0 else "sb_w_in_%d") % i for i in range(DEPTH)]
    w_out_names = [("conv_w_out_%d" if i % 2 == 0 else "sb_w_out_%d") % i for i in range(DEPTH)]

    s, d = x.shape[1:]
    h = x.reshape(s, d)
    target = loss_target.reshape(s, d)
    gains = {n: w[n].reshape(1, d) for n in names if n.startswith("ln_")}
    place = 4 * lax.axis_index("x") + 2 * lax.axis_index("y") + lax.axis_index("c")
    core = lax.axis_index("c").astype(jnp.int32).reshape(1)

    w_in_all, w_out_all, conv_all = _all_gather(
        [_stack_cast([w[n] for n in w_in_names], "cast_w_in"), _stack_cast([w[n] for n in w_out_names], "cast_w_out"),
         jnp.stack([w["conv_w_%d" % i] for i in conv_layers])], "gather_weights")
    bdim = w_out_all.shape[1] * w_out_all.shape[2]
    w_out_all = w_out_all.reshape(DEPTH, bdim, d)
    conv_all = jnp.swapaxes(conv_all, 1, 2).reshape(len(conv_layers), CONV_K, bdim)
    conv_full = {i: conv_all[n] for n, i in enumerate(conv_layers)}

    saved = []
    for i in range(DEPTH):
        u, u_t = _rmsnorm_fwd(h, gains["ln_pre_%d" % i], "pre_norm_%d" % i)
        if i % 2 == 0:
            proj = _proj(u, w_in_all, i, 0, N_DEV, F32, "proj_%d" % i)
            a, a_t = _conv_gate_fwd(proj, conv_full[i], "conv_gate_%d" % i)
            extra = (proj,)
        else:
            qkv = _proj(u, w_in_all, i, 0, 6, BF16, "proj_qkv_%d" % i)
            z = _proj(u, w_in_all, i, 6, 2, F32, "proj_z_%d" % i)
            o, carries = _sb_attn_fwd(qkv, "sb_attn_%d" % i)
            a, a_t = _sb_gate_fwd(z, o, "sb_gate_%d" % i)
            extra = (qkv, z, o, carries)
        m = _out_proj(a, w_out_all, i, "out_proj_%d" % i)
        saved.append((h, u_t, a_t, m, extra))
        h = _post_norm_residual(h, m, gains["ln_post_%d" % i], "post_norm_%d" % i)

    dh, loss = _loss_head(h, target, "loss_head")
    loss = lax.psum(loss[0, 0], ("x", "y", "c"))

    small = {}
    big = {}
    for i in reversed(range(DEPTH)):
        h_in, u_t, a_t, m, extra = saved[i]
        dm, small["ln_post_%d" % i] = _post_norm_bwd(dh, m, gains["ln_post_%d" % i], "post_norm_bwd_%d" % i)
        big[w_out_names[i]] = _weight_grad(a_t, dm, 1, "grad_w_out_%d" % i).reshape(N_CHIP, 2, bdim // N_DEV, d)
        da = _out_proj_bwd_act(dm, w_out_all, i, "out_proj_bwd_%d" % i)
        if i % 2 == 0:
            (proj,) = extra
            db, dc, dxt, dz, small["conv_w_%d" % i] = _conv_gate_bwd(proj, da, conv_full[i], "conv_gate_bwd_%d" % i)
            dproj = jnp.concatenate([db, dc, dxt, dz], axis=1)
        else:
            qkv, z, o, carries = extra
            do, dz = _sb_gate_bwd(da, z, o, "sb_gate_bwd_%d" % i)
            dq, dk, dv = _sb_attn_bwd(qkv, do, carries, "sb_attn_bwd_%d" % i)
            dproj = jnp.concatenate([dq, dk, dv, dz], axis=1)
        g_in = _weight_grad(u_t, dproj, N_DEV, "grad_w_in_%d" % i)
        big[w_in_names[i]] = g_in.reshape((N_CHIP, 2) + g_in.shape[1:])
        du = _proj_bwd_act(dproj, w_in_all, i, "proj_bwd_%d" % i)
        dh, small["ln_pre_%d" % i] = _pre_norm_bwd(du, h_in, gains["ln_pre_%d" % i], dh, "pre_norm_bwd_%d" % i)

    big_names = w_in_names + w_out_names
    from_sibling = _exchange_core_pair([big[n] for n in big_names], "reduce_core_pair")
    pair_sums = [_add_core_pair(big[n], r, core, "add_core_pair_" + n) for n, r in zip(big_names, from_sibling)]
    chip_parts = dict(zip(big_names, _exchange_chips(pair_sums, "reduce_chips")))

    gain_names = [n for n in names if n.startswith("ln_")]
    conv_names = ["conv_w_%d" % i for i in conv_layers]
    rows = [small[n] for n in gain_names] + [small[n] for n in conv_names]
    n_rows = len(gain_names) + CONV_K * len(conv_names)
    pad = -n_rows % SUBLANES
    stacked = jnp.concatenate(rows + [jnp.zeros((pad, d), F32)], axis=0)
    (small_all,) = _all_gather([stacked[None]], "gather_small_grads")
    small_all = small_all[0]

    out_g, out_d, out_m, out_v = {}, {}, {}, {}

    def update(n, w2, parts, m2, v2, shape):
        g2, d2, nm2, nv2 = _adamw(w2, parts, m2, v2, "adamw_" + n)
        out_g[n], out_d[n], out_m[n], out_v[n] = (t.reshape(shape) for t in (g2, d2, nm2, nv2))

    for n in big_names:
        update(n, w[n], chip_parts[n], mom[n], var[n], w[n].shape)
    n_gain = len(gain_names)
    stack = lambda src: jnp.stack([src[n] for n in gain_names])
    g2, d2, nm2, nv2 = _adamw(stack(w), small_all[:, :n_gain], stack(mom), stack(var), "adamw_gains")
    for k, n in enumerate(gain_names):
        out_g[n], out_d[n], out_m[n], out_v[n] = g2[k], d2[k], nm2[k], nv2[k]
    wc = bdim // N_DEV
    for k, n in enumerate(conv_names):
        rows_k = small_all[:, n_gain + CONV_K * k:n_gain + CONV_K * (k + 1)]
        parts = lax.dynamic_slice_in_dim(rows_k, place * wc, wc, axis=2)
        update(n, w[n], parts, mom[n], var[n], w[n].shape)

    grad_x = dh.reshape(x.shape)
    return (loss, grad_x, *[out_g[n] for n in names], *[out_d[n] for n in names],
            *[out_m[n] for n in names], *[out_v[n] for n in names])
```

```python
import functools
import math

import jax
import jax.numpy as jnp
from jax import lax
from jax.experimental import pallas as pl
from jax.experimental.pallas import tpu as pltpu

F32 = jnp.float32
BF16 = jnp.bfloat16
MESH = pl.DeviceIdType.MESH
ANY = pl.BlockSpec(memory_space=pl.ANY)

N_DEV = 8
N_CHIP = 4
DEPTH = 4
HEAD_DIM = 128
CONV_K = 3
RMS_EPS = 1e-6
ADAM_LR = 0.001
ADAM_B1 = 0.9
ADAM_B2 = 0.999
ADAM_EPS = 1e-08
ADAM_WD = 0.01
ADAM_STEP = 10

V7X_VMEM_BYTES = 64 * 1024 * 1024
VMEM_LIMIT = V7X_VMEM_BYTES * 3 // 4
LANES = 128
SUBLANES = 8
HEADS_PER_STEP = 2


def _params(*sem):
    return pltpu.CompilerParams(dimension_semantics=sem, vmem_limit_bytes=VMEM_LIMIT)


def _silu_parts(z):
    sig = jax.nn.sigmoid(z)
    return z * sig, sig


NN = (((1,), (0,)), ((), ()))
NT = (((1,), (1,)), ((), ()))
TN = (((0,), (0,)), ((), ()))


def _mm(a, b, *, dims, grid, a_spec, b_spec, o_spec, out_shape, acc_shape, name):
    nk = grid[2]

    def body(a_ref, b_ref, o_ref, *scratch):
        p = lax.dot_general(a_ref[...], b_ref[...], dims, preferred_element_type=F32)
        if nk == 1:
            o_ref[...] = p.astype(o_ref.dtype)
        else:
            acc_ref = scratch[0]
            k = pl.program_id(2)

            @pl.when(k == 0)
            def _():
                acc_ref[...] = p

            @pl.when(k > 0)
            def _():
                acc_ref[...] += p

            @pl.when(k == nk - 1)
            def _():
                o_ref[...] = acc_ref[...].astype(o_ref.dtype)

    scratch = [] if nk == 1 else [pltpu.VMEM(acc_shape, F32)]
    return pl.pallas_call(
        body, grid=grid, in_specs=[a_spec, b_spec], out_specs=o_spec, out_shape=out_shape,
        scratch_shapes=scratch, compiler_params=_params("parallel", "parallel", "arbitrary"), name=name,
    )(a, b)


def _proj(u, w_in, shard0, n_shard, out_dtype, name):
    s, d = u.shape
    ws = w_in.shape[-1]
    tm, tn = min(s, 512), min(ws, 1024)
    nj = ws // tn
    return _mm(
        u, w_in, dims=NN, grid=(s // tm, n_shard * nj, 1),
        a_spec=pl.BlockSpec((tm, d), lambda i, j, k: (i, 0)),
        b_spec=pl.BlockSpec((None, d, tn), lambda i, j, k: (shard0 + j // nj, 0, j % nj)),
        o_spec=pl.BlockSpec((tm, tn), lambda i, j, k: (i, j)),
        out_shape=jax.ShapeDtypeStruct((s, n_shard * ws), out_dtype), acc_shape=(tm, tn), name=name,
    )


def _out_proj(a, w_out, name):
    s, bdim = a.shape
    d = w_out.shape[-1]
    tm, tn = min(s, 512), min(d, 1024)
    return _mm(
        a, w_out, dims=NN, grid=(s // tm, d // tn, 1),
        a_spec=pl.BlockSpec((tm, bdim), lambda i, j, k: (i, 0)),
        b_spec=pl.BlockSpec((bdim, tn), lambda i, j, k: (0, j)),
        o_spec=pl.BlockSpec((tm, tn), lambda i, j, k: (i, j)),
        out_shape=jax.ShapeDtypeStruct((s, d), F32), acc_shape=(tm, tn), name=name,
    )


def _out_proj_bwd_act(dm, w_out, name):
    s, d = dm.shape
    bdim = w_out.shape[-2]
    tm, tn = min(s, 512), min(bdim, 1024)
    return _mm(
        dm, w_out, dims=NT, grid=(s // tm, bdim // tn, 1),
        a_spec=pl.BlockSpec((tm, d), lambda i, j, k: (i, 0)),
        b_spec=pl.BlockSpec((tn, d), lambda i, j, k: (j, 0)),
        o_spec=pl.BlockSpec((tm, tn), lambda i, j, k: (i, j)),
        out_shape=jax.ShapeDtypeStruct((s, bdim), F32), acc_shape=(tm, tn), name=name,
    )


def _weight_grad(act_t, dout, n_blocks, name):
    din, s = act_t.shape
    w = dout.shape[1] // n_blocks
    tm, tn = min(din, 512), min(w, 1024)
    nj = w // tn
    return _mm(
        act_t, dout, dims=NN, grid=(din // tm, n_blocks * nj, 1),
        a_spec=pl.BlockSpec((tm, s), lambda i, j, k: (i, 0)),
        b_spec=pl.BlockSpec((s, tn), lambda i, j, k: (0, j)),
        o_spec=pl.BlockSpec((None, tm, tn), lambda i, j, k: (j // nj, i, j % nj)),
        out_shape=jax.ShapeDtypeStruct((n_blocks, din, w), BF16), acc_shape=(tm, tn), name=name,
    )


def _proj_bwd_act(dproj, w_in, name):
    s = dproj.shape[0]
    n_shards, d, ws = w_in.shape
    tm, tn = min(s, 512), min(d, 512)

    def body(a_ref, b_ref, o_ref):
        acc = None
        for k in range(n_shards):
            p = lax.dot_general(a_ref[:, k * ws:(k + 1) * ws], b_ref[k], NT, preferred_element_type=F32)
            acc = p if acc is None else acc + p
        o_ref[...] = acc

    return pl.pallas_call(
        body, grid=(s // tm, d // tn),
        in_specs=[pl.BlockSpec((tm, n_shards * ws), lambda i, j: (i, 0)),
                  pl.BlockSpec((n_shards, tn, ws), lambda i, j: (0, j, 0))],
        out_specs=pl.BlockSpec((tm, tn), lambda i, j: (i, j)), out_shape=jax.ShapeDtypeStruct((s, d), F32),
        compiler_params=_params("parallel", "parallel"), name=name,
    )(dproj, w_in)


def _row_spec(tm, d):
    return pl.BlockSpec((tm, d), lambda i: (i, 0))


def _gain_spec(d):
    return pl.BlockSpec((1, d), lambda i: (0, 0))


def _rstd(x):
    return lax.rsqrt(jnp.mean(x * x, axis=-1, keepdims=True) + RMS_EPS)


def _rmsnorm_fwd(h, gain, name):
    s, d = h.shape
    tm = min(s, 512)

    def body(h_ref, g_ref, u_ref, ut_ref):
        x = h_ref[...]
        u = (x * _rstd(x) * g_ref[...]).astype(u_ref.dtype)
        u_ref[...] = u
        ut_ref[...] = u.T

    return pl.pallas_call(
        body, grid=(s // tm,), in_specs=[_row_spec(tm, d), _gain_spec(d)],
        out_specs=[_row_spec(tm, d), pl.BlockSpec((d, tm), lambda i: (0, i))],
        out_shape=[jax.ShapeDtypeStruct((s, d), BF16), jax.ShapeDtypeStruct((d, s), BF16)],
        compiler_params=_params("parallel"), name=name,
    )(h, gain)


def _post_norm_residual(h, m, gain, name):
    s, d = h.shape
    tm = min(s, 512)

    def body(h_ref, m_ref, g_ref, o_ref):
        x = m_ref[...]
        o_ref[...] = h_ref[...] + x * _rstd(x) * g_ref[...]

    return pl.pallas_call(
        body, grid=(s // tm,), in_specs=[_row_spec(tm, d), _row_spec(tm, d), _gain_spec(d)],
        out_specs=_row_spec(tm, d), out_shape=jax.ShapeDtypeStruct((s, d), F32),
        compiler_params=_params("parallel"), name=name,
    )(h, m, gain)


def _sum_rows_into(acc_ref, x):
    tm, d = x.shape
    acc_ref[...] += jnp.sum(x.reshape(tm // SUBLANES, SUBLANES, d), axis=0)


def _norm_bwd_body(n_steps, with_residual, n_after=0):
    def body(*refs):
        n_in = 4 if with_residual else 3
        dy_ref, x_ref, g_ref = refs[:3]
        dres_ref = refs[3] if with_residual else None
        dx_ref, dg_ref, acc_ref = refs[n_in + n_after:]
        i = pl.program_id(0)

        @pl.when(i == 0)
        def _():
            acc_ref[...] = jnp.zeros_like(acc_ref)

        x = x_ref[...]
        dy = dy_ref[...]
        rstd = _rstd(x)
        n = x * rstd
        dn = dy * g_ref[...]
        dx = rstd * (dn - n * jnp.mean(dn * n, axis=-1, keepdims=True))
        if with_residual:
            dx = dres_ref[...] + dx
        dx_ref[...] = dx.astype(dx_ref.dtype)
        _sum_rows_into(acc_ref, dy * n)

        @pl.when(i == n_steps - 1)
        def _():
            dg_ref[...] = jnp.sum(acc_ref[...], axis=0, keepdims=True)

    return body


def _post_norm_bwd(dh, m, gain, after, name):
    s, d = m.shape
    tm = min(s, 512)
    n_steps = s // tm
    return pl.pallas_call(
        _norm_bwd_body(n_steps, False, len(after)), grid=(n_steps,),
        in_specs=[_row_spec(tm, d), _row_spec(tm, d), _gain_spec(d)] + [ANY] * len(after),
        out_specs=[_row_spec(tm, d), _gain_spec(d)],
        out_shape=[jax.ShapeDtypeStruct((s, d), BF16), jax.ShapeDtypeStruct((1, d), F32)],
        scratch_shapes=[pltpu.VMEM((SUBLANES, d), F32)], compiler_params=_params("arbitrary"), name=name,
    )(dh, m, gain, *after)


def _pre_norm_bwd(du, h, gain, dh, after, name):
    s, d = h.shape
    tm = min(s, 512)
    n_steps = s // tm
    return pl.pallas_call(
        _norm_bwd_body(n_steps, True, len(after)), grid=(n_steps,),
        in_specs=[_row_spec(tm, d), _row_spec(tm, d), _gain_spec(d), _row_spec(tm, d)] + [ANY] * len(after),
        out_specs=[_row_spec(tm, d), _gain_spec(d)],
        out_shape=[jax.ShapeDtypeStruct((s, d), F32), jax.ShapeDtypeStruct((1, d), F32)],
        scratch_shapes=[pltpu.VMEM((SUBLANES, d), F32)], compiler_params=_params("arbitrary"), name=name,
    )(du, h, gain, dh, *after)


def _loss_head(y, target, name):
    s, d = y.shape
    tm = min(s, 512)
    n_steps = s // tm

    def body(y_ref, t_ref, dy_ref, loss_ref, acc_ref):
        i = pl.program_id(0)

        @pl.when(i == 0)
        def _():
            acc_ref[...] = jnp.zeros_like(acc_ref)

        err = y_ref[...] - t_ref[...]
        dy_ref[...] = err / d
        _sum_rows_into(acc_ref, err * err)

        @pl.when(i == n_steps - 1)
        def _():
            total = jnp.sum(jnp.sum(acc_ref[...], axis=0, keepdims=True), axis=1, keepdims=True)
            loss_ref[...] = 0.5 * total / d

    return pl.pallas_call(
        body, grid=(n_steps,), in_specs=[_row_spec(tm, d), _row_spec(tm, d)],
        out_specs=[_row_spec(tm, d), pl.BlockSpec((1, 1), lambda i: (0, 0))],
        out_shape=[jax.ShapeDtypeStruct((s, d), F32), jax.ShapeDtypeStruct((1, 1), F32)],
        scratch_shapes=[pltpu.VMEM((SUBLANES, d), F32)], compiler_params=_params("arbitrary"), name=name,
    )(y, target)


def _shift_down(p, halo, row, n):
    out = jnp.where(row == 0, halo[SUBLANES - n:SUBLANES - n + 1], pltpu.roll(p, n, 0))
    if n == 2:
        out = jnp.where(row == 1, halo[SUBLANES - 1:SUBLANES], out)
    return out


def _shift_up(p, halo, row, n):
    tm = p.shape[0]
    out = jnp.where(row == tm - 1, halo[n - 1:n], pltpu.roll(p, tm - n, 0))
    if n == 2:
        out = jnp.where(row == tm - 2, halo[0:1], out)
    return out


def _conv_specs(tm, tc, nb, n_row_blocks):
    hb = tm // SUBLANES
    cur = lambda part: pl.BlockSpec((tm, tc), lambda i, j: (i, part * nb + j))
    prev = lambda part: pl.BlockSpec((SUBLANES, tc), lambda i, j: (jnp.maximum(i * hb - 1, 0), part * nb + j))
    nxt = lambda part: pl.BlockSpec(
        (SUBLANES, tc), lambda i, j: (jnp.minimum((i + 1) * hb, n_row_blocks * hb - 1), part * nb + j))
    return cur, prev, nxt


def _conv_gate_fwd(proj, conv_w, name):
    s, b4 = proj.shape
    bdim = b4 // 4
    tm, tc = min(s, 512), min(bdim, 512)
    nb = bdim // tc
    cur, prev, _ = _conv_specs(tm, tc, nb, s // tm)

    def body(b_ref, c_ref, x_ref, z_ref, cp_ref, xp_ref, w_ref, a_ref, at_ref):
        i = pl.program_id(0)
        row = lax.broadcasted_iota(jnp.int32, (tm, tc), 0)
        p = c_ref[...] * x_ref[...]
        halo = jnp.where(i > 0, cp_ref[...] * xp_ref[...], 0.0)
        w = w_ref[...]
        cv = w[0:1] * _shift_down(p, halo, row, 2) + w[1:2] * _shift_down(p, halo, row, 1) + w[2:3] * p
        silu, _ = _silu_parts(z_ref[...])
        a = (silu * (b_ref[...] * cv)).astype(a_ref.dtype)
        a_ref[...] = a
        at_ref[...] = a.T

    return pl.pallas_call(
        body, grid=(s // tm, nb),
        in_specs=[cur(0), cur(1), cur(2), cur(3), prev(1), prev(2), pl.BlockSpec((CONV_K, tc), lambda i, j: (0, j))],
        out_specs=[pl.BlockSpec((tm, tc), lambda i, j: (i, j)), pl.BlockSpec((tc, tm), lambda i, j: (j, i))],
        out_shape=[jax.ShapeDtypeStruct((s, bdim), BF16), jax.ShapeDtypeStruct((bdim, s), BF16)],
        compiler_params=_params("parallel", "parallel"), name=name,
    )(proj, proj, proj, proj, proj, proj, conv_w)


def _conv_gate_bwd(proj, da, conv_w, name):
    s, b4 = proj.shape
    bdim = b4 // 4
    tm, tc = min(s, 512), min(bdim, 512)
    nb = bdim // tc
    n_rows = s // tm
    cur, prev, nxt = _conv_specs(tm, tc, nb, n_rows)
    da_cur = pl.BlockSpec((tm, tc), lambda j, i: (i, j))
    hb = tm // SUBLANES
    da_nxt = pl.BlockSpec((SUBLANES, tc), lambda j, i: (jnp.minimum((i + 1) * hb, n_rows * hb - 1), j))
    swap = lambda spec: pl.BlockSpec(spec.block_shape, lambda j, i, f=spec.index_map: f(i, j))

    def body(b_ref, c_ref, x_ref, z_ref, cp_ref, xp_ref, bn_ref, zn_ref, da_ref, dan_ref, w_ref,
             db_ref, dc_ref, dx_ref, dz_ref, dw_ref, acc_ref):
        i = pl.program_id(1)

        @pl.when(i == 0)
        def _():
            acc_ref[...] = jnp.zeros_like(acc_ref)

        row = lax.broadcasted_iota(jnp.int32, (tm, tc), 0)
        w = w_ref[...]
        b, c, x = b_ref[...], c_ref[...], x_ref[...]
        p = c * x
        halo_p = jnp.where(i > 0, cp_ref[...] * xp_ref[...], 0.0)
        p1, p2 = _shift_down(p, halo_p, row, 1), _shift_down(p, halo_p, row, 2)
        cv = w[0:1] * p2 + w[1:2] * p1 + w[2:3] * p
        z = z_ref[...]
        silu, sig = _silu_parts(z)
        da = da_ref[...]
        dy = da * silu
        dcv = dy * b
        silu_n, _ = _silu_parts(zn_ref[...])
        halo_d = jnp.where(i < n_rows - 1, dan_ref[...] * silu_n * bn_ref[...], 0.0)
        dp = w[2:3] * dcv + w[1:2] * _shift_up(dcv, halo_d, row, 1) + w[0:1] * _shift_up(dcv, halo_d, row, 2)
        db_ref[...] = (dy * cv).astype(db_ref.dtype)
        dc_ref[...] = (dp * x).astype(dc_ref.dtype)
        dx_ref[...] = (dp * c).astype(dx_ref.dtype)
        dz_ref[...] = (da * (b * cv) * (sig * (1.0 + z * (1.0 - sig)))).astype(dz_ref.dtype)
        for k, pk in enumerate((p2, p1, p)):
            _sum_rows_into(acc_ref.at[k], dcv * pk)

        @pl.when(i == n_rows - 1)
        def _():
            for k in range(CONV_K):
                dw_ref[k:k + 1, :] = jnp.sum(acc_ref[k], axis=0, keepdims=True)

    out = pl.BlockSpec((tm, tc), lambda j, i: (i, j))
    act = jax.ShapeDtypeStruct((s, bdim), BF16)
    return pl.pallas_call(
        body, grid=(nb, n_rows),
        in_specs=[swap(cur(0)), swap(cur(1)), swap(cur(2)), swap(cur(3)), swap(prev(1)), swap(prev(2)),
                  swap(nxt(0)), swap(nxt(3)), da_cur, da_nxt, pl.BlockSpec((CONV_K, tc), lambda j, i: (0, j))],
        out_specs=[out, out, out, out, pl.BlockSpec((CONV_K, tc), lambda j, i: (0, j))],
        out_shape=[act, act, act, act, jax.ShapeDtypeStruct((CONV_K, bdim), F32)],
        scratch_shapes=[pltpu.VMEM((CONV_K, SUBLANES, tc), F32)],
        compiler_params=_params("parallel", "arbitrary"), name=name,
    )(proj, proj, proj, proj, proj, proj, proj, proj, da, da, conv_w)


def _split(x):
    hi = x.astype(BF16)
    lo = (x - hi.astype(F32)).astype(BF16)
    return jnp.concatenate([hi, lo], axis=1)


def _sb_tiles(qs, ks, carries, suffix_ones, mask):
    heads = range(len(qs))
    scale = 1.0 / math.sqrt(HEAD_DIM)
    logits = [lax.dot_general(qs[g], ks[g], NT, preferred_element_type=F32) * scale for g in heads]
    es = [jnp.exp(-jnp.abs(logits[g])) for g in heads]
    keeps = []
    for g in heads:
        log_keep = -(jnp.maximum(logits[g], 0.0) + jnp.log(1.0 + es[g]))
        if mask is not None:
            log_keep = jnp.where(mask, log_keep, 0.0)
        keeps.append(_split(log_keep))
    tails = [lax.dot_general(keeps[g], suffix_ones, NN, preferred_element_type=F32) for g in heads]
    ws = []
    for g in heads:
        w = jnp.exp(logits[g] + tails[g] + carries[g])
        if mask is not None:
            w = jnp.where(mask, w, 0.0)
        ws.append(w)
    return logits, es, tails, ws


def _tri_twice(n, upper):
    r = lax.broadcasted_iota(jnp.int32, (2 * n, n), 0)
    r = jnp.where(r >= n, r - n, r)
    c = lax.broadcasted_iota(jnp.int32, (2 * n, n), 1)
    return jnp.where(r <= c if upper else r >= c, 1.0, 0.0).astype(BF16)


def _group_spec(s, width, part, n_groups):
    return pl.BlockSpec((s, width), lambda h: (0, part * n_groups + h))


def _head_cols(g):
    return slice(g * HEAD_DIM, (g + 1) * HEAD_DIM)


def _lane_tile(x, n):
    return x if n == LANES else jnp.concatenate([x] * (n // LANES), axis=1)


def _sb_attn_fwd(qkv, name):
    s, b3 = qkv.shape
    bdim = b3 // 3
    hps = min(HEADS_PER_STEP, bdim // HEAD_DIM)
    width = hps * HEAD_DIM
    n_groups = bdim // width
    blk = min(s, 256)
    n_blk = s // blk

    def body(q_ref, k_ref, v_ref, o_ref, car_ref, carry_ref):
        suffix_ones = _tri_twice(blk, upper=False)
        r = lax.broadcasted_iota(jnp.int32, (blk, blk), 0)
        c = lax.broadcasted_iota(jnp.int32, (blk, blk), 1)
        diag_mask = c < r
        lane = lax.broadcasted_iota(jnp.int32, (blk, LANES), 1)

        def q_block(qi, _):
            q0 = pl.multiple_of(qi * blk, blk)
            rows = pl.ds(q0, blk)
            qs = [q_ref[rows, _head_cols(g)] for g in range(hps)]
            o_ref[rows, :] = jnp.zeros((blk, width), F32)
            car_ref[rows, :] = jnp.zeros((blk, width), F32)
            carry_ref[...] = jnp.zeros_like(carry_ref)

            def step(j, mask):
                k0 = pl.multiple_of(j * blk, blk)
                ks = [k_ref[pl.ds(k0, blk), _head_cols(g)] for g in range(hps)]
                carries = [carry_ref[g] for g in range(hps)]
                _, _, tails, ws = _sb_tiles(qs, ks, [_lane_tile(cy, blk) for cy in carries], suffix_ones, mask)
                for g in range(hps):
                    v = v_ref[pl.ds(k0, blk), _head_cols(g)]
                    o_ref[rows, _head_cols(g)] += lax.dot_general(ws[g].astype(BF16), v, NN, preferred_element_type=F32)
                    car_ref[rows, _head_cols(g)] = jnp.where(lane == j, carries[g], car_ref[rows, _head_cols(g)])
                    carry_ref[g] = carries[g] + jnp.broadcast_to(tails[g][:, 0:1], (blk, LANES))

            step(qi, diag_mask)

            def off_diagonal(jj, _):
                step(qi - 1 - jj, None)
                return 0

            lax.fori_loop(0, qi, off_diagonal, 0)
            return 0

        lax.fori_loop(0, n_blk, q_block, 0)

    out = pl.BlockSpec((s, width), lambda h: (0, h))
    shape = jax.ShapeDtypeStruct((s, bdim), F32)
    return pl.pallas_call(
        body, grid=(n_groups,),
        in_specs=[_group_spec(s, width, part, n_groups) for part in range(3)],
        out_specs=[out, out], out_shape=[shape, shape], scratch_shapes=[pltpu.VMEM((hps, blk, LANES), F32)],
        compiler_params=_params("parallel"), name=name,
    )(qkv, qkv, qkv)


def _sb_attn_bwd(qkv, do, carries, name):
    s, b3 = qkv.shape
    bdim = b3 // 3
    hps = min(HEADS_PER_STEP, bdim // HEAD_DIM)
    width = hps * HEAD_DIM
    n_groups = bdim // width
    blk = min(s, 256)
    n_blk = s // blk
    scale = 1.0 / math.sqrt(HEAD_DIM)

    def body(q_ref, k_ref, v_ref, do_ref, car_ref, dq_ref, dk_ref, dv_ref, dk_acc, dv_acc, dq_acc, before_ref):
        suffix_ones = _tri_twice(blk, upper=False)
        prefix_ones = _tri_twice(blk, upper=True)
        r = lax.broadcasted_iota(jnp.int32, (blk, blk), 0)
        c = lax.broadcasted_iota(jnp.int32, (blk, blk), 1)
        diag_mask = c < r
        lane = lax.broadcasted_iota(jnp.int32, (blk, LANES), 1)
        dk_acc[...] = jnp.zeros_like(dk_acc)
        dv_acc[...] = jnp.zeros_like(dv_acc)

        def q_block(qi, _):
            q0 = pl.multiple_of(qi * blk, blk)
            rows = pl.ds(q0, blk)
            qs = [q_ref[rows, _head_cols(g)] for g in range(hps)]
            dos = [do_ref[rows, _head_cols(g)] for g in range(hps)]
            dq_acc[...] = jnp.zeros_like(dq_acc)
            before_ref[...] = jnp.zeros_like(before_ref)

            def step(j, mask):
                k0 = pl.multiple_of(j * blk, blk)
                heads = range(hps)
                ks = [k_ref[pl.ds(k0, blk), _head_cols(g)] for g in heads]
                dws = [lax.dot_general(dos[g], v_ref[pl.ds(k0, blk), _head_cols(g)], NT, preferred_element_type=F32)
                       for g in heads]
                carries = [jnp.sum(jnp.where(lane == j, car_ref[rows, _head_cols(g)], 0.0), axis=1, keepdims=True)
                           for g in heads]
                logits, es, _, ws = _sb_tiles(qs, ks, carries, suffix_ones, mask)
                gws = [dws[g] * ws[g] for g in heads]
                g_upto = [lax.dot_general(_split(gws[g]), prefix_ones, NN, preferred_element_type=F32) for g in heads]
                dss = []
                for g in heads:
                    sig = jnp.where(logits[g] >= 0.0, 1.0, es[g]) / (1.0 + es[g])
                    dlogits = gws[g] - sig * (_lane_tile(before_ref[g], blk) + g_upto[g])
                    if mask is not None:
                        dlogits = jnp.where(mask, dlogits, 0.0)
                    dss.append((dlogits * scale).astype(BF16))
                for g in heads:
                    dq_acc[:, _head_cols(g)] += lax.dot_general(dss[g], ks[g], NN, preferred_element_type=F32)
                    dk_acc[pl.ds(k0, blk), _head_cols(g)] += lax.dot_general(
                        dss[g], qs[g], TN, preferred_element_type=F32)
                    dv_acc[pl.ds(k0, blk), _head_cols(g)] += lax.dot_general(
                        ws[g].astype(BF16), dos[g], TN, preferred_element_type=F32)
                    before_ref[g] += jnp.broadcast_to(g_upto[g][:, blk - 1:blk], (blk, LANES))

            def off_diagonal(j, _):
                step(j, None)
                return 0

            lax.fori_loop(0, qi, off_diagonal, 0)
            step(qi, diag_mask)
            dq_ref[rows, :] = dq_acc[...].astype(dq_ref.dtype)
            return 0

        lax.fori_loop(0, n_blk, q_block, 0)
        dk_ref[...] = dk_acc[...].astype(dk_ref.dtype)
        dv_ref[...] = dv_acc[...].astype(dv_ref.dtype)

    group = pl.BlockSpec((s, width), lambda h: (0, h))
    once = pl.BlockSpec((s, width), lambda h: (0, h), pipeline_mode=pl.Buffered(1))
    shape = jax.ShapeDtypeStruct((s, bdim), BF16)
    return pl.pallas_call(
        body, grid=(n_groups,),
        in_specs=[_group_spec(s, width, part, n_groups) for part in range(3)] + [once, once],
        out_specs=[group, group, group], out_shape=[shape, shape, shape],
        scratch_shapes=[pltpu.VMEM((s, width), F32), pltpu.VMEM((s, width), F32), pltpu.VMEM((blk, width), F32),
                        pltpu.VMEM((hps, blk, LANES), F32)],
        compiler_params=_params("parallel"), name=name,
    )(qkv, qkv, qkv, do, carries)


def _sb_gate_fwd(z, o, name):
    s, bdim = z.shape
    tm = min(s, 512)

    def body(z_ref, o_ref, a_ref, at_ref):
        silu, _ = _silu_parts(z_ref[...])
        a = (silu * o_ref[...]).astype(a_ref.dtype)
        a_ref[...] = a
        at_ref[...] = a.T

    return pl.pallas_call(
        body, grid=(s // tm,), in_specs=[_row_spec(tm, bdim), _row_spec(tm, bdim)],
        out_specs=[_row_spec(tm, bdim), pl.BlockSpec((bdim, tm), lambda i: (0, i))],
        out_shape=[jax.ShapeDtypeStruct((s, bdim), BF16), jax.ShapeDtypeStruct((bdim, s), BF16)],
        compiler_params=_params("parallel"), name=name,
    )(z, o)


def _sb_gate_bwd(da, z, o, name):
    s, bdim = z.shape
    tm = min(s, 512)

    def body(da_ref, z_ref, o_ref, do_ref, dz_ref):
        z = z_ref[...]
        da = da_ref[...]
        silu, sig = _silu_parts(z)
        do_ref[...] = (da * silu).astype(do_ref.dtype)
        dz_ref[...] = (da * o_ref[...] * (sig * (1.0 + z * (1.0 - sig)))).astype(dz_ref.dtype)

    spec = _row_spec(tm, bdim)
    shape = jax.ShapeDtypeStruct((s, bdim), BF16)
    return pl.pallas_call(
        body, grid=(s // tm,), in_specs=[spec, spec, spec], out_specs=[spec, spec], out_shape=[shape, shape],
        compiler_params=_params("parallel"), name=name,
    )(da, z, o)


def _into_slot(block, place, dtype, name):
    r, c = block.shape
    tr = min(r, 256)

    def body(place_ref, b_ref, o_ref):
        o_ref[...] = b_ref[...].astype(o_ref.dtype)

    grid_spec = pltpu.PrefetchScalarGridSpec(
        num_scalar_prefetch=1, grid=(r // tr,),
        in_specs=[pl.BlockSpec((tr, c), lambda i, place_ref: (i, 0))],
        out_specs=pl.BlockSpec((None, tr, c), lambda i, place_ref: (place_ref[0], i, 0)),
    )
    return pl.pallas_call(
        body, grid_spec=grid_spec, out_shape=jax.ShapeDtypeStruct((N_DEV, r, c), dtype),
        compiler_params=_params("parallel"), name=name,
    )(place, block)


def _adamw_step(g, w, m, v, g_ref, d_ref, nm_ref, nv_ref):
    new_m = ADAM_B1 * m + (1.0 - ADAM_B1) * g
    new_v = ADAM_B2 * v + (1.0 - ADAM_B2) * (g * g)
    m_hat = new_m / (1.0 - ADAM_B1 ** ADAM_STEP)
    v_hat = new_v / (1.0 - ADAM_B2 ** ADAM_STEP)
    g_ref[...] = g
    d_ref[...] = -ADAM_LR * (m_hat / (jnp.sqrt(v_hat) + ADAM_EPS) + ADAM_WD * w)
    nm_ref[...] = new_m
    nv_ref[...] = new_v


def _adamw(w, parts, m, v, name):
    r, c = w.shape
    n_parts = parts.shape[0]
    tr = min(r, 256)

    def body(w_ref, p_ref, m_ref, v_ref, *out_refs):
        g = p_ref[0].astype(F32)
        for k in range(1, n_parts):
            g = g + p_ref[k].astype(F32)
        _adamw_step(g, w_ref[...], m_ref[...], v_ref[...], *out_refs)

    spec = pl.BlockSpec((tr, c), lambda i: (i, 0))
    shape = jax.ShapeDtypeStruct((r, c), F32)
    return pl.pallas_call(
        body, grid=(r // tr,), in_specs=[spec, pl.BlockSpec((n_parts, tr, c), lambda i: (0, i, 0)), spec, spec],
        out_specs=[spec] * 4, out_shape=[shape] * 4, compiler_params=_params("parallel"), name=name,
    )(w, parts, m, v)


def _adamw_shard(w, grads, landed, m, v, place, name):
    r, c = w.shape
    n_landed = landed.shape[0]
    tr = min(r, 256)

    def body(place_ref, w_ref, own_ref, l_ref, m_ref, v_ref, *out_refs):
        g = own_ref[...].astype(F32)
        for k in range(n_landed):
            g = g + l_ref[k].astype(F32)
        _adamw_step(g, w_ref[...], m_ref[...], v_ref[...], *out_refs)

    spec = pl.BlockSpec((tr, c), lambda i, place_ref: (i, 0))
    grid_spec = pltpu.PrefetchScalarGridSpec(
        num_scalar_prefetch=1, grid=(r // tr,),
        in_specs=[spec, pl.BlockSpec((None, tr, c), lambda i, place_ref: (place_ref[0], i, 0)),
                  pl.BlockSpec((n_landed, tr, c), lambda i, place_ref: (0, i, 0)), spec, spec],
        out_specs=[spec] * 4,
    )
    return pl.pallas_call(
        body, grid_spec=grid_spec, out_shape=[jax.ShapeDtypeStruct((r, c), F32)] * 4,
        compiler_params=_params("parallel"), name=name,
    )(place, w, grads, landed, m, v)


def _place():
    x, y, c = lax.axis_index("x"), lax.axis_index("y"), lax.axis_index("c")
    other_chips = [(1 - x, y), (x, 1 - y), (1 - x, 1 - y)]
    return x, y, c, other_chips


def _all_gather(blocks, name):
    n_arr = len(blocks)
    items = [(a, i) for a, blk in enumerate(blocks) for i in range(blk.shape[0])]
    n_items = len(items)

    def body(*refs):
        srcs, outs = refs[:n_arr], refs[n_arr:2 * n_arr]
        send_sems, recv_sems, local_sems = refs[2 * n_arr:]
        x, y, c, other_chips = _place()
        me, sibling = (x, y, c), (x, y, 1 - c)

        def slot(it, dev):
            a, i = items[it]
            return outs[a].at[i, 4 * dev[0] + 2 * dev[1] + dev[2]]

        def copy(it, k, block_of, to, from_src=False):
            a, i = items[it]
            return pltpu.make_async_remote_copy(
                src_ref=srcs[a].at[i] if from_src else slot(it, block_of), dst_ref=slot(it, block_of),
                send_sem=send_sems.at[it * 7 + k], recv_sem=recv_sems.at[it * 7 + k],
                device_id=to, device_id_type=MESH)

        own = [pltpu.make_async_copy(srcs[items[it][0]].at[items[it][1]], slot(it, me), local_sems.at[it])
               for it in range(n_items)]
        for cp in own:
            cp.start()
        first = []
        for it in range(n_items):
            first.append(copy(it, 0, me, sibling, from_src=True))
            first += [copy(it, 1 + j, me, (*chip, c), from_src=True) for j, chip in enumerate(other_chips)]
        for cp in first:
            cp.start()
        passed = []
        for it in range(n_items):
            for j, chip in enumerate(other_chips):
                copy(it, 1 + j, (*chip, c), me).wait_recv()
                passed.append(copy(it, 4 + j, (*chip, c), sibling))
                passed[-1].start()
        for it in range(n_items):
            copy(it, 0, sibling, me).wait_recv()
            for j, chip in enumerate(other_chips):
                copy(it, 4 + j, (*chip, 1 - c), me).wait_recv()
        for cp in first + passed:
            cp.wait_send()
        for cp in own:
            cp.wait()

    return pl.pallas_call(
        body, in_specs=[ANY] * n_arr, out_specs=[ANY] * n_arr,
        out_shape=[jax.ShapeDtypeStruct((b.shape[0], N_DEV) + b.shape[1:], b.dtype) for b in blocks],
        scratch_shapes=[pltpu.SemaphoreType.DMA((7 * n_items,)), pltpu.SemaphoreType.DMA((7 * n_items,)),
                        pltpu.SemaphoreType.DMA((n_items,))],
        name=name,
    )(*blocks)


HBM = pl.BlockSpec(memory_space=pltpu.HBM)
SEM = pl.BlockSpec(memory_space=pltpu.SEMAPHORE)
DATAFLOW = pltpu.SideEffectType.DATAFLOW_SIDE_EFFECTING
N_PEERS = N_DEV - 1
FLIPS = [(dx, dy, dc) for dx in (0, 1) for dy in (0, 1) for dc in (0, 1) if (dx, dy, dc) != (0, 0, 0)]


def _peers():
    x, y, c = lax.axis_index("x"), lax.axis_index("y"), lax.axis_index("c")
    flip = lambda v, d: 1 - v if d else v
    return 4 * x + 2 * y + c, [(flip(x, dx), flip(y, dy), flip(c, dc)) for dx, dy, dc in FLIPS]


def _lin(p):
    return 4 * p[0] + 2 * p[1] + p[2]


def _in_hbm(a):
    return pltpu.with_memory_space_constraint(a, pltpu.HBM)


def _token_spec():
    return pl.BlockSpec(memory_space=pltpu.VMEM), jax.ShapeDtypeStruct((SUBLANES, LANES), F32)


def _gather_copy(land_ref, send_sems, recv_sems, k, me, peer, landed_from):
    return pltpu.make_async_remote_copy(
        src_ref=land_ref.at[me], dst_ref=land_ref.at[me if landed_from is None else landed_from],
        send_sem=send_sems.at[k], recv_sem=recv_sems.at[k], device_id=peer, device_id_type=MESH)


def _gather_start(groups, name):
    flat = [a for g in groups for a in g]
    n, ng = len(flat), len(groups)
    token_spec, token_shape = _token_spec()

    def body(*refs):
        land, sems, token_ref = refs[:n], refs[n:n + 2 * ng], refs[2 * n + 2 * ng]
        me, peers = _peers()
        a = 0
        for gi, group in enumerate(groups):
            for i in range(len(group)):
                for r, peer in enumerate(peers):
                    _gather_copy(land[a], sems[2 * gi], sems[2 * gi + 1], i * N_PEERS + r, me, peer, None).start()
                a += 1
        token_ref[...] = jnp.zeros_like(token_ref)

    sem_shapes = [pltpu.SemaphoreType.DMA((N_PEERS * len(g),)) for g in groups for _ in (0, 1)]
    out = pl.pallas_call(
        body, name=name, in_specs=[HBM] * n, out_specs=[SEM] * (2 * ng) + [HBM] * n + [token_spec],
        out_shape=sem_shapes + [pltpu.HBM(a.shape, a.dtype) for a in flat] + [token_shape],
        input_output_aliases={a: 2 * ng + a for a in range(n)},
        compiler_params=pltpu.CompilerParams(has_side_effects=DATAFLOW),
    )(*[_in_hbm(a) for a in flat])
    sems = [(out[2 * gi], out[2 * gi + 1]) for gi in range(ng)]
    thru, a = [], 2 * ng
    for g in groups:
        thru.append(list(out[a:a + len(g)]))
        a += len(g)
    return sems, thru, out[-1]


def _gather_wait(lands, sems, after, name):
    n = len(lands)

    def body(*refs):
        land, send_sems, recv_sems = refs[:n], refs[n], refs[n + 1]
        me, peers = _peers()
        for i in range(n):
            for r, peer in enumerate(peers):
                cp = _gather_copy(land[i], send_sems, recv_sems, i * N_PEERS + r, me, peer, _lin(peer))
                cp.wait_send()
                cp.wait_recv()

    return pl.pallas_call(
        body, name=name, in_specs=[HBM] * n + [SEM, SEM] + [ANY] * len(after), out_specs=[HBM] * n,
        out_shape=[pltpu.HBM(a.shape, a.dtype) for a in lands], input_output_aliases={i: i for i in range(n)},
        compiler_params=pltpu.CompilerParams(has_side_effects=DATAFLOW),
    )(*lands, *sems, *after)


def _scatter_copy(grad_ref, land_ref, send_sems, recv_sems, k, me, peer, start):
    mine, theirs = (me, _lin(peer)) if start else (_lin(peer), me)
    return pltpu.make_async_remote_copy(
        src_ref=grad_ref.at[_lin(peer)], dst_ref=land_ref.at[lax.rem(mine - theirs + N_PEERS + N_DEV, N_DEV)],
        send_sem=send_sems.at[k], recv_sem=recv_sems.at[k], device_id=peer, device_id_type=MESH)


def _scatter_start(grads, name):
    n = len(grads)
    lands = [lax.empty((N_PEERS,) + g.shape[1:], g.dtype) for g in grads]
    token_spec, token_shape = _token_spec()

    def body(*refs):
        grad, land, send_sems, recv_sems = refs[:n], refs[n:2 * n], refs[2 * n], refs[2 * n + 1]
        token_ref = refs[4 * n + 2]
        me, peers = _peers()
        for i in range(n):
            for r, peer in enumerate(peers):
                _scatter_copy(grad[i], land[i], send_sems, recv_sems, i * N_PEERS + r, me, peer, True).start()
        token_ref[...] = jnp.zeros_like(token_ref)

    sem_shape = pltpu.SemaphoreType.DMA((N_PEERS * n,))
    out = pl.pallas_call(
        body, name=name, in_specs=[HBM] * (2 * n), out_specs=[SEM, SEM] + [HBM] * (2 * n) + [token_spec],
        out_shape=[sem_shape, sem_shape] + [pltpu.HBM(a.shape, a.dtype) for a in grads + lands] + [token_shape],
        input_output_aliases={a: 2 + a for a in range(2 * n)},
        compiler_params=pltpu.CompilerParams(has_side_effects=DATAFLOW),
    )(*[_in_hbm(a) for a in grads + lands])
    return (out[0], out[1]), list(out[2:2 + n]), list(out[2 + n:2 + 2 * n]), out[-1]


def _scatter_wait(grads, lands, sems, after, name):
    n = len(grads)

    def body(*refs):
        grad, land, send_sems, recv_sems = refs[:n], refs[n:2 * n], refs[2 * n], refs[2 * n + 1]
        me, peers = _peers()
        for i in range(n):
            for r, peer in enumerate(peers):
                cp = _scatter_copy(grad[i], land[i], send_sems, recv_sems, i * N_PEERS + r, me, peer, False)
                cp.wait_send()
                cp.wait_recv()

    out = pl.pallas_call(
        body, name=name, in_specs=[HBM] * (2 * n) + [SEM, SEM] + [ANY] * len(after), out_specs=[HBM] * (2 * n),
        out_shape=[pltpu.HBM(a.shape, a.dtype) for a in grads + lands],
        input_output_aliases={a: a for a in range(2 * n)},
        compiler_params=pltpu.CompilerParams(has_side_effects=DATAFLOW),
    )(*grads, *lands, *sems, *after)
    return list(out[:n]), list(out[n:])


def kernel(x, ln_pre_0, conv_w_in_0, conv_w_0, conv_w_out_0, ln_post_0, ln_pre_1, sb_w_in_1, sb_w_out_1, ln_post_1, ln_pre_2, conv_w_in_2, conv_w_2, conv_w_out_2, ln_post_2, ln_pre_3, sb_w_in_3, sb_w_out_3, ln_post_3, loss_target, m_ln_pre_0, m_conv_w_in_0, m_conv_w_0, m_conv_w_out_0, m_ln_post_0, m_ln_pre_1, m_sb_w_in_1, m_sb_w_out_1, m_ln_post_1, m_ln_pre_2, m_conv_w_in_2, m_conv_w_2, m_conv_w_out_2, m_ln_post_2, m_ln_pre_3, m_sb_w_in_3, m_sb_w_out_3, m_ln_post_3, v_ln_pre_0, v_conv_w_in_0, v_conv_w_0, v_conv_w_out_0, v_ln_post_0, v_ln_pre_1, v_sb_w_in_1, v_sb_w_out_1, v_ln_post_1, v_ln_pre_2, v_conv_w_in_2, v_conv_w_2, v_conv_w_out_2, v_ln_post_2, v_ln_pre_3, v_sb_w_in_3, v_sb_w_out_3, v_ln_post_3):
    names = ['ln_pre_0', 'conv_w_in_0', 'conv_w_0', 'conv_w_out_0', 'ln_post_0', 'ln_pre_1', 'sb_w_in_1', 'sb_w_out_1',
             'ln_post_1', 'ln_pre_2', 'conv_w_in_2', 'conv_w_2', 'conv_w_out_2', 'ln_post_2', 'ln_pre_3', 'sb_w_in_3',
             'sb_w_out_3', 'ln_post_3']
    given = dict(locals())
    w = {n: given[n] for n in names}
    mom = {n: given["m_" + n] for n in names}
    var = {n: given["v_" + n] for n in names}
    conv_layers = [i for i in range(DEPTH) if i % 2 == 0]
    w_in_names = [("conv_w_in_%d" if i % 2 == 0 else "sb_w_in_%d") % i for i in range(DEPTH)]
    w_out_names = [("conv_w_out_%d" if i % 2 == 0 else "sb_w_out_%d") % i for i in range(DEPTH)]

    s, d = x.shape[1:]
    h = x.reshape(s, d)
    target = loss_target.reshape(s, d)
    gains = {n: w[n].reshape(1, d) for n in names if n.startswith("ln_")}
    place = 4 * lax.axis_index("x") + 2 * lax.axis_index("y") + lax.axis_index("c")
    place_arr = place.astype(jnp.int32).reshape(1)
    bdim = w[w_out_names[0]].shape[0] * N_DEV
    wc = bdim // N_DEV

    conv_rows = jnp.concatenate([w["conv_w_%d" % i] for i in conv_layers], axis=0)
    groups = [[_into_slot(w[w_in_names[i]], place_arr, BF16, "slot_w_in_%d" % i),
               _into_slot(w[w_out_names[i]], place_arr, BF16, "slot_w_out_%d" % i)] for i in range(DEPTH)]
    groups[0].append(_into_slot(conv_rows, place_arr, F32, "slot_conv_w"))
    gather_sems, gather_lands, _ = _gather_start(groups, "gather_start")

    saved = []
    weights = []
    conv_full = {}
    for i in range(DEPTH):
        landed = _gather_wait(gather_lands[i], gather_sems[i], [h] if i else [], "gather_wait_%d" % i)
        w_in, w_out = landed[0], landed[1].reshape(bdim, d)
        weights.append((w_in, w_out))
        if i == 0:
            conv_all = landed[2].reshape(N_DEV, len(conv_layers), CONV_K, wc)
            conv_all = jnp.transpose(conv_all, (1, 2, 0, 3)).reshape(len(conv_layers), CONV_K, bdim)
            conv_full = {layer: conv_all[n] for n, layer in enumerate(conv_layers)}
        u, u_t = _rmsnorm_fwd(h, gains["ln_pre_%d" % i], "pre_norm_%d" % i)
        if i % 2 == 0:
            proj = _proj(u, w_in, 0, N_DEV, F32, "proj_%d" % i)
            a, a_t = _conv_gate_fwd(proj, conv_full[i], "conv_gate_%d" % i)
            extra = (proj,)
        else:
            qkv = _proj(u, w_in, 0, 6, BF16, "proj_qkv_%d" % i)
            z = _proj(u, w_in, 6, 2, F32, "proj_z_%d" % i)
            o, carries = _sb_attn_fwd(qkv, "sb_attn_%d" % i)
            a, a_t = _sb_gate_fwd(z, o, "sb_gate_%d" % i)
            extra = (qkv, z, o, carries)
        m = _out_proj(a, w_out, "out_proj_%d" % i)
        saved.append((h, u_t, a_t, m, extra))
        h = _post_norm_residual(h, m, gains["ln_post_%d" % i], "post_norm_%d" % i)

    dh, loss = _loss_head(h, target, "loss_head")
    loss = lax.psum(loss[0, 0], ("x", "y", "c"))

    small = {}
    scattered = {}
    after = []
    for i in reversed(range(DEPTH)):
        h_in, u_t, a_t, m, extra = saved[i]
        w_in, w_out = weights[i]
        dm, small["ln_post_%d" % i] = _post_norm_bwd(dh, m, gains["ln_post_%d" % i], after, "post_norm_bwd_%d" % i)
        g_out = _weight_grad(a_t, dm, 1, "grad_w_out_%d" % i).reshape(N_DEV, wc, d)
        da = _out_proj_bwd_act(dm, w_out, "out_proj_bwd_%d" % i)
        if i % 2 == 0:
            (proj,) = extra
            db, dc, dxt, dz, small["conv_w_%d" % i] = _conv_gate_bwd(proj, da, conv_full[i], "conv_gate_bwd_%d" % i)
            dproj = jnp.concatenate([db, dc, dxt, dz], axis=1)
        else:
            qkv, z, o, carries = extra
            do, dz = _sb_gate_bwd(da, z, o, "sb_gate_bwd_%d" % i)
            dq, dk, dv = _sb_attn_bwd(qkv, do, carries, "sb_attn_bwd_%d" % i)
            dproj = jnp.concatenate([dq, dk, dv, dz], axis=1)
        g_in = _weight_grad(u_t, dproj, N_DEV, "grad_w_in_%d" % i)
        sems, grads, lands, token = _scatter_start([g_in, g_out], "scatter_start_%d" % i)
        scattered[i] = (sems, grads, lands)
        after = [token]
        du = _proj_bwd_act(dproj, w_in, "proj_bwd_%d" % i)
        dh, small["ln_pre_%d" % i] = _pre_norm_bwd(du, h_in, gains["ln_pre_%d" % i], dh, after, "pre_norm_bwd_%d" % i)

    gain_names = [n for n in names if n.startswith("ln_")]
    conv_names = ["conv_w_%d" % i for i in conv_layers]
    rows = [small[n] for n in gain_names] + [small[n] for n in conv_names]
    n_rows = len(gain_names) + CONV_K * len(conv_names)
    pad = -n_rows % SUBLANES
    stacked = jnp.concatenate(rows + [jnp.ones((pad, d), F32)], axis=0)
    (small_all,) = _all_gather([stacked[None]], "gather_small_grads")
    small_all = small_all[0]

    out_g, out_d, out_m, out_v = {}, {}, {}, {}

    def update(n, w2, parts, m2, v2, shape):
        g2, d2, nm2, nv2 = _adamw(w2, parts, m2, v2, "adamw_" + n)
        out_g[n], out_d[n], out_m[n], out_v[n] = (t.reshape(shape) for t in (g2, d2, nm2, nv2))

    after = [small_all]
    for i in reversed(range(DEPTH)):
        sems, grads, lands = scattered[i]
        grads, lands = _scatter_wait(grads, lands, sems, after, "scatter_wait_%d" % i)
        for n, g8, l7 in zip((w_in_names[i], w_out_names[i]), grads, lands):
            res = _adamw_shard(w[n], g8, l7, mom[n], var[n], place_arr, "adamw_" + n)
            out_g[n], out_d[n], out_m[n], out_v[n] = res
        after = [out_g[w_out_names[i]]]
    n_gain = len(gain_names)
    stack = lambda src: jnp.stack([src[n] for n in gain_names])
    g2, d2, nm2, nv2 = _adamw(stack(w), small_all[:, :n_gain], stack(mom), stack(var), "adamw_gains")
    for k, n in enumerate(gain_names):
        out_g[n], out_d[n], out_m[n], out_v[n] = g2[k], d2[k], nm2[k], nv2[k]
    wc = bdim // N_DEV
    for k, n in enumerate(conv_names):
        rows_k = small_all[:, n_gain + CONV_K * k:n_gain + CONV_K * (k + 1)]
        parts = lax.dynamic_slice_in_dim(rows_k, place * wc, wc, axis=2)
        update(n, w[n], parts, mom[n], var[n], w[n].shape)

    grad_x = dh.reshape(x.shape)
    return (loss, grad_x, *[out_g[n] for n in names], *[out_d[n] for n in names],
            *[out_m[n] for n in names], *[out_v[n] for n in names])
```

```python
import functools
import math

import jax
import jax.numpy as jnp
from jax import lax
from jax.experimental import pallas as pl
from jax.experimental.pallas import tpu as pltpu

F32 = jnp.float32
BF16 = jnp.bfloat16
MESH = pl.DeviceIdType.MESH
ANY = pl.BlockSpec(memory_space=pl.ANY)

N_DEV = 8
N_CHIP = 4
DEPTH = 4
HEAD_DIM = 128
CONV_K = 3
RMS_EPS = 1e-6
ADAM_LR = 0.001
ADAM_B1 = 0.9
ADAM_B2 = 0.999
ADAM_EPS = 1e-08
ADAM_WD = 0.01
ADAM_STEP = 10

V7X_VMEM_BYTES = 64 * 1024 * 1024
VMEM_LIMIT = V7X_VMEM_BYTES * 3 // 4
LANES = 128
SUBLANES = 8
HEADS_PER_STEP = 2


def _params(*sem):
    return pltpu.CompilerParams(dimension_semantics=sem, vmem_limit_bytes=VMEM_LIMIT)


def _silu_parts(z):
    sig = jax.nn.sigmoid(z)
    return z * sig, sig


NN = (((1,), (0,)), ((), ()))
NT = (((1,), (1,)), ((), ()))
TN = (((0,), (0,)), ((), ()))


def _mm(a, b, *, dims, grid, a_spec, b_spec, o_spec, out_shape, acc_shape, name):
    nk = grid[2]

    def body(a_ref, b_ref, o_ref, *scratch):
        p = lax.dot_general(a_ref[...], b_ref[...], dims, preferred_element_type=F32)
        if nk == 1:
            o_ref[...] = p.astype(o_ref.dtype)
        else:
            acc_ref = scratch[0]
            k = pl.program_id(2)

            @pl.when(k == 0)
            def _():
                acc_ref[...] = p

            @pl.when(k > 0)
            def _():
                acc_ref[...] += p

            @pl.when(k == nk - 1)
            def _():
                o_ref[...] = acc_ref[...].astype(o_ref.dtype)

    scratch = [] if nk == 1 else [pltpu.VMEM(acc_shape, F32)]
    return pl.pallas_call(
        body, grid=grid, in_specs=[a_spec, b_spec], out_specs=o_spec, out_shape=out_shape,
        scratch_shapes=scratch, compiler_params=_params("parallel", "parallel", "arbitrary"), name=name,
    )(a, b)


def _proj(u, w_in, shard0, n_shard, out_dtype, name):
    s, d = u.shape
    ws = w_in.shape[-1]
    tm, tn = min(s, 512), min(ws, 1024)
    nj = ws // tn
    return _mm(
        u, w_in, dims=NN, grid=(s // tm, n_shard * nj, 1),
        a_spec=pl.BlockSpec((tm, d), lambda i, j, k: (i, 0)),
        b_spec=pl.BlockSpec((None, d, tn), lambda i, j, k: (shard0 + j // nj, 0, j % nj)),
        o_spec=pl.BlockSpec((tm, tn), lambda i, j, k: (i, j)),
        out_shape=jax.ShapeDtypeStruct((s, n_shard * ws), out_dtype), acc_shape=(tm, tn), name=name,
    )


def _out_proj(a, w_out, name):
    s, bdim = a.shape
    d = w_out.shape[-1]
    tm, tn = min(s, 512), min(d, 1024)
    return _mm(
        a, w_out, dims=NN, grid=(s // tm, d // tn, 1),
        a_spec=pl.BlockSpec((tm, bdim), lambda i, j, k: (i, 0)),
        b_spec=pl.BlockSpec((bdim, tn), lambda i, j, k: (0, j)),
        o_spec=pl.BlockSpec((tm, tn), lambda i, j, k: (i, j)),
        out_shape=jax.ShapeDtypeStruct((s, d), F32), acc_shape=(tm, tn), name=name,
    )


def _out_proj_bwd_act(dm, w_out, name):
    s, d = dm.shape
    bdim = w_out.shape[-2]
    tm, tn = min(s, 512), min(bdim, 1024)
    return _mm(
        dm, w_out, dims=NT, grid=(s // tm, bdim // tn, 1),
        a_spec=pl.BlockSpec((tm, d), lambda i, j, k: (i, 0)),
        b_spec=pl.BlockSpec((tn, d), lambda i, j, k: (j, 0)),
        o_spec=pl.BlockSpec((tm, tn), lambda i, j, k: (i, j)),
        out_shape=jax.ShapeDtypeStruct((s, bdim), F32), acc_shape=(tm, tn), name=name,
    )


def _weight_grad(act_t, dout, n_blocks, name):
    din, s = act_t.shape
    w = dout.shape[1] // n_blocks
    tm, tn = min(din, 512), min(w, 1024)
    nj = w // tn
    return _mm(
        act_t, dout, dims=NN, grid=(din // tm, n_blocks * nj, 1),
        a_spec=pl.BlockSpec((tm, s), lambda i, j, k: (i, 0)),
        b_spec=pl.BlockSpec((s, tn), lambda i, j, k: (0, j)),
        o_spec=pl.BlockSpec((None, tm, tn), lambda i, j, k: (j // nj, i, j % nj)),
        out_shape=jax.ShapeDtypeStruct((n_blocks, din, w), BF16), acc_shape=(tm, tn), name=name,
    )


def _proj_bwd_act(dproj, w_in, name):
    s = dproj.shape[0]
    n_shards, d, ws = w_in.shape
    tm, tn = min(s, 512), min(d, 512)

    def body(a_ref, b_ref, o_ref):
        acc = None
        for k in range(n_shards):
            p = lax.dot_general(a_ref[:, k * ws:(k + 1) * ws], b_ref[k], NT, preferred_element_type=F32)
            acc = p if acc is None else acc + p
        o_ref[...] = acc

    return pl.pallas_call(
        body, grid=(s // tm, d // tn),
        in_specs=[pl.BlockSpec((tm, n_shards * ws), lambda i, j: (i, 0)),
                  pl.BlockSpec((n_shards, tn, ws), lambda i, j: (0, j, 0))],
        out_specs=pl.BlockSpec((tm, tn), lambda i, j: (i, j)), out_shape=jax.ShapeDtypeStruct((s, d), F32),
        compiler_params=_params("parallel", "parallel"), name=name,
    )(dproj, w_in)


def _row_spec(tm, d):
    return pl.BlockSpec((tm, d), lambda i: (i, 0))


def _gain_spec(d):
    return pl.BlockSpec((1, d), lambda i: (0, 0))


def _rstd(x):
    return lax.rsqrt(jnp.mean(x * x, axis=-1, keepdims=True) + RMS_EPS)


def _rmsnorm_fwd(h, gain, after, name):
    s, d = h.shape
    tm = min(s, 512)
    n_after = len(after)

    def body(*refs):
        h_ref, g_ref = refs[:2]
        u_ref, ut_ref = refs[2 + n_after:]
        x = h_ref[...]
        u = (x * _rstd(x) * g_ref[...]).astype(u_ref.dtype)
        u_ref[...] = u
        ut_ref[...] = u.T

    return pl.pallas_call(
        body, grid=(s // tm,), in_specs=[_row_spec(tm, d), _gain_spec(d)] + [ANY] * n_after,
        out_specs=[_row_spec(tm, d), pl.BlockSpec((d, tm), lambda i: (0, i))],
        out_shape=[jax.ShapeDtypeStruct((s, d), BF16), jax.ShapeDtypeStruct((d, s), BF16)],
        compiler_params=_params("parallel"), name=name,
    )(h, gain, *after)


def _cast(block, dtype, name):
    r, c = block.shape
    tr = min(r, 256)

    def body(b_ref, o_ref):
        o_ref[...] = b_ref[...].astype(o_ref.dtype)

    spec = pl.BlockSpec((tr, c), lambda i: (i, 0))
    return pl.pallas_call(
        body, grid=(r // tr,), in_specs=[spec], out_specs=spec, out_shape=jax.ShapeDtypeStruct((r, c), dtype),
        compiler_params=_params("parallel"), name=name,
    )(block)


def _post_norm_residual(h, m, gain, name):
    s, d = h.shape
    tm = min(s, 512)

    def body(h_ref, m_ref, g_ref, o_ref):
        x = m_ref[...]
        o_ref[...] = h_ref[...] + x * _rstd(x) * g_ref[...]

    return pl.pallas_call(
        body, grid=(s // tm,), in_specs=[_row_spec(tm, d), _row_spec(tm, d), _gain_spec(d)],
        out_specs=_row_spec(tm, d), out_shape=jax.ShapeDtypeStruct((s, d), F32),
        compiler_params=_params("parallel"), name=name,
    )(h, m, gain)


def _sum_rows_into(acc_ref, x):
    tm, d = x.shape
    acc_ref[...] += jnp.sum(x.reshape(tm // SUBLANES, SUBLANES, d), axis=0)


def _norm_bwd_body(n_steps, with_residual, n_after=0):
    def body(*refs):
        n_in = 4 if with_residual else 3
        dy_ref, x_ref, g_ref = refs[:3]
        dres_ref = refs[3] if with_residual else None
        dx_ref, dg_ref, acc_ref = refs[n_in + n_after:]
        i = pl.program_id(0)

        @pl.when(i == 0)
        def _():
            acc_ref[...] = jnp.zeros_like(acc_ref)

        x = x_ref[...]
        dy = dy_ref[...]
        rstd = _rstd(x)
        n = x * rstd
        dn = dy * g_ref[...]
        dx = rstd * (dn - n * jnp.mean(dn * n, axis=-1, keepdims=True))
        if with_residual:
            dx = dres_ref[...] + dx
        dx_ref[...] = dx.astype(dx_ref.dtype)
        _sum_rows_into(acc_ref, dy * n)

        @pl.when(i == n_steps - 1)
        def _():
            dg_ref[...] = jnp.sum(acc_ref[...], axis=0, keepdims=True)

    return body


def _post_norm_bwd(dh, m, gain, after, name):
    s, d = m.shape
    tm = min(s, 512)
    n_steps = s // tm
    return pl.pallas_call(
        _norm_bwd_body(n_steps, False, len(after)), grid=(n_steps,),
        in_specs=[_row_spec(tm, d), _row_spec(tm, d), _gain_spec(d)] + [ANY] * len(after),
        out_specs=[_row_spec(tm, d), _gain_spec(d)],
        out_shape=[jax.ShapeDtypeStruct((s, d), BF16), jax.ShapeDtypeStruct((1, d), F32)],
        scratch_shapes=[pltpu.VMEM((SUBLANES, d), F32)], compiler_params=_params("arbitrary"), name=name,
    )(dh, m, gain, *after)


def _pre_norm_bwd(du, h, gain, dh, after, name):
    s, d = h.shape
    tm = min(s, 512)
    n_steps = s // tm
    return pl.pallas_call(
        _norm_bwd_body(n_steps, True, len(after)), grid=(n_steps,),
        in_specs=[_row_spec(tm, d), _row_spec(tm, d), _gain_spec(d), _row_spec(tm, d)] + [ANY] * len(after),
        out_specs=[_row_spec(tm, d), _gain_spec(d)],
        out_shape=[jax.ShapeDtypeStruct((s, d), F32), jax.ShapeDtypeStruct((1, d), F32)],
        scratch_shapes=[pltpu.VMEM((SUBLANES, d), F32)], compiler_params=_params("arbitrary"), name=name,
    )(du, h, gain, dh, *after)


def _loss_head(y, target, name):
    s, d = y.shape
    tm = min(s, 512)
    n_steps = s // tm

    def body(y_ref, t_ref, dy_ref, loss_ref, acc_ref):
        i = pl.program_id(0)

        @pl.when(i == 0)
        def _():
            acc_ref[...] = jnp.zeros_like(acc_ref)

        err = y_ref[...] - t_ref[...]
        dy_ref[...] = err / d
        _sum_rows_into(acc_ref, err * err)

        @pl.when(i == n_steps - 1)
        def _():
            total = jnp.sum(jnp.sum(acc_ref[...], axis=0, keepdims=True), axis=1, keepdims=True)
            loss_ref[...] = 0.5 * total / d

    return pl.pallas_call(
        body, grid=(n_steps,), in_specs=[_row_spec(tm, d), _row_spec(tm, d)],
        out_specs=[_row_spec(tm, d), pl.BlockSpec((1, 1), lambda i: (0, 0))],
        out_shape=[jax.ShapeDtypeStruct((s, d), F32), jax.ShapeDtypeStruct((1, 1), F32)],
        scratch_shapes=[pltpu.VMEM((SUBLANES, d), F32)], compiler_params=_params("arbitrary"), name=name,
    )(y, target)


def _shift_down(p, halo, row, n):
    out = jnp.where(row == 0, halo[SUBLANES - n:SUBLANES - n + 1], pltpu.roll(p, n, 0))
    if n == 2:
        out = jnp.where(row == 1, halo[SUBLANES - 1:SUBLANES], out)
    return out


def _shift_up(p, halo, row, n):
    tm = p.shape[0]
    out = jnp.where(row == tm - 1, halo[n - 1:n], pltpu.roll(p, tm - n, 0))
    if n == 2:
        out = jnp.where(row == tm - 2, halo[0:1], out)
    return out


def _conv_specs(tm, tc, nb, n_row_blocks):
    hb = tm // SUBLANES
    cur = lambda part: pl.BlockSpec((tm, tc), lambda i, j: (i, part * nb + j))
    prev = lambda part: pl.BlockSpec((SUBLANES, tc), lambda i, j: (jnp.maximum(i * hb - 1, 0), part * nb + j))
    nxt = lambda part: pl.BlockSpec(
        (SUBLANES, tc), lambda i, j: (jnp.minimum((i + 1) * hb, n_row_blocks * hb - 1), part * nb + j))
    return cur, prev, nxt


def _conv_gate_fwd(proj, conv_w, name):
    s, b4 = proj.shape
    bdim = b4 // 4
    tm, tc = min(s, 512), min(bdim, 512)
    nb = bdim // tc
    cur, prev, _ = _conv_specs(tm, tc, nb, s // tm)

    def body(b_ref, c_ref, x_ref, z_ref, cp_ref, xp_ref, w_ref, a_ref, at_ref):
        i = pl.program_id(0)
        row = lax.broadcasted_iota(jnp.int32, (tm, tc), 0)
        p = c_ref[...] * x_ref[...]
        halo = jnp.where(i > 0, cp_ref[...] * xp_ref[...], 0.0)
        w = w_ref[...]
        cv = w[0:1] * _shift_down(p, halo, row, 2) + w[1:2] * _shift_down(p, halo, row, 1) + w[2:3] * p
        silu, _ = _silu_parts(z_ref[...])
        a = (silu * (b_ref[...] * cv)).astype(a_ref.dtype)
        a_ref[...] = a
        at_ref[...] = a.T

    return pl.pallas_call(
        body, grid=(s // tm, nb),
        in_specs=[cur(0), cur(1), cur(2), cur(3), prev(1), prev(2), pl.BlockSpec((CONV_K, tc), lambda i, j: (0, j))],
        out_specs=[pl.BlockSpec((tm, tc), lambda i, j: (i, j)), pl.BlockSpec((tc, tm), lambda i, j: (j, i))],
        out_shape=[jax.ShapeDtypeStruct((s, bdim), BF16), jax.ShapeDtypeStruct((bdim, s), BF16)],
        compiler_params=_params("parallel", "parallel"), name=name,
    )(proj, proj, proj, proj, proj, proj, conv_w)


def _conv_gate_bwd(proj, da, conv_w, name):
    s, b4 = proj.shape
    bdim = b4 // 4
    tm, tc = min(s, 512), min(bdim, 512)
    nb = bdim // tc
    n_rows = s // tm
    cur, prev, nxt = _conv_specs(tm, tc, nb, n_rows)
    da_cur = pl.BlockSpec((tm, tc), lambda j, i: (i, j))
    hb = tm // SUBLANES
    da_nxt = pl.BlockSpec((SUBLANES, tc), lambda j, i: (jnp.minimum((i + 1) * hb, n_rows * hb - 1), j))
    swap = lambda spec: pl.BlockSpec(spec.block_shape, lambda j, i, f=spec.index_map: f(i, j))

    def body(b_ref, c_ref, x_ref, z_ref, cp_ref, xp_ref, bn_ref, zn_ref, da_ref, dan_ref, w_ref,
             db_ref, dc_ref, dx_ref, dz_ref, dw_ref, acc_ref):
        i = pl.program_id(1)

        @pl.when(i == 0)
        def _():
            acc_ref[...] = jnp.zeros_like(acc_ref)

        row = lax.broadcasted_iota(jnp.int32, (tm, tc), 0)
        w = w_ref[...]
        b, c, x = b_ref[...], c_ref[...], x_ref[...]
        p = c * x
        halo_p = jnp.where(i > 0, cp_ref[...] * xp_ref[...], 0.0)
        p1, p2 = _shift_down(p, halo_p, row, 1), _shift_down(p, halo_p, row, 2)
        cv = w[0:1] * p2 + w[1:2] * p1 + w[2:3] * p
        z = z_ref[...]
        silu, sig = _silu_parts(z)
        da = da_ref[...]
        dy = da * silu
        dcv = dy * b
        silu_n, _ = _silu_parts(zn_ref[...])
        halo_d = jnp.where(i < n_rows - 1, dan_ref[...] * silu_n * bn_ref[...], 0.0)
        dp = w[2:3] * dcv + w[1:2] * _shift_up(dcv, halo_d, row, 1) + w[0:1] * _shift_up(dcv, halo_d, row, 2)
        db_ref[...] = (dy * cv).astype(db_ref.dtype)
        dc_ref[...] = (dp * x).astype(dc_ref.dtype)
        dx_ref[...] = (dp * c).astype(dx_ref.dtype)
        dz_ref[...] = (da * (b * cv) * (sig * (1.0 + z * (1.0 - sig)))).astype(dz_ref.dtype)
        for k, pk in enumerate((p2, p1, p)):
            _sum_rows_into(acc_ref.at[k], dcv * pk)

        @pl.when(i == n_rows - 1)
        def _():
            for k in range(CONV_K):
                dw_ref[k:k + 1, :] = jnp.sum(acc_ref[k], axis=0, keepdims=True)

    out = pl.BlockSpec((tm, tc), lambda j, i: (i, j))
    act = jax.ShapeDtypeStruct((s, bdim), BF16)
    return pl.pallas_call(
        body, grid=(nb, n_rows),
        in_specs=[swap(cur(0)), swap(cur(1)), swap(cur(2)), swap(cur(3)), swap(prev(1)), swap(prev(2)),
                  swap(nxt(0)), swap(nxt(3)), da_cur, da_nxt, pl.BlockSpec((CONV_K, tc), lambda j, i: (0, j))],
        out_specs=[out, out, out, out, pl.BlockSpec((CONV_K, tc), lambda j, i: (0, j))],
        out_shape=[act, act, act, act, jax.ShapeDtypeStruct((CONV_K, bdim), F32)],
        scratch_shapes=[pltpu.VMEM((CONV_K, SUBLANES, tc), F32)],
        compiler_params=_params("parallel", "arbitrary"), name=name,
    )(proj, proj, proj, proj, proj, proj, proj, proj, da, da, conv_w)


def _split(x):
    hi = x.astype(BF16)
    lo = (x - hi.astype(F32)).astype(BF16)
    return jnp.concatenate([hi, lo], axis=1)


def _row_total(x, column):
    return jnp.broadcast_to(x[:, column:column + 1], (x.shape[0], LANES))


def _sb_tiles(qs, ks, carries, suffix_ones, mask, chain=0):
    items = range(len(qs))
    bk = ks[0].shape[0]
    scale = 1.0 / math.sqrt(HEAD_DIM)
    logits = [lax.dot_general(qs[n], ks[n], NT, preferred_element_type=F32) * scale for n in items]
    es = [jnp.exp(-jnp.abs(logits[n])) for n in items]
    keeps = []
    for n in items:
        log_keep = -(jnp.maximum(logits[n], 0.0) + jnp.log(1.0 + es[n]))
        if mask is not None:
            log_keep = jnp.where(mask, log_keep, 0.0)
        keeps.append(_split(log_keep))
    tails = [lax.dot_general(keeps[n], suffix_ones, NN, preferred_element_type=F32) for n in items]
    ws, used = [], []
    for n in items:
        carry = carries[n] if n < len(carries) else used[n - chain] + _row_total(tails[n - chain], 0)
        used.append(carry)
        w = jnp.exp(logits[n] + tails[n] + (carry if carry.shape[1] == 1 else _lane_tile(carry, bk)))
        if mask is not None:
            w = jnp.where(mask, w, 0.0)
        ws.append(w)
    return logits, es, tails, ws, used


def _tri_twice(n, upper):
    r = lax.broadcasted_iota(jnp.int32, (2 * n, n), 0)
    r = jnp.where(r >= n, r - n, r)
    c = lax.broadcasted_iota(jnp.int32, (2 * n, n), 1)
    return jnp.where(r <= c if upper else r >= c, 1.0, 0.0).astype(BF16)


def _group_spec(s, width, part, n_groups):
    return pl.BlockSpec((s, width), lambda h: (0, part * n_groups + h))


def _head_cols(g):
    return slice(g * HEAD_DIM, (g + 1) * HEAD_DIM)


def _lane_tile(x, n):
    return x if n == LANES else jnp.concatenate([x] * (n // LANES), axis=1)


def _sb_attn_fwd(qkv, name):
    s, b3 = qkv.shape
    bdim = b3 // 3
    hps = min(HEADS_PER_STEP, bdim // HEAD_DIM)
    width = hps * HEAD_DIM
    n_groups = bdim // width
    blk = min(s, 256)
    n_blk = s // blk

    def body(q_ref, k_ref, v_ref, o_ref, car_ref, carry_ref):
        suffix_ones = _tri_twice(blk, upper=False)
        r = lax.broadcasted_iota(jnp.int32, (blk, blk), 0)
        c = lax.broadcasted_iota(jnp.int32, (blk, blk), 1)
        diag_mask = c < r
        lane = lax.broadcasted_iota(jnp.int32, (blk, LANES), 1)

        def q_block(qi, _):
            q0 = pl.multiple_of(qi * blk, blk)
            rows = pl.ds(q0, blk)
            qs = [q_ref[rows, _head_cols(g)] for g in range(hps)]
            o_ref[rows, :] = jnp.zeros((blk, width), F32)
            car_ref[rows, :] = jnp.zeros((blk, width), F32)
            carry_ref[...] = jnp.zeros_like(carry_ref)

            def step(js, mask):
                k0s = [pl.multiple_of(j * blk, blk) for j in js]
                items = [(t, g) for t in range(len(js)) for g in range(hps)]
                ks = [k_ref[pl.ds(k0s[t], blk), _head_cols(g)] for t, g in items]
                first = [carry_ref[g] for g in range(hps)]
                _, _, tails, ws, carries = _sb_tiles([qs[g] for _, g in items], ks, first, suffix_ones, mask, chain=hps)
                for g in range(hps):
                    mine = [n for n, (_, h) in enumerate(items) if h == g]
                    acc, saved = None, car_ref[rows, _head_cols(g)]
                    for n in mine:
                        v = v_ref[pl.ds(k0s[items[n][0]], blk), _head_cols(g)]
                        p = lax.dot_general(ws[n].astype(BF16), v, NN, preferred_element_type=F32)
                        acc = p if acc is None else acc + p
                        saved = jnp.where(lane == js[items[n][0]], carries[n], saved)
                    o_ref[rows, _head_cols(g)] += acc
                    car_ref[rows, _head_cols(g)] = saved
                    carry_ref[g] = carries[mine[-1]] + _row_total(tails[mine[-1]], 0)

            step([qi], diag_mask)

            def pair(p, _):
                j = qi - 1 - 2 * p
                step([j, j - 1], None)
                return 0

            lax.fori_loop(0, qi // 2, pair, 0)

            @pl.when(qi % 2 == 1)
            def _():
                step([0], None)

            return 0

        lax.fori_loop(0, n_blk, q_block, 0)

    out = pl.BlockSpec((s, width), lambda h: (0, h))
    shape = jax.ShapeDtypeStruct((s, bdim), F32)
    return pl.pallas_call(
        body, grid=(n_groups,),
        in_specs=[_group_spec(s, width, part, n_groups) for part in range(3)],
        out_specs=[out, out], out_shape=[shape, shape], scratch_shapes=[pltpu.VMEM((hps, blk, LANES), F32)],
        compiler_params=_params("parallel"), name=name,
    )(qkv, qkv, qkv)


def _sb_attn_bwd(qkv, do, carries, name):
    s, b3 = qkv.shape
    bdim = b3 // 3
    hps = min(HEADS_PER_STEP, bdim // HEAD_DIM)
    width = hps * HEAD_DIM
    n_groups = bdim // width
    blk = min(s, 256)
    n_blk = s // blk
    scale = 1.0 / math.sqrt(HEAD_DIM)

    def body(q_ref, k_ref, v_ref, do_ref, car_ref, dq_ref, dk_ref, dv_ref, dk_acc, dv_acc, dq_acc, before_ref):
        suffix_ones = _tri_twice(blk, upper=False)
        prefix_ones = _tri_twice(blk, upper=True)
        r = lax.broadcasted_iota(jnp.int32, (blk, blk), 0)
        c = lax.broadcasted_iota(jnp.int32, (blk, blk), 1)
        diag_mask = c < r
        lane = lax.broadcasted_iota(jnp.int32, (blk, LANES), 1)
        dk_acc[...] = jnp.zeros_like(dk_acc)
        dv_acc[...] = jnp.zeros_like(dv_acc)

        def q_block(qi, _):
            q0 = pl.multiple_of(qi * blk, blk)
            rows = pl.ds(q0, blk)
            qs = [q_ref[rows, _head_cols(g)] for g in range(hps)]
            dos = [do_ref[rows, _head_cols(g)] for g in range(hps)]
            dq_acc[...] = jnp.zeros_like(dq_acc)
            before_ref[...] = jnp.zeros_like(before_ref)

            def step(js, mask):
                k0s = [pl.multiple_of(j * blk, blk) for j in js]
                items = [(t, g) for t in range(len(js)) for g in range(hps)]
                every = range(len(items))
                ks = [k_ref[pl.ds(k0s[t], blk), _head_cols(g)] for t, g in items]
                dws = [lax.dot_general(dos[g], v_ref[pl.ds(k0s[t], blk), _head_cols(g)], NT, preferred_element_type=F32)
                       for t, g in items]
                carries = [jnp.sum(jnp.where(lane == js[t], car_ref[rows, _head_cols(g)], 0.0), axis=1, keepdims=True)
                           for t, g in items]
                logits, es, _, ws, _ = _sb_tiles([qs[g] for _, g in items], ks, carries, suffix_ones, mask)
                gws = [dws[n] * ws[n] for n in every]
                g_upto = [lax.dot_general(_split(gws[n]), prefix_ones, NN, preferred_element_type=F32) for n in every]
                dss, befores = [], []
                for n, (t, g) in enumerate(items):
                    before = before_ref[g] if t == 0 else befores[n - hps] + _row_total(g_upto[n - hps], blk - 1)
                    befores.append(before)
                    sig = jnp.where(logits[n] >= 0.0, 1.0, es[n]) / (1.0 + es[n])
                    dlogits = gws[n] - sig * (_lane_tile(before, blk) + g_upto[n])
                    if mask is not None:
                        dlogits = jnp.where(mask, dlogits, 0.0)
                    dss.append((dlogits * scale).astype(BF16))
                for g in range(hps):
                    mine = [n for n in every if items[n][1] == g]
                    dq = None
                    for n in mine:
                        k0 = k0s[items[n][0]]
                        p = lax.dot_general(dss[n], ks[n], NN, preferred_element_type=F32)
                        dq = p if dq is None else dq + p
                        dk_acc[pl.ds(k0, blk), _head_cols(g)] += lax.dot_general(
                            dss[n], qs[g], TN, preferred_element_type=F32)
                        dv_acc[pl.ds(k0, blk), _head_cols(g)] += lax.dot_general(
                            ws[n].astype(BF16), dos[g], TN, preferred_element_type=F32)
                    dq_acc[:, _head_cols(g)] += dq
                    before_ref[g] = befores[mine[-1]] + _row_total(g_upto[mine[-1]], blk - 1)

            def pair(p, _):
                step([2 * p, 2 * p + 1], None)
                return 0

            lax.fori_loop(0, qi // 2, pair, 0)

            @pl.when(qi % 2 == 1)
            def _():
                step([qi - 1], None)

            step([qi], diag_mask)
            dq_ref[rows, :] = dq_acc[...].astype(dq_ref.dtype)
            return 0

        lax.fori_loop(0, n_blk, q_block, 0)
        dk_ref[...] = dk_acc[...].astype(dk_ref.dtype)
        dv_ref[...] = dv_acc[...].astype(dv_ref.dtype)

    group = pl.BlockSpec((s, width), lambda h: (0, h))
    once = pl.BlockSpec((s, width), lambda h: (0, h), pipeline_mode=pl.Buffered(1))
    shape = jax.ShapeDtypeStruct((s, bdim), BF16)
    return pl.pallas_call(
        body, grid=(n_groups,),
        in_specs=[_group_spec(s, width, part, n_groups) for part in range(3)] + [once, once],
        out_specs=[group, group, group], out_shape=[shape, shape, shape],
        scratch_shapes=[pltpu.VMEM((s, width), F32), pltpu.VMEM((s, width), F32), pltpu.VMEM((blk, width), F32),
                        pltpu.VMEM((hps, blk, LANES), F32)],
        compiler_params=_params("parallel"), name=name,
    )(qkv, qkv, qkv, do, carries)


def _sb_gate_fwd(z, o, name):
    s, bdim = z.shape
    tm = min(s, 512)

    def body(z_ref, o_ref, a_ref, at_ref):
        silu, _ = _silu_parts(z_ref[...])
        a = (silu * o_ref[...]).astype(a_ref.dtype)
        a_ref[...] = a
        at_ref[...] = a.T

    return pl.pallas_call(
        body, grid=(s // tm,), in_specs=[_row_spec(tm, bdim), _row_spec(tm, bdim)],
        out_specs=[_row_spec(tm, bdim), pl.BlockSpec((bdim, tm), lambda i: (0, i))],
        out_shape=[jax.ShapeDtypeStruct((s, bdim), BF16), jax.ShapeDtypeStruct((bdim, s), BF16)],
        compiler_params=_params("parallel"), name=name,
    )(z, o)


def _sb_gate_bwd(da, z, o, name):
    s, bdim = z.shape
    tm = min(s, 512)

    def body(da_ref, z_ref, o_ref, do_ref, dz_ref):
        z = z_ref[...]
        da = da_ref[...]
        silu, sig = _silu_parts(z)
        do_ref[...] = (da * silu).astype(do_ref.dtype)
        dz_ref[...] = (da * o_ref[...] * (sig * (1.0 + z * (1.0 - sig)))).astype(dz_ref.dtype)

    spec = _row_spec(tm, bdim)
    shape = jax.ShapeDtypeStruct((s, bdim), BF16)
    return pl.pallas_call(
        body, grid=(s // tm,), in_specs=[spec, spec, spec], out_specs=[spec, spec], out_shape=[shape, shape],
        compiler_params=_params("parallel"), name=name,
    )(da, z, o)


def _into_slot(block, place, dtype, name):
    r, c = block.shape
    tr = min(r, 256)

    def body(place_ref, b_ref, o_ref):
        o_ref[...] = b_ref[...].astype(o_ref.dtype)

    grid_spec = pltpu.PrefetchScalarGridSpec(
        num_scalar_prefetch=1, grid=(r // tr,),
        in_specs=[pl.BlockSpec((tr, c), lambda i, place_ref: (i, 0))],
        out_specs=pl.BlockSpec((None, tr, c), lambda i, place_ref: (place_ref[0], i, 0)),
    )
    return pl.pallas_call(
        body, grid_spec=grid_spec, out_shape=jax.ShapeDtypeStruct((N_DEV, r, c), dtype),
        compiler_params=_params("parallel"), name=name,
    )(place, block)


def _adamw_step(g, w, m, v, g_ref, d_ref, nm_ref, nv_ref):
    new_m = ADAM_B1 * m + (1.0 - ADAM_B1) * g
    new_v = ADAM_B2 * v + (1.0 - ADAM_B2) * (g * g)
    m_hat = new_m / (1.0 - ADAM_B1 ** ADAM_STEP)
    v_hat = new_v / (1.0 - ADAM_B2 ** ADAM_STEP)
    g_ref[...] = g
    d_ref[...] = -ADAM_LR * (m_hat / (jnp.sqrt(v_hat) + ADAM_EPS) + ADAM_WD * w)
    nm_ref[...] = new_m
    nv_ref[...] = new_v


def _adamw(w, parts, m, v, name):
    r, c = w.shape
    n_parts = parts.shape[0]
    tr = min(r, 256)

    def body(w_ref, p_ref, m_ref, v_ref, *out_refs):
        g = p_ref[0].astype(F32)
        for k in range(1, n_parts):
            g = g + p_ref[k].astype(F32)
        _adamw_step(g, w_ref[...], m_ref[...], v_ref[...], *out_refs)

    spec = pl.BlockSpec((tr, c), lambda i: (i, 0))
    shape = jax.ShapeDtypeStruct((r, c), F32)
    return pl.pallas_call(
        body, grid=(r // tr,), in_specs=[spec, pl.BlockSpec((n_parts, tr, c), lambda i: (0, i, 0)), spec, spec],
        out_specs=[spec] * 4, out_shape=[shape] * 4, compiler_params=_params("parallel"), name=name,
    )(w, parts, m, v)


def _adamw_shard(w, grads, landed, m, v, place, name):
    r, c = w.shape
    n_landed = landed.shape[0]
    tr = min(r, 256)

    def body(place_ref, w_ref, own_ref, l_ref, m_ref, v_ref, *out_refs):
        g = own_ref[...].astype(F32)
        for k in range(n_landed):
            g = g + l_ref[k].astype(F32)
        _adamw_step(g, w_ref[...], m_ref[...], v_ref[...], *out_refs)

    spec = pl.BlockSpec((tr, c), lambda i, place_ref: (i, 0))
    grid_spec = pltpu.PrefetchScalarGridSpec(
        num_scalar_prefetch=1, grid=(r // tr,),
        in_specs=[spec, pl.BlockSpec((None, tr, c), lambda i, place_ref: (place_ref[0], i, 0)),
                  pl.BlockSpec((n_landed, tr, c), lambda i, place_ref: (0, i, 0)), spec, spec],
        out_specs=[spec] * 4,
    )
    return pl.pallas_call(
        body, grid_spec=grid_spec, out_shape=[jax.ShapeDtypeStruct((r, c), F32)] * 4,
        compiler_params=_params("parallel"), name=name,
    )(place, w, grads, landed, m, v)


def _place():
    x, y, c = lax.axis_index("x"), lax.axis_index("y"), lax.axis_index("c")
    other_chips = [(1 - x, y), (x, 1 - y), (1 - x, 1 - y)]
    return x, y, c, other_chips


def _all_gather(blocks, name, after=()):
    n_arr = len(blocks)
    items = [(a, i) for a, blk in enumerate(blocks) for i in range(blk.shape[0])]
    n_items = len(items)

    def body(*refs):
        srcs, refs = refs[:n_arr], refs[n_arr + len(after):]
        outs = refs[:n_arr]
        send_sems, recv_sems, local_sems = refs[n_arr:]
        x, y, c, other_chips = _place()
        me, sibling = (x, y, c), (x, y, 1 - c)

        def slot(it, dev):
            a, i = items[it]
            return outs[a].at[i, 4 * dev[0] + 2 * dev[1] + dev[2]]

        def copy(it, k, block_of, to, from_src=False):
            a, i = items[it]
            return pltpu.make_async_remote_copy(
                src_ref=srcs[a].at[i] if from_src else slot(it, block_of), dst_ref=slot(it, block_of),
                send_sem=send_sems.at[it * 7 + k], recv_sem=recv_sems.at[it * 7 + k],
                device_id=to, device_id_type=MESH)

        own = [pltpu.make_async_copy(srcs[items[it][0]].at[items[it][1]], slot(it, me), local_sems.at[it])
               for it in range(n_items)]
        for cp in own:
            cp.start()
        first = []
        for it in range(n_items):
            first.append(copy(it, 0, me, sibling, from_src=True))
            first += [copy(it, 1 + j, me, (*chip, c), from_src=True) for j, chip in enumerate(other_chips)]
        for cp in first:
            cp.start()
        passed = []
        for it in range(n_items):
            for j, chip in enumerate(other_chips):
                copy(it, 1 + j, (*chip, c), me).wait_recv()
                passed.append(copy(it, 4 + j, (*chip, c), sibling))
                passed[-1].start()
        for it in range(n_items):
            copy(it, 0, sibling, me).wait_recv()
            for j, chip in enumerate(other_chips):
                copy(it, 4 + j, (*chip, 1 - c), me).wait_recv()
        for cp in first + passed:
            cp.wait_send()
        for cp in own:
            cp.wait()

    return pl.pallas_call(
        body, in_specs=[ANY] * (n_arr + len(after)), out_specs=[ANY] * n_arr,
        out_shape=[jax.ShapeDtypeStruct((b.shape[0], N_DEV) + b.shape[1:], b.dtype) for b in blocks],
        scratch_shapes=[pltpu.SemaphoreType.DMA((7 * n_items,)), pltpu.SemaphoreType.DMA((7 * n_items,)),
                        pltpu.SemaphoreType.DMA((n_items,))],
        name=name,
    )(*blocks, *after)


HBM = pl.BlockSpec(memory_space=pltpu.HBM)
SEM = pl.BlockSpec(memory_space=pltpu.SEMAPHORE)
DATAFLOW = pltpu.SideEffectType.DATAFLOW_SIDE_EFFECTING
N_PEERS = N_DEV - 1
FLIPS = [(dx, dy, dc) for dx in (0, 1) for dy in (0, 1) for dc in (0, 1) if (dx, dy, dc) != (0, 0, 0)]


def _peers():
    x, y, c = lax.axis_index("x"), lax.axis_index("y"), lax.axis_index("c")
    flip = lambda v, d: 1 - v if d else v
    return 4 * x + 2 * y + c, [(flip(x, dx), flip(y, dy), flip(c, dc)) for dx, dy, dc in FLIPS]


def _lin(p):
    return 4 * p[0] + 2 * p[1] + p[2]


def _in_hbm(a):
    return pltpu.with_memory_space_constraint(a, pltpu.HBM)


def _token_spec():
    return pl.BlockSpec(memory_space=pltpu.VMEM), jax.ShapeDtypeStruct((SUBLANES, LANES), F32)


def _gather_copy(land_ref, send_sems, recv_sems, k, me, peer, landed_from):
    return pltpu.make_async_remote_copy(
        src_ref=land_ref.at[me], dst_ref=land_ref.at[me if landed_from is None else landed_from],
        send_sem=send_sems.at[k], recv_sem=recv_sems.at[k], device_id=peer, device_id_type=MESH)


def _gather_start(groups, after, name):
    flat = [a for g in groups for a in g]
    n, ng, n_after = len(flat), len(groups), len(after)
    token_spec, token_shape = _token_spec()

    def body(*refs):
        land, outs = refs[:n], refs[n + n_after:]
        sems, token_ref = outs[:2 * ng], outs[2 * ng + n]
        me, peers = _peers()
        a = 0
        for gi, group in enumerate(groups):
            for i in range(len(group)):
                for r, peer in enumerate(peers):
                    _gather_copy(land[a], sems[2 * gi], sems[2 * gi + 1], i * N_PEERS + r, me, peer, None).start()
                a += 1
        token_ref[...] = jnp.zeros_like(token_ref)

    sem_shapes = [pltpu.SemaphoreType.DMA((N_PEERS * len(g),)) for g in groups for _ in (0, 1)]
    out = pl.pallas_call(
        body, name=name, in_specs=[HBM] * n + [ANY] * n_after,
        out_specs=[SEM] * (2 * ng) + [HBM] * n + [token_spec],
        out_shape=sem_shapes + [pltpu.HBM(a.shape, a.dtype) for a in flat] + [token_shape],
        input_output_aliases={a: 2 * ng + a for a in range(n)},
        compiler_params=pltpu.CompilerParams(has_side_effects=DATAFLOW),
    )(*[_in_hbm(a) for a in flat], *after)
    sems = [(out[2 * gi], out[2 * gi + 1]) for gi in range(ng)]
    thru, a = [], 2 * ng
    for g in groups:
        thru.append(list(out[a:a + len(g)]))
        a += len(g)
    return sems, thru, out[-1]


def _gather_wait(lands, sems, after, name):
    n = len(lands)

    def body(*refs):
        land, send_sems, recv_sems = refs[:n], refs[n], refs[n + 1]
        me, peers = _peers()
        for i in range(n):
            for r, peer in enumerate(peers):
                cp = _gather_copy(land[i], send_sems, recv_sems, i * N_PEERS + r, me, peer, _lin(peer))
                cp.wait_send()
                cp.wait_recv()

    return pl.pallas_call(
        body, name=name, in_specs=[HBM] * n + [SEM, SEM] + [ANY] * len(after), out_specs=[HBM] * n,
        out_shape=[pltpu.HBM(a.shape, a.dtype) for a in lands], input_output_aliases={i: i for i in range(n)},
        compiler_params=pltpu.CompilerParams(has_side_effects=DATAFLOW),
    )(*lands, *sems, *after)


def _scatter_copy(grad_ref, land_ref, send_sems, recv_sems, k, me, peer, start):
    mine, theirs = (me, _lin(peer)) if start else (_lin(peer), me)
    return pltpu.make_async_remote_copy(
        src_ref=grad_ref.at[_lin(peer)], dst_ref=land_ref.at[lax.rem(mine - theirs + N_PEERS + N_DEV, N_DEV)],
        send_sem=send_sems.at[k], recv_sem=recv_sems.at[k], device_id=peer, device_id_type=MESH)


def _scatter_start(grads, name):
    n = len(grads)
    lands = [lax.empty((N_PEERS,) + g.shape[1:], g.dtype) for g in grads]
    token_spec, token_shape = _token_spec()

    def body(*refs):
        grad, land, send_sems, recv_sems = refs[:n], refs[n:2 * n], refs[2 * n], refs[2 * n + 1]
        token_ref = refs[4 * n + 2]
        me, peers = _peers()
        for i in range(n):
            for r, peer in enumerate(peers):
                _scatter_copy(grad[i], land[i], send_sems, recv_sems, i * N_PEERS + r, me, peer, True).start()
        token_ref[...] = jnp.zeros_like(token_ref)

    sem_shape = pltpu.SemaphoreType.DMA((N_PEERS * n,))
    out = pl.pallas_call(
        body, name=name, in_specs=[HBM] * (2 * n), out_specs=[SEM, SEM] + [HBM] * (2 * n) + [token_spec],
        out_shape=[sem_shape, sem_shape] + [pltpu.HBM(a.shape, a.dtype) for a in grads + lands] + [token_shape],
        input_output_aliases={a: 2 + a for a in range(2 * n)},
        compiler_params=pltpu.CompilerParams(has_side_effects=DATAFLOW),
    )(*[_in_hbm(a) for a in grads + lands])
    return (out[0], out[1]), list(out[2:2 + n]), list(out[2 + n:2 + 2 * n]), out[-1]


def _scatter_wait(grads, lands, sems, after, name):
    n = len(grads)

    def body(*refs):
        grad, land, send_sems, recv_sems = refs[:n], refs[n:2 * n], refs[2 * n], refs[2 * n + 1]
        me, peers = _peers()
        for i in range(n):
            for r, peer in enumerate(peers):
                cp = _scatter_copy(grad[i], land[i], send_sems, recv_sems, i * N_PEERS + r, me, peer, False)
                cp.wait_send()
                cp.wait_recv()

    out = pl.pallas_call(
        body, name=name, in_specs=[HBM] * (2 * n) + [SEM, SEM] + [ANY] * len(after), out_specs=[HBM] * (2 * n),
        out_shape=[pltpu.HBM(a.shape, a.dtype) for a in grads + lands],
        input_output_aliases={a: a for a in range(2 * n)},
        compiler_params=pltpu.CompilerParams(has_side_effects=DATAFLOW),
    )(*grads, *lands, *sems, *after)
    return list(out[:n]), list(out[n:])


def kernel(x, ln_pre_0, conv_w_in_0, conv_w_0, conv_w_out_0, ln_post_0, ln_pre_1, sb_w_in_1, sb_w_out_1, ln_post_1, ln_pre_2, conv_w_in_2, conv_w_2, conv_w_out_2, ln_post_2, ln_pre_3, sb_w_in_3, sb_w_out_3, ln_post_3, loss_target, m_ln_pre_0, m_conv_w_in_0, m_conv_w_0, m_conv_w_out_0, m_ln_post_0, m_ln_pre_1, m_sb_w_in_1, m_sb_w_out_1, m_ln_post_1, m_ln_pre_2, m_conv_w_in_2, m_conv_w_2, m_conv_w_out_2, m_ln_post_2, m_ln_pre_3, m_sb_w_in_3, m_sb_w_out_3, m_ln_post_3, v_ln_pre_0, v_conv_w_in_0, v_conv_w_0, v_conv_w_out_0, v_ln_post_0, v_ln_pre_1, v_sb_w_in_1, v_sb_w_out_1, v_ln_post_1, v_ln_pre_2, v_conv_w_in_2, v_conv_w_2, v_conv_w_out_2, v_ln_post_2, v_ln_pre_3, v_sb_w_in_3, v_sb_w_out_3, v_ln_post_3):
    names = ['ln_pre_0', 'conv_w_in_0', 'conv_w_0', 'conv_w_out_0', 'ln_post_0', 'ln_pre_1', 'sb_w_in_1', 'sb_w_out_1',
             'ln_post_1', 'ln_pre_2', 'conv_w_in_2', 'conv_w_2', 'conv_w_out_2', 'ln_post_2', 'ln_pre_3', 'sb_w_in_3',
             'sb_w_out_3', 'ln_post_3']
    given = dict(locals())
    w = {n: given[n] for n in names}
    mom = {n: given["m_" + n] for n in names}
    var = {n: given["v_" + n] for n in names}
    conv_layers = [i for i in range(DEPTH) if i % 2 == 0]
    w_in_names = [("conv_w_in_%d" if i % 2 == 0 else "sb_w_in_%d") % i for i in range(DEPTH)]
    w_out_names = [("conv_w_out_%d" if i % 2 == 0 else "sb_w_out_%d") % i for i in range(DEPTH)]

    s, d = x.shape[1:]
    h = x.reshape(s, d)
    target = loss_target.reshape(s, d)
    gains = {n: w[n].reshape(1, d) for n in names if n.startswith("ln_")}
    place = 4 * lax.axis_index("x") + 2 * lax.axis_index("y") + lax.axis_index("c")
    place_arr = place.astype(jnp.int32).reshape(1)
    bdim = w[w_out_names[0]].shape[0] * N_DEV
    wc = bdim // N_DEV

    conv_rows = jnp.concatenate([w["conv_w_%d" % i] for i in conv_layers], axis=0)
    first = _all_gather([_cast(w[w_in_names[0]], BF16, "cast_w_in_0")[None], _cast(w[w_out_names[0]], BF16, "cast_w_out_0")[None],
                         conv_rows[None]], "gather_first_layer")
    conv_all = first[2][0].reshape(N_DEV, len(conv_layers), CONV_K, wc)
    conv_all = jnp.transpose(conv_all, (1, 2, 0, 3)).reshape(len(conv_layers), CONV_K, bdim)
    conv_full = {layer: conv_all[n] for n, layer in enumerate(conv_layers)}
    groups = [[_into_slot(w[w_in_names[i]], place_arr, BF16, "slot_w_in_%d" % i),
               _into_slot(w[w_out_names[i]], place_arr, BF16, "slot_w_out_%d" % i)] for i in range(1, DEPTH)]
    gather_sems, gather_lands, gather_token = _gather_start(groups, [first[0]], "gather_start")

    saved = []
    weights = []
    for i in range(DEPTH):
        if i == 0:
            landed, after = [first[0][0], first[1][0]], [gather_token]
        else:
            landed, after = _gather_wait(gather_lands[i - 1], gather_sems[i - 1], [h], "gather_wait_%d" % i), []
        w_in, w_out = landed[0], landed[1].reshape(bdim, d)
        weights.append((w_in, w_out))
        u, u_t = _rmsnorm_fwd(h, gains["ln_pre_%d" % i], after, "pre_norm_%d" % i)
        if i % 2 == 0:
            proj = _proj(u, w_in, 0, N_DEV, F32, "proj_%d" % i)
            a, a_t = _conv_gate_fwd(proj, conv_full[i], "conv_gate_%d" % i)
            extra = (proj,)
        else:
            qkv = _proj(u, w_in, 0, 6, BF16, "proj_qkv_%d" % i)
            z = _proj(u, w_in, 6, 2, F32, "proj_z_%d" % i)
            o, carries = _sb_attn_fwd(qkv, "sb_attn_%d" % i)
            a, a_t = _sb_gate_fwd(z, o, "sb_gate_%d" % i)
            extra = (qkv, z, o, carries)
        m = _out_proj(a, w_out, "out_proj_%d" % i)
        saved.append((h, u_t, a_t, m, extra))
        h = _post_norm_residual(h, m, gains["ln_post_%d" % i], "post_norm_%d" % i)

    dh, loss = _loss_head(h, target, "loss_head")
    loss = lax.psum(loss[0, 0], ("x", "y", "c"))

    small = {}
    scattered = {}
    after = []
    for i in reversed(range(DEPTH)):
        h_in, u_t, a_t, m, extra = saved[i]
        w_in, w_out = weights[i]
        dm, small["ln_post_%d" % i] = _post_norm_bwd(dh, m, gains["ln_post_%d" % i], after, "post_norm_bwd_%d" % i)
        g_out = _weight_grad(a_t, dm, 1, "grad_w_out_%d" % i).reshape(N_DEV, wc, d)
        da = _out_proj_bwd_act(dm, w_out, "out_proj_bwd_%d" % i)
        if i % 2 == 0:
            (proj,) = extra
            db, dc, dxt, dz, small["conv_w_%d" % i] = _conv_gate_bwd(proj, da, conv_full[i], "conv_gate_bwd_%d" % i)
            dproj = jnp.concatenate([db, dc, dxt, dz], axis=1)
        else:
            qkv, z, o, carries = extra
            do, dz = _sb_gate_bwd(da, z, o, "sb_gate_bwd_%d" % i)
            dq, dk, dv = _sb_attn_bwd(qkv, do, carries, "sb_attn_bwd_%d" % i)
            dproj = jnp.concatenate([dq, dk, dv, dz], axis=1)
        g_in = _weight_grad(u_t, dproj, N_DEV, "grad_w_in_%d" % i)
        sems, grads, lands, token = _scatter_start([g_in, g_out], "scatter_start_%d" % i)
        scattered[i] = (sems, grads, lands)
        after = [token]
        du = _proj_bwd_act(dproj, w_in, "proj_bwd_%d" % i)
        dh, small["ln_pre_%d" % i] = _pre_norm_bwd(du, h_in, gains["ln_pre_%d" % i], dh, after, "pre_norm_bwd_%d" % i)

    out_g, out_d, out_m, out_v = {}, {}, {}, {}

    def update(n, w2, parts, m2, v2, shape):
        g2, d2, nm2, nv2 = _adamw(w2, parts, m2, v2, "adamw_" + n)
        out_g[n], out_d[n], out_m[n], out_v[n] = (t.reshape(shape) for t in (g2, d2, nm2, nv2))

    after = [dh]
    for i in reversed(range(DEPTH)):
        sems, grads, lands = scattered[i]
        grads, lands = _scatter_wait(grads, lands, sems, after, "scatter_wait_%d" % i)
        for n, g8, l7 in zip((w_in_names[i], w_out_names[i]), grads, lands):
            res = _adamw_shard(w[n], g8, l7, mom[n], var[n], place_arr, "adamw_" + n)
            out_g[n], out_d[n], out_m[n], out_v[n] = res
        after = [out_g[w_out_names[i]]]

    gain_names = [n for n in names if n.startswith("ln_")]
    conv_names = ["conv_w_%d" % i for i in conv_layers]
    rows = [small[n] for n in gain_names] + [small[n] for n in conv_names]
    n_rows = len(gain_names) + CONV_K * len(conv_names)
    pad = -n_rows % SUBLANES
    stacked = jnp.concatenate(rows + [jnp.ones((pad, d), F32)], axis=0)
    (small_all,) = _all_gather([stacked[None]], "gather_small_grads", after)
    small_all = small_all[0]
    n_gain = len(gain_names)
    stack = lambda src: jnp.stack([src[n] for n in gain_names])
    g2, d2, nm2, nv2 = _adamw(stack(w), small_all[:, :n_gain], stack(mom), stack(var), "adamw_gains")
    for k, n in enumerate(gain_names):
        out_g[n], out_d[n], out_m[n], out_v[n] = g2[k], d2[k], nm2[k], nv2[k]
    wc = bdim // N_DEV
    for k, n in enumerate(conv_names):
        rows_k = small_all[:, n_gain + CONV_K * k:n_gain + CONV_K * (k + 1)]
        parts = lax.dynamic_slice_in_dim(rows_k, place * wc, wc, axis=2)
        update(n, w[n], parts, mom[n], var[n], w[n].shape)

    grad_x = dh.reshape(x.shape)
    return (loss, grad_x, *[out_g[n] for n in names], *[out_d[n] for n in names],
            *[out_m[n] for n in names], *[out_v[n] for n in names])
```

```python
import functools
import math

import jax
import jax.numpy as jnp
from jax import lax
from jax.experimental import pallas as pl
from jax.experimental.pallas import tpu as pltpu

F32 = jnp.float32
BF16 = jnp.bfloat16
MESH = pl.DeviceIdType.MESH
ANY = pl.BlockSpec(memory_space=pl.ANY)

N_DEV = 8
N_CHIP = 4
DEPTH = 4
HEAD_DIM = 128
CONV_K = 3
RMS_EPS = 1e-6
ADAM_LR = 0.001
ADAM_B1 = 0.9
ADAM_B2 = 0.999
ADAM_EPS = 1e-08
ADAM_WD = 0.01
ADAM_STEP = 10

V7X_VMEM_BYTES = 64 * 1024 * 1024
VMEM_LIMIT = V7X_VMEM_BYTES * 3 // 4
LANES = 128
SUBLANES = 8
HEADS_PER_STEP = 2
DEAD_CARRY = -128.0
UNVISITED = -1e30


def _params(*sem):
    return pltpu.CompilerParams(dimension_semantics=sem, vmem_limit_bytes=VMEM_LIMIT)


def _silu_parts(z):
    sig = jax.nn.sigmoid(z)
    return z * sig, sig


NN = (((1,), (0,)), ((), ()))
NT = (((1,), (1,)), ((), ()))
TN = (((0,), (0,)), ((), ()))


def _mm(a, b, *, dims, grid, a_spec, b_spec, o_spec, out_shape, acc_shape, name):
    nk = grid[2]

    def body(a_ref, b_ref, o_ref, *scratch):
        p = lax.dot_general(a_ref[...], b_ref[...], dims, preferred_element_type=F32)
        if nk == 1:
            o_ref[...] = p.astype(o_ref.dtype)
        else:
            acc_ref = scratch[0]
            k = pl.program_id(2)

            @pl.when(k == 0)
            def _():
                acc_ref[...] = p

            @pl.when(k > 0)
            def _():
                acc_ref[...] += p

            @pl.when(k == nk - 1)
            def _():
                o_ref[...] = acc_ref[...].astype(o_ref.dtype)

    scratch = [] if nk == 1 else [pltpu.VMEM(acc_shape, F32)]
    return pl.pallas_call(
        body, grid=grid, in_specs=[a_spec, b_spec], out_specs=o_spec, out_shape=out_shape,
        scratch_shapes=scratch, compiler_params=_params("parallel", "parallel", "arbitrary"), name=name,
    )(a, b)


def _proj(u, w_in, shard0, n_shard, out_dtype, name):
    s, d = u.shape
    ws = w_in.shape[-1]
    tm, tn = min(s, 512), min(ws, 1024)
    nj = ws // tn
    return _mm(
        u, w_in, dims=NN, grid=(s // tm, n_shard * nj, 1),
        a_spec=pl.BlockSpec((tm, d), lambda i, j, k: (i, 0)),
        b_spec=pl.BlockSpec((None, d, tn), lambda i, j, k: (shard0 + j // nj, 0, j % nj)),
        o_spec=pl.BlockSpec((tm, tn), lambda i, j, k: (i, j)),
        out_shape=jax.ShapeDtypeStruct((s, n_shard * ws), out_dtype), acc_shape=(tm, tn), name=name,
    )


def _out_proj(a, w_out, name):
    s, bdim = a.shape
    d = w_out.shape[-1]
    tm, tn = min(s, 512), min(d, 1024)
    return _mm(
        a, w_out, dims=NN, grid=(s // tm, d // tn, 1),
        a_spec=pl.BlockSpec((tm, bdim), lambda i, j, k: (i, 0)),
        b_spec=pl.BlockSpec((bdim, tn), lambda i, j, k: (0, j)),
        o_spec=pl.BlockSpec((tm, tn), lambda i, j, k: (i, j)),
        out_shape=jax.ShapeDtypeStruct((s, d), F32), acc_shape=(tm, tn), name=name,
    )


def _out_proj_bwd_act(dm, w_out, name):
    s, d = dm.shape
    bdim = w_out.shape[-2]
    tm, tn = min(s, 512), min(bdim, 1024)
    return _mm(
        dm, w_out, dims=NT, grid=(s // tm, bdim // tn, 1),
        a_spec=pl.BlockSpec((tm, d), lambda i, j, k: (i, 0)),
        b_spec=pl.BlockSpec((tn, d), lambda i, j, k: (j, 0)),
        o_spec=pl.BlockSpec((tm, tn), lambda i, j, k: (i, j)),
        out_shape=jax.ShapeDtypeStruct((s, bdim), F32), acc_shape=(tm, tn), name=name,
    )


def _weight_grad(act_t, dout, n_blocks, name):
    din, s = act_t.shape
    w = dout.shape[1] // n_blocks
    tm, tn = min(din, 512), min(w, 1024)
    nj = w // tn
    return _mm(
        act_t, dout, dims=NN, grid=(din // tm, n_blocks * nj, 1),
        a_spec=pl.BlockSpec((tm, s), lambda i, j, k: (i, 0)),
        b_spec=pl.BlockSpec((s, tn), lambda i, j, k: (0, j)),
        o_spec=pl.BlockSpec((None, tm, tn), lambda i, j, k: (j // nj, i, j % nj)),
        out_shape=jax.ShapeDtypeStruct((n_blocks, din, w), BF16), acc_shape=(tm, tn), name=name,
    )


def _proj_bwd_act(dproj, w_in, name):
    s = dproj.shape[0]
    n_shards, d, ws = w_in.shape
    tm, tn = min(s, 512), min(d, 512)

    def body(a_ref, b_ref, o_ref):
        acc = None
        for k in range(n_shards):
            p = lax.dot_general(a_ref[:, k * ws:(k + 1) * ws], b_ref[k], NT, preferred_element_type=F32)
            acc = p if acc is None else acc + p
        o_ref[...] = acc

    return pl.pallas_call(
        body, grid=(s // tm, d // tn),
        in_specs=[pl.BlockSpec((tm, n_shards * ws), lambda i, j: (i, 0)),
                  pl.BlockSpec((n_shards, tn, ws), lambda i, j: (0, j, 0))],
        out_specs=pl.BlockSpec((tm, tn), lambda i, j: (i, j)), out_shape=jax.ShapeDtypeStruct((s, d), F32),
        compiler_params=_params("parallel", "parallel"), name=name,
    )(dproj, w_in)


def _row_spec(tm, d):
    return pl.BlockSpec((tm, d), lambda i: (i, 0))


def _gain_spec(d):
    return pl.BlockSpec((1, d), lambda i: (0, 0))


def _rstd(x):
    return lax.rsqrt(jnp.mean(x * x, axis=-1, keepdims=True) + RMS_EPS)


def _rmsnorm_fwd(h, gain, after, name):
    s, d = h.shape
    tm = min(s, 512)
    n_after = len(after)

    def body(*refs):
        h_ref, g_ref = refs[:2]
        u_ref, ut_ref = refs[2 + n_after:]
        x = h_ref[...]
        u = (x * _rstd(x) * g_ref[...]).astype(u_ref.dtype)
        u_ref[...] = u
        ut_ref[...] = u.T

    return pl.pallas_call(
        body, grid=(s // tm,), in_specs=[_row_spec(tm, d), _gain_spec(d)] + [ANY] * n_after,
        out_specs=[_row_spec(tm, d), pl.BlockSpec((d, tm), lambda i: (0, i))],
        out_shape=[jax.ShapeDtypeStruct((s, d), BF16), jax.ShapeDtypeStruct((d, s), BF16)],
        compiler_params=_params("parallel"), name=name,
    )(h, gain, *after)


def _cast(block, dtype, name):
    r, c = block.shape
    tr = min(r, 256)

    def body(b_ref, o_ref):
        o_ref[...] = b_ref[...].astype(o_ref.dtype)

    spec = pl.BlockSpec((tr, c), lambda i: (i, 0))
    return pl.pallas_call(
        body, grid=(r // tr,), in_specs=[spec], out_specs=spec, out_shape=jax.ShapeDtypeStruct((r, c), dtype),
        compiler_params=_params("parallel"), name=name,
    )(block)


def _post_norm_residual(h, m, gain, name):
    s, d = h.shape
    tm = min(s, 512)

    def body(h_ref, m_ref, g_ref, o_ref):
        x = m_ref[...]
        o_ref[...] = h_ref[...] + x * _rstd(x) * g_ref[...]

    return pl.pallas_call(
        body, grid=(s // tm,), in_specs=[_row_spec(tm, d), _row_spec(tm, d), _gain_spec(d)],
        out_specs=_row_spec(tm, d), out_shape=jax.ShapeDtypeStruct((s, d), F32),
        compiler_params=_params("parallel"), name=name,
    )(h, m, gain)


def _sum_rows_into(acc_ref, x):
    tm, d = x.shape
    acc_ref[...] += jnp.sum(x.reshape(tm // SUBLANES, SUBLANES, d), axis=0)


def _norm_bwd_body(n_steps, with_residual, n_after=0):
    def body(*refs):
        n_in = 4 if with_residual else 3
        dy_ref, x_ref, g_ref = refs[:3]
        dres_ref = refs[3] if with_residual else None
        dx_ref, dg_ref, acc_ref = refs[n_in + n_after:]
        i = pl.program_id(0)

        @pl.when(i == 0)
        def _():
            acc_ref[...] = jnp.zeros_like(acc_ref)

        x = x_ref[...]
        dy = dy_ref[...]
        rstd = _rstd(x)
        n = x * rstd
        dn = dy * g_ref[...]
        dx = rstd * (dn - n * jnp.mean(dn * n, axis=-1, keepdims=True))
        if with_residual:
            dx = dres_ref[...] + dx
        dx_ref[...] = dx.astype(dx_ref.dtype)
        _sum_rows_into(acc_ref, dy * n)

        @pl.when(i == n_steps - 1)
        def _():
            dg_ref[...] = jnp.sum(acc_ref[...], axis=0, keepdims=True)

    return body


def _post_norm_bwd(dh, m, gain, after, name):
    s, d = m.shape
    tm = min(s, 512)
    n_steps = s // tm
    return pl.pallas_call(
        _norm_bwd_body(n_steps, False, len(after)), grid=(n_steps,),
        in_specs=[_row_spec(tm, d), _row_spec(tm, d), _gain_spec(d)] + [ANY] * len(after),
        out_specs=[_row_spec(tm, d), _gain_spec(d)],
        out_shape=[jax.ShapeDtypeStruct((s, d), BF16), jax.ShapeDtypeStruct((1, d), F32)],
        scratch_shapes=[pltpu.VMEM((SUBLANES, d), F32)], compiler_params=_params("arbitrary"), name=name,
    )(dh, m, gain, *after)


def _pre_norm_bwd(du, h, gain, dh, after, name):
    s, d = h.shape
    tm = min(s, 512)
    n_steps = s // tm
    return pl.pallas_call(
        _norm_bwd_body(n_steps, True, len(after)), grid=(n_steps,),
        in_specs=[_row_spec(tm, d), _row_spec(tm, d), _gain_spec(d), _row_spec(tm, d)] + [ANY] * len(after),
        out_specs=[_row_spec(tm, d), _gain_spec(d)],
        out_shape=[jax.ShapeDtypeStruct((s, d), F32), jax.ShapeDtypeStruct((1, d), F32)],
        scratch_shapes=[pltpu.VMEM((SUBLANES, d), F32)], compiler_params=_params("arbitrary"), name=name,
    )(du, h, gain, dh, *after)


def _loss_head(y, target, name):
    s, d = y.shape
    tm = min(s, 512)
    n_steps = s // tm

    def body(y_ref, t_ref, dy_ref, loss_ref, acc_ref):
        i = pl.program_id(0)

        @pl.when(i == 0)
        def _():
            acc_ref[...] = jnp.zeros_like(acc_ref)

        err = y_ref[...] - t_ref[...]
        dy_ref[...] = err / d
        _sum_rows_into(acc_ref, err * err)

        @pl.when(i == n_steps - 1)
        def _():
            total = jnp.sum(jnp.sum(acc_ref[...], axis=0, keepdims=True), axis=1, keepdims=True)
            loss_ref[...] = 0.5 * total / d

    return pl.pallas_call(
        body, grid=(n_steps,), in_specs=[_row_spec(tm, d), _row_spec(tm, d)],
        out_specs=[_row_spec(tm, d), pl.BlockSpec((1, 1), lambda i: (0, 0))],
        out_shape=[jax.ShapeDtypeStruct((s, d), F32), jax.ShapeDtypeStruct((1, 1), F32)],
        scratch_shapes=[pltpu.VMEM((SUBLANES, d), F32)], compiler_params=_params("arbitrary"), name=name,
    )(y, target)


def _shift_down(p, halo, row, n):
    out = jnp.where(row == 0, halo[SUBLANES - n:SUBLANES - n + 1], pltpu.roll(p, n, 0))
    if n == 2:
        out = jnp.where(row == 1, halo[SUBLANES - 1:SUBLANES], out)
    return out


def _shift_up(p, halo, row, n):
    tm = p.shape[0]
    out = jnp.where(row == tm - 1, halo[n - 1:n], pltpu.roll(p, tm - n, 0))
    if n == 2:
        out = jnp.where(row == tm - 2, halo[0:1], out)
    return out


def _conv_specs(tm, tc, nb, n_row_blocks):
    hb = tm // SUBLANES
    cur = lambda part: pl.BlockSpec((tm, tc), lambda i, j: (i, part * nb + j))
    prev = lambda part: pl.BlockSpec((SUBLANES, tc), lambda i, j: (jnp.maximum(i * hb - 1, 0), part * nb + j))
    nxt = lambda part: pl.BlockSpec(
        (SUBLANES, tc), lambda i, j: (jnp.minimum((i + 1) * hb, n_row_blocks * hb - 1), part * nb + j))
    return cur, prev, nxt


def _conv_gate_fwd(proj, conv_w, name):
    s, b4 = proj.shape
    bdim = b4 // 4
    tm, tc = min(s, 512), min(bdim, 512)
    nb = bdim // tc
    cur, prev, _ = _conv_specs(tm, tc, nb, s // tm)

    def body(b_ref, c_ref, x_ref, z_ref, cp_ref, xp_ref, w_ref, a_ref, at_ref):
        i = pl.program_id(0)
        row = lax.broadcasted_iota(jnp.int32, (tm, tc), 0)
        p = c_ref[...] * x_ref[...]
        halo = jnp.where(i > 0, cp_ref[...] * xp_ref[...], 0.0)
        w = w_ref[...]
        cv = w[0:1] * _shift_down(p, halo, row, 2) + w[1:2] * _shift_down(p, halo, row, 1) + w[2:3] * p
        silu, _ = _silu_parts(z_ref[...])
        a = (silu * (b_ref[...] * cv)).astype(a_ref.dtype)
        a_ref[...] = a
        at_ref[...] = a.T

    return pl.pallas_call(
        body, grid=(s // tm, nb),
        in_specs=[cur(0), cur(1), cur(2), cur(3), prev(1), prev(2), pl.BlockSpec((CONV_K, tc), lambda i, j: (0, j))],
        out_specs=[pl.BlockSpec((tm, tc), lambda i, j: (i, j)), pl.BlockSpec((tc, tm), lambda i, j: (j, i))],
        out_shape=[jax.ShapeDtypeStruct((s, bdim), BF16), jax.ShapeDtypeStruct((bdim, s), BF16)],
        compiler_params=_params("parallel", "parallel"), name=name,
    )(proj, proj, proj, proj, proj, proj, conv_w)


def _conv_gate_bwd(proj, da, conv_w, name):
    s, b4 = proj.shape
    bdim = b4 // 4
    tm, tc = min(s, 512), min(bdim, 512)
    nb = bdim // tc
    n_rows = s // tm
    cur, prev, nxt = _conv_specs(tm, tc, nb, n_rows)
    da_cur = pl.BlockSpec((tm, tc), lambda j, i: (i, j))
    hb = tm // SUBLANES
    da_nxt = pl.BlockSpec((SUBLANES, tc), lambda j, i: (jnp.minimum((i + 1) * hb, n_rows * hb - 1), j))
    swap = lambda spec: pl.BlockSpec(spec.block_shape, lambda j, i, f=spec.index_map: f(i, j))

    def body(b_ref, c_ref, x_ref, z_ref, cp_ref, xp_ref, bn_ref, zn_ref, da_ref, dan_ref, w_ref,
             db_ref, dc_ref, dx_ref, dz_ref, dw_ref, acc_ref):
        i = pl.program_id(1)

        @pl.when(i == 0)
        def _():
            acc_ref[...] = jnp.zeros_like(acc_ref)

        row = lax.broadcasted_iota(jnp.int32, (tm, tc), 0)
        w = w_ref[...]
        b, c, x = b_ref[...], c_ref[...], x_ref[...]
        p = c * x
        halo_p = jnp.where(i > 0, cp_ref[...] * xp_ref[...], 0.0)
        p1, p2 = _shift_down(p, halo_p, row, 1), _shift_down(p, halo_p, row, 2)
        cv = w[0:1] * p2 + w[1:2] * p1 + w[2:3] * p
        z = z_ref[...]
        silu, sig = _silu_parts(z)
        da = da_ref[...]
        dy = da * silu
        dcv = dy * b
        silu_n, _ = _silu_parts(zn_ref[...])
        halo_d = jnp.where(i < n_rows - 1, dan_ref[...] * silu_n * bn_ref[...], 0.0)
        dp = w[2:3] * dcv + w[1:2] * _shift_up(dcv, halo_d, row, 1) + w[0:1] * _shift_up(dcv, halo_d, row, 2)
        db_ref[...] = (dy * cv).astype(db_ref.dtype)
        dc_ref[...] = (dp * x).astype(dc_ref.dtype)
        dx_ref[...] = (dp * c).astype(dx_ref.dtype)
        dz_ref[...] = (da * (b * cv) * (sig * (1.0 + z * (1.0 - sig)))).astype(dz_ref.dtype)
        for k, pk in enumerate((p2, p1, p)):
            _sum_rows_into(acc_ref.at[k], dcv * pk)

        @pl.when(i == n_rows - 1)
        def _():
            for k in range(CONV_K):
                dw_ref[k:k + 1, :] = jnp.sum(acc_ref[k], axis=0, keepdims=True)

    out = pl.BlockSpec((tm, tc), lambda j, i: (i, j))
    act = jax.ShapeDtypeStruct((s, bdim), BF16)
    return pl.pallas_call(
        body, grid=(nb, n_rows),
        in_specs=[swap(cur(0)), swap(cur(1)), swap(cur(2)), swap(cur(3)), swap(prev(1)), swap(prev(2)),
                  swap(nxt(0)), swap(nxt(3)), da_cur, da_nxt, pl.BlockSpec((CONV_K, tc), lambda j, i: (0, j))],
        out_specs=[out, out, out, out, pl.BlockSpec((CONV_K, tc), lambda j, i: (0, j))],
        out_shape=[act, act, act, act, jax.ShapeDtypeStruct((CONV_K, bdim), F32)],
        scratch_shapes=[pltpu.VMEM((CONV_K, SUBLANES, tc), F32)],
        compiler_params=_params("parallel", "arbitrary"), name=name,
    )(proj, proj, proj, proj, proj, proj, proj, proj, da, da, conv_w)


def _split(x):
    hi = x.astype(BF16)
    lo = (x - hi.astype(F32)).astype(BF16)
    return jnp.concatenate([hi, lo], axis=1)


def _row_total(x, column):
    return jnp.broadcast_to(x[:, column:column + 1], (x.shape[0], LANES))


def _sb_tiles(qs, ks, carries, suffix_ones, masks, chain=0):
    items = range(len(qs))
    bk = ks[0].shape[0]
    scale = 1.0 / math.sqrt(HEAD_DIM)
    logits = [lax.dot_general(qs[n], ks[n], NT, preferred_element_type=F32) * scale for n in items]
    es = [jnp.exp(-jnp.abs(logits[n])) for n in items]
    keeps = []
    for n in items:
        log_keep = -(jnp.maximum(logits[n], 0.0) + jnp.log(1.0 + es[n]))
        if masks[n] is not None:
            log_keep = jnp.where(masks[n], log_keep, 0.0)
        keeps.append(_split(log_keep))
    tails = [lax.dot_general(keeps[n], suffix_ones, NN, preferred_element_type=F32) for n in items]
    ws, used = [], []
    for n in items:
        carry = carries[n] if n < len(carries) else used[n - chain] + _row_total(tails[n - chain], 0)
        used.append(carry)
        w = jnp.exp(logits[n] + tails[n] + (carry if carry.shape[1] == 1 else _lane_tile(carry, bk)))
        if masks[n] is not None:
            w = jnp.where(masks[n], w, 0.0)
        ws.append(w)
    return logits, es, tails, ws, used


def _tri_twice(n, upper):
    r = lax.broadcasted_iota(jnp.int32, (2 * n, n), 0)
    r = jnp.where(r >= n, r - n, r)
    c = lax.broadcasted_iota(jnp.int32, (2 * n, n), 1)
    return jnp.where(r <= c if upper else r >= c, 1.0, 0.0).astype(BF16)


def _group_spec(s, width, part, n_groups):
    return pl.BlockSpec((s, width), lambda h: (0, part * n_groups + h))


def _head_cols(g):
    return slice(g * HEAD_DIM, (g + 1) * HEAD_DIM)


def _lane_tile(x, n):
    return x if n == LANES else jnp.concatenate([x] * (n // LANES), axis=1)


def _sb_attn_fwd(qkv, name):
    s, b3 = qkv.shape
    bdim = b3 // 3
    hps = min(HEADS_PER_STEP, bdim // HEAD_DIM)
    width = hps * HEAD_DIM
    n_groups = bdim // width
    blk = min(s, 256)
    n_blk = s // blk

    def body(q_ref, k_ref, v_ref, o_ref, car_ref, carry_ref):
        suffix_ones = _tri_twice(blk, upper=False)
        r = lax.broadcasted_iota(jnp.int32, (blk, blk), 0)
        c = lax.broadcasted_iota(jnp.int32, (blk, blk), 1)
        diag_mask = c < r
        lane = lax.broadcasted_iota(jnp.int32, (blk, LANES), 1)

        def q_block(qi, _):
            q0 = pl.multiple_of(qi * blk, blk)
            rows = pl.ds(q0, blk)
            qs = [q_ref[rows, _head_cols(g)] for g in range(hps)]
            o_ref[rows, :] = jnp.zeros((blk, width), F32)
            car_ref[rows, :] = jnp.full((blk, width), UNVISITED, F32)
            carry_ref[...] = jnp.zeros_like(carry_ref)

            def step(js, tile_masks):
                k0s = [pl.multiple_of(j * blk, blk) for j in js]
                items = [(t, g) for t in range(len(js)) for g in range(hps)]
                ks = [k_ref[pl.ds(k0s[t], blk), _head_cols(g)] for t, g in items]
                first = [carry_ref[g] for g in range(hps)]
                _, _, tails, ws, carries = _sb_tiles([qs[g] for _, g in items], ks, first, suffix_ones,
                                                     [tile_masks[t] for t, _ in items], chain=hps)
                for g in range(hps):
                    mine = [n for n, (_, h) in enumerate(items) if h == g]
                    acc, saved = None, car_ref[rows, _head_cols(g)]
                    for n in mine:
                        v = v_ref[pl.ds(k0s[items[n][0]], blk), _head_cols(g)]
                        p = lax.dot_general(ws[n].astype(BF16), v, NN, preferred_element_type=F32)
                        acc = p if acc is None else acc + p
                        saved = jnp.where(lane == js[items[n][0]], carries[n], saved)
                    o_ref[rows, _head_cols(g)] += acc
                    car_ref[rows, _head_cols(g)] = saved
                    carry_ref[g] = carries[mine[-1]] + _row_total(tails[mine[-1]], 0)

            @pl.when(qi == 0)
            def _():
                step([0], [diag_mask])

            @pl.when(qi > 0)
            def _():
                step([qi, qi - 1], [diag_mask, None])

            def alive():
                top = jnp.max(jnp.max(carry_ref[...], axis=0), axis=0, keepdims=True)
                return (jnp.max(top, axis=1, keepdims=True)[0, 0] >= DEAD_CARRY).astype(jnp.int32)

            left = jnp.maximum(qi - 1, 0)

            def pair(state):
                p, _ = state
                j = qi - 2 - 2 * p
                step([j, j - 1], [None, None])
                return p + 1, alive()

            p, live = lax.while_loop(lambda state: (state[0] < left // 2) & (state[1] > 0), pair, (0, alive()))

            @pl.when((left % 2 == 1) & (p == left // 2) & (live > 0))
            def _():
                step([0], [None])

            return 0

        lax.fori_loop(0, n_blk, q_block, 0)

    out = pl.BlockSpec((s, width), lambda h: (0, h))
    shape = jax.ShapeDtypeStruct((s, bdim), F32)
    return pl.pallas_call(
        body, grid=(n_groups,),
        in_specs=[_group_spec(s, width, part, n_groups) for part in range(3)],
        out_specs=[out, out], out_shape=[shape, shape], scratch_shapes=[pltpu.VMEM((hps, blk, LANES), F32)],
        compiler_params=_params("parallel"), name=name,
    )(qkv, qkv, qkv)


def _sb_attn_bwd(qkv, do, carries, name):
    s, b3 = qkv.shape
    bdim = b3 // 3
    hps = min(HEADS_PER_STEP, bdim // HEAD_DIM)
    width = hps * HEAD_DIM
    n_groups = bdim // width
    blk = min(s, 256)
    n_blk = s // blk
    scale = 1.0 / math.sqrt(HEAD_DIM)

    def body(q_ref, k_ref, v_ref, do_ref, car_ref, dq_ref, dk_ref, dv_ref, dk_acc, dv_acc, dq_acc, before_ref):
        suffix_ones = _tri_twice(blk, upper=False)
        prefix_ones = _tri_twice(blk, upper=True)
        r = lax.broadcasted_iota(jnp.int32, (blk, blk), 0)
        c = lax.broadcasted_iota(jnp.int32, (blk, blk), 1)
        diag_mask = c < r
        lane = lax.broadcasted_iota(jnp.int32, (blk, LANES), 1)
        dk_acc[...] = jnp.zeros_like(dk_acc)
        dv_acc[...] = jnp.zeros_like(dv_acc)

        def q_block(qi, _):
            q0 = pl.multiple_of(qi * blk, blk)
            rows = pl.ds(q0, blk)
            qs = [q_ref[rows, _head_cols(g)] for g in range(hps)]
            dos = [do_ref[rows, _head_cols(g)] for g in range(hps)]
            dq_acc[...] = jnp.zeros_like(dq_acc)
            before_ref[...] = jnp.zeros_like(before_ref)

            def step(js, tile_masks):
                masks = [tile_masks[t] for t in range(len(js)) for _ in range(hps)]
                k0s = [pl.multiple_of(j * blk, blk) for j in js]
                items = [(t, g) for t in range(len(js)) for g in range(hps)]
                every = range(len(items))
                ks = [k_ref[pl.ds(k0s[t], blk), _head_cols(g)] for t, g in items]
                dws = [lax.dot_general(dos[g], v_ref[pl.ds(k0s[t], blk), _head_cols(g)], NT, preferred_element_type=F32)
                       for t, g in items]
                carries = [jnp.sum(jnp.where(lane == js[t], car_ref[rows, _head_cols(g)], 0.0), axis=1, keepdims=True)
                           for t, g in items]
                logits, es, _, ws, _ = _sb_tiles([qs[g] for _, g in items], ks, carries, suffix_ones, masks)
                gws = [dws[n] * ws[n] for n in every]
                g_upto = [lax.dot_general(_split(gws[n]), prefix_ones, NN, preferred_element_type=F32) for n in every]
                dss, befores = [], []
                for n, (t, g) in enumerate(items):
                    before = before_ref[g] if t == 0 else befores[n - hps] + _row_total(g_upto[n - hps], blk - 1)
                    befores.append(before)
                    sig = jnp.where(logits[n] >= 0.0, 1.0, es[n]) / (1.0 + es[n])
                    dlogits = gws[n] - sig * (_lane_tile(before, blk) + g_upto[n])
                    if masks[n] is not None:
                        dlogits = jnp.where(masks[n], dlogits, 0.0)
                    dss.append((dlogits * scale).astype(BF16))
                for g in range(hps):
                    mine = [n for n in every if items[n][1] == g]
                    dq = None
                    for n in mine:
                        k0 = k0s[items[n][0]]
                        p = lax.dot_general(dss[n], ks[n], NN, preferred_element_type=F32)
                        dq = p if dq is None else dq + p
                        dk_acc[pl.ds(k0, blk), _head_cols(g)] += lax.dot_general(
                            dss[n], qs[g], TN, preferred_element_type=F32)
                        dv_acc[pl.ds(k0, blk), _head_cols(g)] += lax.dot_general(
                            ws[n].astype(BF16), dos[g], TN, preferred_element_type=F32)
                    dq_acc[:, _head_cols(g)] += dq
                    before_ref[g] = befores[mine[-1]] + _row_total(g_upto[mine[-1]], blk - 1)

            top = car_ref[rows, _head_cols(0)]
            for g in range(1, hps):
                top = jnp.maximum(top, car_ref[rows, _head_cols(g)])
            top = jnp.max(top, axis=0, keepdims=True)
            lane_row = lax.broadcasted_iota(jnp.int32, (1, LANES), 1)
            counted = jnp.where((top >= DEAD_CARRY) & (lane_row < qi), 1.0, 0.0)
            n_alive = jnp.sum(counted, axis=1, keepdims=True)[0, 0].astype(jnp.int32)
            left = jnp.maximum(n_alive - 1, 0)
            start = qi - 1 - left

            @pl.when(left % 2 == 1)
            def _():
                step([start], [None])

            def pair(p, _):
                j = start + left % 2 + 2 * p
                step([j, j + 1], [None, None])
                return 0

            lax.fori_loop(0, left // 2, pair, 0)

            @pl.when(qi == 0)
            def _():
                step([0], [diag_mask])

            @pl.when(qi > 0)
            def _():
                step([qi - 1, qi], [None, diag_mask])
            dq_ref[rows, :] = dq_acc[...].astype(dq_ref.dtype)
            return 0

        lax.fori_loop(0, n_blk, q_block, 0)
        dk_ref[...] = dk_acc[...].astype(dk_ref.dtype)
        dv_ref[...] = dv_acc[...].astype(dv_ref.dtype)

    group = pl.BlockSpec((s, width), lambda h: (0, h))
    once = pl.BlockSpec((s, width), lambda h: (0, h), pipeline_mode=pl.Buffered(1))
    shape = jax.ShapeDtypeStruct((s, bdim), BF16)
    return pl.pallas_call(
        body, grid=(n_groups,),
        in_specs=[_group_spec(s, width, part, n_groups) for part in range(3)] + [once, once],
        out_specs=[group, group, group], out_shape=[shape, shape, shape],
        scratch_shapes=[pltpu.VMEM((s, width), F32), pltpu.VMEM((s, width), F32), pltpu.VMEM((blk, width), F32),
                        pltpu.VMEM((hps, blk, LANES), F32)],
        compiler_params=_params("parallel"), name=name,
    )(qkv, qkv, qkv, do, carries)


def _sb_gate_fwd(z, o, name):
    s, bdim = z.shape
    tm = min(s, 512)

    def body(z_ref, o_ref, a_ref, at_ref):
        silu, _ = _silu_parts(z_ref[...])
        a = (silu * o_ref[...]).astype(a_ref.dtype)
        a_ref[...] = a
        at_ref[...] = a.T

    return pl.pallas_call(
        body, grid=(s // tm,), in_specs=[_row_spec(tm, bdim), _row_spec(tm, bdim)],
        out_specs=[_row_spec(tm, bdim), pl.BlockSpec((bdim, tm), lambda i: (0, i))],
        out_shape=[jax.ShapeDtypeStruct((s, bdim), BF16), jax.ShapeDtypeStruct((bdim, s), BF16)],
        compiler_params=_params("parallel"), name=name,
    )(z, o)


def _sb_gate_bwd(da, z, o, name):
    s, bdim = z.shape
    tm = min(s, 512)

    def body(da_ref, z_ref, o_ref, do_ref, dz_ref):
        z = z_ref[...]
        da = da_ref[...]
        silu, sig = _silu_parts(z)
        do_ref[...] = (da * silu).astype(do_ref.dtype)
        dz_ref[...] = (da * o_ref[...] * (sig * (1.0 + z * (1.0 - sig)))).astype(dz_ref.dtype)

    spec = _row_spec(tm, bdim)
    shape = jax.ShapeDtypeStruct((s, bdim), BF16)
    return pl.pallas_call(
        body, grid=(s // tm,), in_specs=[spec, spec, spec], out_specs=[spec, spec], out_shape=[shape, shape],
        compiler_params=_params("parallel"), name=name,
    )(da, z, o)


def _into_slot(block, place, dtype, name):
    r, c = block.shape
    tr = min(r, 256)

    def body(place_ref, b_ref, o_ref):
        o_ref[...] = b_ref[...].astype(o_ref.dtype)

    grid_spec = pltpu.PrefetchScalarGridSpec(
        num_scalar_prefetch=1, grid=(r // tr,),
        in_specs=[pl.BlockSpec((tr, c), lambda i, place_ref: (i, 0))],
        out_specs=pl.BlockSpec((None, tr, c), lambda i, place_ref: (place_ref[0], i, 0)),
    )
    return pl.pallas_call(
        body, grid_spec=grid_spec, out_shape=jax.ShapeDtypeStruct((N_DEV, r, c), dtype),
        compiler_params=_params("parallel"), name=name,
    )(place, block)


def _add_core_pair(grads, received, core, name):
    _, _, r, c = grads.shape
    tr = min(r, 256)

    def body(core_ref, g_ref, r_ref, o_ref):
        o_ref[...] = (g_ref[...].astype(F32) + r_ref[...].astype(F32)).astype(o_ref.dtype)

    grid_spec = pltpu.PrefetchScalarGridSpec(
        num_scalar_prefetch=1, grid=(N_CHIP, r // tr),
        in_specs=[pl.BlockSpec((None, None, tr, c), lambda q, i, core_ref: (q, core_ref[0], i, 0)),
                  pl.BlockSpec((None, tr, c), lambda q, i, core_ref: (q, i, 0))],
        out_specs=pl.BlockSpec((None, tr, c), lambda q, i, core_ref: (q, i, 0)),
    )
    return pl.pallas_call(
        body, grid_spec=grid_spec, out_shape=jax.ShapeDtypeStruct((N_CHIP, r, c), BF16),
        compiler_params=_params("parallel", "parallel"), name=name,
    )(core, grads, received)


def _adamw_step(g, w, m, v, g_ref, d_ref, nm_ref, nv_ref):
    new_m = ADAM_B1 * m + (1.0 - ADAM_B1) * g
    new_v = ADAM_B2 * v + (1.0 - ADAM_B2) * (g * g)
    m_hat = new_m / (1.0 - ADAM_B1 ** ADAM_STEP)
    v_hat = new_v / (1.0 - ADAM_B2 ** ADAM_STEP)
    g_ref[...] = g
    d_ref[...] = -ADAM_LR * (m_hat / (jnp.sqrt(v_hat) + ADAM_EPS) + ADAM_WD * w)
    nm_ref[...] = new_m
    nv_ref[...] = new_v


def _adamw(w, parts, m, v, name):
    r, c = w.shape
    n_parts = parts.shape[0]
    tr = min(r, 256)

    def body(w_ref, p_ref, m_ref, v_ref, *out_refs):
        g = p_ref[0].astype(F32)
        for k in range(1, n_parts):
            g = g + p_ref[k].astype(F32)
        _adamw_step(g, w_ref[...], m_ref[...], v_ref[...], *out_refs)

    spec = pl.BlockSpec((tr, c), lambda i: (i, 0))
    shape = jax.ShapeDtypeStruct((r, c), F32)
    return pl.pallas_call(
        body, grid=(r // tr,), in_specs=[spec, pl.BlockSpec((n_parts, tr, c), lambda i: (0, i, 0)), spec, spec],
        out_specs=[spec] * 4, out_shape=[shape] * 4, compiler_params=_params("parallel"), name=name,
    )(w, parts, m, v)


def _adamw_shard(w, grads, landed, m, v, place, name):
    r, c = w.shape
    n_landed = landed.shape[0]
    tr = min(r, 256)

    def body(place_ref, w_ref, own_ref, l_ref, m_ref, v_ref, *out_refs):
        g = own_ref[...].astype(F32)
        for k in range(n_landed):
            g = g + l_ref[k].astype(F32)
        _adamw_step(g, w_ref[...], m_ref[...], v_ref[...], *out_refs)

    spec = pl.BlockSpec((tr, c), lambda i, place_ref: (i, 0))
    grid_spec = pltpu.PrefetchScalarGridSpec(
        num_scalar_prefetch=1, grid=(r // tr,),
        in_specs=[spec, pl.BlockSpec((None, tr, c), lambda i, place_ref: (place_ref[0], i, 0)),
                  pl.BlockSpec((n_landed, tr, c), lambda i, place_ref: (0, i, 0)), spec, spec],
        out_specs=[spec] * 4,
    )
    return pl.pallas_call(
        body, grid_spec=grid_spec, out_shape=[jax.ShapeDtypeStruct((r, c), F32)] * 4,
        compiler_params=_params("parallel"), name=name,
    )(place, w, grads, landed, m, v)


def _place():
    x, y, c = lax.axis_index("x"), lax.axis_index("y"), lax.axis_index("c")
    other_chips = [(1 - x, y), (x, 1 - y), (1 - x, 1 - y)]
    return x, y, c, other_chips


def _all_gather(blocks, name, after=()):
    n_arr = len(blocks)
    items = [(a, i) for a, blk in enumerate(blocks) for i in range(blk.shape[0])]
    n_items = len(items)

    def body(*refs):
        srcs, refs = refs[:n_arr], refs[n_arr + len(after):]
        outs = refs[:n_arr]
        send_sems, recv_sems, local_sems = refs[n_arr:]
        x, y, c, other_chips = _place()
        me, sibling = (x, y, c), (x, y, 1 - c)

        def slot(it, dev):
            a, i = items[it]
            return outs[a].at[i, 4 * dev[0] + 2 * dev[1] + dev[2]]

        def copy(it, k, block_of, to, from_src=False):
            a, i = items[it]
            return pltpu.make_async_remote_copy(
                src_ref=srcs[a].at[i] if from_src else slot(it, block_of), dst_ref=slot(it, block_of),
                send_sem=send_sems.at[it * 7 + k], recv_sem=recv_sems.at[it * 7 + k],
                device_id=to, device_id_type=MESH)

        own = [pltpu.make_async_copy(srcs[items[it][0]].at[items[it][1]], slot(it, me), local_sems.at[it])
               for it in range(n_items)]
        for cp in own:
            cp.start()
        first = []
        for it in range(n_items):
            first.append(copy(it, 0, me, sibling, from_src=True))
            first += [copy(it, 1 + j, me, (*chip, c), from_src=True) for j, chip in enumerate(other_chips)]
        for cp in first:
            cp.start()
        passed = []
        for it in range(n_items):
            for j, chip in enumerate(other_chips):
                copy(it, 1 + j, (*chip, c), me).wait_recv()
                passed.append(copy(it, 4 + j, (*chip, c), sibling))
                passed[-1].start()
        for it in range(n_items):
            copy(it, 0, sibling, me).wait_recv()
            for j, chip in enumerate(other_chips):
                copy(it, 4 + j, (*chip, 1 - c), me).wait_recv()
        for cp in first + passed:
            cp.wait_send()
        for cp in own:
            cp.wait()

    return pl.pallas_call(
        body, in_specs=[ANY] * (n_arr + len(after)), out_specs=[ANY] * n_arr,
        out_shape=[jax.ShapeDtypeStruct((b.shape[0], N_DEV) + b.shape[1:], b.dtype) for b in blocks],
        scratch_shapes=[pltpu.SemaphoreType.DMA((7 * n_items,)), pltpu.SemaphoreType.DMA((7 * n_items,)),
                        pltpu.SemaphoreType.DMA((n_items,))],
        name=name,
    )(*blocks, *after)


def _exchange_core_pair(grads, name):
    n_arr = len(grads)

    def body(*refs):
        srcs, outs = refs[:n_arr], refs[n_arr:2 * n_arr]
        send_sems, recv_sems = refs[2 * n_arr:]
        x, y, c, _ = _place()
        copies = [
            pltpu.make_async_remote_copy(
                src_ref=srcs[a].at[q, 1 - c], dst_ref=outs[a].at[q],
                send_sem=send_sems.at[a * N_CHIP + q], recv_sem=recv_sems.at[a * N_CHIP + q],
                device_id=(x, y, 1 - c), device_id_type=MESH)
            for a in range(n_arr) for q in range(N_CHIP)]
        for cp in copies:
            cp.start()
        for cp in copies:
            cp.wait_recv()
        for cp in copies:
            cp.wait_send()

    return pl.pallas_call(
        body, in_specs=[ANY] * n_arr, out_specs=[ANY] * n_arr,
        out_shape=[jax.ShapeDtypeStruct((N_CHIP,) + g.shape[2:], g.dtype) for g in grads],
        scratch_shapes=[pltpu.SemaphoreType.DMA((N_CHIP * n_arr,)), pltpu.SemaphoreType.DMA((N_CHIP * n_arr,))],
        name=name,
    )(*grads)


def _exchange_chips(partials, name):
    n_arr = len(partials)

    def body(*refs):
        srcs, outs = refs[:n_arr], refs[n_arr:2 * n_arr]
        send_sems, recv_sems, local_sems = refs[2 * n_arr:]
        x, y, c, other_chips = _place()
        my_chip = 2 * x + y
        own = [pltpu.make_async_copy(srcs[a].at[my_chip], outs[a].at[my_chip], local_sems.at[a]) for a in range(n_arr)]
        for cp in own:
            cp.start()
        copies = [
            pltpu.make_async_remote_copy(
                src_ref=srcs[a].at[2 * chip[0] + chip[1]], dst_ref=outs[a].at[my_chip],
                send_sem=send_sems.at[a * 3 + j], recv_sem=recv_sems.at[a * 3 + j],
                device_id=(*chip, c), device_id_type=MESH)
            for a in range(n_arr) for j, chip in enumerate(other_chips)]
        for cp in copies:
            cp.start()
        for cp in copies:
            cp.wait_recv()
        for cp in copies:
            cp.wait_send()
        for cp in own:
            cp.wait()

    return pl.pallas_call(
        body, in_specs=[ANY] * n_arr, out_specs=[ANY] * n_arr,
        out_shape=[jax.ShapeDtypeStruct(p.shape, p.dtype) for p in partials],
        scratch_shapes=[pltpu.SemaphoreType.DMA((3 * n_arr,)), pltpu.SemaphoreType.DMA((3 * n_arr,)),
                        pltpu.SemaphoreType.DMA((n_arr,))],
        name=name,
    )(*partials)


HBM = pl.BlockSpec(memory_space=pltpu.HBM)
SEM = pl.BlockSpec(memory_space=pltpu.SEMAPHORE)
DATAFLOW = pltpu.SideEffectType.DATAFLOW_SIDE_EFFECTING
N_PEERS = N_DEV - 1
FLIPS = [(dx, dy, dc) for dx in (0, 1) for dy in (0, 1) for dc in (0, 1) if (dx, dy, dc) != (0, 0, 0)]


def _peers():
    x, y, c = lax.axis_index("x"), lax.axis_index("y"), lax.axis_index("c")
    flip = lambda v, d: 1 - v if d else v
    return 4 * x + 2 * y + c, [(flip(x, dx), flip(y, dy), flip(c, dc)) for dx, dy, dc in FLIPS]


def _lin(p):
    return 4 * p[0] + 2 * p[1] + p[2]


def _in_hbm(a):
    return pltpu.with_memory_space_constraint(a, pltpu.HBM)


def _token_spec():
    return pl.BlockSpec(memory_space=pltpu.VMEM), jax.ShapeDtypeStruct((SUBLANES, LANES), F32)


def _gather_copy(land_ref, send_sems, recv_sems, k, me, peer, landed_from):
    return pltpu.make_async_remote_copy(
        src_ref=land_ref.at[me], dst_ref=land_ref.at[me if landed_from is None else landed_from],
        send_sem=send_sems.at[k], recv_sem=recv_sems.at[k], device_id=peer, device_id_type=MESH)


def _gather_start(groups, after, name):
    flat = [a for g in groups for a in g]
    n, ng, n_after = len(flat), len(groups), len(after)
    token_spec, token_shape = _token_spec()

    def body(*refs):
        land, outs = refs[:n], refs[n + n_after:]
        sems, token_ref = outs[:2 * ng], outs[2 * ng + n]
        me, peers = _peers()
        a = 0
        for gi, group in enumerate(groups):
            for i in range(len(group)):
                for r, peer in enumerate(peers):
                    _gather_copy(land[a], sems[2 * gi], sems[2 * gi + 1], i * N_PEERS + r, me, peer, None).start()
                a += 1
        token_ref[...] = jnp.zeros_like(token_ref)

    sem_shapes = [pltpu.SemaphoreType.DMA((N_PEERS * len(g),)) for g in groups for _ in (0, 1)]
    out = pl.pallas_call(
        body, name=name, in_specs=[HBM] * n + [ANY] * n_after,
        out_specs=[SEM] * (2 * ng) + [HBM] * n + [token_spec],
        out_shape=sem_shapes + [pltpu.HBM(a.shape, a.dtype) for a in flat] + [token_shape],
        input_output_aliases={a: 2 * ng + a for a in range(n)},
        compiler_params=pltpu.CompilerParams(has_side_effects=DATAFLOW),
    )(*[_in_hbm(a) for a in flat], *after)
    sems = [(out[2 * gi], out[2 * gi + 1]) for gi in range(ng)]
    thru, a = [], 2 * ng
    for g in groups:
        thru.append(list(out[a:a + len(g)]))
        a += len(g)
    return sems, thru, out[-1]


def _gather_wait(lands, sems, after, name):
    n = len(lands)

    def body(*refs):
        land, send_sems, recv_sems = refs[:n], refs[n], refs[n + 1]
        me, peers = _peers()
        for i in range(n):
            for r, peer in enumerate(peers):
                cp = _gather_copy(land[i], send_sems, recv_sems, i * N_PEERS + r, me, peer, _lin(peer))
                cp.wait_send()
                cp.wait_recv()

    return pl.pallas_call(
        body, name=name, in_specs=[HBM] * n + [SEM, SEM] + [ANY] * len(after), out_specs=[HBM] * n,
        out_shape=[pltpu.HBM(a.shape, a.dtype) for a in lands], input_output_aliases={i: i for i in range(n)},
        compiler_params=pltpu.CompilerParams(has_side_effects=DATAFLOW),
    )(*lands, *sems, *after)


def _scatter_copy(grad_ref, land_ref, send_sems, recv_sems, k, me, peer, start):
    mine, theirs = (me, _lin(peer)) if start else (_lin(peer), me)
    return pltpu.make_async_remote_copy(
        src_ref=grad_ref.at[_lin(peer)], dst_ref=land_ref.at[lax.rem(mine - theirs + N_PEERS + N_DEV, N_DEV)],
        send_sem=send_sems.at[k], recv_sem=recv_sems.at[k], device_id=peer, device_id_type=MESH)


def _scatter_start(grads, name):
    n = len(grads)
    lands = [lax.empty((N_PEERS,) + g.shape[1:], g.dtype) for g in grads]
    token_spec, token_shape = _token_spec()

    def body(*refs):
        grad, land, send_sems, recv_sems = refs[:n], refs[n:2 * n], refs[2 * n], refs[2 * n + 1]
        token_ref = refs[4 * n + 2]
        me, peers = _peers()
        for i in range(n):
            for r, peer in enumerate(peers):
                _scatter_copy(grad[i], land[i], send_sems, recv_sems, i * N_PEERS + r, me, peer, True).start()
        token_ref[...] = jnp.zeros_like(token_ref)

    sem_shape = pltpu.SemaphoreType.DMA((N_PEERS * n,))
    out = pl.pallas_call(
        body, name=name, in_specs=[HBM] * (2 * n), out_specs=[SEM, SEM] + [HBM] * (2 * n) + [token_spec],
        out_shape=[sem_shape, sem_shape] + [pltpu.HBM(a.shape, a.dtype) for a in grads + lands] + [token_shape],
        input_output_aliases={a: 2 + a for a in range(2 * n)},
        compiler_params=pltpu.CompilerParams(has_side_effects=DATAFLOW),
    )(*[_in_hbm(a) for a in grads + lands])
    return (out[0], out[1]), list(out[2:2 + n]), list(out[2 + n:2 + 2 * n]), out[-1]


def _scatter_wait(grads, lands, sems, after, name):
    n = len(grads)

    def body(*refs):
        grad, land, send_sems, recv_sems = refs[:n], refs[n:2 * n], refs[2 * n], refs[2 * n + 1]
        me, peers = _peers()
        for i in range(n):
            for r, peer in enumerate(peers):
                cp = _scatter_copy(grad[i], land[i], send_sems, recv_sems, i * N_PEERS + r, me, peer, False)
                cp.wait_send()
                cp.wait_recv()

    out = pl.pallas_call(
        body, name=name, in_specs=[HBM] * (2 * n) + [SEM, SEM] + [ANY] * len(after), out_specs=[HBM] * (2 * n),
        out_shape=[pltpu.HBM(a.shape, a.dtype) for a in grads + lands],
        input_output_aliases={a: a for a in range(2 * n)},
        compiler_params=pltpu.CompilerParams(has_side_effects=DATAFLOW),
    )(*grads, *lands, *sems, *after)
    return list(out[:n]), list(out[n:])


def kernel(x, ln_pre_0, conv_w_in_0, conv_w_0, conv_w_out_0, ln_post_0, ln_pre_1, sb_w_in_1, sb_w_out_1, ln_post_1, ln_pre_2, conv_w_in_2, conv_w_2, conv_w_out_2, ln_post_2, ln_pre_3, sb_w_in_3, sb_w_out_3, ln_post_3, loss_target, m_ln_pre_0, m_conv_w_in_0, m_conv_w_0, m_conv_w_out_0, m_ln_post_0, m_ln_pre_1, m_sb_w_in_1, m_sb_w_out_1, m_ln_post_1, m_ln_pre_2, m_conv_w_in_2, m_conv_w_2, m_conv_w_out_2, m_ln_post_2, m_ln_pre_3, m_sb_w_in_3, m_sb_w_out_3, m_ln_post_3, v_ln_pre_0, v_conv_w_in_0, v_conv_w_0, v_conv_w_out_0, v_ln_post_0, v_ln_pre_1, v_sb_w_in_1, v_sb_w_out_1, v_ln_post_1, v_ln_pre_2, v_conv_w_in_2, v_conv_w_2, v_conv_w_out_2, v_ln_post_2, v_ln_pre_3, v_sb_w_in_3, v_sb_w_out_3, v_ln_post_3):
    names = ['ln_pre_0', 'conv_w_in_0', 'conv_w_0', 'conv_w_out_0', 'ln_post_0', 'ln_pre_1', 'sb_w_in_1', 'sb_w_out_1',
             'ln_post_1', 'ln_pre_2', 'conv_w_in_2', 'conv_w_2', 'conv_w_out_2', 'ln_post_2', 'ln_pre_3', 'sb_w_in_3',
             'sb_w_out_3', 'ln_post_3']
    given = dict(locals())
    w = {n: given[n] for n in names}
    mom = {n: given["m_" + n] for n in names}
    var = {n: given["v_" + n] for n in names}
    conv_layers = [i for i in range(DEPTH) if i % 2 == 0]
    w_in_names = [("conv_w_in_%d" if i % 2 == 0 else "sb_w_in_%d") % i for i in range(DEPTH)]
    w_out_names = [("conv_w_out_%d" if i % 2 == 0 else "sb_w_out_%d") % i for i in range(DEPTH)]

    s, d = x.shape[1:]
    h = x.reshape(s, d)
    target = loss_target.reshape(s, d)
    gains = {n: w[n].reshape(1, d) for n in names if n.startswith("ln_")}
    place = 4 * lax.axis_index("x") + 2 * lax.axis_index("y") + lax.axis_index("c")
    place_arr = place.astype(jnp.int32).reshape(1)
    bdim = w[w_out_names[0]].shape[0] * N_DEV
    wc = bdim // N_DEV

    conv_rows = jnp.concatenate([w["conv_w_%d" % i] for i in conv_layers], axis=0)
    shards = [_cast(w[n], BF16, "cast_" + n)[None] for n in w_in_names + w_out_names]
    gathered = _all_gather(shards + [conv_rows[None]], "gather_weights")
    conv_all = gathered[-1][0].reshape(N_DEV, len(conv_layers), CONV_K, wc)
    conv_all = jnp.transpose(conv_all, (1, 2, 0, 3)).reshape(len(conv_layers), CONV_K, bdim)
    conv_full = {layer: conv_all[n] for n, layer in enumerate(conv_layers)}
    weights = [(gathered[i][0], gathered[DEPTH + i][0].reshape(bdim, d)) for i in range(DEPTH)]

    saved = []
    for i in range(DEPTH):
        w_in, w_out = weights[i]
        u, u_t = _rmsnorm_fwd(h, gains["ln_pre_%d" % i], [], "pre_norm_%d" % i)
        if i % 2 == 0:
            proj = _proj(u, w_in, 0, N_DEV, F32, "proj_%d" % i)
            a, a_t = _conv_gate_fwd(proj, conv_full[i], "conv_gate_%d" % i)
            extra = (proj,)
        else:
            qkv = _proj(u, w_in, 0, 6, BF16, "proj_qkv_%d" % i)
            z = _proj(u, w_in, 6, 2, F32, "proj_z_%d" % i)
            o, carries = _sb_attn_fwd(qkv, "sb_attn_%d" % i)
            a, a_t = _sb_gate_fwd(z, o, "sb_gate_%d" % i)
            extra = (qkv, z, o, carries)
        m = _out_proj(a, w_out, "out_proj_%d" % i)
        saved.append((h, u_t, a_t, m, extra))
        h = _post_norm_residual(h, m, gains["ln_post_%d" % i], "post_norm_%d" % i)

    dh, loss = _loss_head(h, target, "loss_head")
    loss = lax.psum(loss[0, 0], ("x", "y", "c"))

    small = {}
    big = {}
    for i in reversed(range(DEPTH)):
        h_in, u_t, a_t, m, extra = saved[i]
        w_in, w_out = weights[i]
        dm, small["ln_post_%d" % i] = _post_norm_bwd(dh, m, gains["ln_post_%d" % i], [], "post_norm_bwd_%d" % i)
        big[w_out_names[i]] = _weight_grad(a_t, dm, 1, "grad_w_out_%d" % i).reshape(N_CHIP, 2, wc, d)
        da = _out_proj_bwd_act(dm, w_out, "out_proj_bwd_%d" % i)
        if i % 2 == 0:
            (proj,) = extra
            db, dc, dxt, dz, small["conv_w_%d" % i] = _conv_gate_bwd(proj, da, conv_full[i], "conv_gate_bwd_%d" % i)
            dproj = jnp.concatenate([db, dc, dxt, dz], axis=1)
        else:
            qkv, z, o, carries = extra
            do, dz = _sb_gate_bwd(da, z, o, "sb_gate_bwd_%d" % i)
            dq, dk, dv = _sb_attn_bwd(qkv, do, carries, "sb_attn_bwd_%d" % i)
            dproj = jnp.concatenate([dq, dk, dv, dz], axis=1)
        g_in = _weight_grad(u_t, dproj, N_DEV, "grad_w_in_%d" % i)
        big[w_in_names[i]] = g_in.reshape((N_CHIP, 2) + g_in.shape[1:])
        du = _proj_bwd_act(dproj, w_in, "proj_bwd_%d" % i)
        dh, small["ln_pre_%d" % i] = _pre_norm_bwd(du, h_in, gains["ln_pre_%d" % i], dh, [], "pre_norm_bwd_%d" % i)

    big_names = w_in_names + w_out_names
    core = lax.axis_index("c").astype(jnp.int32).reshape(1)
    from_sibling = _exchange_core_pair([big[n] for n in big_names], "reduce_core_pair")
    pair_sums = [_add_core_pair(big[n], r, core, "add_core_pair_" + n) for n, r in zip(big_names, from_sibling)]
    chip_parts = dict(zip(big_names, _exchange_chips(pair_sums, "reduce_chips")))

    gain_names = [n for n in names if n.startswith("ln_")]
    conv_names = ["conv_w_%d" % i for i in conv_layers]
    rows = [small[n] for n in gain_names] + [small[n] for n in conv_names]
    n_rows = len(gain_names) + CONV_K * len(conv_names)
    pad = -n_rows % SUBLANES
    stacked = jnp.concatenate(rows + [jnp.zeros((pad, d), F32)], axis=0)
    (small_all,) = _all_gather([stacked[None]], "gather_small_grads")
    small_all = small_all[0]

    out_g, out_d, out_m, out_v = {}, {}, {}, {}

    def update(n, w2, parts, m2, v2, shape):
        g2, d2, nm2, nv2 = _adamw(w2, parts, m2, v2, "adamw_" + n)
        out_g[n], out_d[n], out_m[n], out_v[n] = (t.reshape(shape) for t in (g2, d2, nm2, nv2))

    for n in big_names:
        update(n, w[n], chip_parts[n], mom[n], var[n], w[n].shape)
    n_gain = len(gain_names)
    stack = lambda src: jnp.stack([src[n] for n in gain_names])
    g2, d2, nm2, nv2 = _adamw(stack(w), small_all[:, :n_gain], stack(mom), stack(var), "adamw_gains")
    for k, n in enumerate(gain_names):
        out_g[n], out_d[n], out_m[n], out_v[n] = g2[k], d2[k], nm2[k], nv2[k]
    wc = bdim // N_DEV
    for k, n in enumerate(conv_names):
        rows_k = small_all[:, n_gain + CONV_K * k:n_gain + CONV_K * (k + 1)]
        parts = lax.dynamic_slice_in_dim(rows_k, place * wc, wc, axis=2)
        update(n, w[n], parts, mom[n], var[n], w[n].shape)

    grad_x = dh.reshape(x.shape)
    return (loss, grad_x, *[out_g[n] for n in names], *[out_d[n] for n in names],
            *[out_m[n] for n in names], *[out_v[n] for n in names])
```

```python
import functools
import math

import jax
import jax.numpy as jnp
from jax import lax
from jax.experimental import pallas as pl
from jax.experimental.pallas import tpu as pltpu

F32 = jnp.float32
BF16 = jnp.bfloat16
MESH = pl.DeviceIdType.MESH
ANY = pl.BlockSpec(memory_space=pl.ANY)

N_DEV = 8
N_CHIP = 4
DEPTH = 4
HEAD_DIM = 128
CONV_K = 3
RMS_EPS = 1e-6
ADAM_LR = 0.001
ADAM_B1 = 0.9
ADAM_B2 = 0.999
ADAM_EPS = 1e-08
ADAM_WD = 0.01
ADAM_STEP = 10

V7X_VMEM_BYTES = 64 * 1024 * 1024
VMEM_LIMIT = V7X_VMEM_BYTES * 3 // 4
LANES = 128
SUBLANES = 8
HEADS_PER_STEP = 2
DEAD_CARRY = -128.0
UNVISITED = -1e30


def _params(*sem):
    return pltpu.CompilerParams(dimension_semantics=sem, vmem_limit_bytes=VMEM_LIMIT)


def _silu_parts(z):
    sig = jax.nn.sigmoid(z)
    return z * sig, sig


NN = (((1,), (0,)), ((), ()))
NT = (((1,), (1,)), ((), ()))
TN = (((0,), (0,)), ((), ()))


def _gridded_call(body, operands, *, grid, in_specs, out_specs, out_shape, scratch_shapes, semantics, name, hosted=None):
    if hosted is None:
        return pl.pallas_call(
            body, grid=grid, in_specs=in_specs, out_specs=out_specs, out_shape=out_shape,
            scratch_shapes=scratch_shapes, compiler_params=_params(*semantics), name=name)(*operands)
    arrays, scatter = hosted
    n_in, n_out, n_ex, n_scr = len(in_specs), len(out_specs), len(arrays), len(scratch_shapes)

    def hosting_body(*refs):
        ins, refs = refs[:n_in], refs[n_in:]
        ex_in, refs = refs[:n_ex], refs[n_ex:]
        outs, refs = refs[:n_out], refs[n_out:]
        ex_out, refs = refs[:n_ex], refs[n_ex:]
        scratch, sems = refs[:n_scr], refs[n_scr:]
        first = last = None
        for axis, size in enumerate(grid):
            at_start, at_end = pl.program_id(axis) == 0, pl.program_id(axis) == size - 1
            first = at_start if first is None else first & at_start
            last = at_end if last is None else last & at_end

        @pl.when(first)
        def _():
            _start_all(*_chip_copies(ex_in, ex_out, *sems, scatter))

        body(*ins, *outs, *scratch)

        @pl.when(last)
        def _():
            _wait_all(*_chip_copies(ex_in, ex_out, *sems, scatter))

    ex_shapes = [jax.ShapeDtypeStruct((N_CHIP,) + (a.shape[1:] if scatter else a.shape), a.dtype) for a in arrays]
    out = pl.pallas_call(
        hosting_body, grid=grid, in_specs=list(in_specs) + [ANY] * n_ex, out_specs=list(out_specs) + [ANY] * n_ex,
        out_shape=list(out_shape) + ex_shapes,
        scratch_shapes=list(scratch_shapes) + [pltpu.SemaphoreType.DMA((3 * n_ex,)), pltpu.SemaphoreType.DMA((3 * n_ex,)),
                                               pltpu.SemaphoreType.DMA((n_ex,))],
        compiler_params=_params(*["arbitrary"] * len(grid)), name=name)(*operands, *arrays)
    return out[:n_out], out[n_out:]


def _mm(a, b, *, dims, grid, a_spec, b_spec, o_spec, out_shape, acc_shape, name, hosted=None):
    nk = grid[2]

    def body(a_ref, b_ref, o_ref, *scratch):
        p = lax.dot_general(a_ref[...], b_ref[...], dims, preferred_element_type=F32)
        if nk == 1:
            o_ref[...] = p.astype(o_ref.dtype)
        else:
            acc_ref = scratch[0]
            k = pl.program_id(2)

            @pl.when(k == 0)
            def _():
                acc_ref[...] = p

            @pl.when(k > 0)
            def _():
                acc_ref[...] += p

            @pl.when(k == nk - 1)
            def _():
                o_ref[...] = acc_ref[...].astype(o_ref.dtype)

    scratch = [] if nk == 1 else [pltpu.VMEM(acc_shape, F32)]
    res = _gridded_call(
        body, (a, b), grid=grid, in_specs=[a_spec, b_spec], out_specs=[o_spec], out_shape=[out_shape],
        scratch_shapes=scratch, semantics=("parallel", "parallel", "arbitrary"), name=name, hosted=hosted)
    return res[0] if hosted is None else (res[0][0], res[1])


def _proj(u, w_in, shard0, n_shard, out_dtype, name, hosted=None):
    s, d = u.shape
    ws = w_in.shape[-1]
    tm, tn = min(s, 512), min(ws, 1024)
    nj = ws // tn
    return _mm(
        u, w_in, dims=NN, grid=(s // tm, n_shard * nj, 1),
        a_spec=pl.BlockSpec((tm, d), lambda i, j, k: (i, 0)),
        b_spec=pl.BlockSpec((None, d, tn), lambda i, j, k: (shard0 + j // nj, 0, j % nj)),
        o_spec=pl.BlockSpec((tm, tn), lambda i, j, k: (i, j)),
        out_shape=jax.ShapeDtypeStruct((s, n_shard * ws), out_dtype), acc_shape=(tm, tn), name=name, hosted=hosted,
    )


def _out_proj(a, w_out, name):
    s, bdim = a.shape
    d = w_out.shape[-1]
    tm, tn = min(s, 512), min(d, 1024)
    return _mm(
        a, w_out, dims=NN, grid=(s // tm, d // tn, 1),
        a_spec=pl.BlockSpec((tm, bdim), lambda i, j, k: (i, 0)),
        b_spec=pl.BlockSpec((bdim, tn), lambda i, j, k: (0, j)),
        o_spec=pl.BlockSpec((tm, tn), lambda i, j, k: (i, j)),
        out_shape=jax.ShapeDtypeStruct((s, d), F32), acc_shape=(tm, tn), name=name,
    )


def _out_proj_bwd_act(dm, w_out, name):
    s, d = dm.shape
    bdim = w_out.shape[-2]
    tm, tn = min(s, 512), min(bdim, 1024)
    return _mm(
        dm, w_out, dims=NT, grid=(s // tm, bdim // tn, 1),
        a_spec=pl.BlockSpec((tm, d), lambda i, j, k: (i, 0)),
        b_spec=pl.BlockSpec((tn, d), lambda i, j, k: (j, 0)),
        o_spec=pl.BlockSpec((tm, tn), lambda i, j, k: (i, j)),
        out_shape=jax.ShapeDtypeStruct((s, bdim), F32), acc_shape=(tm, tn), name=name,
    )


def _weight_grad(act_t, dout, n_blocks, name):
    din, s = act_t.shape
    w = dout.shape[1] // n_blocks
    tm, tn = min(din, 512), min(w, 1024)
    nj = w // tn
    return _mm(
        act_t, dout, dims=NN, grid=(din // tm, n_blocks * nj, 1),
        a_spec=pl.BlockSpec((tm, s), lambda i, j, k: (i, 0)),
        b_spec=pl.BlockSpec((s, tn), lambda i, j, k: (0, j)),
        o_spec=pl.BlockSpec((None, tm, tn), lambda i, j, k: (j // nj, i, j % nj)),
        out_shape=jax.ShapeDtypeStruct((n_blocks, din, w), BF16), acc_shape=(tm, tn), name=name,
    )


def _proj_bwd_act(dproj, w_in, name, hosted=None):
    s = dproj.shape[0]
    n_shards, d, ws = w_in.shape
    tm, tn = min(s, 512), min(d, 512)

    def body(a_ref, b_ref, o_ref):
        acc = None
        for k in range(n_shards):
            p = lax.dot_general(a_ref[:, k * ws:(k + 1) * ws], b_ref[k], NT, preferred_element_type=F32)
            acc = p if acc is None else acc + p
        o_ref[...] = acc

    res = _gridded_call(
        body, (dproj, w_in), grid=(s // tm, d // tn),
        in_specs=[pl.BlockSpec((tm, n_shards * ws), lambda i, j: (i, 0)),
                  pl.BlockSpec((n_shards, tn, ws), lambda i, j: (0, j, 0))],
        out_specs=[pl.BlockSpec((tm, tn), lambda i, j: (i, j))], out_shape=[jax.ShapeDtypeStruct((s, d), F32)],
        scratch_shapes=[], semantics=("parallel", "parallel"), name=name, hosted=hosted)
    return res[0] if hosted is None else (res[0][0], res[1])


def _row_spec(tm, d):
    return pl.BlockSpec((tm, d), lambda i: (i, 0))


def _gain_spec(d):
    return pl.BlockSpec((1, d), lambda i: (0, 0))


def _rstd(x):
    return lax.rsqrt(jnp.mean(x * x, axis=-1, keepdims=True) + RMS_EPS)


def _rmsnorm_fwd(h, gain, after, name):
    s, d = h.shape
    tm = min(s, 512)
    n_after = len(after)

    def body(*refs):
        h_ref, g_ref = refs[:2]
        u_ref, ut_ref = refs[2 + n_after:]
        x = h_ref[...]
        u = (x * _rstd(x) * g_ref[...]).astype(u_ref.dtype)
        u_ref[...] = u
        ut_ref[...] = u.T

    return pl.pallas_call(
        body, grid=(s // tm,), in_specs=[_row_spec(tm, d), _gain_spec(d)] + [ANY] * n_after,
        out_specs=[_row_spec(tm, d), pl.BlockSpec((d, tm), lambda i: (0, i))],
        out_shape=[jax.ShapeDtypeStruct((s, d), BF16), jax.ShapeDtypeStruct((d, s), BF16)],
        compiler_params=_params("parallel"), name=name,
    )(h, gain, *after)


def _cast(block, dtype, name):
    r, c = block.shape
    tr = min(r, 256)

    def body(b_ref, o_ref):
        o_ref[...] = b_ref[...].astype(o_ref.dtype)

    spec = pl.BlockSpec((tr, c), lambda i: (i, 0))
    return pl.pallas_call(
        body, grid=(r // tr,), in_specs=[spec], out_specs=spec, out_shape=jax.ShapeDtypeStruct((r, c), dtype),
        compiler_params=_params("parallel"), name=name,
    )(block)


def _post_norm_residual(h, m, gain, name):
    s, d = h.shape
    tm = min(s, 512)

    def body(h_ref, m_ref, g_ref, o_ref):
        x = m_ref[...]
        o_ref[...] = h_ref[...] + x * _rstd(x) * g_ref[...]

    return pl.pallas_call(
        body, grid=(s // tm,), in_specs=[_row_spec(tm, d), _row_spec(tm, d), _gain_spec(d)],
        out_specs=_row_spec(tm, d), out_shape=jax.ShapeDtypeStruct((s, d), F32),
        compiler_params=_params("parallel"), name=name,
    )(h, m, gain)


def _sum_rows_into(acc_ref, x):
    tm, d = x.shape
    acc_ref[...] += jnp.sum(x.reshape(tm // SUBLANES, SUBLANES, d), axis=0)


def _norm_bwd_body(n_steps, with_residual, n_after=0):
    def body(*refs):
        n_in = 4 if with_residual else 3
        dy_ref, x_ref, g_ref = refs[:3]
        dres_ref = refs[3] if with_residual else None
        dx_ref, dg_ref, acc_ref = refs[n_in + n_after:]
        i = pl.program_id(0)

        @pl.when(i == 0)
        def _():
            acc_ref[...] = jnp.zeros_like(acc_ref)

        x = x_ref[...]
        dy = dy_ref[...]
        rstd = _rstd(x)
        n = x * rstd
        dn = dy * g_ref[...]
        dx = rstd * (dn - n * jnp.mean(dn * n, axis=-1, keepdims=True))
        if with_residual:
            dx = dres_ref[...] + dx
        dx_ref[...] = dx.astype(dx_ref.dtype)
        _sum_rows_into(acc_ref, dy * n)

        @pl.when(i == n_steps - 1)
        def _():
            dg_ref[...] = jnp.sum(acc_ref[...], axis=0, keepdims=True)

    return body


def _post_norm_bwd(dh, m, gain, after, name):
    s, d = m.shape
    tm = min(s, 512)
    n_steps = s // tm
    return pl.pallas_call(
        _norm_bwd_body(n_steps, False, len(after)), grid=(n_steps,),
        in_specs=[_row_spec(tm, d), _row_spec(tm, d), _gain_spec(d)] + [ANY] * len(after),
        out_specs=[_row_spec(tm, d), _gain_spec(d)],
        out_shape=[jax.ShapeDtypeStruct((s, d), BF16), jax.ShapeDtypeStruct((1, d), F32)],
        scratch_shapes=[pltpu.VMEM((SUBLANES, d), F32)], compiler_params=_params("arbitrary"), name=name,
    )(dh, m, gain, *after)


def _pre_norm_bwd(du, h, gain, dh, after, name):
    s, d = h.shape
    tm = min(s, 512)
    n_steps = s // tm
    return pl.pallas_call(
        _norm_bwd_body(n_steps, True, len(after)), grid=(n_steps,),
        in_specs=[_row_spec(tm, d), _row_spec(tm, d), _gain_spec(d), _row_spec(tm, d)] + [ANY] * len(after),
        out_specs=[_row_spec(tm, d), _gain_spec(d)],
        out_shape=[jax.ShapeDtypeStruct((s, d), F32), jax.ShapeDtypeStruct((1, d), F32)],
        scratch_shapes=[pltpu.VMEM((SUBLANES, d), F32)], compiler_params=_params("arbitrary"), name=name,
    )(du, h, gain, dh, *after)


def _loss_head(y, target, name):
    s, d = y.shape
    tm = min(s, 512)
    n_steps = s // tm

    def body(y_ref, t_ref, dy_ref, loss_ref, acc_ref):
        i = pl.program_id(0)

        @pl.when(i == 0)
        def _():
            acc_ref[...] = jnp.zeros_like(acc_ref)

        err = y_ref[...] - t_ref[...]
        dy_ref[...] = err / d
        _sum_rows_into(acc_ref, err * err)

        @pl.when(i == n_steps - 1)
        def _():
            total = jnp.sum(jnp.sum(acc_ref[...], axis=0, keepdims=True), axis=1, keepdims=True)
            loss_ref[...] = 0.5 * total / d

    return pl.pallas_call(
        body, grid=(n_steps,), in_specs=[_row_spec(tm, d), _row_spec(tm, d)],
        out_specs=[_row_spec(tm, d), pl.BlockSpec((1, 1), lambda i: (0, 0))],
        out_shape=[jax.ShapeDtypeStruct((s, d), F32), jax.ShapeDtypeStruct((1, 1), F32)],
        scratch_shapes=[pltpu.VMEM((SUBLANES, d), F32)], compiler_params=_params("arbitrary"), name=name,
    )(y, target)


def _shift_down(p, halo, row, n):
    out = jnp.where(row == 0, halo[SUBLANES - n:SUBLANES - n + 1], pltpu.roll(p, n, 0))
    if n == 2:
        out = jnp.where(row == 1, halo[SUBLANES - 1:SUBLANES], out)
    return out


def _shift_up(p, halo, row, n):
    tm = p.shape[0]
    out = jnp.where(row == tm - 1, halo[n - 1:n], pltpu.roll(p, tm - n, 0))
    if n == 2:
        out = jnp.where(row == tm - 2, halo[0:1], out)
    return out


def _conv_specs(tm, tc, nb, n_row_blocks):
    hb = tm // SUBLANES
    cur = lambda part: pl.BlockSpec((tm, tc), lambda i, j: (i, part * nb + j))
    prev = lambda part: pl.BlockSpec((SUBLANES, tc), lambda i, j: (jnp.maximum(i * hb - 1, 0), part * nb + j))
    nxt = lambda part: pl.BlockSpec(
        (SUBLANES, tc), lambda i, j: (jnp.minimum((i + 1) * hb, n_row_blocks * hb - 1), part * nb + j))
    return cur, prev, nxt


def _conv_gate_fwd(proj, conv_w, name):
    s, b4 = proj.shape
    bdim = b4 // 4
    tm, tc = min(s, 512), min(bdim, 512)
    nb = bdim // tc
    cur, prev, _ = _conv_specs(tm, tc, nb, s // tm)

    def body(b_ref, c_ref, x_ref, z_ref, cp_ref, xp_ref, w_ref, a_ref, at_ref):
        i = pl.program_id(0)
        row = lax.broadcasted_iota(jnp.int32, (tm, tc), 0)
        p = c_ref[...] * x_ref[...]
        halo = jnp.where(i > 0, cp_ref[...] * xp_ref[...], 0.0)
        w = w_ref[...]
        cv = w[0:1] * _shift_down(p, halo, row, 2) + w[1:2] * _shift_down(p, halo, row, 1) + w[2:3] * p
        silu, _ = _silu_parts(z_ref[...])
        a = (silu * (b_ref[...] * cv)).astype(a_ref.dtype)
        a_ref[...] = a
        at_ref[...] = a.T

    return pl.pallas_call(
        body, grid=(s // tm, nb),
        in_specs=[cur(0), cur(1), cur(2), cur(3), prev(1), prev(2), pl.BlockSpec((CONV_K, tc), lambda i, j: (0, j))],
        out_specs=[pl.BlockSpec((tm, tc), lambda i, j: (i, j)), pl.BlockSpec((tc, tm), lambda i, j: (j, i))],
        out_shape=[jax.ShapeDtypeStruct((s, bdim), BF16), jax.ShapeDtypeStruct((bdim, s), BF16)],
        compiler_params=_params("parallel", "parallel"), name=name,
    )(proj, proj, proj, proj, proj, proj, conv_w)


def _conv_gate_bwd(proj, da, conv_w, name):
    s, b4 = proj.shape
    bdim = b4 // 4
    tm, tc = min(s, 512), min(bdim, 512)
    nb = bdim // tc
    n_rows = s // tm
    cur, prev, nxt = _conv_specs(tm, tc, nb, n_rows)
    da_cur = pl.BlockSpec((tm, tc), lambda j, i: (i, j))
    hb = tm // SUBLANES
    da_nxt = pl.BlockSpec((SUBLANES, tc), lambda j, i: (jnp.minimum((i + 1) * hb, n_rows * hb - 1), j))
    swap = lambda spec: pl.BlockSpec(spec.block_shape, lambda j, i, f=spec.index_map: f(i, j))

    def body(b_ref, c_ref, x_ref, z_ref, cp_ref, xp_ref, bn_ref, zn_ref, da_ref, dan_ref, w_ref,
             db_ref, dc_ref, dx_ref, dz_ref, dw_ref, acc_ref):
        i = pl.program_id(1)

        @pl.when(i == 0)
        def _():
            acc_ref[...] = jnp.zeros_like(acc_ref)

        row = lax.broadcasted_iota(jnp.int32, (tm, tc), 0)
        w = w_ref[...]
        b, c, x = b_ref[...], c_ref[...], x_ref[...]
        p = c * x
        halo_p = jnp.where(i > 0, cp_ref[...] * xp_ref[...], 0.0)
        p1, p2 = _shift_down(p, halo_p, row, 1), _shift_down(p, halo_p, row, 2)
        cv = w[0:1] * p2 + w[1:2] * p1 + w[2:3] * p
        z = z_ref[...]
        silu, sig = _silu_parts(z)
        da = da_ref[...]
        dy = da * silu
        dcv = dy * b
        silu_n, _ = _silu_parts(zn_ref[...])
        halo_d = jnp.where(i < n_rows - 1, dan_ref[...] * silu_n * bn_ref[...], 0.0)
        dp = w[2:3] * dcv + w[1:2] * _shift_up(dcv, halo_d, row, 1) + w[0:1] * _shift_up(dcv, halo_d, row, 2)
        db_ref[...] = (dy * cv).astype(db_ref.dtype)
        dc_ref[...] = (dp * x).astype(dc_ref.dtype)
        dx_ref[...] = (dp * c).astype(dx_ref.dtype)
        dz_ref[...] = (da * (b * cv) * (sig * (1.0 + z * (1.0 - sig)))).astype(dz_ref.dtype)
        for k, pk in enumerate((p2, p1, p)):
            _sum_rows_into(acc_ref.at[k], dcv * pk)

        @pl.when(i == n_rows - 1)
        def _():
            for k in range(CONV_K):
                dw_ref[k:k + 1, :] = jnp.sum(acc_ref[k], axis=0, keepdims=True)

    out = pl.BlockSpec((tm, tc), lambda j, i: (i, j))
    act = jax.ShapeDtypeStruct((s, bdim), BF16)
    return pl.pallas_call(
        body, grid=(nb, n_rows),
        in_specs=[swap(cur(0)), swap(cur(1)), swap(cur(2)), swap(cur(3)), swap(prev(1)), swap(prev(2)),
                  swap(nxt(0)), swap(nxt(3)), da_cur, da_nxt, pl.BlockSpec((CONV_K, tc), lambda j, i: (0, j))],
        out_specs=[out, out, out, out, pl.BlockSpec((CONV_K, tc), lambda j, i: (0, j))],
        out_shape=[act, act, act, act, jax.ShapeDtypeStruct((CONV_K, bdim), F32)],
        scratch_shapes=[pltpu.VMEM((CONV_K, SUBLANES, tc), F32)],
        compiler_params=_params("parallel", "arbitrary"), name=name,
    )(proj, proj, proj, proj, proj, proj, proj, proj, da, da, conv_w)


def _split(x):
    hi = x.astype(BF16)
    lo = (x - hi.astype(F32)).astype(BF16)
    return jnp.concatenate([hi, lo], axis=1)


def _row_total(x, column):
    return jnp.broadcast_to(x[:, column:column + 1], (x.shape[0], LANES))


def _sb_tiles(qs, ks, carries, suffix_ones, masks, chain=0):
    items = range(len(qs))
    bk = ks[0].shape[0]
    scale = 1.0 / math.sqrt(HEAD_DIM)
    logits = [lax.dot_general(qs[n], ks[n], NT, preferred_element_type=F32) * scale for n in items]
    es = [jnp.exp(-jnp.abs(logits[n])) for n in items]
    keeps = []
    for n in items:
        log_keep = -(jnp.maximum(logits[n], 0.0) + jnp.log(1.0 + es[n]))
        if masks[n] is not None:
            log_keep = jnp.where(masks[n], log_keep, 0.0)
        keeps.append(_split(log_keep))
    tails = [lax.dot_general(keeps[n], suffix_ones, NN, preferred_element_type=F32) for n in items]
    ws, used = [], []
    for n in items:
        carry = carries[n] if n < len(carries) else used[n - chain] + _row_total(tails[n - chain], 0)
        used.append(carry)
        w = jnp.exp(logits[n] + tails[n] + (carry if carry.shape[1] == 1 else _lane_tile(carry, bk)))
        if masks[n] is not None:
            w = jnp.where(masks[n], w, 0.0)
        ws.append(w)
    return logits, es, tails, ws, used


def _tri_twice(n, upper):
    r = lax.broadcasted_iota(jnp.int32, (2 * n, n), 0)
    r = jnp.where(r >= n, r - n, r)
    c = lax.broadcasted_iota(jnp.int32, (2 * n, n), 1)
    return jnp.where(r <= c if upper else r >= c, 1.0, 0.0).astype(BF16)


def _group_spec(s, width, part, n_groups):
    return pl.BlockSpec((s, width), lambda h: (0, part * n_groups + h))


def _head_cols(g):
    return slice(g * HEAD_DIM, (g + 1) * HEAD_DIM)


def _lane_tile(x, n):
    return x if n == LANES else jnp.concatenate([x] * (n // LANES), axis=1)


def _sb_attn_fwd(qkv, name):
    s, b3 = qkv.shape
    bdim = b3 // 3
    hps = min(HEADS_PER_STEP, bdim // HEAD_DIM)
    width = hps * HEAD_DIM
    n_groups = bdim // width
    blk = min(s, 256)
    n_blk = s // blk

    def body(q_ref, k_ref, v_ref, o_ref, car_ref, carry_ref):
        suffix_ones = _tri_twice(blk, upper=False)
        r = lax.broadcasted_iota(jnp.int32, (blk, blk), 0)
        c = lax.broadcasted_iota(jnp.int32, (blk, blk), 1)
        diag_mask = c < r
        lane = lax.broadcasted_iota(jnp.int32, (blk, LANES), 1)

        def q_block(qi, _):
            q0 = pl.multiple_of(qi * blk, blk)
            rows = pl.ds(q0, blk)
            qs = [q_ref[rows, _head_cols(g)] for g in range(hps)]
            o_ref[rows, :] = jnp.zeros((blk, width), F32)
            car_ref[rows, :] = jnp.full((blk, width), UNVISITED, F32)
            carry_ref[...] = jnp.zeros_like(carry_ref)

            def step(js, tile_masks):
                k0s = [pl.multiple_of(j * blk, blk) for j in js]
                items = [(t, g) for t in range(len(js)) for g in range(hps)]
                ks = [k_ref[pl.ds(k0s[t], blk), _head_cols(g)] for t, g in items]
                first = [carry_ref[g] for g in range(hps)]
                _, _, tails, ws, carries = _sb_tiles([qs[g] for _, g in items], ks, first, suffix_ones,
                                                     [tile_masks[t] for t, _ in items], chain=hps)
                for g in range(hps):
                    mine = [n for n, (_, h) in enumerate(items) if h == g]
                    acc, saved = None, car_ref[rows, _head_cols(g)]
                    for n in mine:
                        v = v_ref[pl.ds(k0s[items[n][0]], blk), _head_cols(g)]
                        p = lax.dot_general(ws[n].astype(BF16), v, NN, preferred_element_type=F32)
                        acc = p if acc is None else acc + p
                        saved = jnp.where(lane == js[items[n][0]], carries[n], saved)
                    o_ref[rows, _head_cols(g)] += acc
                    car_ref[rows, _head_cols(g)] = saved
                    carry_ref[g] = carries[mine[-1]] + _row_total(tails[mine[-1]], 0)

            @pl.when(qi == 0)
            def _():
                step([0], [diag_mask])

            @pl.when(qi > 0)
            def _():
                step([qi, qi - 1], [diag_mask, None])

            def alive():
                top = jnp.max(jnp.max(carry_ref[...], axis=0), axis=0, keepdims=True)
                return (jnp.max(top, axis=1, keepdims=True)[0, 0] >= DEAD_CARRY).astype(jnp.int32)

            left = jnp.maximum(qi - 1, 0)

            def pair(state):
                p, _ = state
                j = qi - 2 - 2 * p
                step([j, j - 1], [None, None])
                return p + 1, alive()

            p, live = lax.while_loop(lambda state: (state[0] < left // 2) & (state[1] > 0), pair, (0, alive()))

            @pl.when((left % 2 == 1) & (p == left // 2) & (live > 0))
            def _():
                step([0], [None])

            return 0

        lax.fori_loop(0, n_blk, q_block, 0)

    out = pl.BlockSpec((s, width), lambda h: (0, h))
    shape = jax.ShapeDtypeStruct((s, bdim), F32)
    return pl.pallas_call(
        body, grid=(n_groups,),
        in_specs=[_group_spec(s, width, part, n_groups) for part in range(3)],
        out_specs=[out, out], out_shape=[shape, shape], scratch_shapes=[pltpu.VMEM((hps, blk, LANES), F32)],
        compiler_params=_params("parallel"), name=name,
    )(qkv, qkv, qkv)


def _sb_attn_bwd(qkv, do, carries, name):
    s, b3 = qkv.shape
    bdim = b3 // 3
    hps = min(HEADS_PER_STEP, bdim // HEAD_DIM)
    width = hps * HEAD_DIM
    n_groups = bdim // width
    blk = min(s, 256)
    n_blk = s // blk
    scale = 1.0 / math.sqrt(HEAD_DIM)

    def body(q_ref, k_ref, v_ref, do_ref, car_ref, dq_ref, dk_ref, dv_ref, dk_acc, dv_acc, dq_acc, before_ref):
        suffix_ones = _tri_twice(blk, upper=False)
        prefix_ones = _tri_twice(blk, upper=True)
        r = lax.broadcasted_iota(jnp.int32, (blk, blk), 0)
        c = lax.broadcasted_iota(jnp.int32, (blk, blk), 1)
        diag_mask = c < r
        lane = lax.broadcasted_iota(jnp.int32, (blk, LANES), 1)
        dk_acc[...] = jnp.zeros_like(dk_acc)
        dv_acc[...] = jnp.zeros_like(dv_acc)

        def q_block(qi, _):
            q0 = pl.multiple_of(qi * blk, blk)
            rows = pl.ds(q0, blk)
            qs = [q_ref[rows, _head_cols(g)] for g in range(hps)]
            dos = [do_ref[rows, _head_cols(g)] for g in range(hps)]
            dq_acc[...] = jnp.zeros_like(dq_acc)
            before_ref[...] = jnp.zeros_like(before_ref)

            def step(js, tile_masks):
                masks = [tile_masks[t] for t in range(len(js)) for _ in range(hps)]
                k0s = [pl.multiple_of(j * blk, blk) for j in js]
                items = [(t, g) for t in range(len(js)) for g in range(hps)]
                every = range(len(items))
                ks = [k_ref[pl.ds(k0s[t], blk), _head_cols(g)] for t, g in items]
                dws = [lax.dot_general(dos[g], v_ref[pl.ds(k0s[t], blk), _head_cols(g)], NT, preferred_element_type=F32)
                       for t, g in items]
                carries = [jnp.sum(jnp.where(lane == js[t], car_ref[rows, _head_cols(g)], 0.0), axis=1, keepdims=True)
                           for t, g in items]
                logits, es, _, ws, _ = _sb_tiles([qs[g] for _, g in items], ks, carries, suffix_ones, masks)
                gws = [dws[n] * ws[n] for n in every]
                g_upto = [lax.dot_general(_split(gws[n]), prefix_ones, NN, preferred_element_type=F32) for n in every]
                dss, befores = [], []
                for n, (t, g) in enumerate(items):
                    before = before_ref[g] if t == 0 else befores[n - hps] + _row_total(g_upto[n - hps], blk - 1)
                    befores.append(before)
                    sig = jnp.where(logits[n] >= 0.0, 1.0, es[n]) / (1.0 + es[n])
                    dlogits = gws[n] - sig * (_lane_tile(before, blk) + g_upto[n])
                    if masks[n] is not None:
                        dlogits = jnp.where(masks[n], dlogits, 0.0)
                    dss.append((dlogits * scale).astype(BF16))
                for g in range(hps):
                    mine = [n for n in every if items[n][1] == g]
                    dq = None
                    for n in mine:
                        k0 = k0s[items[n][0]]
                        p = lax.dot_general(dss[n], ks[n], NN, preferred_element_type=F32)
                        dq = p if dq is None else dq + p
                        dk_acc[pl.ds(k0, blk), _head_cols(g)] += lax.dot_general(
                            dss[n], qs[g], TN, preferred_element_type=F32)
                        dv_acc[pl.ds(k0, blk), _head_cols(g)] += lax.dot_general(
                            ws[n].astype(BF16), dos[g], TN, preferred_element_type=F32)
                    dq_acc[:, _head_cols(g)] += dq
                    before_ref[g] = befores[mine[-1]] + _row_total(g_upto[mine[-1]], blk - 1)

            top = car_ref[rows, _head_cols(0)]
            for g in range(1, hps):
                top = jnp.maximum(top, car_ref[rows, _head_cols(g)])
            top = jnp.max(top, axis=0, keepdims=True)
            lane_row = lax.broadcasted_iota(jnp.int32, (1, LANES), 1)
            counted = jnp.where((top >= DEAD_CARRY) & (lane_row < qi), 1.0, 0.0)
            n_alive = jnp.sum(counted, axis=1, keepdims=True)[0, 0].astype(jnp.int32)
            left = jnp.maximum(n_alive - 1, 0)
            start = qi - 1 - left

            @pl.when(left % 2 == 1)
            def _():
                step([start], [None])

            def pair(p, _):
                j = start + left % 2 + 2 * p
                step([j, j + 1], [None, None])
                return 0

            lax.fori_loop(0, left // 2, pair, 0)

            @pl.when(qi == 0)
            def _():
                step([0], [diag_mask])

            @pl.when(qi > 0)
            def _():
                step([qi - 1, qi], [None, diag_mask])
            dq_ref[rows, :] = dq_acc[...].astype(dq_ref.dtype)
            return 0

        lax.fori_loop(0, n_blk, q_block, 0)
        dk_ref[...] = dk_acc[...].astype(dk_ref.dtype)
        dv_ref[...] = dv_acc[...].astype(dv_ref.dtype)

    group = pl.BlockSpec((s, width), lambda h: (0, h))
    once = pl.BlockSpec((s, width), lambda h: (0, h), pipeline_mode=pl.Buffered(1))
    shape = jax.ShapeDtypeStruct((s, bdim), BF16)
    return pl.pallas_call(
        body, grid=(n_groups,),
        in_specs=[_group_spec(s, width, part, n_groups) for part in range(3)] + [once, once],
        out_specs=[group, group, group], out_shape=[shape, shape, shape],
        scratch_shapes=[pltpu.VMEM((s, width), F32), pltpu.VMEM((s, width), F32), pltpu.VMEM((blk, width), F32),
                        pltpu.VMEM((hps, blk, LANES), F32)],
        compiler_params=_params("parallel"), name=name,
    )(qkv, qkv, qkv, do, carries)


def _sb_gate_fwd(z, o, name):
    s, bdim = z.shape
    tm = min(s, 512)

    def body(z_ref, o_ref, a_ref, at_ref):
        silu, _ = _silu_parts(z_ref[...])
        a = (silu * o_ref[...]).astype(a_ref.dtype)
        a_ref[...] = a
        at_ref[...] = a.T

    return pl.pallas_call(
        body, grid=(s // tm,), in_specs=[_row_spec(tm, bdim), _row_spec(tm, bdim)],
        out_specs=[_row_spec(tm, bdim), pl.BlockSpec((bdim, tm), lambda i: (0, i))],
        out_shape=[jax.ShapeDtypeStruct((s, bdim), BF16), jax.ShapeDtypeStruct((bdim, s), BF16)],
        compiler_params=_params("parallel"), name=name,
    )(z, o)


def _sb_gate_bwd(da, z, o, name):
    s, bdim = z.shape
    tm = min(s, 512)

    def body(da_ref, z_ref, o_ref, do_ref, dz_ref):
        z = z_ref[...]
        da = da_ref[...]
        silu, sig = _silu_parts(z)
        do_ref[...] = (da * silu).astype(do_ref.dtype)
        dz_ref[...] = (da * o_ref[...] * (sig * (1.0 + z * (1.0 - sig)))).astype(dz_ref.dtype)

    spec = _row_spec(tm, bdim)
    shape = jax.ShapeDtypeStruct((s, bdim), BF16)
    return pl.pallas_call(
        body, grid=(s // tm,), in_specs=[spec, spec, spec], out_specs=[spec, spec], out_shape=[shape, shape],
        compiler_params=_params("parallel"), name=name,
    )(da, z, o)


def _into_slot(block, place, dtype, name):
    r, c = block.shape
    tr = min(r, 256)

    def body(place_ref, b_ref, o_ref):
        o_ref[...] = b_ref[...].astype(o_ref.dtype)

    grid_spec = pltpu.PrefetchScalarGridSpec(
        num_scalar_prefetch=1, grid=(r // tr,),
        in_specs=[pl.BlockSpec((tr, c), lambda i, place_ref: (i, 0))],
        out_specs=pl.BlockSpec((None, tr, c), lambda i, place_ref: (place_ref[0], i, 0)),
    )
    return pl.pallas_call(
        body, grid_spec=grid_spec, out_shape=jax.ShapeDtypeStruct((N_DEV, r, c), dtype),
        compiler_params=_params("parallel"), name=name,
    )(place, block)


def _add_core_pair(grads, received, core, name):
    _, _, r, c = grads.shape
    tr = min(r, 256)

    def body(core_ref, g_ref, r_ref, o_ref):
        o_ref[...] = (g_ref[...].astype(F32) + r_ref[...].astype(F32)).astype(o_ref.dtype)

    grid_spec = pltpu.PrefetchScalarGridSpec(
        num_scalar_prefetch=1, grid=(N_CHIP, r // tr),
        in_specs=[pl.BlockSpec((None, None, tr, c), lambda q, i, core_ref: (q, core_ref[0], i, 0)),
                  pl.BlockSpec((None, tr, c), lambda q, i, core_ref: (q, i, 0))],
        out_specs=pl.BlockSpec((None, tr, c), lambda q, i, core_ref: (q, i, 0)),
    )
    return pl.pallas_call(
        body, grid_spec=grid_spec, out_shape=jax.ShapeDtypeStruct((N_CHIP, r, c), BF16),
        compiler_params=_params("parallel", "parallel"), name=name,
    )(core, grads, received)


def _adamw_step(g, w, m, v, g_ref, d_ref, nm_ref, nv_ref):
    new_m = ADAM_B1 * m + (1.0 - ADAM_B1) * g
    new_v = ADAM_B2 * v + (1.0 - ADAM_B2) * (g * g)
    m_hat = new_m / (1.0 - ADAM_B1 ** ADAM_STEP)
    v_hat = new_v / (1.0 - ADAM_B2 ** ADAM_STEP)
    g_ref[...] = g
    d_ref[...] = -ADAM_LR * (m_hat / (jnp.sqrt(v_hat) + ADAM_EPS) + ADAM_WD * w)
    nm_ref[...] = new_m
    nv_ref[...] = new_v


def _adamw(w, parts, m, v, name):
    r, c = w.shape
    n_parts = parts.shape[0]
    tr = min(r, 256)

    def body(w_ref, p_ref, m_ref, v_ref, *out_refs):
        g = p_ref[0].astype(F32)
        for k in range(1, n_parts):
            g = g + p_ref[k].astype(F32)
        _adamw_step(g, w_ref[...], m_ref[...], v_ref[...], *out_refs)

    spec = pl.BlockSpec((tr, c), lambda i: (i, 0))
    shape = jax.ShapeDtypeStruct((r, c), F32)
    return pl.pallas_call(
        body, grid=(r // tr,), in_specs=[spec, pl.BlockSpec((n_parts, tr, c), lambda i: (0, i, 0)), spec, spec],
        out_specs=[spec] * 4, out_shape=[shape] * 4, compiler_params=_params("parallel"), name=name,
    )(w, parts, m, v)


def _adamw_shard(w, grads, landed, m, v, place, name):
    r, c = w.shape
    n_landed = landed.shape[0]
    tr = min(r, 256)

    def body(place_ref, w_ref, own_ref, l_ref, m_ref, v_ref, *out_refs):
        g = own_ref[...].astype(F32)
        for k in range(n_landed):
            g = g + l_ref[k].astype(F32)
        _adamw_step(g, w_ref[...], m_ref[...], v_ref[...], *out_refs)

    spec = pl.BlockSpec((tr, c), lambda i, place_ref: (i, 0))
    grid_spec = pltpu.PrefetchScalarGridSpec(
        num_scalar_prefetch=1, grid=(r // tr,),
        in_specs=[spec, pl.BlockSpec((None, tr, c), lambda i, place_ref: (place_ref[0], i, 0)),
                  pl.BlockSpec((n_landed, tr, c), lambda i, place_ref: (0, i, 0)), spec, spec],
        out_specs=[spec] * 4,
    )
    return pl.pallas_call(
        body, grid_spec=grid_spec, out_shape=[jax.ShapeDtypeStruct((r, c), F32)] * 4,
        compiler_params=_params("parallel"), name=name,
    )(place, w, grads, landed, m, v)


def _place():
    x, y, c = lax.axis_index("x"), lax.axis_index("y"), lax.axis_index("c")
    other_chips = [(1 - x, y), (x, 1 - y), (1 - x, 1 - y)]
    return x, y, c, other_chips


def _all_gather(blocks, name, after=()):
    n_arr = len(blocks)
    items = [(a, i) for a, blk in enumerate(blocks) for i in range(blk.shape[0])]
    n_items = len(items)

    def body(*refs):
        srcs, refs = refs[:n_arr], refs[n_arr + len(after):]
        outs = refs[:n_arr]
        send_sems, recv_sems, local_sems = refs[n_arr:]
        x, y, c, other_chips = _place()
        me, sibling = (x, y, c), (x, y, 1 - c)

        def slot(it, dev):
            a, i = items[it]
            return outs[a].at[i, 4 * dev[0] + 2 * dev[1] + dev[2]]

        def copy(it, k, block_of, to, from_src=False):
            a, i = items[it]
            return pltpu.make_async_remote_copy(
                src_ref=srcs[a].at[i] if from_src else slot(it, block_of), dst_ref=slot(it, block_of),
                send_sem=send_sems.at[it * 7 + k], recv_sem=recv_sems.at[it * 7 + k],
                device_id=to, device_id_type=MESH)

        own = [pltpu.make_async_copy(srcs[items[it][0]].at[items[it][1]], slot(it, me), local_sems.at[it])
               for it in range(n_items)]
        for cp in own:
            cp.start()
        first = []
        for it in range(n_items):
            first.append(copy(it, 0, me, sibling, from_src=True))
            first += [copy(it, 1 + j, me, (*chip, c), from_src=True) for j, chip in enumerate(other_chips)]
        for cp in first:
            cp.start()
        passed = []
        for it in range(n_items):
            for j, chip in enumerate(other_chips):
                copy(it, 1 + j, (*chip, c), me).wait_recv()
                passed.append(copy(it, 4 + j, (*chip, c), sibling))
                passed[-1].start()
        for it in range(n_items):
            copy(it, 0, sibling, me).wait_recv()
            for j, chip in enumerate(other_chips):
                copy(it, 4 + j, (*chip, 1 - c), me).wait_recv()
        for cp in first + passed:
            cp.wait_send()
        for cp in own:
            cp.wait()

    return pl.pallas_call(
        body, in_specs=[ANY] * (n_arr + len(after)), out_specs=[ANY] * n_arr,
        out_shape=[jax.ShapeDtypeStruct((b.shape[0], N_DEV) + b.shape[1:], b.dtype) for b in blocks],
        scratch_shapes=[pltpu.SemaphoreType.DMA((7 * n_items,)), pltpu.SemaphoreType.DMA((7 * n_items,)),
                        pltpu.SemaphoreType.DMA((n_items,))],
        name=name,
    )(*blocks, *after)


def _exchange_core_pair(grads, name):
    n_arr = len(grads)

    def body(*refs):
        srcs, outs = refs[:n_arr], refs[n_arr:2 * n_arr]
        send_sems, recv_sems = refs[2 * n_arr:]
        x, y, c, _ = _place()
        copies = [
            pltpu.make_async_remote_copy(
                src_ref=srcs[a].at[q, 1 - c], dst_ref=outs[a].at[q],
                send_sem=send_sems.at[a * N_CHIP + q], recv_sem=recv_sems.at[a * N_CHIP + q],
                device_id=(x, y, 1 - c), device_id_type=MESH)
            for a in range(n_arr) for q in range(N_CHIP)]
        for cp in copies:
            cp.start()
        for cp in copies:
            cp.wait_recv()
        for cp in copies:
            cp.wait_send()

    return pl.pallas_call(
        body, in_specs=[ANY] * n_arr, out_specs=[ANY] * n_arr,
        out_shape=[jax.ShapeDtypeStruct((N_CHIP,) + g.shape[2:], g.dtype) for g in grads],
        scratch_shapes=[pltpu.SemaphoreType.DMA((N_CHIP * n_arr,)), pltpu.SemaphoreType.DMA((N_CHIP * n_arr,))],
        name=name,
    )(*grads)


def _chip_copies(srcs, outs, send_sems, recv_sems, local_sems, scatter):
    x, y, c, other_chips = _place()
    my_chip = 2 * x + y
    pick = (lambda a, chip: srcs[a].at[chip]) if scatter else (lambda a, chip: srcs[a])
    local = [pltpu.make_async_copy(pick(a, my_chip), outs[a].at[my_chip], local_sems.at[a]) for a in range(len(srcs))]
    remote = [
        pltpu.make_async_remote_copy(
            src_ref=pick(a, 2 * chip[0] + chip[1]), dst_ref=outs[a].at[my_chip],
            send_sem=send_sems.at[a * 3 + j], recv_sem=recv_sems.at[a * 3 + j],
            device_id=(*chip, c), device_id_type=MESH)
        for a in range(len(srcs)) for j, chip in enumerate(other_chips)]
    return remote, local


def _start_all(remote, local):
    for cp in local + remote:
        cp.start()


def _wait_all(remote, local):
    for cp in remote:
        cp.wait_recv()
    for cp in remote:
        cp.wait_send()
    for cp in local:
        cp.wait()


def _gather_core_pair(blocks, name):
    n_arr = len(blocks)

    def body(*refs):
        srcs, outs = refs[:n_arr], refs[n_arr:2 * n_arr]
        send_sems, recv_sems, local_sems = refs[2 * n_arr:]
        x, y, c, _ = _place()
        pairs = [(a, q) for a in range(n_arr) for q in range(N_CHIP)]
        local = [pltpu.make_async_copy(srcs[a].at[q], outs[a].at[q, c], local_sems.at[k]) for k, (a, q) in enumerate(pairs)]
        remote = [
            pltpu.make_async_remote_copy(
                src_ref=srcs[a].at[q], dst_ref=outs[a].at[q, c], send_sem=send_sems.at[k], recv_sem=recv_sems.at[k],
                device_id=(x, y, 1 - c), device_id_type=MESH)
            for k, (a, q) in enumerate(pairs)]
        _start_all(remote, local)
        _wait_all(remote, local)

    n_sem = N_CHIP * n_arr
    return pl.pallas_call(
        body, in_specs=[ANY] * n_arr, out_specs=[ANY] * n_arr,
        out_shape=[jax.ShapeDtypeStruct((N_CHIP, 2) + b.shape[1:], b.dtype) for b in blocks],
        scratch_shapes=[pltpu.SemaphoreType.DMA((n_sem,))] * 3, name=name,
    )(*blocks)


HBM = pl.BlockSpec(memory_space=pltpu.HBM)
SEM = pl.BlockSpec(memory_space=pltpu.SEMAPHORE)
DATAFLOW = pltpu.SideEffectType.DATAFLOW_SIDE_EFFECTING
N_PEERS = N_DEV - 1
FLIPS = [(dx, dy, dc) for dx in (0, 1) for dy in (0, 1) for dc in (0, 1) if (dx, dy, dc) != (0, 0, 0)]


def _peers():
    x, y, c = lax.axis_index("x"), lax.axis_index("y"), lax.axis_index("c")
    flip = lambda v, d: 1 - v if d else v
    return 4 * x + 2 * y + c, [(flip(x, dx), flip(y, dy), flip(c, dc)) for dx, dy, dc in FLIPS]


def _lin(p):
    return 4 * p[0] + 2 * p[1] + p[2]


def _in_hbm(a):
    return pltpu.with_memory_space_constraint(a, pltpu.HBM)


def _token_spec():
    return pl.BlockSpec(memory_space=pltpu.VMEM), jax.ShapeDtypeStruct((SUBLANES, LANES), F32)


def _gather_copy(land_ref, send_sems, recv_sems, k, me, peer, landed_from):
    return pltpu.make_async_remote_copy(
        src_ref=land_ref.at[me], dst_ref=land_ref.at[me if landed_from is None else landed_from],
        send_sem=send_sems.at[k], recv_sem=recv_sems.at[k], device_id=peer, device_id_type=MESH)


def _gather_start(groups, after, name):
    flat = [a for g in groups for a in g]
    n, ng, n_after = len(flat), len(groups), len(after)
    token_spec, token_shape = _token_spec()

    def body(*refs):
        land, outs = refs[:n], refs[n + n_after:]
        sems, token_ref = outs[:2 * ng], outs[2 * ng + n]
        me, peers = _peers()
        a = 0
        for gi, group in enumerate(groups):
            for i in range(len(group)):
                for r, peer in enumerate(peers):
                    _gather_copy(land[a], sems[2 * gi], sems[2 * gi + 1], i * N_PEERS + r, me, peer, None).start()
                a += 1
        token_ref[...] = jnp.zeros_like(token_ref)

    sem_shapes = [pltpu.SemaphoreType.DMA((N_PEERS * len(g),)) for g in groups for _ in (0, 1)]
    out = pl.pallas_call(
        body, name=name, in_specs=[HBM] * n + [ANY] * n_after,
        out_specs=[SEM] * (2 * ng) + [HBM] * n + [token_spec],
        out_shape=sem_shapes + [pltpu.HBM(a.shape, a.dtype) for a in flat] + [token_shape],
        input_output_aliases={a: 2 * ng + a for a in range(n)},
        compiler_params=pltpu.CompilerParams(has_side_effects=DATAFLOW),
    )(*[_in_hbm(a) for a in flat], *after)
    sems = [(out[2 * gi], out[2 * gi + 1]) for gi in range(ng)]
    thru, a = [], 2 * ng
    for g in groups:
        thru.append(list(out[a:a + len(g)]))
        a += len(g)
    return sems, thru, out[-1]


def _gather_wait(lands, sems, after, name):
    n = len(lands)

    def body(*refs):
        land, send_sems, recv_sems = refs[:n], refs[n], refs[n + 1]
        me, peers = _peers()
        for i in range(n):
            for r, peer in enumerate(peers):
                cp = _gather_copy(land[i], send_sems, recv_sems, i * N_PEERS + r, me, peer, _lin(peer))
                cp.wait_send()
                cp.wait_recv()

    return pl.pallas_call(
        body, name=name, in_specs=[HBM] * n + [SEM, SEM] + [ANY] * len(after), out_specs=[HBM] * n,
        out_shape=[pltpu.HBM(a.shape, a.dtype) for a in lands], input_output_aliases={i: i for i in range(n)},
        compiler_params=pltpu.CompilerParams(has_side_effects=DATAFLOW),
    )(*lands, *sems, *after)


def _scatter_copy(grad_ref, land_ref, send_sems, recv_sems, k, me, peer, start):
    mine, theirs = (me, _lin(peer)) if start else (_lin(peer), me)
    return pltpu.make_async_remote_copy(
        src_ref=grad_ref.at[_lin(peer)], dst_ref=land_ref.at[lax.rem(mine - theirs + N_PEERS + N_DEV, N_DEV)],
        send_sem=send_sems.at[k], recv_sem=recv_sems.at[k], device_id=peer, device_id_type=MESH)


def _scatter_start(grads, name):
    n = len(grads)
    lands = [lax.empty((N_PEERS,) + g.shape[1:], g.dtype) for g in grads]
    token_spec, token_shape = _token_spec()

    def body(*refs):
        grad, land, send_sems, recv_sems = refs[:n], refs[n:2 * n], refs[2 * n], refs[2 * n + 1]
        token_ref = refs[4 * n + 2]
        me, peers = _peers()
        for i in range(n):
            for r, peer in enumerate(peers):
                _scatter_copy(grad[i], land[i], send_sems, recv_sems, i * N_PEERS + r, me, peer, True).start()
        token_ref[...] = jnp.zeros_like(token_ref)

    sem_shape = pltpu.SemaphoreType.DMA((N_PEERS * n,))
    out = pl.pallas_call(
        body, name=name, in_specs=[HBM] * (2 * n), out_specs=[SEM, SEM] + [HBM] * (2 * n) + [token_spec],
        out_shape=[sem_shape, sem_shape] + [pltpu.HBM(a.shape, a.dtype) for a in grads + lands] + [token_shape],
        input_output_aliases={a: 2 + a for a in range(2 * n)},
        compiler_params=pltpu.CompilerParams(has_side_effects=DATAFLOW),
    )(*[_in_hbm(a) for a in grads + lands])
    return (out[0], out[1]), list(out[2:2 + n]), list(out[2 + n:2 + 2 * n]), out[-1]


def _scatter_wait(grads, lands, sems, after, name):
    n = len(grads)

    def body(*refs):
        grad, land, send_sems, recv_sems = refs[:n], refs[n:2 * n], refs[2 * n], refs[2 * n + 1]
        me, peers = _peers()
        for i in range(n):
            for r, peer in enumerate(peers):
                cp = _scatter_copy(grad[i], land[i], send_sems, recv_sems, i * N_PEERS + r, me, peer, False)
                cp.wait_send()
                cp.wait_recv()

    out = pl.pallas_call(
        body, name=name, in_specs=[HBM] * (2 * n) + [SEM, SEM] + [ANY] * len(after), out_specs=[HBM] * (2 * n),
        out_shape=[pltpu.HBM(a.shape, a.dtype) for a in grads + lands],
        input_output_aliases={a: a for a in range(2 * n)},
        compiler_params=pltpu.CompilerParams(has_side_effects=DATAFLOW),
    )(*grads, *lands, *sems, *after)
    return list(out[:n]), list(out[n:])


def kernel(x, ln_pre_0, conv_w_in_0, conv_w_0, conv_w_out_0, ln_post_0, ln_pre_1, sb_w_in_1, sb_w_out_1, ln_post_1, ln_pre_2, conv_w_in_2, conv_w_2, conv_w_out_2, ln_post_2, ln_pre_3, sb_w_in_3, sb_w_out_3, ln_post_3, loss_target, m_ln_pre_0, m_conv_w_in_0, m_conv_w_0, m_conv_w_out_0, m_ln_post_0, m_ln_pre_1, m_sb_w_in_1, m_sb_w_out_1, m_ln_post_1, m_ln_pre_2, m_conv_w_in_2, m_conv_w_2, m_conv_w_out_2, m_ln_post_2, m_ln_pre_3, m_sb_w_in_3, m_sb_w_out_3, m_ln_post_3, v_ln_pre_0, v_conv_w_in_0, v_conv_w_0, v_conv_w_out_0, v_ln_post_0, v_ln_pre_1, v_sb_w_in_1, v_sb_w_out_1, v_ln_post_1, v_ln_pre_2, v_conv_w_in_2, v_conv_w_2, v_conv_w_out_2, v_ln_post_2, v_ln_pre_3, v_sb_w_in_3, v_sb_w_out_3, v_ln_post_3):
    names = ['ln_pre_0', 'conv_w_in_0', 'conv_w_0', 'conv_w_out_0', 'ln_post_0', 'ln_pre_1', 'sb_w_in_1', 'sb_w_out_1',
             'ln_post_1', 'ln_pre_2', 'conv_w_in_2', 'conv_w_2', 'conv_w_out_2', 'ln_post_2', 'ln_pre_3', 'sb_w_in_3',
             'sb_w_out_3', 'ln_post_3']
    given = dict(locals())
    w = {n: given[n] for n in names}
    mom = {n: given["m_" + n] for n in names}
    var = {n: given["v_" + n] for n in names}
    conv_layers = [i for i in range(DEPTH) if i % 2 == 0]
    w_in_names = [("conv_w_in_%d" if i % 2 == 0 else "sb_w_in_%d") % i for i in range(DEPTH)]
    w_out_names = [("conv_w_out_%d" if i % 2 == 0 else "sb_w_out_%d") % i for i in range(DEPTH)]

    s, d = x.shape[1:]
    h = x.reshape(s, d)
    target = loss_target.reshape(s, d)
    gains = {n: w[n].reshape(1, d) for n in names if n.startswith("ln_")}
    place = 4 * lax.axis_index("x") + 2 * lax.axis_index("y") + lax.axis_index("c")
    place_arr = place.astype(jnp.int32).reshape(1)
    bdim = w[w_out_names[0]].shape[0] * N_DEV
    wc = bdim // N_DEV

    conv_rows = jnp.concatenate([w["conv_w_%d" % i] for i in conv_layers], axis=0)
    shards = {n: _cast(w[n], BF16, "cast_" + n) for n in w_in_names + w_out_names}
    first = _all_gather([shards[w_in_names[0]][None], shards[w_out_names[0]][None], conv_rows[None]], "gather_first_layer")
    conv_all = first[2][0].reshape(N_DEV, len(conv_layers), CONV_K, wc)
    conv_all = jnp.transpose(conv_all, (1, 2, 0, 3)).reshape(len(conv_layers), CONV_K, bdim)
    conv_full = {layer: conv_all[n] for n, layer in enumerate(conv_layers)}
    weights = [(first[0][0], first[1][0].reshape(bdim, d))]

    saved = []
    for i in range(DEPTH):
        w_in, w_out = weights[i]
        nxt = ([shards[w_in_names[i + 1]], shards[w_out_names[i + 1]]], False) if i + 1 < DEPTH else None
        u, u_t = _rmsnorm_fwd(h, gains["ln_pre_%d" % i], [], "pre_norm_%d" % i)
        if i % 2 == 0:
            proj = _proj(u, w_in, 0, N_DEV, F32, "proj_%d" % i, hosted=nxt)
            if nxt:
                proj, crossed = proj
            a, a_t = _conv_gate_fwd(proj, conv_full[i], "conv_gate_%d" % i)
            extra = (proj,)
        else:
            qkv = _proj(u, w_in, 0, 6, BF16, "proj_qkv_%d" % i, hosted=nxt)
            if nxt:
                qkv, crossed = qkv
            z = _proj(u, w_in, 6, 2, F32, "proj_z_%d" % i)
            o, carries = _sb_attn_fwd(qkv, "sb_attn_%d" % i)
            a, a_t = _sb_gate_fwd(z, o, "sb_gate_%d" % i)
            extra = (qkv, z, o, carries)
        if nxt:
            both = _gather_core_pair(crossed, "gather_core_pair_%d" % (i + 1))
            weights.append((both[0].reshape((N_DEV,) + both[0].shape[2:]), both[1].reshape(bdim, d)))
        m = _out_proj(a, w_out, "out_proj_%d" % i)
        saved.append((h, u_t, a_t, m, extra))
        h = _post_norm_residual(h, m, gains["ln_post_%d" % i], "post_norm_%d" % i)

    dh, loss = _loss_head(h, target, "loss_head")
    loss = lax.psum(loss[0, 0], ("x", "y", "c"))

    small = {}
    chip_parts = {}
    core = lax.axis_index("c").astype(jnp.int32).reshape(1)
    for i in reversed(range(DEPTH)):
        h_in, u_t, a_t, m, extra = saved[i]
        w_in, w_out = weights[i]
        dm, small["ln_post_%d" % i] = _post_norm_bwd(dh, m, gains["ln_post_%d" % i], [], "post_norm_bwd_%d" % i)
        g_out = _weight_grad(a_t, dm, 1, "grad_w_out_%d" % i).reshape(N_CHIP, 2, wc, d)
        da = _out_proj_bwd_act(dm, w_out, "out_proj_bwd_%d" % i)
        if i % 2 == 0:
            (proj,) = extra
            db, dc, dxt, dz, small["conv_w_%d" % i] = _conv_gate_bwd(proj, da, conv_full[i], "conv_gate_bwd_%d" % i)
            dproj = jnp.concatenate([db, dc, dxt, dz], axis=1)
        else:
            qkv, z, o, carries = extra
            do, dz = _sb_gate_bwd(da, z, o, "sb_gate_bwd_%d" % i)
            dq, dk, dv = _sb_attn_bwd(qkv, do, carries, "sb_attn_bwd_%d" % i)
            dproj = jnp.concatenate([dq, dk, dv, dz], axis=1)
        g_in = _weight_grad(u_t, dproj, N_DEV, "grad_w_in_%d" % i)
        g_in = g_in.reshape((N_CHIP, 2) + g_in.shape[1:])
        from_sibling = _exchange_core_pair([g_in, g_out], "reduce_core_pair_%d" % i)
        pair_sums = [_add_core_pair(g, r, core, "add_core_pair_%s_%d" % (kind, i))
                     for g, r, kind in zip((g_in, g_out), from_sibling, ("in", "out"))]
        du, (chip_parts[w_in_names[i]], chip_parts[w_out_names[i]]) = _proj_bwd_act(
            dproj, w_in, "proj_bwd_%d" % i, hosted=(pair_sums, True))
        dh, small["ln_pre_%d" % i] = _pre_norm_bwd(du, h_in, gains["ln_pre_%d" % i], dh, [], "pre_norm_bwd_%d" % i)
    big_names = w_in_names + w_out_names

    gain_names = [n for n in names if n.startswith("ln_")]
    conv_names = ["conv_w_%d" % i for i in conv_layers]
    rows = [small[n] for n in gain_names] + [small[n] for n in conv_names]
    n_rows = len(gain_names) + CONV_K * len(conv_names)
    pad = -n_rows % SUBLANES
    stacked = jnp.concatenate(rows + [jnp.zeros((pad, d), F32)], axis=0)
    (small_all,) = _all_gather([stacked[None]], "gather_small_grads")
    small_all = small_all[0]

    out_g, out_d, out_m, out_v = {}, {}, {}, {}

    def update(n, w2, parts, m2, v2, shape):
        g2, d2, nm2, nv2 = _adamw(w2, parts, m2, v2, "adamw_" + n)
        out_g[n], out_d[n], out_m[n], out_v[n] = (t.reshape(shape) for t in (g2, d2, nm2, nv2))

    for n in big_names:
        update(n, w[n], chip_parts[n], mom[n], var[n], w[n].shape)
    n_gain = len(gain_names)
    stack = lambda src: jnp.stack([src[n] for n in gain_names])
    g2, d2, nm2, nv2 = _adamw(stack(w), small_all[:, :n_gain], stack(mom), stack(var), "adamw_gains")
    for k, n in enumerate(gain_names):
        out_g[n], out_d[n], out_m[n], out_v[n] = g2[k], d2[k], nm2[k], nv2[k]
    wc = bdim // N_DEV
    for k, n in enumerate(conv_names):
        rows_k = small_all[:, n_gain + CONV_K * k:n_gain + CONV_K * (k + 1)]
        parts = lax.dynamic_slice_in_dim(rows_k, place * wc, wc, axis=2)
        update(n, w[n], parts, mom[n], var[n], w[n].shape)

    grad_x = dh.reshape(x.shape)
    return (loss, grad_x, *[out_g[n] for n in names], *[out_d[n] for n in names],
            *[out_m[n] for n in names], *[out_v[n] for n in names])
```

```python
import functools
import math

import jax
import jax.numpy as jnp
from jax import lax
from jax.experimental import pallas as pl
from jax.experimental.pallas import tpu as pltpu

F32 = jnp.float32
BF16 = jnp.bfloat16
MESH = pl.DeviceIdType.MESH
ANY = pl.BlockSpec(memory_space=pl.ANY)

N_DEV = 8
N_CHIP = 4
DEPTH = 4
HEAD_DIM = 128
CONV_K = 3
RMS_EPS = 1e-6
ADAM_LR = 0.001
ADAM_B1 = 0.9
ADAM_B2 = 0.999
ADAM_EPS = 1e-08
ADAM_WD = 0.01
ADAM_STEP = 10

V7X_VMEM_BYTES = 64 * 1024 * 1024
VMEM_LIMIT = V7X_VMEM_BYTES * 3 // 4
LANES = 128
SUBLANES = 8
HEADS_PER_STEP = 2
DEAD_CARRY = -128.0
UNVISITED = -1e30


def _params(*sem):
    return pltpu.CompilerParams(dimension_semantics=sem, vmem_limit_bytes=VMEM_LIMIT)


def _silu_parts(z):
    sig = jax.nn.sigmoid(z)
    return z * sig, sig


NN = (((1,), (0,)), ((), ()))
NT = (((1,), (1,)), ((), ()))
TN = (((0,), (0,)), ((), ()))


def _gridded_call(body, operands, *, grid, in_specs, out_specs, out_shape, scratch_shapes, semantics, name, hosted=None):
    if hosted is None:
        return pl.pallas_call(
            body, grid=grid, in_specs=in_specs, out_specs=out_specs, out_shape=out_shape,
            scratch_shapes=scratch_shapes, compiler_params=_params(*semantics), name=name)(*operands)
    arrays, scatter = hosted
    n_in, n_out, n_ex, n_scr = len(in_specs), len(out_specs), len(arrays), len(scratch_shapes)

    def hosting_body(*refs):
        ins, refs = refs[:n_in], refs[n_in:]
        ex_in, refs = refs[:n_ex], refs[n_ex:]
        outs, refs = refs[:n_out], refs[n_out:]
        ex_out, refs = refs[:n_ex], refs[n_ex:]
        scratch, sems = refs[:n_scr], refs[n_scr:]
        first = last = None
        for axis, size in enumerate(grid):
            at_start, at_end = pl.program_id(axis) == 0, pl.program_id(axis) == size - 1
            first = at_start if first is None else first & at_start
            last = at_end if last is None else last & at_end

        @pl.when(first)
        def _():
            for cp in _chip_copies(ex_in, ex_out, *sems, scatter):
                cp.start()

        body(*ins, *outs, *scratch)

        @pl.when(last)
        def _():
            _wait_all(_chip_copies(ex_in, ex_out, *sems, scatter))

    if scatter:
        ex_shapes, aliases = [jax.ShapeDtypeStruct((N_CHIP - 1,) + a.shape[1:], a.dtype) for a in arrays], {}
    else:
        ex_shapes, aliases = [jax.ShapeDtypeStruct(a.shape, a.dtype) for a in arrays], {n_in + a: n_out + a for a in range(n_ex)}
    out = pl.pallas_call(
        hosting_body, grid=grid, in_specs=list(in_specs) + [ANY] * n_ex, out_specs=list(out_specs) + [ANY] * n_ex,
        out_shape=list(out_shape) + ex_shapes, input_output_aliases=aliases,
        scratch_shapes=list(scratch_shapes) + [pltpu.SemaphoreType.DMA((3 * n_ex,)), pltpu.SemaphoreType.DMA((3 * n_ex,))],
        compiler_params=_params(*["arbitrary"] * len(grid)), name=name)(*operands, *arrays)
    return out[:n_out], out[n_out:]


def _mm(a, b, *, dims, grid, a_spec, b_spec, o_spec, out_shape, acc_shape, name, hosted=None):
    nk = grid[2]

    def body(a_ref, b_ref, o_ref, *scratch):
        p = lax.dot_general(a_ref[...], b_ref[...], dims, preferred_element_type=F32)
        if nk == 1:
            o_ref[...] = p.astype(o_ref.dtype)
        else:
            acc_ref = scratch[0]
            k = pl.program_id(2)

            @pl.when(k == 0)
            def _():
                acc_ref[...] = p

            @pl.when(k > 0)
            def _():
                acc_ref[...] += p

            @pl.when(k == nk - 1)
            def _():
                o_ref[...] = acc_ref[...].astype(o_ref.dtype)

    scratch = [] if nk == 1 else [pltpu.VMEM(acc_shape, F32)]
    res = _gridded_call(
        body, (a, b), grid=grid, in_specs=[a_spec, b_spec], out_specs=[o_spec], out_shape=[out_shape],
        scratch_shapes=scratch, semantics=("parallel", "parallel", "arbitrary"), name=name, hosted=hosted)
    return res[0] if hosted is None else (res[0][0], res[1])


def _proj(u, w_in, shard0, n_shard, out_dtype, name, hosted=None):
    s, d = u.shape
    ws = w_in.shape[-1]
    tm, tn = min(s, 512), min(ws, 1024)
    nj = ws // tn
    return _mm(
        u, w_in, dims=NN, grid=(s // tm, n_shard * nj, 1),
        a_spec=pl.BlockSpec((tm, d), lambda i, j, k: (i, 0)),
        b_spec=pl.BlockSpec((None, d, tn), lambda i, j, k: (shard0 + j // nj, 0, j % nj)),
        o_spec=pl.BlockSpec((tm, tn), lambda i, j, k: (i, j)),
        out_shape=jax.ShapeDtypeStruct((s, n_shard * ws), out_dtype), acc_shape=(tm, tn), name=name, hosted=hosted,
    )


def _out_proj(a, w_out, name):
    s, bdim = a.shape
    d = w_out.shape[-1]
    tm, tn = min(s, 512), min(d, 1024)
    return _mm(
        a, w_out, dims=NN, grid=(s // tm, d // tn, 1),
        a_spec=pl.BlockSpec((tm, bdim), lambda i, j, k: (i, 0)),
        b_spec=pl.BlockSpec((bdim, tn), lambda i, j, k: (0, j)),
        o_spec=pl.BlockSpec((tm, tn), lambda i, j, k: (i, j)),
        out_shape=jax.ShapeDtypeStruct((s, d), F32), acc_shape=(tm, tn), name=name,
    )


def _out_proj_bwd_act(dm, w_out, name):
    s, d = dm.shape
    bdim = w_out.shape[-2]
    tm, tn = min(s, 512), min(bdim, 1024)
    return _mm(
        dm, w_out, dims=NT, grid=(s // tm, bdim // tn, 1),
        a_spec=pl.BlockSpec((tm, d), lambda i, j, k: (i, 0)),
        b_spec=pl.BlockSpec((tn, d), lambda i, j, k: (j, 0)),
        o_spec=pl.BlockSpec((tm, tn), lambda i, j, k: (i, j)),
        out_shape=jax.ShapeDtypeStruct((s, bdim), F32), acc_shape=(tm, tn), name=name,
    )


def _weight_grad(act_t, dout, n_blocks, name):
    din, s = act_t.shape
    w = dout.shape[1] // n_blocks
    tm, tn = min(din, 512), min(w, 1024)
    nj = w // tn
    return _mm(
        act_t, dout, dims=NN, grid=(din // tm, n_blocks * nj, 1),
        a_spec=pl.BlockSpec((tm, s), lambda i, j, k: (i, 0)),
        b_spec=pl.BlockSpec((s, tn), lambda i, j, k: (0, j)),
        o_spec=pl.BlockSpec((None, tm, tn), lambda i, j, k: (j // nj, i, j % nj)),
        out_shape=jax.ShapeDtypeStruct((n_blocks, din, w), BF16), acc_shape=(tm, tn), name=name,
    )


def _proj_bwd_act(dproj, w_in, name, hosted=None):
    s = dproj.shape[0]
    n_shards, d, ws = w_in.shape
    tm, tn = min(s, 512), min(d, 512)

    def body(a_ref, b_ref, o_ref):
        acc = None
        for k in range(n_shards):
            p = lax.dot_general(a_ref[:, k * ws:(k + 1) * ws], b_ref[k], NT, preferred_element_type=F32)
            acc = p if acc is None else acc + p
        o_ref[...] = acc

    res = _gridded_call(
        body, (dproj, w_in), grid=(s // tm, d // tn),
        in_specs=[pl.BlockSpec((tm, n_shards * ws), lambda i, j: (i, 0)),
                  pl.BlockSpec((n_shards, tn, ws), lambda i, j: (0, j, 0))],
        out_specs=[pl.BlockSpec((tm, tn), lambda i, j: (i, j))], out_shape=[jax.ShapeDtypeStruct((s, d), F32)],
        scratch_shapes=[], semantics=("parallel", "parallel"), name=name, hosted=hosted)
    return res[0] if hosted is None else (res[0][0], res[1])


def _row_spec(tm, d):
    return pl.BlockSpec((tm, d), lambda i: (i, 0))


def _gain_spec(d):
    return pl.BlockSpec((1, d), lambda i: (0, 0))


def _rstd(x):
    return lax.rsqrt(jnp.mean(x * x, axis=-1, keepdims=True) + RMS_EPS)


def _rmsnorm_fwd(h, gain, after, name):
    s, d = h.shape
    tm = min(s, 512)
    n_after = len(after)

    def body(*refs):
        h_ref, g_ref = refs[:2]
        u_ref, ut_ref = refs[2 + n_after:]
        x = h_ref[...]
        u = (x * _rstd(x) * g_ref[...]).astype(u_ref.dtype)
        u_ref[...] = u
        ut_ref[...] = u.T

    return pl.pallas_call(
        body, grid=(s // tm,), in_specs=[_row_spec(tm, d), _gain_spec(d)] + [ANY] * n_after,
        out_specs=[_row_spec(tm, d), pl.BlockSpec((d, tm), lambda i: (0, i))],
        out_shape=[jax.ShapeDtypeStruct((s, d), BF16), jax.ShapeDtypeStruct((d, s), BF16)],
        compiler_params=_params("parallel"), name=name,
    )(h, gain, *after)


def _cast(block, dtype, name):
    r, c = block.shape
    tr = min(r, 256)

    def body(b_ref, o_ref):
        o_ref[...] = b_ref[...].astype(o_ref.dtype)

    spec = pl.BlockSpec((tr, c), lambda i: (i, 0))
    return pl.pallas_call(
        body, grid=(r // tr,), in_specs=[spec], out_specs=spec, out_shape=jax.ShapeDtypeStruct((r, c), dtype),
        compiler_params=_params("parallel"), name=name,
    )(block)


def _post_norm_residual(h, m, gain, name):
    s, d = h.shape
    tm = min(s, 512)

    def body(h_ref, m_ref, g_ref, o_ref):
        x = m_ref[...]
        o_ref[...] = h_ref[...] + x * _rstd(x) * g_ref[...]

    return pl.pallas_call(
        body, grid=(s // tm,), in_specs=[_row_spec(tm, d), _row_spec(tm, d), _gain_spec(d)],
        out_specs=_row_spec(tm, d), out_shape=jax.ShapeDtypeStruct((s, d), F32),
        compiler_params=_params("parallel"), name=name,
    )(h, m, gain)


def _sum_rows_into(acc_ref, x):
    tm, d = x.shape
    acc_ref[...] += jnp.sum(x.reshape(tm // SUBLANES, SUBLANES, d), axis=0)


def _norm_bwd_body(n_steps, with_residual, n_after=0):
    def body(*refs):
        n_in = 4 if with_residual else 3
        dy_ref, x_ref, g_ref = refs[:3]
        dres_ref = refs[3] if with_residual else None
        dx_ref, dg_ref, acc_ref = refs[n_in + n_after:]
        i = pl.program_id(0)

        @pl.when(i == 0)
        def _():
            acc_ref[...] = jnp.zeros_like(acc_ref)

        x = x_ref[...]
        dy = dy_ref[...]
        rstd = _rstd(x)
        n = x * rstd
        dn = dy * g_ref[...]
        dx = rstd * (dn - n * jnp.mean(dn * n, axis=-1, keepdims=True))
        if with_residual:
            dx = dres_ref[...] + dx
        dx_ref[...] = dx.astype(dx_ref.dtype)
        _sum_rows_into(acc_ref, dy * n)

        @pl.when(i == n_steps - 1)
        def _():
            dg_ref[...] = jnp.sum(acc_ref[...], axis=0, keepdims=True)

    return body


def _post_norm_bwd(dh, m, gain, after, name):
    s, d = m.shape
    tm = min(s, 512)
    n_steps = s // tm
    return pl.pallas_call(
        _norm_bwd_body(n_steps, False, len(after)), grid=(n_steps,),
        in_specs=[_row_spec(tm, d), _row_spec(tm, d), _gain_spec(d)] + [ANY] * len(after),
        out_specs=[_row_spec(tm, d), _gain_spec(d)],
        out_shape=[jax.ShapeDtypeStruct((s, d), BF16), jax.ShapeDtypeStruct((1, d), F32)],
        scratch_shapes=[pltpu.VMEM((SUBLANES, d), F32)], compiler_params=_params("arbitrary"), name=name,
    )(dh, m, gain, *after)


def _pre_norm_bwd(du, h, gain, dh, after, name):
    s, d = h.shape
    tm = min(s, 512)
    n_steps = s // tm
    return pl.pallas_call(
        _norm_bwd_body(n_steps, True, len(after)), grid=(n_steps,),
        in_specs=[_row_spec(tm, d), _row_spec(tm, d), _gain_spec(d), _row_spec(tm, d)] + [ANY] * len(after),
        out_specs=[_row_spec(tm, d), _gain_spec(d)],
        out_shape=[jax.ShapeDtypeStruct((s, d), F32), jax.ShapeDtypeStruct((1, d), F32)],
        scratch_shapes=[pltpu.VMEM((SUBLANES, d), F32)], compiler_params=_params("arbitrary"), name=name,
    )(du, h, gain, dh, *after)


def _loss_head(y, target, name):
    s, d = y.shape
    tm = min(s, 512)
    n_steps = s // tm

    def body(y_ref, t_ref, dy_ref, loss_ref, acc_ref):
        i = pl.program_id(0)

        @pl.when(i == 0)
        def _():
            acc_ref[...] = jnp.zeros_like(acc_ref)

        err = y_ref[...] - t_ref[...]
        dy_ref[...] = err / d
        _sum_rows_into(acc_ref, err * err)

        @pl.when(i == n_steps - 1)
        def _():
            total = jnp.sum(jnp.sum(acc_ref[...], axis=0, keepdims=True), axis=1, keepdims=True)
            loss_ref[...] = 0.5 * total / d

    return pl.pallas_call(
        body, grid=(n_steps,), in_specs=[_row_spec(tm, d), _row_spec(tm, d)],
        out_specs=[_row_spec(tm, d), pl.BlockSpec((1, 1), lambda i: (0, 0))],
        out_shape=[jax.ShapeDtypeStruct((s, d), F32), jax.ShapeDtypeStruct((1, 1), F32)],
        scratch_shapes=[pltpu.VMEM((SUBLANES, d), F32)], compiler_params=_params("arbitrary"), name=name,
    )(y, target)


def _shift_down(p, halo, row, n):
    out = jnp.where(row == 0, halo[SUBLANES - n:SUBLANES - n + 1], pltpu.roll(p, n, 0))
    if n == 2:
        out = jnp.where(row == 1, halo[SUBLANES - 1:SUBLANES], out)
    return out


def _shift_up(p, halo, row, n):
    tm = p.shape[0]
    out = jnp.where(row == tm - 1, halo[n - 1:n], pltpu.roll(p, tm - n, 0))
    if n == 2:
        out = jnp.where(row == tm - 2, halo[0:1], out)
    return out


def _conv_specs(tm, tc, nb, n_row_blocks):
    hb = tm // SUBLANES
    cur = lambda part: pl.BlockSpec((tm, tc), lambda i, j: (i, part * nb + j))
    prev = lambda part: pl.BlockSpec((SUBLANES, tc), lambda i, j: (jnp.maximum(i * hb - 1, 0), part * nb + j))
    nxt = lambda part: pl.BlockSpec(
        (SUBLANES, tc), lambda i, j: (jnp.minimum((i + 1) * hb, n_row_blocks * hb - 1), part * nb + j))
    return cur, prev, nxt


def _conv_gate_fwd(proj, conv_w, name):
    s, b4 = proj.shape
    bdim = b4 // 4
    tm, tc = min(s, 512), min(bdim, 512)
    nb = bdim // tc
    cur, prev, _ = _conv_specs(tm, tc, nb, s // tm)

    def body(b_ref, c_ref, x_ref, z_ref, cp_ref, xp_ref, w_ref, a_ref, at_ref):
        i = pl.program_id(0)
        row = lax.broadcasted_iota(jnp.int32, (tm, tc), 0)
        p = c_ref[...] * x_ref[...]
        halo = jnp.where(i > 0, cp_ref[...] * xp_ref[...], 0.0)
        w = w_ref[...]
        cv = w[0:1] * _shift_down(p, halo, row, 2) + w[1:2] * _shift_down(p, halo, row, 1) + w[2:3] * p
        silu, _ = _silu_parts(z_ref[...])
        a = (silu * (b_ref[...] * cv)).astype(a_ref.dtype)
        a_ref[...] = a
        at_ref[...] = a.T

    return pl.pallas_call(
        body, grid=(s // tm, nb),
        in_specs=[cur(0), cur(1), cur(2), cur(3), prev(1), prev(2), pl.BlockSpec((CONV_K, tc), lambda i, j: (0, j))],
        out_specs=[pl.BlockSpec((tm, tc), lambda i, j: (i, j)), pl.BlockSpec((tc, tm), lambda i, j: (j, i))],
        out_shape=[jax.ShapeDtypeStruct((s, bdim), BF16), jax.ShapeDtypeStruct((bdim, s), BF16)],
        compiler_params=_params("parallel", "parallel"), name=name,
    )(proj, proj, proj, proj, proj, proj, conv_w)


def _conv_gate_bwd(proj, da, conv_w, name):
    s, b4 = proj.shape
    bdim = b4 // 4
    tm, tc = min(s, 512), min(bdim, 512)
    nb = bdim // tc
    n_rows = s // tm
    cur, prev, nxt = _conv_specs(tm, tc, nb, n_rows)
    da_cur = pl.BlockSpec((tm, tc), lambda j, i: (i, j))
    hb = tm // SUBLANES
    da_nxt = pl.BlockSpec((SUBLANES, tc), lambda j, i: (jnp.minimum((i + 1) * hb, n_rows * hb - 1), j))
    swap = lambda spec: pl.BlockSpec(spec.block_shape, lambda j, i, f=spec.index_map: f(i, j))

    def body(b_ref, c_ref, x_ref, z_ref, cp_ref, xp_ref, bn_ref, zn_ref, da_ref, dan_ref, w_ref,
             db_ref, dc_ref, dx_ref, dz_ref, dw_ref, acc_ref):
        i = pl.program_id(1)

        @pl.when(i == 0)
        def _():
            acc_ref[...] = jnp.zeros_like(acc_ref)

        row = lax.broadcasted_iota(jnp.int32, (tm, tc), 0)
        w = w_ref[...]
        b, c, x = b_ref[...], c_ref[...], x_ref[...]
        p = c * x
        halo_p = jnp.where(i > 0, cp_ref[...] * xp_ref[...], 0.0)
        p1, p2 = _shift_down(p, halo_p, row, 1), _shift_down(p, halo_p, row, 2)
        cv = w[0:1] * p2 + w[1:2] * p1 + w[2:3] * p
        z = z_ref[...]
        silu, sig = _silu_parts(z)
        da = da_ref[...]
        dy = da * silu
        dcv = dy * b
        silu_n, _ = _silu_parts(zn_ref[...])
        halo_d = jnp.where(i < n_rows - 1, dan_ref[...] * silu_n * bn_ref[...], 0.0)
        dp = w[2:3] * dcv + w[1:2] * _shift_up(dcv, halo_d, row, 1) + w[0:1] * _shift_up(dcv, halo_d, row, 2)
        db_ref[...] = (dy * cv).astype(db_ref.dtype)
        dc_ref[...] = (dp * x).astype(dc_ref.dtype)
        dx_ref[...] = (dp * c).astype(dx_ref.dtype)
        dz_ref[...] = (da * (b * cv) * (sig * (1.0 + z * (1.0 - sig)))).astype(dz_ref.dtype)
        for k, pk in enumerate((p2, p1, p)):
            _sum_rows_into(acc_ref.at[k], dcv * pk)

        @pl.when(i == n_rows - 1)
        def _():
            for k in range(CONV_K):
                dw_ref[k:k + 1, :] = jnp.sum(acc_ref[k], axis=0, keepdims=True)

    out = pl.BlockSpec((tm, tc), lambda j, i: (i, j))
    act = jax.ShapeDtypeStruct((s, bdim), BF16)
    return pl.pallas_call(
        body, grid=(nb, n_rows),
        in_specs=[swap(cur(0)), swap(cur(1)), swap(cur(2)), swap(cur(3)), swap(prev(1)), swap(prev(2)),
                  swap(nxt(0)), swap(nxt(3)), da_cur, da_nxt, pl.BlockSpec((CONV_K, tc), lambda j, i: (0, j))],
        out_specs=[out, out, out, out, pl.BlockSpec((CONV_K, tc), lambda j, i: (0, j))],
        out_shape=[act, act, act, act, jax.ShapeDtypeStruct((CONV_K, bdim), F32)],
        scratch_shapes=[pltpu.VMEM((CONV_K, SUBLANES, tc), F32)],
        compiler_params=_params("parallel", "arbitrary"), name=name,
    )(proj, proj, proj, proj, proj, proj, proj, proj, da, da, conv_w)


def _split(x):
    hi = x.astype(BF16)
    lo = (x - hi.astype(F32)).astype(BF16)
    return jnp.concatenate([hi, lo], axis=1)


def _row_total(x, column):
    return jnp.broadcast_to(x[:, column:column + 1], (x.shape[0], LANES))


def _sb_tiles(qs, ks, carries, suffix_ones, masks, chain=0):
    items = range(len(qs))
    bk = ks[0].shape[0]
    scale = 1.0 / math.sqrt(HEAD_DIM)
    logits = [lax.dot_general(qs[n], ks[n], NT, preferred_element_type=F32) * scale for n in items]
    es = [jnp.exp(-jnp.abs(logits[n])) for n in items]
    keeps = []
    for n in items:
        log_keep = -(jnp.maximum(logits[n], 0.0) + jnp.log(1.0 + es[n]))
        if masks[n] is not None:
            log_keep = jnp.where(masks[n], log_keep, 0.0)
        keeps.append(_split(log_keep))
    tails = [lax.dot_general(keeps[n], suffix_ones, NN, preferred_element_type=F32) for n in items]
    ws, used = [], []
    for n in items:
        carry = carries[n] if n < len(carries) else used[n - chain] + _row_total(tails[n - chain], 0)
        used.append(carry)
        w = jnp.exp(logits[n] + tails[n] + (carry if carry.shape[1] == 1 else _lane_tile(carry, bk)))
        if masks[n] is not None:
            w = jnp.where(masks[n], w, 0.0)
        ws.append(w)
    return logits, es, tails, ws, used


def _tri_twice(n, upper):
    r = lax.broadcasted_iota(jnp.int32, (2 * n, n), 0)
    r = jnp.where(r >= n, r - n, r)
    c = lax.broadcasted_iota(jnp.int32, (2 * n, n), 1)
    return jnp.where(r <= c if upper else r >= c, 1.0, 0.0).astype(BF16)


def _group_spec(s, width, part, n_groups):
    return pl.BlockSpec((s, width), lambda h: (0, part * n_groups + h))


def _head_cols(g):
    return slice(g * HEAD_DIM, (g + 1) * HEAD_DIM)


def _lane_tile(x, n):
    return x if n == LANES else jnp.concatenate([x] * (n // LANES), axis=1)


def _sb_attn_fwd(qkv, name):
    s, b3 = qkv.shape
    bdim = b3 // 3
    hps = min(HEADS_PER_STEP, bdim // HEAD_DIM)
    width = hps * HEAD_DIM
    n_groups = bdim // width
    blk = min(s, 256)
    n_blk = s // blk

    def body(q_ref, k_ref, v_ref, o_ref, car_ref, carry_ref):
        suffix_ones = _tri_twice(blk, upper=False)
        r = lax.broadcasted_iota(jnp.int32, (blk, blk), 0)
        c = lax.broadcasted_iota(jnp.int32, (blk, blk), 1)
        diag_mask = c < r
        lane = lax.broadcasted_iota(jnp.int32, (blk, LANES), 1)

        def q_block(qi, _):
            q0 = pl.multiple_of(qi * blk, blk)
            rows = pl.ds(q0, blk)
            qs = [q_ref[rows, _head_cols(g)] for g in range(hps)]
            o_ref[rows, :] = jnp.zeros((blk, width), F32)
            car_ref[rows, :] = jnp.full((blk, width), UNVISITED, F32)
            carry_ref[...] = jnp.zeros_like(carry_ref)

            def step(js, tile_masks):
                k0s = [pl.multiple_of(j * blk, blk) for j in js]
                items = [(t, g) for t in range(len(js)) for g in range(hps)]
                ks = [k_ref[pl.ds(k0s[t], blk), _head_cols(g)] for t, g in items]
                first = [carry_ref[g] for g in range(hps)]
                _, _, tails, ws, carries = _sb_tiles([qs[g] for _, g in items], ks, first, suffix_ones,
                                                     [tile_masks[t] for t, _ in items], chain=hps)
                for g in range(hps):
                    mine = [n for n, (_, h) in enumerate(items) if h == g]
                    acc, saved = None, car_ref[rows, _head_cols(g)]
                    for n in mine:
                        v = v_ref[pl.ds(k0s[items[n][0]], blk), _head_cols(g)]
                        p = lax.dot_general(ws[n].astype(BF16), v, NN, preferred_element_type=F32)
                        acc = p if acc is None else acc + p
                        saved = jnp.where(lane == js[items[n][0]], carries[n], saved)
                    o_ref[rows, _head_cols(g)] += acc
                    car_ref[rows, _head_cols(g)] = saved
                    carry_ref[g] = carries[mine[-1]] + _row_total(tails[mine[-1]], 0)

            @pl.when(qi == 0)
            def _():
                step([0], [diag_mask])

            @pl.when(qi > 0)
            def _():
                step([qi, qi - 1], [diag_mask, None])

            def alive():
                top = jnp.max(jnp.max(carry_ref[...], axis=0), axis=0, keepdims=True)
                return (jnp.max(top, axis=1, keepdims=True)[0, 0] >= DEAD_CARRY).astype(jnp.int32)

            left = jnp.maximum(qi - 1, 0)

            def pair(state):
                p, _ = state
                j = qi - 2 - 2 * p
                step([j, j - 1], [None, None])
                return p + 1, alive()

            p, live = lax.while_loop(lambda state: (state[0] < left // 2) & (state[1] > 0), pair, (0, alive()))

            @pl.when((left % 2 == 1) & (p == left // 2) & (live > 0))
            def _():
                step([0], [None])

            return 0

        lax.fori_loop(0, n_blk, q_block, 0)

    out = pl.BlockSpec((s, width), lambda h: (0, h))
    shape = jax.ShapeDtypeStruct((s, bdim), F32)
    return pl.pallas_call(
        body, grid=(n_groups,),
        in_specs=[_group_spec(s, width, part, n_groups) for part in range(3)],
        out_specs=[out, out], out_shape=[shape, shape], scratch_shapes=[pltpu.VMEM((hps, blk, LANES), F32)],
        compiler_params=_params("parallel"), name=name,
    )(qkv, qkv, qkv)


def _sb_attn_bwd(qkv, do, carries, name):
    s, b3 = qkv.shape
    bdim = b3 // 3
    hps = min(HEADS_PER_STEP, bdim // HEAD_DIM)
    width = hps * HEAD_DIM
    n_groups = bdim // width
    blk = min(s, 256)
    n_blk = s // blk
    scale = 1.0 / math.sqrt(HEAD_DIM)

    def body(q_ref, k_ref, v_ref, do_ref, car_ref, dq_ref, dk_ref, dv_ref, dk_acc, dv_acc, dq_acc, before_ref):
        suffix_ones = _tri_twice(blk, upper=False)
        prefix_ones = _tri_twice(blk, upper=True)
        r = lax.broadcasted_iota(jnp.int32, (blk, blk), 0)
        c = lax.broadcasted_iota(jnp.int32, (blk, blk), 1)
        diag_mask = c < r
        lane = lax.broadcasted_iota(jnp.int32, (blk, LANES), 1)
        dk_acc[...] = jnp.zeros_like(dk_acc)
        dv_acc[...] = jnp.zeros_like(dv_acc)

        def q_block(qi, _):
            q0 = pl.multiple_of(qi * blk, blk)
            rows = pl.ds(q0, blk)
            qs = [q_ref[rows, _head_cols(g)] for g in range(hps)]
            dos = [do_ref[rows, _head_cols(g)] for g in range(hps)]
            dq_acc[...] = jnp.zeros_like(dq_acc)
            before_ref[...] = jnp.zeros_like(before_ref)

            def step(js, tile_masks):
                masks = [tile_masks[t] for t in range(len(js)) for _ in range(hps)]
                k0s = [pl.multiple_of(j * blk, blk) for j in js]
                items = [(t, g) for t in range(len(js)) for g in range(hps)]
                every = range(len(items))
                ks = [k_ref[pl.ds(k0s[t], blk), _head_cols(g)] for t, g in items]
                dws = [lax.dot_general(dos[g], v_ref[pl.ds(k0s[t], blk), _head_cols(g)], NT, preferred_element_type=F32)
                       for t, g in items]
                carries = [jnp.sum(jnp.where(lane == js[t], car_ref[rows, _head_cols(g)], 0.0), axis=1, keepdims=True)
                           for t, g in items]
                logits, es, _, ws, _ = _sb_tiles([qs[g] for _, g in items], ks, carries, suffix_ones, masks)
                gws = [dws[n] * ws[n] for n in every]
                g_upto = [lax.dot_general(_split(gws[n]), prefix_ones, NN, preferred_element_type=F32) for n in every]
                dss, befores = [], []
                for n, (t, g) in enumerate(items):
                    before = before_ref[g] if t == 0 else befores[n - hps] + _row_total(g_upto[n - hps], blk - 1)
                    befores.append(before)
                    sig = jnp.where(logits[n] >= 0.0, 1.0, es[n]) / (1.0 + es[n])
                    dlogits = gws[n] - sig * (_lane_tile(before, blk) + g_upto[n])
                    if masks[n] is not None:
                        dlogits = jnp.where(masks[n], dlogits, 0.0)
                    dss.append((dlogits * scale).astype(BF16))
                for g in range(hps):
                    mine = [n for n in every if items[n][1] == g]
                    dq = None
                    for n in mine:
                        k0 = k0s[items[n][0]]
                        p = lax.dot_general(dss[n], ks[n], NN, preferred_element_type=F32)
                        dq = p if dq is None else dq + p
                        dk_acc[pl.ds(k0, blk), _head_cols(g)] += lax.dot_general(
                            dss[n], qs[g], TN, preferred_element_type=F32)
                        dv_acc[pl.ds(k0, blk), _head_cols(g)] += lax.dot_general(
                            ws[n].astype(BF16), dos[g], TN, preferred_element_type=F32)
                    dq_acc[:, _head_cols(g)] += dq
                    before_ref[g] = befores[mine[-1]] + _row_total(g_upto[mine[-1]], blk - 1)

            top = car_ref[rows, _head_cols(0)]
            for g in range(1, hps):
                top = jnp.maximum(top, car_ref[rows, _head_cols(g)])
            top = jnp.max(top, axis=0, keepdims=True)
            lane_row = lax.broadcasted_iota(jnp.int32, (1, LANES), 1)
            counted = jnp.where((top >= DEAD_CARRY) & (lane_row < qi), 1.0, 0.0)
            n_alive = jnp.sum(counted, axis=1, keepdims=True)[0, 0].astype(jnp.int32)
            left = jnp.maximum(n_alive - 1, 0)
            start = qi - 1 - left

            @pl.when(left % 2 == 1)
            def _():
                step([start], [None])

            def pair(p, _):
                j = start + left % 2 + 2 * p
                step([j, j + 1], [None, None])
                return 0

            lax.fori_loop(0, left // 2, pair, 0)

            @pl.when(qi == 0)
            def _():
                step([0], [diag_mask])

            @pl.when(qi > 0)
            def _():
                step([qi - 1, qi], [None, diag_mask])
            dq_ref[rows, :] = dq_acc[...].astype(dq_ref.dtype)
            return 0

        lax.fori_loop(0, n_blk, q_block, 0)
        dk_ref[...] = dk_acc[...].astype(dk_ref.dtype)
        dv_ref[...] = dv_acc[...].astype(dv_ref.dtype)

    group = pl.BlockSpec((s, width), lambda h: (0, h))
    once = pl.BlockSpec((s, width), lambda h: (0, h), pipeline_mode=pl.Buffered(1))
    shape = jax.ShapeDtypeStruct((s, bdim), BF16)
    return pl.pallas_call(
        body, grid=(n_groups,),
        in_specs=[_group_spec(s, width, part, n_groups) for part in range(3)] + [once, once],
        out_specs=[group, group, group], out_shape=[shape, shape, shape],
        scratch_shapes=[pltpu.VMEM((s, width), F32), pltpu.VMEM((s, width), F32), pltpu.VMEM((blk, width), F32),
                        pltpu.VMEM((hps, blk, LANES), F32)],
        compiler_params=_params("parallel"), name=name,
    )(qkv, qkv, qkv, do, carries)


def _sb_gate_fwd(z, o, name):
    s, bdim = z.shape
    tm = min(s, 512)

    def body(z_ref, o_ref, a_ref, at_ref):
        silu, _ = _silu_parts(z_ref[...])
        a = (silu * o_ref[...]).astype(a_ref.dtype)
        a_ref[...] = a
        at_ref[...] = a.T

    return pl.pallas_call(
        body, grid=(s // tm,), in_specs=[_row_spec(tm, bdim), _row_spec(tm, bdim)],
        out_specs=[_row_spec(tm, bdim), pl.BlockSpec((bdim, tm), lambda i: (0, i))],
        out_shape=[jax.ShapeDtypeStruct((s, bdim), BF16), jax.ShapeDtypeStruct((bdim, s), BF16)],
        compiler_params=_params("parallel"), name=name,
    )(z, o)


def _sb_gate_bwd(da, z, o, name):
    s, bdim = z.shape
    tm = min(s, 512)

    def body(da_ref, z_ref, o_ref, do_ref, dz_ref):
        z = z_ref[...]
        da = da_ref[...]
        silu, sig = _silu_parts(z)
        do_ref[...] = (da * silu).astype(do_ref.dtype)
        dz_ref[...] = (da * o_ref[...] * (sig * (1.0 + z * (1.0 - sig)))).astype(dz_ref.dtype)

    spec = _row_spec(tm, bdim)
    shape = jax.ShapeDtypeStruct((s, bdim), BF16)
    return pl.pallas_call(
        body, grid=(s // tm,), in_specs=[spec, spec, spec], out_specs=[spec, spec], out_shape=[shape, shape],
        compiler_params=_params("parallel"), name=name,
    )(da, z, o)


def _into_slot(block, place, dtype, name):
    r, c = block.shape
    tr = min(r, 256)

    def body(place_ref, b_ref, o_ref):
        o_ref[...] = b_ref[...].astype(o_ref.dtype)

    grid_spec = pltpu.PrefetchScalarGridSpec(
        num_scalar_prefetch=1, grid=(r // tr,),
        in_specs=[pl.BlockSpec((tr, c), lambda i, place_ref: (i, 0))],
        out_specs=pl.BlockSpec((None, tr, c), lambda i, place_ref: (place_ref[0], i, 0)),
    )
    return pl.pallas_call(
        body, grid_spec=grid_spec, out_shape=jax.ShapeDtypeStruct((N_DEV, r, c), dtype),
        compiler_params=_params("parallel"), name=name,
    )(place, block)


def _add_core_pair(grads, received, core, name):
    _, _, r, c = grads.shape
    tr = min(r, 256)

    def body(core_ref, g_ref, r_ref, o_ref):
        o_ref[...] = (g_ref[...].astype(F32) + r_ref[...].astype(F32)).astype(o_ref.dtype)

    grid_spec = pltpu.PrefetchScalarGridSpec(
        num_scalar_prefetch=1, grid=(N_CHIP, r // tr),
        in_specs=[pl.BlockSpec((None, None, tr, c), lambda q, i, core_ref: (q, core_ref[0], i, 0)),
                  pl.BlockSpec((None, tr, c), lambda q, i, core_ref: (q, i, 0))],
        out_specs=pl.BlockSpec((None, tr, c), lambda q, i, core_ref: (q, i, 0)),
    )
    return pl.pallas_call(
        body, grid_spec=grid_spec, out_shape=jax.ShapeDtypeStruct((N_CHIP, r, c), BF16),
        compiler_params=_params("parallel", "parallel"), name=name,
    )(core, grads, received)


def _adamw_step(g, w, m, v, g_ref, d_ref, nm_ref, nv_ref):
    new_m = ADAM_B1 * m + (1.0 - ADAM_B1) * g
    new_v = ADAM_B2 * v + (1.0 - ADAM_B2) * (g * g)
    m_hat = new_m / (1.0 - ADAM_B1 ** ADAM_STEP)
    v_hat = new_v / (1.0 - ADAM_B2 ** ADAM_STEP)
    g_ref[...] = g
    d_ref[...] = -ADAM_LR * (m_hat / (jnp.sqrt(v_hat) + ADAM_EPS) + ADAM_WD * w)
    nm_ref[...] = new_m
    nv_ref[...] = new_v


def _adamw(w, parts, m, v, name):
    r, c = w.shape
    n_parts = parts.shape[0]
    tr = min(r, 256)

    def body(w_ref, p_ref, m_ref, v_ref, *out_refs):
        g = p_ref[0].astype(F32)
        for k in range(1, n_parts):
            g = g + p_ref[k].astype(F32)
        _adamw_step(g, w_ref[...], m_ref[...], v_ref[...], *out_refs)

    spec = pl.BlockSpec((tr, c), lambda i: (i, 0))
    shape = jax.ShapeDtypeStruct((r, c), F32)
    return pl.pallas_call(
        body, grid=(r // tr,), in_specs=[spec, pl.BlockSpec((n_parts, tr, c), lambda i: (0, i, 0)), spec, spec],
        out_specs=[spec] * 4, out_shape=[shape] * 4, compiler_params=_params("parallel"), name=name,
    )(w, parts, m, v)


def _adamw_shard(w, grads, landed, m, v, place, name):
    r, c = w.shape
    n_landed = landed.shape[0]
    tr = min(r, 256)

    def body(place_ref, w_ref, own_ref, l_ref, m_ref, v_ref, *out_refs):
        g = own_ref[...].astype(F32)
        for k in range(n_landed):
            g = g + l_ref[k].astype(F32)
        _adamw_step(g, w_ref[...], m_ref[...], v_ref[...], *out_refs)

    spec = pl.BlockSpec((tr, c), lambda i, place_ref: (i, 0))
    grid_spec = pltpu.PrefetchScalarGridSpec(
        num_scalar_prefetch=1, grid=(r // tr,),
        in_specs=[spec, pl.BlockSpec((None, tr, c), lambda i, place_ref: (place_ref[0], i, 0)),
                  pl.BlockSpec((n_landed, tr, c), lambda i, place_ref: (0, i, 0)), spec, spec],
        out_specs=[spec] * 4,
    )
    return pl.pallas_call(
        body, grid_spec=grid_spec, out_shape=[jax.ShapeDtypeStruct((r, c), F32)] * 4,
        compiler_params=_params("parallel"), name=name,
    )(place, w, grads, landed, m, v)


def _place():
    x, y, c = lax.axis_index("x"), lax.axis_index("y"), lax.axis_index("c")
    other_chips = [(1 - x, y), (x, 1 - y), (1 - x, 1 - y)]
    return x, y, c, other_chips


def _all_gather(blocks, name, after=()):
    n_arr = len(blocks)
    items = [(a, i) for a, blk in enumerate(blocks) for i in range(blk.shape[0])]
    n_items = len(items)

    def body(*refs):
        srcs, refs = refs[:n_arr], refs[n_arr + len(after):]
        outs = refs[:n_arr]
        send_sems, recv_sems, local_sems = refs[n_arr:]
        x, y, c, other_chips = _place()
        me, sibling = (x, y, c), (x, y, 1 - c)

        def slot(it, dev):
            a, i = items[it]
            return outs[a].at[i, 4 * dev[0] + 2 * dev[1] + dev[2]]

        def copy(it, k, block_of, to, from_src=False):
            a, i = items[it]
            return pltpu.make_async_remote_copy(
                src_ref=srcs[a].at[i] if from_src else slot(it, block_of), dst_ref=slot(it, block_of),
                send_sem=send_sems.at[it * 7 + k], recv_sem=recv_sems.at[it * 7 + k],
                device_id=to, device_id_type=MESH)

        own = [pltpu.make_async_copy(srcs[items[it][0]].at[items[it][1]], slot(it, me), local_sems.at[it])
               for it in range(n_items)]
        for cp in own:
            cp.start()
        first = []
        for it in range(n_items):
            first.append(copy(it, 0, me, sibling, from_src=True))
            first += [copy(it, 1 + j, me, (*chip, c), from_src=True) for j, chip in enumerate(other_chips)]
        for cp in first:
            cp.start()
        passed = []
        for it in range(n_items):
            for j, chip in enumerate(other_chips):
                copy(it, 1 + j, (*chip, c), me).wait_recv()
                passed.append(copy(it, 4 + j, (*chip, c), sibling))
                passed[-1].start()
        for it in range(n_items):
            copy(it, 0, sibling, me).wait_recv()
            for j, chip in enumerate(other_chips):
                copy(it, 4 + j, (*chip, 1 - c), me).wait_recv()
        for cp in first + passed:
            cp.wait_send()
        for cp in own:
            cp.wait()

    return pl.pallas_call(
        body, in_specs=[ANY] * (n_arr + len(after)), out_specs=[ANY] * n_arr,
        out_shape=[jax.ShapeDtypeStruct((b.shape[0], N_DEV) + b.shape[1:], b.dtype) for b in blocks],
        scratch_shapes=[pltpu.SemaphoreType.DMA((7 * n_items,)), pltpu.SemaphoreType.DMA((7 * n_items,)),
                        pltpu.SemaphoreType.DMA((n_items,))],
        name=name,
    )(*blocks, *after)


def _exchange_core_pair(grads, name):
    n_arr = len(grads)

    def body(*refs):
        srcs, outs = refs[:n_arr], refs[n_arr:2 * n_arr]
        send_sems, recv_sems = refs[2 * n_arr:]
        x, y, c, _ = _place()
        copies = [
            pltpu.make_async_remote_copy(
                src_ref=srcs[a].at[q, 1 - c], dst_ref=outs[a].at[q],
                send_sem=send_sems.at[a * N_CHIP + q], recv_sem=recv_sems.at[a * N_CHIP + q],
                device_id=(x, y, 1 - c), device_id_type=MESH)
            for a in range(n_arr) for q in range(N_CHIP)]
        for cp in copies:
            cp.start()
        for cp in copies:
            cp.wait_recv()
        for cp in copies:
            cp.wait_send()

    return pl.pallas_call(
        body, in_specs=[ANY] * n_arr, out_specs=[ANY] * n_arr,
        out_shape=[jax.ShapeDtypeStruct((N_CHIP,) + g.shape[2:], g.dtype) for g in grads],
        scratch_shapes=[pltpu.SemaphoreType.DMA((N_CHIP * n_arr,)), pltpu.SemaphoreType.DMA((N_CHIP * n_arr,))],
        name=name,
    )(*grads)


def _chip_copies(srcs, outs, send_sems, recv_sems, scatter):
    x, y, c, other_chips = _place()
    my_chip = 2 * x + y
    copies = []
    for a in range(len(srcs)):
        for j, chip in enumerate(other_chips):
            src, dst = (srcs[a].at[2 * chip[0] + chip[1]], outs[a].at[j]) if scatter else (outs[a].at[my_chip, c],) * 2
            copies.append(pltpu.make_async_remote_copy(
                src_ref=src, dst_ref=dst, send_sem=send_sems.at[a * 3 + j], recv_sem=recv_sems.at[a * 3 + j],
                device_id=(*chip, c), device_id_type=MESH))
    return copies


def _wait_all(copies):
    for cp in copies:
        cp.wait_recv()
    for cp in copies:
        cp.wait_send()


def _gather_core_pair(gathered, name):
    n_arr = len(gathered)

    def body(*refs):
        bufs = refs[n_arr:2 * n_arr]
        send_sems, recv_sems = refs[2 * n_arr:]
        x, y, c, _ = _place()
        copies = [
            pltpu.make_async_remote_copy(
                src_ref=bufs[a].at[q, c], dst_ref=bufs[a].at[q, c], send_sem=send_sems.at[a * N_CHIP + q],
                recv_sem=recv_sems.at[a * N_CHIP + q], device_id=(x, y, 1 - c), device_id_type=MESH)
            for a in range(n_arr) for q in range(N_CHIP)]
        for cp in copies:
            cp.start()
        _wait_all(copies)

    return pl.pallas_call(
        body, in_specs=[ANY] * n_arr, out_specs=[ANY] * n_arr,
        out_shape=[jax.ShapeDtypeStruct(b.shape, b.dtype) for b in gathered],
        input_output_aliases={a: a for a in range(n_arr)},
        scratch_shapes=[pltpu.SemaphoreType.DMA((N_CHIP * n_arr,))] * 2, name=name,
    )(*gathered)


HBM = pl.BlockSpec(memory_space=pltpu.HBM)
SEM = pl.BlockSpec(memory_space=pltpu.SEMAPHORE)
DATAFLOW = pltpu.SideEffectType.DATAFLOW_SIDE_EFFECTING
N_PEERS = N_DEV - 1
FLIPS = [(dx, dy, dc) for dx in (0, 1) for dy in (0, 1) for dc in (0, 1) if (dx, dy, dc) != (0, 0, 0)]


def _peers():
    x, y, c = lax.axis_index("x"), lax.axis_index("y"), lax.axis_index("c")
    flip = lambda v, d: 1 - v if d else v
    return 4 * x + 2 * y + c, [(flip(x, dx), flip(y, dy), flip(c, dc)) for dx, dy, dc in FLIPS]


def _lin(p):
    return 4 * p[0] + 2 * p[1] + p[2]


def _in_hbm(a):
    return pltpu.with_memory_space_constraint(a, pltpu.HBM)


def _token_spec():
    return pl.BlockSpec(memory_space=pltpu.VMEM), jax.ShapeDtypeStruct((SUBLANES, LANES), F32)


def _gather_copy(land_ref, send_sems, recv_sems, k, me, peer, landed_from):
    return pltpu.make_async_remote_copy(
        src_ref=land_ref.at[me], dst_ref=land_ref.at[me if landed_from is None else landed_from],
        send_sem=send_sems.at[k], recv_sem=recv_sems.at[k], device_id=peer, device_id_type=MESH)


def _gather_start(groups, after, name):
    flat = [a for g in groups for a in g]
    n, ng, n_after = len(flat), len(groups), len(after)
    token_spec, token_shape = _token_spec()

    def body(*refs):
        land, outs = refs[:n], refs[n + n_after:]
        sems, token_ref = outs[:2 * ng], outs[2 * ng + n]
        me, peers = _peers()
        a = 0
        for gi, group in enumerate(groups):
            for i in range(len(group)):
                for r, peer in enumerate(peers):
                    _gather_copy(land[a], sems[2 * gi], sems[2 * gi + 1], i * N_PEERS + r, me, peer, None).start()
                a += 1
        token_ref[...] = jnp.zeros_like(token_ref)

    sem_shapes = [pltpu.SemaphoreType.DMA((N_PEERS * len(g),)) for g in groups for _ in (0, 1)]
    out = pl.pallas_call(
        body, name=name, in_specs=[HBM] * n + [ANY] * n_after,
        out_specs=[SEM] * (2 * ng) + [HBM] * n + [token_spec],
        out_shape=sem_shapes + [pltpu.HBM(a.shape, a.dtype) for a in flat] + [token_shape],
        input_output_aliases={a: 2 * ng + a for a in range(n)},
        compiler_params=pltpu.CompilerParams(has_side_effects=DATAFLOW),
    )(*[_in_hbm(a) for a in flat], *after)
    sems = [(out[2 * gi], out[2 * gi + 1]) for gi in range(ng)]
    thru, a = [], 2 * ng
    for g in groups:
        thru.append(list(out[a:a + len(g)]))
        a += len(g)
    return sems, thru, out[-1]


def _gather_wait(lands, sems, after, name):
    n = len(lands)

    def body(*refs):
        land, send_sems, recv_sems = refs[:n], refs[n], refs[n + 1]
        me, peers = _peers()
        for i in range(n):
            for r, peer in enumerate(peers):
                cp = _gather_copy(land[i], send_sems, recv_sems, i * N_PEERS + r, me, peer, _lin(peer))
                cp.wait_send()
                cp.wait_recv()

    return pl.pallas_call(
        body, name=name, in_specs=[HBM] * n + [SEM, SEM] + [ANY] * len(after), out_specs=[HBM] * n,
        out_shape=[pltpu.HBM(a.shape, a.dtype) for a in lands], input_output_aliases={i: i for i in range(n)},
        compiler_params=pltpu.CompilerParams(has_side_effects=DATAFLOW),
    )(*lands, *sems, *after)


def _scatter_copy(grad_ref, land_ref, send_sems, recv_sems, k, me, peer, start):
    mine, theirs = (me, _lin(peer)) if start else (_lin(peer), me)
    return pltpu.make_async_remote_copy(
        src_ref=grad_ref.at[_lin(peer)], dst_ref=land_ref.at[lax.rem(mine - theirs + N_PEERS + N_DEV, N_DEV)],
        send_sem=send_sems.at[k], recv_sem=recv_sems.at[k], device_id=peer, device_id_type=MESH)


def _scatter_start(grads, name):
    n = len(grads)
    lands = [lax.empty((N_PEERS,) + g.shape[1:], g.dtype) for g in grads]
    token_spec, token_shape = _token_spec()

    def body(*refs):
        grad, land, send_sems, recv_sems = refs[:n], refs[n:2 * n], refs[2 * n], refs[2 * n + 1]
        token_ref = refs[4 * n + 2]
        me, peers = _peers()
        for i in range(n):
            for r, peer in enumerate(peers):
                _scatter_copy(grad[i], land[i], send_sems, recv_sems, i * N_PEERS + r, me, peer, True).start()
        token_ref[...] = jnp.zeros_like(token_ref)

    sem_shape = pltpu.SemaphoreType.DMA((N_PEERS * n,))
    out = pl.pallas_call(
        body, name=name, in_specs=[HBM] * (2 * n), out_specs=[SEM, SEM] + [HBM] * (2 * n) + [token_spec],
        out_shape=[sem_shape, sem_shape] + [pltpu.HBM(a.shape, a.dtype) for a in grads + lands] + [token_shape],
        input_output_aliases={a: 2 + a for a in range(2 * n)},
        compiler_params=pltpu.CompilerParams(has_side_effects=DATAFLOW),
    )(*[_in_hbm(a) for a in grads + lands])
    return (out[0], out[1]), list(out[2:2 + n]), list(out[2 + n:2 + 2 * n]), out[-1]


def _scatter_wait(grads, lands, sems, after, name):
    n = len(grads)

    def body(*refs):
        grad, land, send_sems, recv_sems = refs[:n], refs[n:2 * n], refs[2 * n], refs[2 * n + 1]
        me, peers = _peers()
        for i in range(n):
            for r, peer in enumerate(peers):
                cp = _scatter_copy(grad[i], land[i], send_sems, recv_sems, i * N_PEERS + r, me, peer, False)
                cp.wait_send()
                cp.wait_recv()

    out = pl.pallas_call(
        body, name=name, in_specs=[HBM] * (2 * n) + [SEM, SEM] + [ANY] * len(after), out_specs=[HBM] * (2 * n),
        out_shape=[pltpu.HBM(a.shape, a.dtype) for a in grads + lands],
        input_output_aliases={a: a for a in range(2 * n)},
        compiler_params=pltpu.CompilerParams(has_side_effects=DATAFLOW),
    )(*grads, *lands, *sems, *after)
    return list(out[:n]), list(out[n:])


def kernel(x, ln_pre_0, conv_w_in_0, conv_w_0, conv_w_out_0, ln_post_0, ln_pre_1, sb_w_in_1, sb_w_out_1, ln_post_1, ln_pre_2, conv_w_in_2, conv_w_2, conv_w_out_2, ln_post_2, ln_pre_3, sb_w_in_3, sb_w_out_3, ln_post_3, loss_target, m_ln_pre_0, m_conv_w_in_0, m_conv_w_0, m_conv_w_out_0, m_ln_post_0, m_ln_pre_1, m_sb_w_in_1, m_sb_w_out_1, m_ln_post_1, m_ln_pre_2, m_conv_w_in_2, m_conv_w_2, m_conv_w_out_2, m_ln_post_2, m_ln_pre_3, m_sb_w_in_3, m_sb_w_out_3, m_ln_post_3, v_ln_pre_0, v_conv_w_in_0, v_conv_w_0, v_conv_w_out_0, v_ln_post_0, v_ln_pre_1, v_sb_w_in_1, v_sb_w_out_1, v_ln_post_1, v_ln_pre_2, v_conv_w_in_2, v_conv_w_2, v_conv_w_out_2, v_ln_post_2, v_ln_pre_3, v_sb_w_in_3, v_sb_w_out_3, v_ln_post_3):
    names = ['ln_pre_0', 'conv_w_in_0', 'conv_w_0', 'conv_w_out_0', 'ln_post_0', 'ln_pre_1', 'sb_w_in_1', 'sb_w_out_1',
             'ln_post_1', 'ln_pre_2', 'conv_w_in_2', 'conv_w_2', 'conv_w_out_2', 'ln_post_2', 'ln_pre_3', 'sb_w_in_3',
             'sb_w_out_3', 'ln_post_3']
    given = dict(locals())
    w = {n: given[n] for n in names}
    mom = {n: given["m_" + n] for n in names}
    var = {n: given["v_" + n] for n in names}
    conv_layers = [i for i in range(DEPTH) if i % 2 == 0]
    w_in_names = [("conv_w_in_%d" if i % 2 == 0 else "sb_w_in_%d") % i for i in range(DEPTH)]
    w_out_names = [("conv_w_out_%d" if i % 2 == 0 else "sb_w_out_%d") % i for i in range(DEPTH)]

    s, d = x.shape[1:]
    h = x.reshape(s, d)
    target = loss_target.reshape(s, d)
    gains = {n: w[n].reshape(1, d) for n in names if n.startswith("ln_")}
    place = 4 * lax.axis_index("x") + 2 * lax.axis_index("y") + lax.axis_index("c")
    place_arr = place.astype(jnp.int32).reshape(1)
    bdim = w[w_out_names[0]].shape[0] * N_DEV
    wc = bdim // N_DEV

    conv_rows = jnp.concatenate([w["conv_w_%d" % i] for i in conv_layers], axis=0)
    first = _all_gather([_cast(w[n], BF16, "cast_" + n)[None] for n in (w_in_names[0], w_out_names[0])] + [conv_rows[None]],
                        "gather_first_layer")
    slots = {n: _into_slot(w[n], place_arr, BF16, "slot_" + n) for n in w_in_names[1:] + w_out_names[1:]}
    slots = {n: a.reshape((N_CHIP, 2) + a.shape[1:]) for n, a in slots.items()}
    conv_all = first[2][0].reshape(N_DEV, len(conv_layers), CONV_K, wc)
    conv_all = jnp.transpose(conv_all, (1, 2, 0, 3)).reshape(len(conv_layers), CONV_K, bdim)
    conv_full = {layer: conv_all[n] for n, layer in enumerate(conv_layers)}
    weights = [(first[0][0], first[1][0].reshape(bdim, d))]

    saved = []
    for i in range(DEPTH):
        w_in, w_out = weights[i]
        nxt = ([slots[w_in_names[i + 1]], slots[w_out_names[i + 1]]], False) if i + 1 < DEPTH else None
        u, u_t = _rmsnorm_fwd(h, gains["ln_pre_%d" % i], [], "pre_norm_%d" % i)
        if i % 2 == 0:
            proj = _proj(u, w_in, 0, N_DEV, F32, "proj_%d" % i, hosted=nxt)
            if nxt:
                proj, crossed = proj
            a, a_t = _conv_gate_fwd(proj, conv_full[i], "conv_gate_%d" % i)
            extra = (proj,)
        else:
            qkv = _proj(u, w_in, 0, 6, BF16, "proj_qkv_%d" % i, hosted=nxt)
            if nxt:
                qkv, crossed = qkv
            z = _proj(u, w_in, 6, 2, F32, "proj_z_%d" % i)
            o, carries = _sb_attn_fwd(qkv, "sb_attn_%d" % i)
            a, a_t = _sb_gate_fwd(z, o, "sb_gate_%d" % i)
            extra = (qkv, z, o, carries)
        if nxt:
            both = _gather_core_pair(crossed, "gather_core_pair_%d" % (i + 1))
            weights.append((both[0].reshape((N_DEV,) + both[0].shape[2:]), both[1].reshape(bdim, d)))
        m = _out_proj(a, w_out, "out_proj_%d" % i)
        saved.append((h, u_t, a_t, m, extra))
        h = _post_norm_residual(h, m, gains["ln_post_%d" % i], "post_norm_%d" % i)

    dh, loss = _loss_head(h, target, "loss_head")
    loss = lax.psum(loss[0, 0], ("x", "y", "c"))

    small = {}
    chip_parts = {}
    core = lax.axis_index("c").astype(jnp.int32).reshape(1)
    for i in reversed(range(DEPTH)):
        h_in, u_t, a_t, m, extra = saved[i]
        w_in, w_out = weights[i]
        dm, small["ln_post_%d" % i] = _post_norm_bwd(dh, m, gains["ln_post_%d" % i], [], "post_norm_bwd_%d" % i)
        g_out = _weight_grad(a_t, dm, 1, "grad_w_out_%d" % i).reshape(N_CHIP, 2, wc, d)
        da = _out_proj_bwd_act(dm, w_out, "out_proj_bwd_%d" % i)
        if i % 2 == 0:
            (proj,) = extra
            db, dc, dxt, dz, small["conv_w_%d" % i] = _conv_gate_bwd(proj, da, conv_full[i], "conv_gate_bwd_%d" % i)
            dproj = jnp.concatenate([db, dc, dxt, dz], axis=1)
        else:
            qkv, z, o, carries = extra
            do, dz = _sb_gate_bwd(da, z, o, "sb_gate_bwd_%d" % i)
            dq, dk, dv = _sb_attn_bwd(qkv, do, carries, "sb_attn_bwd_%d" % i)
            dproj = jnp.concatenate([dq, dk, dv, dz], axis=1)
        g_in = _weight_grad(u_t, dproj, N_DEV, "grad_w_in_%d" % i)
        g_in = g_in.reshape((N_CHIP, 2) + g_in.shape[1:])
        from_sibling = _exchange_core_pair([g_in, g_out], "reduce_core_pair_%d" % i)
        pair_sums = [_add_core_pair(g, r, core, "add_core_pair_%s_%d" % (kind, i))
                     for g, r, kind in zip((g_in, g_out), from_sibling, ("in", "out"))]
        du, landed = _proj_bwd_act(dproj, w_in, "proj_bwd_%d" % i, hosted=(pair_sums, True))
        chip_parts[w_in_names[i]], chip_parts[w_out_names[i]] = zip(pair_sums, landed)
        dh, small["ln_pre_%d" % i] = _pre_norm_bwd(du, h_in, gains["ln_pre_%d" % i], dh, [], "pre_norm_bwd_%d" % i)
    big_names = w_in_names + w_out_names

    gain_names = [n for n in names if n.startswith("ln_")]
    conv_names = ["conv_w_%d" % i for i in conv_layers]
    rows = [small[n] for n in gain_names] + [small[n] for n in conv_names]
    n_rows = len(gain_names) + CONV_K * len(conv_names)
    pad = -n_rows % SUBLANES
    stacked = jnp.concatenate(rows + [jnp.zeros((pad, d), F32)], axis=0)
    (small_all,) = _all_gather([stacked[None]], "gather_small_grads")
    small_all = small_all[0]

    out_g, out_d, out_m, out_v = {}, {}, {}, {}

    def update(n, w2, parts, m2, v2, shape):
        g2, d2, nm2, nv2 = _adamw(w2, parts, m2, v2, "adamw_" + n)
        out_g[n], out_d[n], out_m[n], out_v[n] = (t.reshape(shape) for t in (g2, d2, nm2, nv2))

    chip_arr = (2 * lax.axis_index("x") + lax.axis_index("y")).astype(jnp.int32).reshape(1)
    for n in big_names:
        own, landed = chip_parts[n]
        out_g[n], out_d[n], out_m[n], out_v[n] = _adamw_shard(w[n], own, landed, mom[n], var[n], chip_arr, "adamw_" + n)
    n_gain = len(gain_names)
    stack = lambda src: jnp.stack([src[n] for n in gain_names])
    g2, d2, nm2, nv2 = _adamw(stack(w), small_all[:, :n_gain], stack(mom), stack(var), "adamw_gains")
    for k, n in enumerate(gain_names):
        out_g[n], out_d[n], out_m[n], out_v[n] = g2[k], d2[k], nm2[k], nv2[k]
    wc = bdim // N_DEV
    for k, n in enumerate(conv_names):
        rows_k = small_all[:, n_gain + CONV_K * k:n_gain + CONV_K * (k + 1)]
        parts = lax.dynamic_slice_in_dim(rows_k, place * wc, wc, axis=2)
        update(n, w[n], parts, mom[n], var[n], w[n].shape)

    grad_x = dh.reshape(x.shape)
    return (loss, grad_x, *[out_g[n] for n in names], *[out_d[n] for n in names],
            *[out_m[n] for n in names], *[out_v[n] for n in names])
```

```python
import functools
import math

import jax
import jax.numpy as jnp
from jax import lax
from jax.experimental import pallas as pl
from jax.experimental.pallas import tpu as pltpu

F32 = jnp.float32
BF16 = jnp.bfloat16
MESH = pl.DeviceIdType.MESH
ANY = pl.BlockSpec(memory_space=pl.ANY)

N_DEV = 8
N_CHIP = 4
DEPTH = 4
HEAD_DIM = 128
CONV_K = 3
RMS_EPS = 1e-6
ADAM_LR = 0.001
ADAM_B1 = 0.9
ADAM_B2 = 0.999
ADAM_EPS = 1e-08
ADAM_WD = 0.01
ADAM_STEP = 10

V7X_VMEM_BYTES = 64 * 1024 * 1024
VMEM_LIMIT = V7X_VMEM_BYTES * 3 // 4
LANES = 128
SUBLANES = 8
HEADS_PER_STEP = 2
DEAD_CARRY = -128.0
UNVISITED = -1e30


def _params(*sem):
    return pltpu.CompilerParams(dimension_semantics=sem, vmem_limit_bytes=VMEM_LIMIT)


def _silu_parts(z):
    sig = jax.nn.sigmoid(z)
    return z * sig, sig


NN = (((1,), (0,)), ((), ()))
NT = (((1,), (1,)), ((), ()))
TN = (((0,), (0,)), ((), ()))


def _gridded_call(body, operands, *, grid, in_specs, out_specs, out_shape, scratch_shapes, semantics, name, hosted=None):
    if hosted is None:
        return pl.pallas_call(
            body, grid=grid, in_specs=in_specs, out_specs=out_specs, out_shape=out_shape,
            scratch_shapes=scratch_shapes, compiler_params=_params(*semantics), name=name)(*operands)
    arrays, scatter = hosted
    n_in, n_out, n_ex, n_scr = len(in_specs), len(out_specs), len(arrays), len(scratch_shapes)

    def hosting_body(*refs):
        ins, refs = refs[:n_in], refs[n_in:]
        ex_in, refs = refs[:n_ex], refs[n_ex:]
        outs, refs = refs[:n_out], refs[n_out:]
        ex_out, refs = refs[:n_ex], refs[n_ex:]
        scratch, sems = refs[:n_scr], refs[n_scr:]
        first = last = None
        for axis, size in enumerate(grid):
            at_start, at_end = pl.program_id(axis) == 0, pl.program_id(axis) == size - 1
            first = at_start if first is None else first & at_start
            last = at_end if last is None else last & at_end

        @pl.when(first)
        def _():
            for cp in _chip_copies(ex_in, ex_out, *sems, scatter):
                cp.start()

        body(*ins, *outs, *scratch)

        @pl.when(last)
        def _():
            _wait_all(_chip_copies(ex_in, ex_out, *sems, scatter))

    if scatter:
        ex_shapes, aliases = [jax.ShapeDtypeStruct((N_CHIP - 1,) + a.shape[1:], a.dtype) for a in arrays], {}
    else:
        ex_shapes, aliases = [jax.ShapeDtypeStruct(a.shape, a.dtype) for a in arrays], {n_in + a: n_out + a for a in range(n_ex)}
    out = pl.pallas_call(
        hosting_body, grid=grid, in_specs=list(in_specs) + [ANY] * n_ex, out_specs=list(out_specs) + [ANY] * n_ex,
        out_shape=list(out_shape) + ex_shapes, input_output_aliases=aliases,
        scratch_shapes=list(scratch_shapes) + [pltpu.SemaphoreType.DMA((3 * n_ex,)), pltpu.SemaphoreType.DMA((3 * n_ex,))],
        compiler_params=_params(*["arbitrary"] * len(grid)), name=name)(*operands, *arrays)
    return out[:n_out], out[n_out:]


def _mm(a, b, *, dims, grid, a_spec, b_spec, o_spec, out_shape, acc_shape, name, hosted=None):
    nk = grid[2]

    def body(a_ref, b_ref, o_ref, *scratch):
        p = lax.dot_general(a_ref[...], b_ref[...], dims, preferred_element_type=F32)
        if nk == 1:
            o_ref[...] = p.astype(o_ref.dtype)
        else:
            acc_ref = scratch[0]
            k = pl.program_id(2)

            @pl.when(k == 0)
            def _():
                acc_ref[...] = p

            @pl.when(k > 0)
            def _():
                acc_ref[...] += p

            @pl.when(k == nk - 1)
            def _():
                o_ref[...] = acc_ref[...].astype(o_ref.dtype)

    scratch = [] if nk == 1 else [pltpu.VMEM(acc_shape, F32)]
    res = _gridded_call(
        body, (a, b), grid=grid, in_specs=[a_spec, b_spec], out_specs=[o_spec], out_shape=[out_shape],
        scratch_shapes=scratch, semantics=("parallel", "parallel", "arbitrary"), name=name, hosted=hosted)
    return res[0] if hosted is None else (res[0][0], res[1])


def _proj(u, w_in, shard0, n_shard, out_dtype, name, hosted=None):
    s, d = u.shape
    ws = w_in.shape[-1]
    tm, tn = min(s, 512), min(ws, 1024)
    nj = ws // tn
    return _mm(
        u, w_in, dims=NN, grid=(s // tm, n_shard * nj, 1),
        a_spec=pl.BlockSpec((tm, d), lambda i, j, k: (i, 0)),
        b_spec=pl.BlockSpec((None, d, tn), lambda i, j, k: (shard0 + j // nj, 0, j % nj)),
        o_spec=pl.BlockSpec((tm, tn), lambda i, j, k: (i, j)),
        out_shape=jax.ShapeDtypeStruct((s, n_shard * ws), out_dtype), acc_shape=(tm, tn), name=name, hosted=hosted,
    )


def _out_proj(a, w_out, name, hosted=None):
    s, bdim = a.shape
    d = w_out.shape[-1]
    tm, tn = min(s, 512), min(d, 1024)
    return _mm(
        a, w_out, dims=NN, grid=(s // tm, d // tn, 1),
        a_spec=pl.BlockSpec((tm, bdim), lambda i, j, k: (i, 0)),
        b_spec=pl.BlockSpec((bdim, tn), lambda i, j, k: (0, j)),
        o_spec=pl.BlockSpec((tm, tn), lambda i, j, k: (i, j)),
        out_shape=jax.ShapeDtypeStruct((s, d), F32), acc_shape=(tm, tn), name=name, hosted=hosted,
    )


def _out_proj_bwd_act(dm, w_out, name, hosted=None):
    s, d = dm.shape
    bdim = w_out.shape[-2]
    tm, tn = min(s, 512), min(bdim, 1024)
    return _mm(
        dm, w_out, dims=NT, grid=(s // tm, bdim // tn, 1),
        a_spec=pl.BlockSpec((tm, d), lambda i, j, k: (i, 0)),
        b_spec=pl.BlockSpec((tn, d), lambda i, j, k: (j, 0)),
        o_spec=pl.BlockSpec((tm, tn), lambda i, j, k: (i, j)),
        out_shape=jax.ShapeDtypeStruct((s, bdim), F32), acc_shape=(tm, tn), name=name, hosted=hosted,
    )


def _weight_grad(act_t, dout, n_blocks, name):
    din, s = act_t.shape
    w = dout.shape[1] // n_blocks
    tm, tn = min(din, 512), min(w, 1024)
    nj = w // tn
    return _mm(
        act_t, dout, dims=NN, grid=(din // tm, n_blocks * nj, 1),
        a_spec=pl.BlockSpec((tm, s), lambda i, j, k: (i, 0)),
        b_spec=pl.BlockSpec((s, tn), lambda i, j, k: (0, j)),
        o_spec=pl.BlockSpec((None, tm, tn), lambda i, j, k: (j // nj, i, j % nj)),
        out_shape=jax.ShapeDtypeStruct((n_blocks, din, w), BF16), acc_shape=(tm, tn), name=name,
    )


def _proj_bwd_act(dproj, w_in, name, hosted=None):
    s = dproj.shape[0]
    n_shards, d, ws = w_in.shape
    tm, tn = min(s, 512), min(d, 512)

    def body(a_ref, b_ref, o_ref):
        acc = None
        for k in range(n_shards):
            p = lax.dot_general(a_ref[:, k * ws:(k + 1) * ws], b_ref[k], NT, preferred_element_type=F32)
            acc = p if acc is None else acc + p
        o_ref[...] = acc

    res = _gridded_call(
        body, (dproj, w_in), grid=(s // tm, d // tn),
        in_specs=[pl.BlockSpec((tm, n_shards * ws), lambda i, j: (i, 0)),
                  pl.BlockSpec((n_shards, tn, ws), lambda i, j: (0, j, 0))],
        out_specs=[pl.BlockSpec((tm, tn), lambda i, j: (i, j))], out_shape=[jax.ShapeDtypeStruct((s, d), F32)],
        scratch_shapes=[], semantics=("parallel", "parallel"), name=name, hosted=hosted)
    return res[0] if hosted is None else (res[0][0], res[1])


def _row_spec(tm, d):
    return pl.BlockSpec((tm, d), lambda i: (i, 0))


def _gain_spec(d):
    return pl.BlockSpec((1, d), lambda i: (0, 0))


def _rstd(x):
    return lax.rsqrt(jnp.mean(x * x, axis=-1, keepdims=True) + RMS_EPS)


def _rmsnorm_fwd(h, gain, after, name):
    s, d = h.shape
    tm = min(s, 512)
    n_after = len(after)

    def body(*refs):
        h_ref, g_ref = refs[:2]
        u_ref, ut_ref = refs[2 + n_after:]
        x = h_ref[...]
        u = (x * _rstd(x) * g_ref[...]).astype(u_ref.dtype)
        u_ref[...] = u
        ut_ref[...] = u.T

    return pl.pallas_call(
        body, grid=(s // tm,), in_specs=[_row_spec(tm, d), _gain_spec(d)] + [ANY] * n_after,
        out_specs=[_row_spec(tm, d), pl.BlockSpec((d, tm), lambda i: (0, i))],
        out_shape=[jax.ShapeDtypeStruct((s, d), BF16), jax.ShapeDtypeStruct((d, s), BF16)],
        compiler_params=_params("parallel"), name=name,
    )(h, gain, *after)


def _cast(block, dtype, name):
    r, c = block.shape
    tr = min(r, 256)

    def body(b_ref, o_ref):
        o_ref[...] = b_ref[...].astype(o_ref.dtype)

    spec = pl.BlockSpec((tr, c), lambda i: (i, 0))
    return pl.pallas_call(
        body, grid=(r // tr,), in_specs=[spec], out_specs=spec, out_shape=jax.ShapeDtypeStruct((r, c), dtype),
        compiler_params=_params("parallel"), name=name,
    )(block)


def _post_norm_residual(h, m, gain, name):
    s, d = h.shape
    tm = min(s, 512)

    def body(h_ref, m_ref, g_ref, o_ref):
        x = m_ref[...]
        o_ref[...] = h_ref[...] + x * _rstd(x) * g_ref[...]

    return pl.pallas_call(
        body, grid=(s // tm,), in_specs=[_row_spec(tm, d), _row_spec(tm, d), _gain_spec(d)],
        out_specs=_row_spec(tm, d), out_shape=jax.ShapeDtypeStruct((s, d), F32),
        compiler_params=_params("parallel"), name=name,
    )(h, m, gain)


def _sum_rows_into(acc_ref, x):
    tm, d = x.shape
    acc_ref[...] += jnp.sum(x.reshape(tm // SUBLANES, SUBLANES, d), axis=0)


def _norm_bwd_body(n_steps, with_residual, n_after=0):
    def body(*refs):
        n_in = 4 if with_residual else 3
        dy_ref, x_ref, g_ref = refs[:3]
        dres_ref = refs[3] if with_residual else None
        dx_ref, dg_ref, acc_ref = refs[n_in + n_after:]
        i = pl.program_id(0)

        @pl.when(i == 0)
        def _():
            acc_ref[...] = jnp.zeros_like(acc_ref)

        x = x_ref[...]
        dy = dy_ref[...]
        rstd = _rstd(x)
        n = x * rstd
        dn = dy * g_ref[...]
        dx = rstd * (dn - n * jnp.mean(dn * n, axis=-1, keepdims=True))
        if with_residual:
            dx = dres_ref[...] + dx
        dx_ref[...] = dx.astype(dx_ref.dtype)
        _sum_rows_into(acc_ref, dy * n)

        @pl.when(i == n_steps - 1)
        def _():
            dg_ref[...] = jnp.sum(acc_ref[...], axis=0, keepdims=True)

    return body


def _post_norm_bwd(dh, m, gain, after, name):
    s, d = m.shape
    tm = min(s, 512)
    n_steps = s // tm
    return pl.pallas_call(
        _norm_bwd_body(n_steps, False, len(after)), grid=(n_steps,),
        in_specs=[_row_spec(tm, d), _row_spec(tm, d), _gain_spec(d)] + [ANY] * len(after),
        out_specs=[_row_spec(tm, d), _gain_spec(d)],
        out_shape=[jax.ShapeDtypeStruct((s, d), BF16), jax.ShapeDtypeStruct((1, d), F32)],
        scratch_shapes=[pltpu.VMEM((SUBLANES, d), F32)], compiler_params=_params("arbitrary"), name=name,
    )(dh, m, gain, *after)


def _pre_norm_bwd(du, h, gain, dh, after, name):
    s, d = h.shape
    tm = min(s, 512)
    n_steps = s // tm
    return pl.pallas_call(
        _norm_bwd_body(n_steps, True, len(after)), grid=(n_steps,),
        in_specs=[_row_spec(tm, d), _row_spec(tm, d), _gain_spec(d), _row_spec(tm, d)] + [ANY] * len(after),
        out_specs=[_row_spec(tm, d), _gain_spec(d)],
        out_shape=[jax.ShapeDtypeStruct((s, d), F32), jax.ShapeDtypeStruct((1, d), F32)],
        scratch_shapes=[pltpu.VMEM((SUBLANES, d), F32)], compiler_params=_params("arbitrary"), name=name,
    )(du, h, gain, dh, *after)


def _loss_head(y, target, name):
    s, d = y.shape
    tm = min(s, 512)
    n_steps = s // tm

    def body(y_ref, t_ref, dy_ref, loss_ref, acc_ref):
        i = pl.program_id(0)

        @pl.when(i == 0)
        def _():
            acc_ref[...] = jnp.zeros_like(acc_ref)

        err = y_ref[...] - t_ref[...]
        dy_ref[...] = err / d
        _sum_rows_into(acc_ref, err * err)

        @pl.when(i == n_steps - 1)
        def _():
            total = jnp.sum(jnp.sum(acc_ref[...], axis=0, keepdims=True), axis=1, keepdims=True)
            loss_ref[...] = 0.5 * total / d

    return pl.pallas_call(
        body, grid=(n_steps,), in_specs=[_row_spec(tm, d), _row_spec(tm, d)],
        out_specs=[_row_spec(tm, d), pl.BlockSpec((1, 1), lambda i: (0, 0))],
        out_shape=[jax.ShapeDtypeStruct((s, d), F32), jax.ShapeDtypeStruct((1, 1), F32)],
        scratch_shapes=[pltpu.VMEM((SUBLANES, d), F32)], compiler_params=_params("arbitrary"), name=name,
    )(y, target)


def _shift_down(p, halo, row, n):
    out = jnp.where(row == 0, halo[SUBLANES - n:SUBLANES - n + 1], pltpu.roll(p, n, 0))
    if n == 2:
        out = jnp.where(row == 1, halo[SUBLANES - 1:SUBLANES], out)
    return out


def _shift_up(p, halo, row, n):
    tm = p.shape[0]
    out = jnp.where(row == tm - 1, halo[n - 1:n], pltpu.roll(p, tm - n, 0))
    if n == 2:
        out = jnp.where(row == tm - 2, halo[0:1], out)
    return out


def _conv_specs(tm, tc, nb, n_row_blocks):
    hb = tm // SUBLANES
    cur = lambda part: pl.BlockSpec((tm, tc), lambda i, j: (i, part * nb + j))
    prev = lambda part: pl.BlockSpec((SUBLANES, tc), lambda i, j: (jnp.maximum(i * hb - 1, 0), part * nb + j))
    nxt = lambda part: pl.BlockSpec(
        (SUBLANES, tc), lambda i, j: (jnp.minimum((i + 1) * hb, n_row_blocks * hb - 1), part * nb + j))
    return cur, prev, nxt


def _conv_gate_fwd(proj, conv_w, name):
    s, b4 = proj.shape
    bdim = b4 // 4
    tm, tc = min(s, 512), min(bdim, 512)
    nb = bdim // tc
    cur, prev, _ = _conv_specs(tm, tc, nb, s // tm)

    def body(b_ref, c_ref, x_ref, z_ref, cp_ref, xp_ref, w_ref, a_ref, at_ref):
        i = pl.program_id(0)
        row = lax.broadcasted_iota(jnp.int32, (tm, tc), 0)
        p = c_ref[...] * x_ref[...]
        halo = jnp.where(i > 0, cp_ref[...] * xp_ref[...], 0.0)
        w = w_ref[...]
        cv = w[0:1] * _shift_down(p, halo, row, 2) + w[1:2] * _shift_down(p, halo, row, 1) + w[2:3] * p
        silu, _ = _silu_parts(z_ref[...])
        a = (silu * (b_ref[...] * cv)).astype(a_ref.dtype)
        a_ref[...] = a
        at_ref[...] = a.T

    return pl.pallas_call(
        body, grid=(s // tm, nb),
        in_specs=[cur(0), cur(1), cur(2), cur(3), prev(1), prev(2), pl.BlockSpec((CONV_K, tc), lambda i, j: (0, j))],
        out_specs=[pl.BlockSpec((tm, tc), lambda i, j: (i, j)), pl.BlockSpec((tc, tm), lambda i, j: (j, i))],
        out_shape=[jax.ShapeDtypeStruct((s, bdim), BF16), jax.ShapeDtypeStruct((bdim, s), BF16)],
        compiler_params=_params("parallel", "parallel"), name=name,
    )(proj, proj, proj, proj, proj, proj, conv_w)


def _conv_gate_bwd(proj, da, conv_w, name):
    s, b4 = proj.shape
    bdim = b4 // 4
    tm, tc = min(s, 128), bdim
    nb = bdim // tc
    n_rows = s // tm
    cur, prev, nxt = _conv_specs(tm, tc, nb, n_rows)
    da_cur = pl.BlockSpec((tm, tc), lambda j, i: (i, j))
    hb = tm // SUBLANES
    da_nxt = pl.BlockSpec((SUBLANES, tc), lambda j, i: (jnp.minimum((i + 1) * hb, n_rows * hb - 1), j))
    swap = lambda spec: pl.BlockSpec(spec.block_shape, lambda j, i, f=spec.index_map: f(i, j))

    def body(b_ref, c_ref, x_ref, z_ref, cp_ref, xp_ref, bn_ref, zn_ref, da_ref, dan_ref, w_ref,
             dproj_ref, dw_ref, acc_ref):
        i = pl.program_id(1)

        @pl.when(i == 0)
        def _():
            acc_ref[...] = jnp.zeros_like(acc_ref)

        row = lax.broadcasted_iota(jnp.int32, (tm, tc), 0)
        w = w_ref[...]
        b, c, x = b_ref[...], c_ref[...], x_ref[...]
        p = c * x
        halo_p = jnp.where(i > 0, cp_ref[...] * xp_ref[...], 0.0)
        p1, p2 = _shift_down(p, halo_p, row, 1), _shift_down(p, halo_p, row, 2)
        cv = w[0:1] * p2 + w[1:2] * p1 + w[2:3] * p
        z = z_ref[...]
        silu, sig = _silu_parts(z)
        da = da_ref[...]
        dy = da * silu
        dcv = dy * b
        silu_n, _ = _silu_parts(zn_ref[...])
        halo_d = jnp.where(i < n_rows - 1, dan_ref[...] * silu_n * bn_ref[...], 0.0)
        dp = w[2:3] * dcv + w[1:2] * _shift_up(dcv, halo_d, row, 1) + w[0:1] * _shift_up(dcv, halo_d, row, 2)
        gates = (dy * cv, dp * x, dp * c, da * (b * cv) * (sig * (1.0 + z * (1.0 - sig))))
        for part, dgate in enumerate(gates):
            dproj_ref[:, part * bdim:(part + 1) * bdim] = dgate.astype(dproj_ref.dtype)
        for k, pk in enumerate((p2, p1, p)):
            _sum_rows_into(acc_ref.at[k], dcv * pk)

        @pl.when(i == n_rows - 1)
        def _():
            for k in range(CONV_K):
                dw_ref[k:k + 1, :] = jnp.sum(acc_ref[k], axis=0, keepdims=True)

    out = pl.BlockSpec((tm, b4), lambda j, i: (i, 0))
    return pl.pallas_call(
        body, grid=(nb, n_rows),
        in_specs=[swap(cur(0)), swap(cur(1)), swap(cur(2)), swap(cur(3)), swap(prev(1)), swap(prev(2)),
                  swap(nxt(0)), swap(nxt(3)), da_cur, da_nxt, pl.BlockSpec((CONV_K, tc), lambda j, i: (0, j))],
        out_specs=[out, pl.BlockSpec((CONV_K, tc), lambda j, i: (0, j))],
        out_shape=[jax.ShapeDtypeStruct((s, b4), BF16), jax.ShapeDtypeStruct((CONV_K, bdim), F32)],
        scratch_shapes=[pltpu.VMEM((CONV_K, SUBLANES, tc), F32)],
        compiler_params=_params("parallel", "arbitrary"), name=name,
    )(proj, proj, proj, proj, proj, proj, proj, proj, da, da, conv_w)


def _split(x):
    hi = x.astype(BF16)
    lo = (x - hi.astype(F32)).astype(BF16)
    return jnp.concatenate([hi, lo], axis=1)


def _row_total(x, column):
    return jnp.broadcast_to(x[:, column:column + 1], (x.shape[0], LANES))


def _sb_tiles(qs, ks, carries, suffix_ones, masks, chain=0):
    items = range(len(qs))
    bk = ks[0].shape[0]
    scale = 1.0 / math.sqrt(HEAD_DIM)
    logits = [lax.dot_general(qs[n], ks[n], NT, preferred_element_type=F32) * scale for n in items]
    es = [jnp.exp(-jnp.abs(logits[n])) for n in items]
    keeps = []
    for n in items:
        log_keep = -(jnp.maximum(logits[n], 0.0) + jnp.log(1.0 + es[n]))
        if masks[n] is not None:
            log_keep = jnp.where(masks[n], log_keep, 0.0)
        keeps.append(_split(log_keep))
    tails = [lax.dot_general(keeps[n], suffix_ones, NN, preferred_element_type=F32) for n in items]
    ws, used = [], []
    for n in items:
        carry = carries[n] if n < len(carries) else used[n - chain] + _row_total(tails[n - chain], 0)
        used.append(carry)
        w = jnp.exp(logits[n] + tails[n] + (carry if carry.shape[1] == 1 else _lane_tile(carry, bk)))
        if masks[n] is not None:
            w = jnp.where(masks[n], w, 0.0)
        ws.append(w)
    return logits, es, tails, ws, used


def _tri_twice(n, upper):
    r = lax.broadcasted_iota(jnp.int32, (2 * n, n), 0)
    r = jnp.where(r >= n, r - n, r)
    c = lax.broadcasted_iota(jnp.int32, (2 * n, n), 1)
    return jnp.where(r <= c if upper else r >= c, 1.0, 0.0).astype(BF16)


def _group_spec(s, width, part, n_groups):
    return pl.BlockSpec((s, width), lambda h: (0, part * n_groups + h))


def _head_cols(g):
    return slice(g * HEAD_DIM, (g + 1) * HEAD_DIM)


def _lane_tile(x, n):
    return x if n == LANES else jnp.concatenate([x] * (n // LANES), axis=1)


def _sb_attn_fwd(qkv, name):
    s, b3 = qkv.shape
    bdim = b3 // 3
    hps = min(HEADS_PER_STEP, bdim // HEAD_DIM)
    width = hps * HEAD_DIM
    n_groups = bdim // width
    blk = min(s, 256)
    n_blk = s // blk

    def body(q_ref, k_ref, v_ref, o_ref, car_ref, carry_ref):
        suffix_ones = _tri_twice(blk, upper=False)
        r = lax.broadcasted_iota(jnp.int32, (blk, blk), 0)
        c = lax.broadcasted_iota(jnp.int32, (blk, blk), 1)
        diag_mask = c < r
        lane = lax.broadcasted_iota(jnp.int32, (blk, LANES), 1)

        def q_block(qi, _):
            q0 = pl.multiple_of(qi * blk, blk)
            rows = pl.ds(q0, blk)
            qs = [q_ref[rows, _head_cols(g)] for g in range(hps)]
            o_ref[rows, :] = jnp.zeros((blk, width), F32)
            car_ref[rows, :] = jnp.full((blk, width), UNVISITED, F32)
            carry_ref[...] = jnp.zeros_like(carry_ref)

            def step(js, tile_masks):
                k0s = [pl.multiple_of(j * blk, blk) for j in js]
                items = [(t, g) for t in range(len(js)) for g in range(hps)]
                ks = [k_ref[pl.ds(k0s[t], blk), _head_cols(g)] for t, g in items]
                first = [carry_ref[g] for g in range(hps)]
                _, _, tails, ws, carries = _sb_tiles([qs[g] for _, g in items], ks, first, suffix_ones,
                                                     [tile_masks[t] for t, _ in items], chain=hps)
                for g in range(hps):
                    mine = [n for n, (_, h) in enumerate(items) if h == g]
                    acc, saved = None, car_ref[rows, _head_cols(g)]
                    for n in mine:
                        v = v_ref[pl.ds(k0s[items[n][0]], blk), _head_cols(g)]
                        p = lax.dot_general(ws[n].astype(BF16), v, NN, preferred_element_type=F32)
                        acc = p if acc is None else acc + p
                        saved = jnp.where(lane == js[items[n][0]], carries[n], saved)
                    o_ref[rows, _head_cols(g)] += acc
                    car_ref[rows, _head_cols(g)] = saved
                    carry_ref[g] = carries[mine[-1]] + _row_total(tails[mine[-1]], 0)

            @pl.when(qi == 0)
            def _():
                step([0], [diag_mask])

            @pl.when(qi > 0)
            def _():
                step([qi, qi - 1], [diag_mask, None])

            def alive():
                top = jnp.max(jnp.max(carry_ref[...], axis=0), axis=0, keepdims=True)
                return (jnp.max(top, axis=1, keepdims=True)[0, 0] >= DEAD_CARRY).astype(jnp.int32)

            left = jnp.maximum(qi - 1, 0)

            def pair(state):
                p, _ = state
                j = qi - 2 - 2 * p
                step([j, j - 1], [None, None])
                return p + 1, alive()

            p, live = lax.while_loop(lambda state: (state[0] < left // 2) & (state[1] > 0), pair, (0, alive()))

            @pl.when((left % 2 == 1) & (p == left // 2) & (live > 0))
            def _():
                step([0], [None])

            return 0

        lax.fori_loop(0, n_blk, q_block, 0)

    out = pl.BlockSpec((s, width), lambda h: (0, h))
    shape = jax.ShapeDtypeStruct((s, bdim), F32)
    return pl.pallas_call(
        body, grid=(n_groups,),
        in_specs=[_group_spec(s, width, part, n_groups) for part in range(3)],
        out_specs=[out, out], out_shape=[shape, shape], scratch_shapes=[pltpu.VMEM((hps, blk, LANES), F32)],
        compiler_params=_params("parallel"), name=name,
    )(qkv, qkv, qkv)


def _sb_attn_bwd(qkv, do, carries, name):
    s, b3 = qkv.shape
    bdim = b3 // 3
    hps = min(HEADS_PER_STEP, bdim // HEAD_DIM)
    width = hps * HEAD_DIM
    n_groups = bdim // width
    blk = min(s, 256)
    n_blk = s // blk
    scale = 1.0 / math.sqrt(HEAD_DIM)

    def body(q_ref, k_ref, v_ref, do_ref, car_ref, dq_ref, dk_ref, dv_ref, dk_acc, dv_acc, dq_acc, before_ref):
        suffix_ones = _tri_twice(blk, upper=False)
        prefix_ones = _tri_twice(blk, upper=True)
        r = lax.broadcasted_iota(jnp.int32, (blk, blk), 0)
        c = lax.broadcasted_iota(jnp.int32, (blk, blk), 1)
        diag_mask = c < r
        lane = lax.broadcasted_iota(jnp.int32, (blk, LANES), 1)
        dk_acc[...] = jnp.zeros_like(dk_acc)
        dv_acc[...] = jnp.zeros_like(dv_acc)

        def q_block(qi, _):
            q0 = pl.multiple_of(qi * blk, blk)
            rows = pl.ds(q0, blk)
            qs = [q_ref[rows, _head_cols(g)] for g in range(hps)]
            dos = [do_ref[rows, _head_cols(g)] for g in range(hps)]
            dq_acc[...] = jnp.zeros_like(dq_acc)
            before_ref[...] = jnp.zeros_like(before_ref)

            def step(js, tile_masks):
                masks = [tile_masks[t] for t in range(len(js)) for _ in range(hps)]
                k0s = [pl.multiple_of(j * blk, blk) for j in js]
                items = [(t, g) for t in range(len(js)) for g in range(hps)]
                every = range(len(items))
                ks = [k_ref[pl.ds(k0s[t], blk), _head_cols(g)] for t, g in items]
                dws = [lax.dot_general(dos[g], v_ref[pl.ds(k0s[t], blk), _head_cols(g)], NT, preferred_element_type=F32)
                       for t, g in items]
                carries = [jnp.sum(jnp.where(lane == js[t], car_ref[rows, _head_cols(g)], 0.0), axis=1, keepdims=True)
                           for t, g in items]
                logits, es, _, ws, _ = _sb_tiles([qs[g] for _, g in items], ks, carries, suffix_ones, masks)
                gws = [dws[n] * ws[n] for n in every]
                g_upto = [lax.dot_general(_split(gws[n]), prefix_ones, NN, preferred_element_type=F32) for n in every]
                dss, befores = [], []
                for n, (t, g) in enumerate(items):
                    before = before_ref[g] if t == 0 else befores[n - hps] + _row_total(g_upto[n - hps], blk - 1)
                    befores.append(before)
                    sig = jnp.where(logits[n] >= 0.0, 1.0, es[n]) / (1.0 + es[n])
                    dlogits = gws[n] - sig * (_lane_tile(before, blk) + g_upto[n])
                    if masks[n] is not None:
                        dlogits = jnp.where(masks[n], dlogits, 0.0)
                    dss.append((dlogits * scale).astype(BF16))
                for g in range(hps):
                    mine = [n for n in every if items[n][1] == g]
                    dq = None
                    for n in mine:
                        k0 = k0s[items[n][0]]
                        p = lax.dot_general(dss[n], ks[n], NN, preferred_element_type=F32)
                        dq = p if dq is None else dq + p
                        dk_acc[pl.ds(k0, blk), _head_cols(g)] += lax.dot_general(
                            dss[n], qs[g], TN, preferred_element_type=F32)
                        dv_acc[pl.ds(k0, blk), _head_cols(g)] += lax.dot_general(
                            ws[n].astype(BF16), dos[g], TN, preferred_element_type=F32)
                    dq_acc[:, _head_cols(g)] += dq
                    before_ref[g] = befores[mine[-1]] + _row_total(g_upto[mine[-1]], blk - 1)

            top = car_ref[rows, _head_cols(0)]
            for g in range(1, hps):
                top = jnp.maximum(top, car_ref[rows, _head_cols(g)])
            top = jnp.max(top, axis=0, keepdims=True)
            lane_row = lax.broadcasted_iota(jnp.int32, (1, LANES), 1)
            counted = jnp.where((top >= DEAD_CARRY) & (lane_row < qi), 1.0, 0.0)
            n_alive = jnp.sum(counted, axis=1, keepdims=True)[0, 0].astype(jnp.int32)
            left = jnp.maximum(n_alive - 1, 0)
            start = qi - 1 - left

            @pl.when(left % 2 == 1)
            def _():
                step([start], [None])

            def pair(p, _):
                j = start + left % 2 + 2 * p
                step([j, j + 1], [None, None])
                return 0

            lax.fori_loop(0, left // 2, pair, 0)

            @pl.when(qi == 0)
            def _():
                step([0], [diag_mask])

            @pl.when(qi > 0)
            def _():
                step([qi - 1, qi], [None, diag_mask])
            dq_ref[rows, :] = dq_acc[...].astype(dq_ref.dtype)
            return 0

        lax.fori_loop(0, n_blk, q_block, 0)
        dk_ref[...] = dk_acc[...].astype(dk_ref.dtype)
        dv_ref[...] = dv_acc[...].astype(dv_ref.dtype)

    group = pl.BlockSpec((s, width), lambda h: (0, h))
    once = pl.BlockSpec((s, width), lambda h: (0, h), pipeline_mode=pl.Buffered(1))
    shape = jax.ShapeDtypeStruct((s, bdim), BF16)
    return pl.pallas_call(
        body, grid=(n_groups,),
        in_specs=[_group_spec(s, width, part, n_groups) for part in range(3)] + [once, once],
        out_specs=[group, group, group], out_shape=[shape, shape, shape],
        scratch_shapes=[pltpu.VMEM((s, width), F32), pltpu.VMEM((s, width), F32), pltpu.VMEM((blk, width), F32),
                        pltpu.VMEM((hps, blk, LANES), F32)],
        compiler_params=_params("parallel"), name=name,
    )(qkv, qkv, qkv, do, carries)


def _sb_gate_fwd(z, o, name):
    s, bdim = z.shape
    tm = min(s, 512)

    def body(z_ref, o_ref, a_ref, at_ref):
        silu, _ = _silu_parts(z_ref[...])
        a = (silu * o_ref[...]).astype(a_ref.dtype)
        a_ref[...] = a
        at_ref[...] = a.T

    return pl.pallas_call(
        body, grid=(s // tm,), in_specs=[_row_spec(tm, bdim), _row_spec(tm, bdim)],
        out_specs=[_row_spec(tm, bdim), pl.BlockSpec((bdim, tm), lambda i: (0, i))],
        out_shape=[jax.ShapeDtypeStruct((s, bdim), BF16), jax.ShapeDtypeStruct((bdim, s), BF16)],
        compiler_params=_params("parallel"), name=name,
    )(z, o)


def _sb_gate_bwd(da, z, o, name):
    s, bdim = z.shape
    tm = min(s, 512)

    def body(da_ref, z_ref, o_ref, do_ref, dz_ref):
        z = z_ref[...]
        da = da_ref[...]
        silu, sig = _silu_parts(z)
        do_ref[...] = (da * silu).astype(do_ref.dtype)
        dz_ref[...] = (da * o_ref[...] * (sig * (1.0 + z * (1.0 - sig)))).astype(dz_ref.dtype)

    spec = _row_spec(tm, bdim)
    shape = jax.ShapeDtypeStruct((s, bdim), BF16)
    return pl.pallas_call(
        body, grid=(s // tm,), in_specs=[spec, spec, spec], out_specs=[spec, spec], out_shape=[shape, shape],
        compiler_params=_params("parallel"), name=name,
    )(da, z, o)


def _into_slot(block, place, dtype, name):
    r, c = block.shape
    tr = min(r, 256)

    def body(place_ref, b_ref, o_ref):
        o_ref[...] = b_ref[...].astype(o_ref.dtype)

    grid_spec = pltpu.PrefetchScalarGridSpec(
        num_scalar_prefetch=1, grid=(r // tr,),
        in_specs=[pl.BlockSpec((tr, c), lambda i, place_ref: (i, 0))],
        out_specs=pl.BlockSpec((None, tr, c), lambda i, place_ref: (place_ref[0], i, 0)),
    )
    return pl.pallas_call(
        body, grid_spec=grid_spec, out_shape=jax.ShapeDtypeStruct((N_DEV, r, c), dtype),
        compiler_params=_params("parallel"), name=name,
    )(place, block)


def _add_core_pair(grads, received, core, name):
    _, _, r, c = grads.shape
    tr = min(r, 256)

    def body(core_ref, g_ref, r_ref, o_ref):
        o_ref[...] = (g_ref[...].astype(F32) + r_ref[...].astype(F32)).astype(o_ref.dtype)

    grid_spec = pltpu.PrefetchScalarGridSpec(
        num_scalar_prefetch=1, grid=(N_CHIP, r // tr),
        in_specs=[pl.BlockSpec((None, None, tr, c), lambda q, i, core_ref: (q, core_ref[0], i, 0)),
                  pl.BlockSpec((None, tr, c), lambda q, i, core_ref: (q, i, 0))],
        out_specs=pl.BlockSpec((None, tr, c), lambda q, i, core_ref: (q, i, 0)),
    )
    return pl.pallas_call(
        body, grid_spec=grid_spec, out_shape=jax.ShapeDtypeStruct((N_CHIP, r, c), BF16),
        compiler_params=_params("parallel", "parallel"), name=name,
    )(core, grads, received)


def _adamw_step(g, w, m, v, g_ref, d_ref, nm_ref, nv_ref):
    new_m = ADAM_B1 * m + (1.0 - ADAM_B1) * g
    new_v = ADAM_B2 * v + (1.0 - ADAM_B2) * (g * g)
    m_hat = new_m / (1.0 - ADAM_B1 ** ADAM_STEP)
    v_hat = new_v / (1.0 - ADAM_B2 ** ADAM_STEP)
    g_ref[...] = g
    d_ref[...] = -ADAM_LR * (m_hat / (jnp.sqrt(v_hat) + ADAM_EPS) + ADAM_WD * w)
    nm_ref[...] = new_m
    nv_ref[...] = new_v


def _adamw(w, parts, m, v, name):
    r, c = w.shape
    n_parts = parts.shape[0]
    tr = min(r, 256)

    def body(w_ref, p_ref, m_ref, v_ref, *out_refs):
        g = p_ref[0].astype(F32)
        for k in range(1, n_parts):
            g = g + p_ref[k].astype(F32)
        _adamw_step(g, w_ref[...], m_ref[...], v_ref[...], *out_refs)

    spec = pl.BlockSpec((tr, c), lambda i: (i, 0))
    shape = jax.ShapeDtypeStruct((r, c), F32)
    return pl.pallas_call(
        body, grid=(r // tr,), in_specs=[spec, pl.BlockSpec((n_parts, tr, c), lambda i: (0, i, 0)), spec, spec],
        out_specs=[spec] * 4, out_shape=[shape] * 4, compiler_params=_params("parallel"), name=name,
    )(w, parts, m, v)


def _adamw_shard(w, grads, landed, m, v, place, name):
    r, c = w.shape
    n_landed = landed.shape[0]
    tr = min(r, 256)

    def body(place_ref, w_ref, own_ref, l_ref, m_ref, v_ref, *out_refs):
        g = own_ref[...].astype(F32)
        for k in range(n_landed):
            g = g + l_ref[k].astype(F32)
        _adamw_step(g, w_ref[...], m_ref[...], v_ref[...], *out_refs)

    spec = pl.BlockSpec((tr, c), lambda i, place_ref: (i, 0))
    grid_spec = pltpu.PrefetchScalarGridSpec(
        num_scalar_prefetch=1, grid=(r // tr,),
        in_specs=[spec, pl.BlockSpec((None, tr, c), lambda i, place_ref: (place_ref[0], i, 0)),
                  pl.BlockSpec((n_landed, tr, c), lambda i, place_ref: (0, i, 0)), spec, spec],
        out_specs=[spec] * 4,
    )
    return pl.pallas_call(
        body, grid_spec=grid_spec, out_shape=[jax.ShapeDtypeStruct((r, c), F32)] * 4,
        compiler_params=_params("parallel"), name=name,
    )(place, w, grads, landed, m, v)


def _place():
    x, y, c = lax.axis_index("x"), lax.axis_index("y"), lax.axis_index("c")
    other_chips = [(1 - x, y), (x, 1 - y), (1 - x, 1 - y)]
    return x, y, c, other_chips


def _all_gather(blocks, name, after=()):
    n_arr = len(blocks)
    items = [(a, i) for a, blk in enumerate(blocks) for i in range(blk.shape[0])]
    n_items = len(items)

    def body(*refs):
        srcs, refs = refs[:n_arr], refs[n_arr + len(after):]
        outs = refs[:n_arr]
        send_sems, recv_sems, local_sems = refs[n_arr:]
        x, y, c, other_chips = _place()
        me, sibling = (x, y, c), (x, y, 1 - c)

        def slot(it, dev):
            a, i = items[it]
            return outs[a].at[i, 4 * dev[0] + 2 * dev[1] + dev[2]]

        def copy(it, k, block_of, to, from_src=False):
            a, i = items[it]
            return pltpu.make_async_remote_copy(
                src_ref=srcs[a].at[i] if from_src else slot(it, block_of), dst_ref=slot(it, block_of),
                send_sem=send_sems.at[it * 7 + k], recv_sem=recv_sems.at[it * 7 + k],
                device_id=to, device_id_type=MESH)

        own = [pltpu.make_async_copy(srcs[items[it][0]].at[items[it][1]], slot(it, me), local_sems.at[it])
               for it in range(n_items)]
        for cp in own:
            cp.start()
        first = []
        for it in range(n_items):
            first.append(copy(it, 0, me, sibling, from_src=True))
            first += [copy(it, 1 + j, me, (*chip, c), from_src=True) for j, chip in enumerate(other_chips)]
        for cp in first:
            cp.start()
        passed = []
        for it in range(n_items):
            for j, chip in enumerate(other_chips):
                copy(it, 1 + j, (*chip, c), me).wait_recv()
                passed.append(copy(it, 4 + j, (*chip, c), sibling))
                passed[-1].start()
        for it in range(n_items):
            copy(it, 0, sibling, me).wait_recv()
            for j, chip in enumerate(other_chips):
                copy(it, 4 + j, (*chip, 1 - c), me).wait_recv()
        for cp in first + passed:
            cp.wait_send()
        for cp in own:
            cp.wait()

    return pl.pallas_call(
        body, in_specs=[ANY] * (n_arr + len(after)), out_specs=[ANY] * n_arr,
        out_shape=[jax.ShapeDtypeStruct((b.shape[0], N_DEV) + b.shape[1:], b.dtype) for b in blocks],
        scratch_shapes=[pltpu.SemaphoreType.DMA((7 * n_items,)), pltpu.SemaphoreType.DMA((7 * n_items,)),
                        pltpu.SemaphoreType.DMA((n_items,))],
        name=name,
    )(*blocks, *after)


def _exchange_core_pair(grads, name):
    n_arr = len(grads)

    def body(*refs):
        srcs, outs = refs[:n_arr], refs[n_arr:2 * n_arr]
        send_sems, recv_sems = refs[2 * n_arr:]
        x, y, c, _ = _place()
        copies = [
            pltpu.make_async_remote_copy(
                src_ref=srcs[a].at[q, 1 - c], dst_ref=outs[a].at[q],
                send_sem=send_sems.at[a * N_CHIP + q], recv_sem=recv_sems.at[a * N_CHIP + q],
                device_id=(x, y, 1 - c), device_id_type=MESH)
            for a in range(n_arr) for q in range(N_CHIP)]
        for cp in copies:
            cp.start()
        for cp in copies:
            cp.wait_recv()
        for cp in copies:
            cp.wait_send()

    return pl.pallas_call(
        body, in_specs=[ANY] * n_arr, out_specs=[ANY] * n_arr,
        out_shape=[jax.ShapeDtypeStruct((N_CHIP,) + g.shape[2:], g.dtype) for g in grads],
        scratch_shapes=[pltpu.SemaphoreType.DMA((N_CHIP * n_arr,)), pltpu.SemaphoreType.DMA((N_CHIP * n_arr,))],
        name=name,
    )(*grads)


def _chip_copies(srcs, outs, send_sems, recv_sems, scatter):
    x, y, c, other_chips = _place()
    my_chip = 2 * x + y
    copies = []
    for a in range(len(srcs)):
        for j, chip in enumerate(other_chips):
            src, dst = (srcs[a].at[2 * chip[0] + chip[1]], outs[a].at[j]) if scatter else (outs[a].at[my_chip, c],) * 2
            copies.append(pltpu.make_async_remote_copy(
                src_ref=src, dst_ref=dst, send_sem=send_sems.at[a * 3 + j], recv_sem=recv_sems.at[a * 3 + j],
                device_id=(*chip, c), device_id_type=MESH))
    return copies


def _wait_all(copies):
    for cp in copies:
        cp.wait_recv()
    for cp in copies:
        cp.wait_send()


def _gather_core_pair(gathered, name):
    n_arr = len(gathered)

    def body(*refs):
        bufs = refs[n_arr:2 * n_arr]
        send_sems, recv_sems = refs[2 * n_arr:]
        x, y, c, _ = _place()
        copies = [
            pltpu.make_async_remote_copy(
                src_ref=bufs[a].at[q, c], dst_ref=bufs[a].at[q, c], send_sem=send_sems.at[a * N_CHIP + q],
                recv_sem=recv_sems.at[a * N_CHIP + q], device_id=(x, y, 1 - c), device_id_type=MESH)
            for a in range(n_arr) for q in range(N_CHIP)]
        for cp in copies:
            cp.start()
        _wait_all(copies)

    return pl.pallas_call(
        body, in_specs=[ANY] * n_arr, out_specs=[ANY] * n_arr,
        out_shape=[jax.ShapeDtypeStruct(b.shape, b.dtype) for b in gathered],
        input_output_aliases={a: a for a in range(n_arr)},
        scratch_shapes=[pltpu.SemaphoreType.DMA((N_CHIP * n_arr,))] * 2, name=name,
    )(*gathered)


HBM = pl.BlockSpec(memory_space=pltpu.HBM)
SEM = pl.BlockSpec(memory_space=pltpu.SEMAPHORE)
DATAFLOW = pltpu.SideEffectType.DATAFLOW_SIDE_EFFECTING
N_PEERS = N_DEV - 1
FLIPS = [(dx, dy, dc) for dx in (0, 1) for dy in (0, 1) for dc in (0, 1) if (dx, dy, dc) != (0, 0, 0)]


def _peers():
    x, y, c = lax.axis_index("x"), lax.axis_index("y"), lax.axis_index("c")
    flip = lambda v, d: 1 - v if d else v
    return 4 * x + 2 * y + c, [(flip(x, dx), flip(y, dy), flip(c, dc)) for dx, dy, dc in FLIPS]


def _lin(p):
    return 4 * p[0] + 2 * p[1] + p[2]


def _in_hbm(a):
    return pltpu.with_memory_space_constraint(a, pltpu.HBM)


def _token_spec():
    return pl.BlockSpec(memory_space=pltpu.VMEM), jax.ShapeDtypeStruct((SUBLANES, LANES), F32)


def _gather_copy(land_ref, send_sems, recv_sems, k, me, peer, landed_from):
    return pltpu.make_async_remote_copy(
        src_ref=land_ref.at[me], dst_ref=land_ref.at[me if landed_from is None else landed_from],
        send_sem=send_sems.at[k], recv_sem=recv_sems.at[k], device_id=peer, device_id_type=MESH)


def _gather_start(groups, after, name):
    flat = [a for g in groups for a in g]
    n, ng, n_after = len(flat), len(groups), len(after)
    token_spec, token_shape = _token_spec()

    def body(*refs):
        land, outs = refs[:n], refs[n + n_after:]
        sems, token_ref = outs[:2 * ng], outs[2 * ng + n]
        me, peers = _peers()
        a = 0
        for gi, group in enumerate(groups):
            for i in range(len(group)):
                for r, peer in enumerate(peers):
                    _gather_copy(land[a], sems[2 * gi], sems[2 * gi + 1], i * N_PEERS + r, me, peer, None).start()
                a += 1
        token_ref[...] = jnp.zeros_like(token_ref)

    sem_shapes = [pltpu.SemaphoreType.DMA((N_PEERS * len(g),)) for g in groups for _ in (0, 1)]
    out = pl.pallas_call(
        body, name=name, in_specs=[HBM] * n + [ANY] * n_after,
        out_specs=[SEM] * (2 * ng) + [HBM] * n + [token_spec],
        out_shape=sem_shapes + [pltpu.HBM(a.shape, a.dtype) for a in flat] + [token_shape],
        input_output_aliases={a: 2 * ng + a for a in range(n)},
        compiler_params=pltpu.CompilerParams(has_side_effects=DATAFLOW),
    )(*[_in_hbm(a) for a in flat], *after)
    sems = [(out[2 * gi], out[2 * gi + 1]) for gi in range(ng)]
    thru, a = [], 2 * ng
    for g in groups:
        thru.append(list(out[a:a + len(g)]))
        a += len(g)
    return sems, thru, out[-1]


def _gather_wait(lands, sems, after, name):
    n = len(lands)

    def body(*refs):
        land, send_sems, recv_sems = refs[:n], refs[n], refs[n + 1]
        me, peers = _peers()
        for i in range(n):
            for r, peer in enumerate(peers):
                cp = _gather_copy(land[i], send_sems, recv_sems, i * N_PEERS + r, me, peer, _lin(peer))
                cp.wait_send()
                cp.wait_recv()

    return pl.pallas_call(
        body, name=name, in_specs=[HBM] * n + [SEM, SEM] + [ANY] * len(after), out_specs=[HBM] * n,
        out_shape=[pltpu.HBM(a.shape, a.dtype) for a in lands], input_output_aliases={i: i for i in range(n)},
        compiler_params=pltpu.CompilerParams(has_side_effects=DATAFLOW),
    )(*lands, *sems, *after)


def _scatter_copy(grad_ref, land_ref, send_sems, recv_sems, k, me, peer, start):
    mine, theirs = (me, _lin(peer)) if start else (_lin(peer), me)
    return pltpu.make_async_remote_copy(
        src_ref=grad_ref.at[_lin(peer)], dst_ref=land_ref.at[lax.rem(mine - theirs + N_PEERS + N_DEV, N_DEV)],
        send_sem=send_sems.at[k], recv_sem=recv_sems.at[k], device_id=peer, device_id_type=MESH)


def _scatter_start(grads, name):
    n = len(grads)
    lands = [lax.empty((N_PEERS,) + g.shape[1:], g.dtype) for g in grads]
    token_spec, token_shape = _token_spec()

    def body(*refs):
        grad, land, send_sems, recv_sems = refs[:n], refs[n:2 * n], refs[2 * n], refs[2 * n + 1]
        token_ref = refs[4 * n + 2]
        me, peers = _peers()
        for i in range(n):
            for r, peer in enumerate(peers):
                _scatter_copy(grad[i], land[i], send_sems, recv_sems, i * N_PEERS + r, me, peer, True).start()
        token_ref[...] = jnp.zeros_like(token_ref)

    sem_shape = pltpu.SemaphoreType.DMA((N_PEERS * n,))
    out = pl.pallas_call(
        body, name=name, in_specs=[HBM] * (2 * n), out_specs=[SEM, SEM] + [HBM] * (2 * n) + [token_spec],
        out_shape=[sem_shape, sem_shape] + [pltpu.HBM(a.shape, a.dtype) for a in grads + lands] + [token_shape],
        input_output_aliases={a: 2 + a for a in range(2 * n)},
        compiler_params=pltpu.CompilerParams(has_side_effects=DATAFLOW),
    )(*[_in_hbm(a) for a in grads + lands])
    return (out[0], out[1]), list(out[2:2 + n]), list(out[2 + n:2 + 2 * n]), out[-1]


def _scatter_wait(grads, lands, sems, after, name):
    n = len(grads)

    def body(*refs):
        grad, land, send_sems, recv_sems = refs[:n], refs[n:2 * n], refs[2 * n], refs[2 * n + 1]
        me, peers = _peers()
        for i in range(n):
            for r, peer in enumerate(peers):
                cp = _scatter_copy(grad[i], land[i], send_sems, recv_sems, i * N_PEERS + r, me, peer, False)
                cp.wait_send()
                cp.wait_recv()

    out = pl.pallas_call(
        body, name=name, in_specs=[HBM] * (2 * n) + [SEM, SEM] + [ANY] * len(after), out_specs=[HBM] * (2 * n),
        out_shape=[pltpu.HBM(a.shape, a.dtype) for a in grads + lands],
        input_output_aliases={a: a for a in range(2 * n)},
        compiler_params=pltpu.CompilerParams(has_side_effects=DATAFLOW),
    )(*grads, *lands, *sems, *after)
    return list(out[:n]), list(out[n:])


def kernel(x, ln_pre_0, conv_w_in_0, conv_w_0, conv_w_out_0, ln_post_0, ln_pre_1, sb_w_in_1, sb_w_out_1, ln_post_1, ln_pre_2, conv_w_in_2, conv_w_2, conv_w_out_2, ln_post_2, ln_pre_3, sb_w_in_3, sb_w_out_3, ln_post_3, loss_target, m_ln_pre_0, m_conv_w_in_0, m_conv_w_0, m_conv_w_out_0, m_ln_post_0, m_ln_pre_1, m_sb_w_in_1, m_sb_w_out_1, m_ln_post_1, m_ln_pre_2, m_conv_w_in_2, m_conv_w_2, m_conv_w_out_2, m_ln_post_2, m_ln_pre_3, m_sb_w_in_3, m_sb_w_out_3, m_ln_post_3, v_ln_pre_0, v_conv_w_in_0, v_conv_w_0, v_conv_w_out_0, v_ln_post_0, v_ln_pre_1, v_sb_w_in_1, v_sb_w_out_1, v_ln_post_1, v_ln_pre_2, v_conv_w_in_2, v_conv_w_2, v_conv_w_out_2, v_ln_post_2, v_ln_pre_3, v_sb_w_in_3, v_sb_w_out_3, v_ln_post_3):
    names = ['ln_pre_0', 'conv_w_in_0', 'conv_w_0', 'conv_w_out_0', 'ln_post_0', 'ln_pre_1', 'sb_w_in_1', 'sb_w_out_1',
             'ln_post_1', 'ln_pre_2', 'conv_w_in_2', 'conv_w_2', 'conv_w_out_2', 'ln_post_2', 'ln_pre_3', 'sb_w_in_3',
             'sb_w_out_3', 'ln_post_3']
    given = dict(locals())
    w = {n: given[n] for n in names}
    mom = {n: given["m_" + n] for n in names}
    var = {n: given["v_" + n] for n in names}
    conv_layers = [i for i in range(DEPTH) if i % 2 == 0]
    w_in_names = [("conv_w_in_%d" if i % 2 == 0 else "sb_w_in_%d") % i for i in range(DEPTH)]
    w_out_names = [("conv_w_out_%d" if i % 2 == 0 else "sb_w_out_%d") % i for i in range(DEPTH)]

    s, d = x.shape[1:]
    h = x.reshape(s, d)
    target = loss_target.reshape(s, d)
    gains = {n: w[n].reshape(1, d) for n in names if n.startswith("ln_")}
    place = 4 * lax.axis_index("x") + 2 * lax.axis_index("y") + lax.axis_index("c")
    place_arr = place.astype(jnp.int32).reshape(1)
    bdim = w[w_out_names[0]].shape[0] * N_DEV
    wc = bdim // N_DEV

    conv_rows = jnp.concatenate([w["conv_w_%d" % i] for i in conv_layers], axis=0)
    first = _all_gather([_cast(w[n], BF16, "cast_" + n)[None] for n in (w_in_names[0], w_out_names[0])] + [conv_rows[None]],
                        "gather_first_layer")
    slots = {n: _into_slot(w[n], place_arr, BF16, "slot_" + n) for n in w_in_names[1:] + w_out_names[1:]}
    slots = {n: a.reshape((N_CHIP, 2) + a.shape[1:]) for n, a in slots.items()}
    conv_all = first[2][0].reshape(N_DEV, len(conv_layers), CONV_K, wc)
    conv_all = jnp.transpose(conv_all, (1, 2, 0, 3)).reshape(len(conv_layers), CONV_K, bdim)
    conv_full = {layer: conv_all[n] for n, layer in enumerate(conv_layers)}
    weights = [(first[0][0], first[1][0].reshape(bdim, d))]

    saved = []
    for i in range(DEPTH):
        w_in, w_out = weights[i]
        more = i + 1 < DEPTH
        nxt_in = ([slots[w_in_names[i + 1]]], False) if more else None
        nxt_out = ([slots[w_out_names[i + 1]]], False) if more else None
        u, u_t = _rmsnorm_fwd(h, gains["ln_pre_%d" % i], [], "pre_norm_%d" % i)
        if i % 2 == 0:
            proj = _proj(u, w_in, 0, N_DEV, F32, "proj_%d" % i, hosted=nxt_in)
            if more:
                proj, crossed_in = proj
            a, a_t = _conv_gate_fwd(proj, conv_full[i], "conv_gate_%d" % i)
            extra = (proj,)
        else:
            qkv = _proj(u, w_in, 0, 6, BF16, "proj_qkv_%d" % i, hosted=nxt_in)
            if more:
                qkv, crossed_in = qkv
            z = _proj(u, w_in, 6, 2, F32, "proj_z_%d" % i)
            o, carries = _sb_attn_fwd(qkv, "sb_attn_%d" % i)
            a, a_t = _sb_gate_fwd(z, o, "sb_gate_%d" % i)
            extra = (qkv, z, o, carries)
        m = _out_proj(a, w_out, "out_proj_%d" % i, hosted=nxt_out)
        if more:
            m, crossed_out = m
            both = _gather_core_pair([crossed_in[0], crossed_out[0]], "gather_core_pair_%d" % (i + 1))
            weights.append((both[0].reshape((N_DEV,) + both[0].shape[2:]), both[1].reshape(bdim, d)))
        saved.append((h, u_t, a_t, m, extra))
        h = _post_norm_residual(h, m, gains["ln_post_%d" % i], "post_norm_%d" % i)

    dh, loss = _loss_head(h, target, "loss_head")
    loss = lax.psum(loss[0, 0], ("x", "y", "c"))

    small = {}
    chip_parts = {}
    core = lax.axis_index("c").astype(jnp.int32).reshape(1)
    for i in reversed(range(DEPTH)):
        h_in, u_t, a_t, m, extra = saved[i]
        w_in, w_out = weights[i]
        dm, small["ln_post_%d" % i] = _post_norm_bwd(dh, m, gains["ln_post_%d" % i], [], "post_norm_bwd_%d" % i)
        g_out = _weight_grad(a_t, dm, 1, "grad_w_out_%d" % i).reshape(N_CHIP, 2, wc, d)

        def pair_sum(g, kind):
            (from_sibling,) = _exchange_core_pair([g], "reduce_core_pair_%s_%d" % (kind, i))
            return _add_core_pair(g, from_sibling, core, "add_core_pair_%s_%d" % (kind, i))

        pair_out = pair_sum(g_out, "out")
        da, landed = _out_proj_bwd_act(dm, w_out, "out_proj_bwd_%d" % i, hosted=([pair_out], True))
        chip_parts[w_out_names[i]] = (pair_out, landed[0])
        if i % 2 == 0:
            (proj,) = extra
            dproj, small["conv_w_%d" % i] = _conv_gate_bwd(proj, da, conv_full[i], "conv_gate_bwd_%d" % i)
        else:
            qkv, z, o, carries = extra
            do, dz = _sb_gate_bwd(da, z, o, "sb_gate_bwd_%d" % i)
            dq, dk, dv = _sb_attn_bwd(qkv, do, carries, "sb_attn_bwd_%d" % i)
            dproj = jnp.concatenate([dq, dk, dv, dz], axis=1)
        g_in = _weight_grad(u_t, dproj, N_DEV, "grad_w_in_%d" % i)
        pair_in = pair_sum(g_in.reshape((N_CHIP, 2) + g_in.shape[1:]), "in")
        du, landed = _proj_bwd_act(dproj, w_in, "proj_bwd_%d" % i, hosted=([pair_in], True))
        chip_parts[w_in_names[i]] = (pair_in, landed[0])
        dh, small["ln_pre_%d" % i] = _pre_norm_bwd(du, h_in, gains["ln_pre_%d" % i], dh, [], "pre_norm_bwd_%d" % i)
    big_names = w_in_names + w_out_names

    gain_names = [n for n in names if n.startswith("ln_")]
    conv_names = ["conv_w_%d" % i for i in conv_layers]
    rows = [small[n] for n in gain_names] + [small[n] for n in conv_names]
    n_rows = len(gain_names) + CONV_K * len(conv_names)
    pad = -n_rows % SUBLANES
    stacked = jnp.concatenate(rows + [jnp.zeros((pad, d), F32)], axis=0)
    (small_all,) = _all_gather([stacked[None]], "gather_small_grads")
    small_all = small_all[0]

    out_g, out_d, out_m, out_v = {}, {}, {}, {}

    def update(n, w2, parts, m2, v2, shape):
        g2, d2, nm2, nv2 = _adamw(w2, parts, m2, v2, "adamw_" + n)
        out_g[n], out_d[n], out_m[n], out_v[n] = (t.reshape(shape) for t in (g2, d2, nm2, nv2))

    chip_arr = (2 * lax.axis_index("x") + lax.axis_index("y")).astype(jnp.int32).reshape(1)
    for n in big_names:
        own, landed = chip_parts[n]
        out_g[n], out_d[n], out_m[n], out_v[n] = _adamw_shard(w[n], own, landed, mom[n], var[n], chip_arr, "adamw_" + n)
    n_gain = len(gain_names)
    stack = lambda src: jnp.stack([src[n] for n in gain_names])
    g2, d2, nm2, nv2 = _adamw(stack(w), small_all[:, :n_gain], stack(mom), stack(var), "adamw_gains")
    for k, n in enumerate(gain_names):
        out_g[n], out_d[n], out_m[n], out_v[n] = g2[k], d2[k], nm2[k], nv2[k]
    wc = bdim // N_DEV
    for k, n in enumerate(conv_names):
        rows_k = small_all[:, n_gain + CONV_K * k:n_gain + CONV_K * (k + 1)]
        parts = lax.dynamic_slice_in_dim(rows_k, place * wc, wc, axis=2)
        update(n, w[n], parts, mom[n], var[n], w[n].shape)

    grad_x = dh.reshape(x.shape)
    return (loss, grad_x, *[out_g[n] for n in names], *[out_d[n] for n in names],
            *[out_m[n] for n in names], *[out_v[n] for n in names])
```

```python
import functools
import math

import jax
import jax.numpy as jnp
from jax import lax
from jax.experimental import pallas as pl
from jax.experimental.pallas import tpu as pltpu

F32 = jnp.float32
BF16 = jnp.bfloat16
MESH = pl.DeviceIdType.MESH
ANY = pl.BlockSpec(memory_space=pl.ANY)

N_DEV = 8
N_CHIP = 4
DEPTH = 4
HEAD_DIM = 128
CONV_K = 3
RMS_EPS = 1e-6
ADAM_LR = 0.001
ADAM_B1 = 0.9
ADAM_B2 = 0.999
ADAM_EPS = 1e-08
ADAM_WD = 0.01
ADAM_STEP = 10

V7X_VMEM_BYTES = 64 * 1024 * 1024
VMEM_LIMIT = V7X_VMEM_BYTES * 3 // 4
LANES = 128
SUBLANES = 8
HEADS_PER_STEP = 2
DEAD_CARRY = -128.0
UNVISITED = -1e30


def _params(*sem):
    return pltpu.CompilerParams(dimension_semantics=sem, vmem_limit_bytes=VMEM_LIMIT)


def _silu_parts(z):
    sig = jax.nn.sigmoid(z)
    return z * sig, sig


NN = (((1,), (0,)), ((), ()))
NT = (((1,), (1,)), ((), ()))
TN = (((0,), (0,)), ((), ()))


def _gridded_call(body, operands, *, grid, in_specs, out_specs, out_shape, scratch_shapes, semantics, name, hosted=None):
    if hosted is None:
        return pl.pallas_call(
            body, grid=grid, in_specs=in_specs, out_specs=out_specs, out_shape=out_shape,
            scratch_shapes=scratch_shapes, compiler_params=_params(*semantics), name=name)(*operands)
    arrays, stage = hosted
    scatter = stage == CHIPS_SCATTER
    n_in, n_out, n_ex, n_scr = len(in_specs), len(out_specs), len(arrays), len(scratch_shapes)

    def hosting_body(*refs):
        ins, refs = refs[:n_in], refs[n_in:]
        ex_in, refs = refs[:n_ex], refs[n_ex:]
        outs, refs = refs[:n_out], refs[n_out:]
        ex_out, refs = refs[:n_ex], refs[n_ex:]
        scratch, sems = refs[:n_scr], refs[n_scr:]
        first = last = None
        for axis, size in enumerate(grid):
            at_start, at_end = pl.program_id(axis) == 0, pl.program_id(axis) == size - 1
            first = at_start if first is None else first & at_start
            last = at_end if last is None else last & at_end

        @pl.when(first)
        def _():
            for cp in _stage_copies(stage, ex_in, ex_out, *sems):
                cp.start()

        body(*ins, *outs, *scratch)

        @pl.when(last)
        def _():
            _wait_all(_stage_copies(stage, ex_in, ex_out, *sems))

    if scatter:
        ex_shapes, aliases = [jax.ShapeDtypeStruct((N_CHIP - 1,) + a.shape[1:], a.dtype) for a in arrays], {}
    else:
        ex_shapes, aliases = [jax.ShapeDtypeStruct(a.shape, a.dtype) for a in arrays], {n_in + a: n_out + a for a in range(n_ex)}
    out = pl.pallas_call(
        hosting_body, grid=grid, in_specs=list(in_specs) + [ANY] * n_ex, out_specs=list(out_specs) + [ANY] * n_ex,
        out_shape=list(out_shape) + ex_shapes, input_output_aliases=aliases,
        scratch_shapes=list(scratch_shapes) + [pltpu.SemaphoreType.DMA((N_CHIP * n_ex,))] * 2,
        compiler_params=_params(*["arbitrary"] * len(grid)), name=name)(*operands, *arrays)
    return out[:n_out], out[n_out:]


def _mm(a, b, *, dims, grid, a_spec, b_spec, o_spec, out_shape, acc_shape, name, hosted=None):
    nk = grid[2]

    def body(a_ref, b_ref, o_ref, *scratch):
        p = lax.dot_general(a_ref[...], b_ref[...], dims, preferred_element_type=F32)
        if nk == 1:
            o_ref[...] = p.astype(o_ref.dtype)
        else:
            acc_ref = scratch[0]
            k = pl.program_id(2)

            @pl.when(k == 0)
            def _():
                acc_ref[...] = p

            @pl.when(k > 0)
            def _():
                acc_ref[...] += p

            @pl.when(k == nk - 1)
            def _():
                o_ref[...] = acc_ref[...].astype(o_ref.dtype)

    scratch = [] if nk == 1 else [pltpu.VMEM(acc_shape, F32)]
    res = _gridded_call(
        body, (a, b), grid=grid, in_specs=[a_spec, b_spec], out_specs=[o_spec], out_shape=[out_shape],
        scratch_shapes=scratch, semantics=("parallel", "parallel", "arbitrary"), name=name, hosted=hosted)
    return res[0] if hosted is None else (res[0][0], res[1])


def _proj(u, w_in, shard0, n_shard, out_dtype, name, hosted=None):
    s, d = u.shape
    ws = w_in.shape[-1]
    tm, tn = min(s, 512), min(ws, 1024)
    nj = ws // tn
    return _mm(
        u, w_in, dims=NN, grid=(s // tm, n_shard * nj, 1),
        a_spec=pl.BlockSpec((tm, d), lambda i, j, k: (i, 0)),
        b_spec=pl.BlockSpec((None, d, tn), lambda i, j, k: (shard0 + j // nj, 0, j % nj)),
        o_spec=pl.BlockSpec((tm, tn), lambda i, j, k: (i, j)),
        out_shape=jax.ShapeDtypeStruct((s, n_shard * ws), out_dtype), acc_shape=(tm, tn), name=name, hosted=hosted,
    )


def _out_proj(a, w_out, name, hosted=None):
    s, bdim = a.shape
    d = w_out.shape[-1]
    tm, tn = min(s, 512), min(d, 1024)
    return _mm(
        a, w_out, dims=NN, grid=(s // tm, d // tn, 1),
        a_spec=pl.BlockSpec((tm, bdim), lambda i, j, k: (i, 0)),
        b_spec=pl.BlockSpec((bdim, tn), lambda i, j, k: (0, j)),
        o_spec=pl.BlockSpec((tm, tn), lambda i, j, k: (i, j)),
        out_shape=jax.ShapeDtypeStruct((s, d), F32), acc_shape=(tm, tn), name=name, hosted=hosted,
    )


def _out_proj_bwd_act(dm, w_out, name, hosted=None):
    s, d = dm.shape
    bdim = w_out.shape[-2]
    tm, tn = min(s, 512), min(bdim, 1024)
    return _mm(
        dm, w_out, dims=NT, grid=(s // tm, bdim // tn, 1),
        a_spec=pl.BlockSpec((tm, d), lambda i, j, k: (i, 0)),
        b_spec=pl.BlockSpec((tn, d), lambda i, j, k: (j, 0)),
        o_spec=pl.BlockSpec((tm, tn), lambda i, j, k: (i, j)),
        out_shape=jax.ShapeDtypeStruct((s, bdim), F32), acc_shape=(tm, tn), name=name, hosted=hosted,
    )


def _weight_grad(act_t, dout, n_blocks, name):
    din, s = act_t.shape
    w = dout.shape[1] // n_blocks
    tm, tn = min(din, 512), min(w, 1024)
    nj = w // tn
    return _mm(
        act_t, dout, dims=NN, grid=(din // tm, n_blocks * nj, 1),
        a_spec=pl.BlockSpec((tm, s), lambda i, j, k: (i, 0)),
        b_spec=pl.BlockSpec((s, tn), lambda i, j, k: (0, j)),
        o_spec=pl.BlockSpec((None, tm, tn), lambda i, j, k: (j // nj, i, j % nj)),
        out_shape=jax.ShapeDtypeStruct((n_blocks, din, w), BF16), acc_shape=(tm, tn), name=name,
    )


def _proj_bwd_act(dproj, w_in, name, hosted=None):
    s = dproj.shape[0]
    n_shards, d, ws = w_in.shape
    tm, tn = min(s, 512), min(d, 512)

    def body(a_ref, b_ref, o_ref):
        acc = None
        for k in range(n_shards):
            p = lax.dot_general(a_ref[:, k * ws:(k + 1) * ws], b_ref[k], NT, preferred_element_type=F32)
            acc = p if acc is None else acc + p
        o_ref[...] = acc

    res = _gridded_call(
        body, (dproj, w_in), grid=(s // tm, d // tn),
        in_specs=[pl.BlockSpec((tm, n_shards * ws), lambda i, j: (i, 0)),
                  pl.BlockSpec((n_shards, tn, ws), lambda i, j: (0, j, 0))],
        out_specs=[pl.BlockSpec((tm, tn), lambda i, j: (i, j))], out_shape=[jax.ShapeDtypeStruct((s, d), F32)],
        scratch_shapes=[], semantics=("parallel", "parallel"), name=name, hosted=hosted)
    return res[0] if hosted is None else (res[0][0], res[1])


def _row_spec(tm, d):
    return pl.BlockSpec((tm, d), lambda i: (i, 0))


def _gain_spec(d):
    return pl.BlockSpec((1, d), lambda i: (0, 0))


def _rstd(x):
    return lax.rsqrt(jnp.mean(x * x, axis=-1, keepdims=True) + RMS_EPS)


def _rmsnorm_fwd(h, gain, name):
    s, d = h.shape
    tm = min(s, 512)

    def body(h_ref, g_ref, u_ref, ut_ref):
        x = h_ref[...]
        u = (x * _rstd(x) * g_ref[...]).astype(u_ref.dtype)
        u_ref[...] = u
        ut_ref[...] = u.T

    return pl.pallas_call(
        body, grid=(s // tm,), in_specs=[_row_spec(tm, d), _gain_spec(d)],
        out_specs=[_row_spec(tm, d), pl.BlockSpec((d, tm), lambda i: (0, i))],
        out_shape=[jax.ShapeDtypeStruct((s, d), BF16), jax.ShapeDtypeStruct((d, s), BF16)],
        compiler_params=_params("parallel"), name=name,
    )(h, gain)


def _cast(block, dtype, name):
    r, c = block.shape
    tr = min(r, 256)

    def body(b_ref, o_ref):
        o_ref[...] = b_ref[...].astype(o_ref.dtype)

    spec = pl.BlockSpec((tr, c), lambda i: (i, 0))
    return pl.pallas_call(
        body, grid=(r // tr,), in_specs=[spec], out_specs=spec, out_shape=jax.ShapeDtypeStruct((r, c), dtype),
        compiler_params=_params("parallel"), name=name,
    )(block)


def _post_norm_residual(h, m, gain, name, hosted=None):
    s, d = h.shape
    tm = min(s, 512)

    def body(h_ref, m_ref, g_ref, o_ref):
        x = m_ref[...]
        o_ref[...] = h_ref[...] + x * _rstd(x) * g_ref[...]

    res = _gridded_call(
        body, (h, m, gain), grid=(s // tm,), in_specs=[_row_spec(tm, d), _row_spec(tm, d), _gain_spec(d)],
        out_specs=[_row_spec(tm, d)], out_shape=[jax.ShapeDtypeStruct((s, d), F32)], scratch_shapes=[],
        semantics=("parallel",), name=name, hosted=hosted)
    return res[0] if hosted is None else (res[0][0], res[1])


def _sum_rows_into(acc_ref, x):
    tm, d = x.shape
    acc_ref[...] += jnp.sum(x.reshape(tm // SUBLANES, SUBLANES, d), axis=0)


def _norm_bwd_body(n_steps, with_residual):
    def body(*refs):
        dy_ref, x_ref, g_ref = refs[:3]
        dres_ref = refs[3] if with_residual else None
        dx_ref, dg_ref, acc_ref = refs[-3:]
        i = pl.program_id(0)

        @pl.when(i == 0)
        def _():
            acc_ref[...] = jnp.zeros_like(acc_ref)

        x = x_ref[...]
        dy = dy_ref[...]
        rstd = _rstd(x)
        n = x * rstd
        dn = dy * g_ref[...]
        dx = rstd * (dn - n * jnp.mean(dn * n, axis=-1, keepdims=True))
        if with_residual:
            dx = dres_ref[...] + dx
        dx_ref[...] = dx.astype(dx_ref.dtype)
        _sum_rows_into(acc_ref, dy * n)

        @pl.when(i == n_steps - 1)
        def _():
            dg_ref[...] = jnp.sum(acc_ref[...], axis=0, keepdims=True)

    return body


def _post_norm_bwd(dh, m, gain, name):
    s, d = m.shape
    tm = min(s, 512)
    n_steps = s // tm
    return pl.pallas_call(
        _norm_bwd_body(n_steps, False), grid=(n_steps,),
        in_specs=[_row_spec(tm, d), _row_spec(tm, d), _gain_spec(d)],
        out_specs=[_row_spec(tm, d), _gain_spec(d)],
        out_shape=[jax.ShapeDtypeStruct((s, d), BF16), jax.ShapeDtypeStruct((1, d), F32)],
        scratch_shapes=[pltpu.VMEM((SUBLANES, d), F32)], compiler_params=_params("arbitrary"), name=name,
    )(dh, m, gain)


def _pre_norm_bwd(du, h, gain, dh, name):
    s, d = h.shape
    tm = min(s, 512)
    n_steps = s // tm
    return pl.pallas_call(
        _norm_bwd_body(n_steps, True), grid=(n_steps,),
        in_specs=[_row_spec(tm, d), _row_spec(tm, d), _gain_spec(d), _row_spec(tm, d)],
        out_specs=[_row_spec(tm, d), _gain_spec(d)],
        out_shape=[jax.ShapeDtypeStruct((s, d), F32), jax.ShapeDtypeStruct((1, d), F32)],
        scratch_shapes=[pltpu.VMEM((SUBLANES, d), F32)], compiler_params=_params("arbitrary"), name=name,
    )(du, h, gain, dh)


def _loss_head(y, target, name):
    s, d = y.shape
    tm = min(s, 512)
    n_steps = s // tm

    def body(y_ref, t_ref, dy_ref, loss_ref, acc_ref):
        i = pl.program_id(0)

        @pl.when(i == 0)
        def _():
            acc_ref[...] = jnp.zeros_like(acc_ref)

        err = y_ref[...] - t_ref[...]
        dy_ref[...] = err / d
        _sum_rows_into(acc_ref, err * err)

        @pl.when(i == n_steps - 1)
        def _():
            total = jnp.sum(jnp.sum(acc_ref[...], axis=0, keepdims=True), axis=1, keepdims=True)
            loss_ref[...] = 0.5 * total / d

    return pl.pallas_call(
        body, grid=(n_steps,), in_specs=[_row_spec(tm, d), _row_spec(tm, d)],
        out_specs=[_row_spec(tm, d), pl.BlockSpec((1, 1), lambda i: (0, 0))],
        out_shape=[jax.ShapeDtypeStruct((s, d), F32), jax.ShapeDtypeStruct((1, 1), F32)],
        scratch_shapes=[pltpu.VMEM((SUBLANES, d), F32)], compiler_params=_params("arbitrary"), name=name,
    )(y, target)


def _shift_down(p, halo, row, n):
    out = jnp.where(row == 0, halo[SUBLANES - n:SUBLANES - n + 1], pltpu.roll(p, n, 0))
    if n == 2:
        out = jnp.where(row == 1, halo[SUBLANES - 1:SUBLANES], out)
    return out


def _shift_up(p, halo, row, n):
    tm = p.shape[0]
    out = jnp.where(row == tm - 1, halo[n - 1:n], pltpu.roll(p, tm - n, 0))
    if n == 2:
        out = jnp.where(row == tm - 2, halo[0:1], out)
    return out


def _conv_specs(tm, tc, nb, n_row_blocks):
    hb = tm // SUBLANES
    cur = lambda part: pl.BlockSpec((tm, tc), lambda i, j: (i, part * nb + j))
    prev = lambda part: pl.BlockSpec((SUBLANES, tc), lambda i, j: (jnp.maximum(i * hb - 1, 0), part * nb + j))
    nxt = lambda part: pl.BlockSpec(
        (SUBLANES, tc), lambda i, j: (jnp.minimum((i + 1) * hb, n_row_blocks * hb - 1), part * nb + j))
    return cur, prev, nxt


def _conv_gate_fwd(proj, conv_w, name):
    s, b4 = proj.shape
    bdim = b4 // 4
    tm, tc = min(s, 512), min(bdim, 512)
    nb = bdim // tc
    cur, prev, _ = _conv_specs(tm, tc, nb, s // tm)

    def body(b_ref, c_ref, x_ref, z_ref, cp_ref, xp_ref, w_ref, a_ref, at_ref):
        i = pl.program_id(0)
        row = lax.broadcasted_iota(jnp.int32, (tm, tc), 0)
        p = c_ref[...] * x_ref[...]
        halo = jnp.where(i > 0, cp_ref[...] * xp_ref[...], 0.0)
        w = w_ref[...]
        cv = w[0:1] * _shift_down(p, halo, row, 2) + w[1:2] * _shift_down(p, halo, row, 1) + w[2:3] * p
        silu, _ = _silu_parts(z_ref[...])
        a = (silu * (b_ref[...] * cv)).astype(a_ref.dtype)
        a_ref[...] = a
        at_ref[...] = a.T

    return pl.pallas_call(
        body, grid=(s // tm, nb),
        in_specs=[cur(0), cur(1), cur(2), cur(3), prev(1), prev(2), pl.BlockSpec((CONV_K, tc), lambda i, j: (0, j))],
        out_specs=[pl.BlockSpec((tm, tc), lambda i, j: (i, j)), pl.BlockSpec((tc, tm), lambda i, j: (j, i))],
        out_shape=[jax.ShapeDtypeStruct((s, bdim), BF16), jax.ShapeDtypeStruct((bdim, s), BF16)],
        compiler_params=_params("parallel", "parallel"), name=name,
    )(proj, proj, proj, proj, proj, proj, conv_w)


def _conv_gate_bwd(proj, da, conv_w, name):
    s, b4 = proj.shape
    bdim = b4 // 4
    tm, tc = min(s, 128), bdim
    nb = bdim // tc
    n_rows = s // tm
    cur, prev, nxt = _conv_specs(tm, tc, nb, n_rows)
    da_cur = pl.BlockSpec((tm, tc), lambda j, i: (i, j))
    hb = tm // SUBLANES
    da_nxt = pl.BlockSpec((SUBLANES, tc), lambda j, i: (jnp.minimum((i + 1) * hb, n_rows * hb - 1), j))
    swap = lambda spec: pl.BlockSpec(spec.block_shape, lambda j, i, f=spec.index_map: f(i, j))

    def body(b_ref, c_ref, x_ref, z_ref, cp_ref, xp_ref, bn_ref, zn_ref, da_ref, dan_ref, w_ref,
             dproj_ref, dw_ref, acc_ref):
        i = pl.program_id(1)

        @pl.when(i == 0)
        def _():
            acc_ref[...] = jnp.zeros_like(acc_ref)

        row = lax.broadcasted_iota(jnp.int32, (tm, tc), 0)
        w = w_ref[...]
        b, c, x = b_ref[...], c_ref[...], x_ref[...]
        p = c * x
        halo_p = jnp.where(i > 0, cp_ref[...] * xp_ref[...], 0.0)
        p1, p2 = _shift_down(p, halo_p, row, 1), _shift_down(p, halo_p, row, 2)
        cv = w[0:1] * p2 + w[1:2] * p1 + w[2:3] * p
        z = z_ref[...]
        silu, sig = _silu_parts(z)
        da = da_ref[...]
        dy = da * silu
        dcv = dy * b
        silu_n, _ = _silu_parts(zn_ref[...])
        halo_d = jnp.where(i < n_rows - 1, dan_ref[...] * silu_n * bn_ref[...], 0.0)
        dp = w[2:3] * dcv + w[1:2] * _shift_up(dcv, halo_d, row, 1) + w[0:1] * _shift_up(dcv, halo_d, row, 2)
        gates = (dy * cv, dp * x, dp * c, da * (b * cv) * (sig * (1.0 + z * (1.0 - sig))))
        for part, dgate in enumerate(gates):
            dproj_ref[:, part * bdim:(part + 1) * bdim] = dgate.astype(dproj_ref.dtype)
        for k, pk in enumerate((p2, p1, p)):
            _sum_rows_into(acc_ref.at[k], dcv * pk)

        @pl.when(i == n_rows - 1)
        def _():
            for k in range(CONV_K):
                dw_ref[k:k + 1, :] = jnp.sum(acc_ref[k], axis=0, keepdims=True)

    out = pl.BlockSpec((tm, b4), lambda j, i: (i, 0))
    return pl.pallas_call(
        body, grid=(nb, n_rows),
        in_specs=[swap(cur(0)), swap(cur(1)), swap(cur(2)), swap(cur(3)), swap(prev(1)), swap(prev(2)),
                  swap(nxt(0)), swap(nxt(3)), da_cur, da_nxt, pl.BlockSpec((CONV_K, tc), lambda j, i: (0, j))],
        out_specs=[out, pl.BlockSpec((CONV_K, tc), lambda j, i: (0, j))],
        out_shape=[jax.ShapeDtypeStruct((s, b4), BF16), jax.ShapeDtypeStruct((CONV_K, bdim), F32)],
        scratch_shapes=[pltpu.VMEM((CONV_K, SUBLANES, tc), F32)],
        compiler_params=_params("parallel", "arbitrary"), name=name,
    )(proj, proj, proj, proj, proj, proj, proj, proj, da, da, conv_w)


def _split(x):
    hi = x.astype(BF16)
    lo = (x - hi.astype(F32)).astype(BF16)
    return jnp.concatenate([hi, lo], axis=1)


def _row_total(x, column):
    return jnp.broadcast_to(x[:, column:column + 1], (x.shape[0], LANES))


def _sb_tiles(qs, ks, carries, suffix_ones, masks, chain=0):
    items = range(len(qs))
    bk = ks[0].shape[0]
    scale = 1.0 / math.sqrt(HEAD_DIM)
    logits = [lax.dot_general(qs[n], ks[n], NT, preferred_element_type=F32) * scale for n in items]
    es = [jnp.exp(-jnp.abs(logits[n])) for n in items]
    keeps = []
    for n in items:
        log_keep = -(jnp.maximum(logits[n], 0.0) + jnp.log(1.0 + es[n]))
        if masks[n] is not None:
            log_keep = jnp.where(masks[n], log_keep, 0.0)
        keeps.append(_split(log_keep))
    tails = [lax.dot_general(keeps[n], suffix_ones, NN, preferred_element_type=F32) for n in items]
    ws, used = [], []
    for n in items:
        carry = carries[n] if n < len(carries) else used[n - chain] + _row_total(tails[n - chain], 0)
        used.append(carry)
        w = jnp.exp(logits[n] + tails[n] + (carry if carry.shape[1] == 1 else _lane_tile(carry, bk)))
        if masks[n] is not None:
            w = jnp.where(masks[n], w, 0.0)
        ws.append(w)
    return logits, es, tails, ws, used


def _tri_twice(n, upper):
    r = lax.broadcasted_iota(jnp.int32, (2 * n, n), 0)
    r = jnp.where(r >= n, r - n, r)
    c = lax.broadcasted_iota(jnp.int32, (2 * n, n), 1)
    return jnp.where(r <= c if upper else r >= c, 1.0, 0.0).astype(BF16)


def _group_spec(s, width, part, n_groups):
    return pl.BlockSpec((s, width), lambda h: (0, part * n_groups + h))


def _head_cols(g):
    return slice(g * HEAD_DIM, (g + 1) * HEAD_DIM)


def _lane_tile(x, n):
    return x if n == LANES else jnp.concatenate([x] * (n // LANES), axis=1)


def _sb_attn_fwd(qkv, name):
    s, b3 = qkv.shape
    bdim = b3 // 3
    hps = min(HEADS_PER_STEP, bdim // HEAD_DIM)
    width = hps * HEAD_DIM
    n_groups = bdim // width
    blk = min(s, 256)
    n_blk = s // blk

    def body(q_ref, k_ref, v_ref, o_ref, car_ref, carry_ref):
        suffix_ones = _tri_twice(blk, upper=False)
        r = lax.broadcasted_iota(jnp.int32, (blk, blk), 0)
        c = lax.broadcasted_iota(jnp.int32, (blk, blk), 1)
        diag_mask = c < r
        lane = lax.broadcasted_iota(jnp.int32, (blk, LANES), 1)

        def q_block(qi, _):
            q0 = pl.multiple_of(qi * blk, blk)
            rows = pl.ds(q0, blk)
            qs = [q_ref[rows, _head_cols(g)] for g in range(hps)]
            o_ref[rows, :] = jnp.zeros((blk, width), F32)
            car_ref[rows, :] = jnp.full((blk, width), UNVISITED, F32)
            carry_ref[...] = jnp.zeros_like(carry_ref)

            def step(js, tile_masks):
                k0s = [pl.multiple_of(j * blk, blk) for j in js]
                items = [(t, g) for t in range(len(js)) for g in range(hps)]
                ks = [k_ref[pl.ds(k0s[t], blk), _head_cols(g)] for t, g in items]
                first = [carry_ref[g] for g in range(hps)]
                _, _, tails, ws, carries = _sb_tiles([qs[g] for _, g in items], ks, first, suffix_ones,
                                                     [tile_masks[t] for t, _ in items], chain=hps)
                for g in range(hps):
                    mine = [n for n, (_, h) in enumerate(items) if h == g]
                    acc, saved = None, car_ref[rows, _head_cols(g)]
                    for n in mine:
                        v = v_ref[pl.ds(k0s[items[n][0]], blk), _head_cols(g)]
                        p = lax.dot_general(ws[n].astype(BF16), v, NN, preferred_element_type=F32)
                        acc = p if acc is None else acc + p
                        saved = jnp.where(lane == js[items[n][0]], carries[n], saved)
                    o_ref[rows, _head_cols(g)] += acc
                    car_ref[rows, _head_cols(g)] = saved
                    carry_ref[g] = carries[mine[-1]] + _row_total(tails[mine[-1]], 0)

            @pl.when(qi == 0)
            def _():
                step([0], [diag_mask])

            @pl.when(qi > 0)
            def _():
                step([qi, qi - 1], [diag_mask, None])

            def alive():
                top = jnp.max(jnp.max(carry_ref[...], axis=0), axis=0, keepdims=True)
                return (jnp.max(top, axis=1, keepdims=True)[0, 0] >= DEAD_CARRY).astype(jnp.int32)

            left = jnp.maximum(qi - 1, 0)

            def pair(state):
                p, _ = state
                j = qi - 2 - 2 * p
                step([j, j - 1], [None, None])
                return p + 1, alive()

            p, live = lax.while_loop(lambda state: (state[0] < left // 2) & (state[1] > 0), pair, (0, alive()))

            @pl.when((left % 2 == 1) & (p == left // 2) & (live > 0))
            def _():
                step([0], [None])

            return 0

        lax.fori_loop(0, n_blk, q_block, 0)

    out = pl.BlockSpec((s, width), lambda h: (0, h))
    shape = jax.ShapeDtypeStruct((s, bdim), F32)
    return pl.pallas_call(
        body, grid=(n_groups,),
        in_specs=[_group_spec(s, width, part, n_groups) for part in range(3)],
        out_specs=[out, out], out_shape=[shape, shape], scratch_shapes=[pltpu.VMEM((hps, blk, LANES), F32)],
        compiler_params=_params("parallel"), name=name,
    )(qkv, qkv, qkv)


def _sb_attn_bwd(qkv, do, carries, name):
    s, b3 = qkv.shape
    bdim = b3 // 3
    hps = min(HEADS_PER_STEP, bdim // HEAD_DIM)
    width = hps * HEAD_DIM
    n_groups = bdim // width
    blk = min(s, 256)
    n_blk = s // blk
    scale = 1.0 / math.sqrt(HEAD_DIM)

    def body(q_ref, k_ref, v_ref, do_ref, car_ref, dq_ref, dk_ref, dv_ref, dk_acc, dv_acc, dq_acc, before_ref):
        suffix_ones = _tri_twice(blk, upper=False)
        prefix_ones = _tri_twice(blk, upper=True)
        r = lax.broadcasted_iota(jnp.int32, (blk, blk), 0)
        c = lax.broadcasted_iota(jnp.int32, (blk, blk), 1)
        diag_mask = c < r
        lane = lax.broadcasted_iota(jnp.int32, (blk, LANES), 1)
        dk_acc[...] = jnp.zeros_like(dk_acc)
        dv_acc[...] = jnp.zeros_like(dv_acc)

        def q_block(qi, _):
            q0 = pl.multiple_of(qi * blk, blk)
            rows = pl.ds(q0, blk)
            qs = [q_ref[rows, _head_cols(g)] for g in range(hps)]
            dos = [do_ref[rows, _head_cols(g)] for g in range(hps)]
            dq_acc[...] = jnp.zeros_like(dq_acc)
            before_ref[...] = jnp.zeros_like(before_ref)

            def step(js, tile_masks):
                masks = [tile_masks[t] for t in range(len(js)) for _ in range(hps)]
                k0s = [pl.multiple_of(j * blk, blk) for j in js]
                items = [(t, g) for t in range(len(js)) for g in range(hps)]
                every = range(len(items))
                ks = [k_ref[pl.ds(k0s[t], blk), _head_cols(g)] for t, g in items]
                dws = [lax.dot_general(dos[g], v_ref[pl.ds(k0s[t], blk), _head_cols(g)], NT, preferred_element_type=F32)
                       for t, g in items]
                carries = [jnp.sum(jnp.where(lane == js[t], car_ref[rows, _head_cols(g)], 0.0), axis=1, keepdims=True)
                           for t, g in items]
                logits, es, _, ws, _ = _sb_tiles([qs[g] for _, g in items], ks, carries, suffix_ones, masks)
                gws = [dws[n] * ws[n] for n in every]
                g_upto = [lax.dot_general(_split(gws[n]), prefix_ones, NN, preferred_element_type=F32) for n in every]
                dss, befores = [], []
                for n, (t, g) in enumerate(items):
                    before = before_ref[g] if t == 0 else befores[n - hps] + _row_total(g_upto[n - hps], blk - 1)
                    befores.append(before)
                    sig = jnp.where(logits[n] >= 0.0, 1.0, es[n]) / (1.0 + es[n])
                    dlogits = gws[n] - sig * (_lane_tile(before, blk) + g_upto[n])
                    if masks[n] is not None:
                        dlogits = jnp.where(masks[n], dlogits, 0.0)
                    dss.append((dlogits * scale).astype(BF16))
                for g in range(hps):
                    mine = [n for n in every if items[n][1] == g]
                    dq = None
                    for n in mine:
                        k0 = k0s[items[n][0]]
                        p = lax.dot_general(dss[n], ks[n], NN, preferred_element_type=F32)
                        dq = p if dq is None else dq + p
                        dk_acc[pl.ds(k0, blk), _head_cols(g)] += lax.dot_general(
                            dss[n], qs[g], TN, preferred_element_type=F32)
                        dv_acc[pl.ds(k0, blk), _head_cols(g)] += lax.dot_general(
                            ws[n].astype(BF16), dos[g], TN, preferred_element_type=F32)
                    dq_acc[:, _head_cols(g)] += dq
                    before_ref[g] = befores[mine[-1]] + _row_total(g_upto[mine[-1]], blk - 1)

            top = car_ref[rows, _head_cols(0)]
            for g in range(1, hps):
                top = jnp.maximum(top, car_ref[rows, _head_cols(g)])
            top = jnp.max(top, axis=0, keepdims=True)
            lane_row = lax.broadcasted_iota(jnp.int32, (1, LANES), 1)
            counted = jnp.where((top >= DEAD_CARRY) & (lane_row < qi), 1.0, 0.0)
            n_alive = jnp.sum(counted, axis=1, keepdims=True)[0, 0].astype(jnp.int32)
            left = jnp.maximum(n_alive - 1, 0)
            start = qi - 1 - left

            @pl.when(left % 2 == 1)
            def _():
                step([start], [None])

            def pair(p, _):
                j = start + left % 2 + 2 * p
                step([j, j + 1], [None, None])
                return 0

            lax.fori_loop(0, left // 2, pair, 0)

            @pl.when(qi == 0)
            def _():
                step([0], [diag_mask])

            @pl.when(qi > 0)
            def _():
                step([qi - 1, qi], [None, diag_mask])
            dq_ref[rows, :] = dq_acc[...].astype(dq_ref.dtype)
            return 0

        lax.fori_loop(0, n_blk, q_block, 0)
        dk_ref[...] = dk_acc[...].astype(dk_ref.dtype)
        dv_ref[...] = dv_acc[...].astype(dv_ref.dtype)

    group = pl.BlockSpec((s, width), lambda h: (0, h))
    once = pl.BlockSpec((s, width), lambda h: (0, h), pipeline_mode=pl.Buffered(1))
    shape = jax.ShapeDtypeStruct((s, bdim), BF16)
    return pl.pallas_call(
        body, grid=(n_groups,),
        in_specs=[_group_spec(s, width, part, n_groups) for part in range(3)] + [once, once],
        out_specs=[group, group, group], out_shape=[shape, shape, shape],
        scratch_shapes=[pltpu.VMEM((s, width), F32), pltpu.VMEM((s, width), F32), pltpu.VMEM((blk, width), F32),
                        pltpu.VMEM((hps, blk, LANES), F32)],
        compiler_params=_params("parallel"), name=name,
    )(qkv, qkv, qkv, do, carries)


def _sb_gate_fwd(z, o, name):
    s, bdim = z.shape
    tm = min(s, 512)

    def body(z_ref, o_ref, a_ref, at_ref):
        silu, _ = _silu_parts(z_ref[...])
        a = (silu * o_ref[...]).astype(a_ref.dtype)
        a_ref[...] = a
        at_ref[...] = a.T

    return pl.pallas_call(
        body, grid=(s // tm,), in_specs=[_row_spec(tm, bdim), _row_spec(tm, bdim)],
        out_specs=[_row_spec(tm, bdim), pl.BlockSpec((bdim, tm), lambda i: (0, i))],
        out_shape=[jax.ShapeDtypeStruct((s, bdim), BF16), jax.ShapeDtypeStruct((bdim, s), BF16)],
        compiler_params=_params("parallel"), name=name,
    )(z, o)


def _sb_gate_bwd(da, z, o, name):
    s, bdim = z.shape
    tm = min(s, 512)

    def body(da_ref, z_ref, o_ref, do_ref, dz_ref):
        z = z_ref[...]
        da = da_ref[...]
        silu, sig = _silu_parts(z)
        do_ref[...] = (da * silu).astype(do_ref.dtype)
        dz_ref[...] = (da * o_ref[...] * (sig * (1.0 + z * (1.0 - sig)))).astype(dz_ref.dtype)

    spec = _row_spec(tm, bdim)
    shape = jax.ShapeDtypeStruct((s, bdim), BF16)
    return pl.pallas_call(
        body, grid=(s // tm,), in_specs=[spec, spec, spec], out_specs=[spec, spec], out_shape=[shape, shape],
        compiler_params=_params("parallel"), name=name,
    )(da, z, o)


def _into_slot(block, place, dtype, name):
    r, c = block.shape
    tr = min(r, 256)

    def body(place_ref, b_ref, o_ref):
        o_ref[...] = b_ref[...].astype(o_ref.dtype)

    grid_spec = pltpu.PrefetchScalarGridSpec(
        num_scalar_prefetch=1, grid=(r // tr,),
        in_specs=[pl.BlockSpec((tr, c), lambda i, place_ref: (i, 0))],
        out_specs=pl.BlockSpec((None, tr, c), lambda i, place_ref: (place_ref[0], i, 0)),
    )
    return pl.pallas_call(
        body, grid_spec=grid_spec, out_shape=jax.ShapeDtypeStruct((N_DEV, r, c), dtype),
        compiler_params=_params("parallel"), name=name,
    )(place, block)


def _add_core_pair(grads, received, core, name):
    _, _, r, c = grads.shape
    tr = min(r, 256)

    def body(core_ref, g_ref, r_ref, o_ref):
        o_ref[...] = (g_ref[...].astype(F32) + r_ref[...].astype(F32)).astype(o_ref.dtype)

    grid_spec = pltpu.PrefetchScalarGridSpec(
        num_scalar_prefetch=1, grid=(N_CHIP, r // tr),
        in_specs=[pl.BlockSpec((None, None, tr, c), lambda q, i, core_ref: (q, core_ref[0], i, 0)),
                  pl.BlockSpec((None, tr, c), lambda q, i, core_ref: (q, i, 0))],
        out_specs=pl.BlockSpec((None, tr, c), lambda q, i, core_ref: (q, i, 0)),
    )
    return pl.pallas_call(
        body, grid_spec=grid_spec, out_shape=jax.ShapeDtypeStruct((N_CHIP, r, c), BF16),
        compiler_params=_params("parallel", "parallel"), name=name,
    )(core, grads, received)


def _adamw_step(g, w, m, v, g_ref, d_ref, nm_ref, nv_ref):
    new_m = ADAM_B1 * m + (1.0 - ADAM_B1) * g
    new_v = ADAM_B2 * v + (1.0 - ADAM_B2) * (g * g)
    m_hat = new_m / (1.0 - ADAM_B1 ** ADAM_STEP)
    v_hat = new_v / (1.0 - ADAM_B2 ** ADAM_STEP)
    g_ref[...] = g
    d_ref[...] = -ADAM_LR * (m_hat / (jnp.sqrt(v_hat) + ADAM_EPS) + ADAM_WD * w)
    nm_ref[...] = new_m
    nv_ref[...] = new_v


def _adamw(w, parts, m, v, name):
    r, c = w.shape
    n_parts = parts.shape[0]
    tr = min(r, 256)

    def body(w_ref, p_ref, m_ref, v_ref, *out_refs):
        g = p_ref[0].astype(F32)
        for k in range(1, n_parts):
            g = g + p_ref[k].astype(F32)
        _adamw_step(g, w_ref[...], m_ref[...], v_ref[...], *out_refs)

    spec = pl.BlockSpec((tr, c), lambda i: (i, 0))
    shape = jax.ShapeDtypeStruct((r, c), F32)
    return pl.pallas_call(
        body, grid=(r // tr,), in_specs=[spec, pl.BlockSpec((n_parts, tr, c), lambda i: (0, i, 0)), spec, spec],
        out_specs=[spec] * 4, out_shape=[shape] * 4, compiler_params=_params("parallel"), name=name,
    )(w, parts, m, v)


def _adamw_shard(w, grads, landed, m, v, place, name):
    r, c = w.shape
    n_landed = landed.shape[0]
    tr = min(r, 256)

    def body(place_ref, w_ref, own_ref, l_ref, m_ref, v_ref, *out_refs):
        g = own_ref[...].astype(F32)
        for k in range(n_landed):
            g = g + l_ref[k].astype(F32)
        _adamw_step(g, w_ref[...], m_ref[...], v_ref[...], *out_refs)

    spec = pl.BlockSpec((tr, c), lambda i, place_ref: (i, 0))
    grid_spec = pltpu.PrefetchScalarGridSpec(
        num_scalar_prefetch=1, grid=(r // tr,),
        in_specs=[spec, pl.BlockSpec((None, tr, c), lambda i, place_ref: (place_ref[0], i, 0)),
                  pl.BlockSpec((n_landed, tr, c), lambda i, place_ref: (0, i, 0)), spec, spec],
        out_specs=[spec] * 4,
    )
    return pl.pallas_call(
        body, grid_spec=grid_spec, out_shape=[jax.ShapeDtypeStruct((r, c), F32)] * 4,
        compiler_params=_params("parallel"), name=name,
    )(place, w, grads, landed, m, v)


def _place():
    x, y, c = lax.axis_index("x"), lax.axis_index("y"), lax.axis_index("c")
    other_chips = [(1 - x, y), (x, 1 - y), (1 - x, 1 - y)]
    return x, y, c, other_chips


def _all_gather(blocks, name):
    n_arr = len(blocks)
    items = [(a, i) for a, blk in enumerate(blocks) for i in range(blk.shape[0])]
    n_items = len(items)

    def body(*refs):
        srcs, outs = refs[:n_arr], refs[n_arr:2 * n_arr]
        send_sems, recv_sems, local_sems = refs[2 * n_arr:]
        x, y, c, other_chips = _place()
        me, sibling = (x, y, c), (x, y, 1 - c)

        def slot(it, dev):
            a, i = items[it]
            return outs[a].at[i, 4 * dev[0] + 2 * dev[1] + dev[2]]

        def copy(it, k, block_of, to, from_src=False):
            a, i = items[it]
            return pltpu.make_async_remote_copy(
                src_ref=srcs[a].at[i] if from_src else slot(it, block_of), dst_ref=slot(it, block_of),
                send_sem=send_sems.at[it * 7 + k], recv_sem=recv_sems.at[it * 7 + k],
                device_id=to, device_id_type=MESH)

        own = [pltpu.make_async_copy(srcs[items[it][0]].at[items[it][1]], slot(it, me), local_sems.at[it])
               for it in range(n_items)]
        for cp in own:
            cp.start()
        first = []
        for it in range(n_items):
            first.append(copy(it, 0, me, sibling, from_src=True))
            first += [copy(it, 1 + j, me, (*chip, c), from_src=True) for j, chip in enumerate(other_chips)]
        for cp in first:
            cp.start()
        passed = []
        for it in range(n_items):
            for j, chip in enumerate(other_chips):
                copy(it, 1 + j, (*chip, c), me).wait_recv()
                passed.append(copy(it, 4 + j, (*chip, c), sibling))
                passed[-1].start()
        for it in range(n_items):
            copy(it, 0, sibling, me).wait_recv()
            for j, chip in enumerate(other_chips):
                copy(it, 4 + j, (*chip, 1 - c), me).wait_recv()
        for cp in first + passed:
            cp.wait_send()
        for cp in own:
            cp.wait()

    return pl.pallas_call(
        body, in_specs=[ANY] * n_arr, out_specs=[ANY] * n_arr,
        out_shape=[jax.ShapeDtypeStruct((b.shape[0], N_DEV) + b.shape[1:], b.dtype) for b in blocks],
        scratch_shapes=[pltpu.SemaphoreType.DMA((7 * n_items,)), pltpu.SemaphoreType.DMA((7 * n_items,)),
                        pltpu.SemaphoreType.DMA((n_items,))],
        name=name,
    )(*blocks)


def _exchange_core_pair(grads, name):
    n_arr = len(grads)

    def body(*refs):
        srcs, outs = refs[:n_arr], refs[n_arr:2 * n_arr]
        send_sems, recv_sems = refs[2 * n_arr:]
        x, y, c, _ = _place()
        copies = [
            pltpu.make_async_remote_copy(
                src_ref=srcs[a].at[q, 1 - c], dst_ref=outs[a].at[q],
                send_sem=send_sems.at[a * N_CHIP + q], recv_sem=recv_sems.at[a * N_CHIP + q],
                device_id=(x, y, 1 - c), device_id_type=MESH)
            for a in range(n_arr) for q in range(N_CHIP)]
        for cp in copies:
            cp.start()
        for cp in copies:
            cp.wait_recv()
        for cp in copies:
            cp.wait_send()

    return pl.pallas_call(
        body, in_specs=[ANY] * n_arr, out_specs=[ANY] * n_arr,
        out_shape=[jax.ShapeDtypeStruct((N_CHIP,) + g.shape[2:], g.dtype) for g in grads],
        scratch_shapes=[pltpu.SemaphoreType.DMA((N_CHIP * n_arr,)), pltpu.SemaphoreType.DMA((N_CHIP * n_arr,))],
        name=name,
    )(*grads)


CHIPS_SCATTER, CHIPS_GATHER, PAIR_GATHER = "chips_scatter", "chips_gather", "pair_gather"


def _stage_copies(stage, srcs, outs, send_sems, recv_sems):
    x, y, c, other_chips = _place()
    my_chip = 2 * x + y
    copies = []
    for a in range(len(outs)):
        if stage == PAIR_GATHER:
            moves = [(outs[a].at[q, c], outs[a].at[q, c], (x, y, 1 - c)) for q in range(N_CHIP)]
        elif stage == CHIPS_GATHER:
            moves = [(outs[a].at[my_chip, c], outs[a].at[my_chip, c], (*chip, c)) for chip in other_chips]
        else:
            moves = [(srcs[a].at[2 * chip[0] + chip[1]], outs[a].at[j], (*chip, c)) for j, chip in enumerate(other_chips)]
        for k, (src, dst, peer) in enumerate(moves):
            copies.append(pltpu.make_async_remote_copy(
                src_ref=src, dst_ref=dst, send_sem=send_sems.at[a * N_CHIP + k], recv_sem=recv_sems.at[a * N_CHIP + k],
                device_id=peer, device_id_type=MESH))
    return copies


def _wait_all(copies):
    for cp in copies:
        cp.wait_recv()
    for cp in copies:
        cp.wait_send()


def kernel(x, ln_pre_0, conv_w_in_0, conv_w_0, conv_w_out_0, ln_post_0, ln_pre_1, sb_w_in_1, sb_w_out_1, ln_post_1, ln_pre_2, conv_w_in_2, conv_w_2, conv_w_out_2, ln_post_2, ln_pre_3, sb_w_in_3, sb_w_out_3, ln_post_3, loss_target, m_ln_pre_0, m_conv_w_in_0, m_conv_w_0, m_conv_w_out_0, m_ln_post_0, m_ln_pre_1, m_sb_w_in_1, m_sb_w_out_1, m_ln_post_1, m_ln_pre_2, m_conv_w_in_2, m_conv_w_2, m_conv_w_out_2, m_ln_post_2, m_ln_pre_3, m_sb_w_in_3, m_sb_w_out_3, m_ln_post_3, v_ln_pre_0, v_conv_w_in_0, v_conv_w_0, v_conv_w_out_0, v_ln_post_0, v_ln_pre_1, v_sb_w_in_1, v_sb_w_out_1, v_ln_post_1, v_ln_pre_2, v_conv_w_in_2, v_conv_w_2, v_conv_w_out_2, v_ln_post_2, v_ln_pre_3, v_sb_w_in_3, v_sb_w_out_3, v_ln_post_3):
    names = ['ln_pre_0', 'conv_w_in_0', 'conv_w_0', 'conv_w_out_0', 'ln_post_0', 'ln_pre_1', 'sb_w_in_1', 'sb_w_out_1',
             'ln_post_1', 'ln_pre_2', 'conv_w_in_2', 'conv_w_2', 'conv_w_out_2', 'ln_post_2', 'ln_pre_3', 'sb_w_in_3',
             'sb_w_out_3', 'ln_post_3']
    given = dict(locals())
    w = {n: given[n] for n in names}
    mom = {n: given["m_" + n] for n in names}
    var = {n: given["v_" + n] for n in names}
    conv_layers = [i for i in range(DEPTH) if i % 2 == 0]
    w_in_names = [("conv_w_in_%d" if i % 2 == 0 else "sb_w_in_%d") % i for i in range(DEPTH)]
    w_out_names = [("conv_w_out_%d" if i % 2 == 0 else "sb_w_out_%d") % i for i in range(DEPTH)]

    s, d = x.shape[1:]
    h = x.reshape(s, d)
    target = loss_target.reshape(s, d)
    gains = {n: w[n].reshape(1, d) for n in names if n.startswith("ln_")}
    place = 4 * lax.axis_index("x") + 2 * lax.axis_index("y") + lax.axis_index("c")
    place_arr = place.astype(jnp.int32).reshape(1)
    bdim = w[w_out_names[0]].shape[0] * N_DEV
    wc = bdim // N_DEV

    conv_rows = jnp.concatenate([w["conv_w_%d" % i] for i in conv_layers], axis=0)
    first = _all_gather([_cast(w[n], BF16, "cast_" + n)[None] for n in (w_in_names[0], w_out_names[0])] + [conv_rows[None]],
                        "gather_first_layer")
    slots = {n: _into_slot(w[n], place_arr, BF16, "slot_" + n) for n in w_in_names[1:] + w_out_names[1:]}
    slots = {n: a.reshape((N_CHIP, 2) + a.shape[1:]) for n, a in slots.items()}
    conv_all = first[2][0].reshape(N_DEV, len(conv_layers), CONV_K, wc)
    conv_all = jnp.transpose(conv_all, (1, 2, 0, 3)).reshape(len(conv_layers), CONV_K, bdim)
    conv_full = {layer: conv_all[n] for n, layer in enumerate(conv_layers)}
    weights = [(first[0][0], first[1][0].reshape(bdim, d))]

    saved = []
    for i in range(DEPTH):
        w_in, w_out = weights[i]
        more = i + 1 < DEPTH
        nxt_in = ([slots[w_in_names[i + 1]]], CHIPS_GATHER) if more else None
        nxt_out = ([slots[w_out_names[i + 1]]], CHIPS_GATHER) if more else None
        u, u_t = _rmsnorm_fwd(h, gains["ln_pre_%d" % i], "pre_norm_%d" % i)
        if i % 2 == 0:
            proj = _proj(u, w_in, 0, N_DEV, F32, "proj_%d" % i, hosted=nxt_in)
            if more:
                proj, crossed_in = proj
            a, a_t = _conv_gate_fwd(proj, conv_full[i], "conv_gate_%d" % i)
            extra = (proj,)
        else:
            qkv = _proj(u, w_in, 0, 6, BF16, "proj_qkv_%d" % i, hosted=nxt_in)
            if more:
                qkv, crossed_in = qkv
            z = _proj(u, w_in, 6, 2, F32, "proj_z_%d" % i)
            o, carries = _sb_attn_fwd(qkv, "sb_attn_%d" % i)
            a, a_t = _sb_gate_fwd(z, o, "sb_gate_%d" % i)
            extra = (qkv, z, o, carries)
        m = _out_proj(a, w_out, "out_proj_%d" % i, hosted=nxt_out)
        h_in = h
        if more:
            m, crossed_out = m
            h, both = _post_norm_residual(h, m, gains["ln_post_%d" % i], "post_norm_%d" % i,
                                          hosted=([crossed_in[0], crossed_out[0]], PAIR_GATHER))
            weights.append((both[0].reshape((N_DEV,) + both[0].shape[2:]), both[1].reshape(bdim, d)))
        else:
            h = _post_norm_residual(h, m, gains["ln_post_%d" % i], "post_norm_%d" % i)
        saved.append((h_in, u_t, a_t, m, extra))

    dh, loss = _loss_head(h, target, "loss_head")
    loss = lax.psum(loss[0, 0], ("x", "y", "c"))

    small = {}
    chip_parts = {}
    core = lax.axis_index("c").astype(jnp.int32).reshape(1)
    for i in reversed(range(DEPTH)):
        h_in, u_t, a_t, m, extra = saved[i]
        w_in, w_out = weights[i]
        dm, small["ln_post_%d" % i] = _post_norm_bwd(dh, m, gains["ln_post_%d" % i], "post_norm_bwd_%d" % i)
        g_out = _weight_grad(a_t, dm, 1, "grad_w_out_%d" % i).reshape(N_CHIP, 2, wc, d)

        def pair_sum(g, kind):
            (from_sibling,) = _exchange_core_pair([g], "reduce_core_pair_%s_%d" % (kind, i))
            return _add_core_pair(g, from_sibling, core, "add_core_pair_%s_%d" % (kind, i))

        pair_out = pair_sum(g_out, "out")
        da, landed = _out_proj_bwd_act(dm, w_out, "out_proj_bwd_%d" % i, hosted=([pair_out], CHIPS_SCATTER))
        chip_parts[w_out_names[i]] = (pair_out, landed[0])
        if i % 2 == 0:
            (proj,) = extra
            dproj, small["conv_w_%d" % i] = _conv_gate_bwd(proj, da, conv_full[i], "conv_gate_bwd_%d" % i)
        else:
            qkv, z, o, carries = extra
            do, dz = _sb_gate_bwd(da, z, o, "sb_gate_bwd_%d" % i)
            dq, dk, dv = _sb_attn_bwd(qkv, do, carries, "sb_attn_bwd_%d" % i)
            dproj = jnp.concatenate([dq, dk, dv, dz], axis=1)
        g_in = _weight_grad(u_t, dproj, N_DEV, "grad_w_in_%d" % i)
        pair_in = pair_sum(g_in.reshape((N_CHIP, 2) + g_in.shape[1:]), "in")
        du, landed = _proj_bwd_act(dproj, w_in, "proj_bwd_%d" % i, hosted=([pair_in], CHIPS_SCATTER))
        chip_parts[w_in_names[i]] = (pair_in, landed[0])
        dh, small["ln_pre_%d" % i] = _pre_norm_bwd(du, h_in, gains["ln_pre_%d" % i], dh, "pre_norm_bwd_%d" % i)
    big_names = w_in_names + w_out_names

    gain_names = [n for n in names if n.startswith("ln_")]
    conv_names = ["conv_w_%d" % i for i in conv_layers]
    rows = [small[n] for n in gain_names] + [small[n] for n in conv_names]
    n_rows = len(gain_names) + CONV_K * len(conv_names)
    pad = -n_rows % SUBLANES
    stacked = jnp.concatenate(rows + [jnp.zeros((pad, d), F32)], axis=0)
    (small_all,) = _all_gather([stacked[None]], "gather_small_grads")
    small_all = small_all[0]

    out_g, out_d, out_m, out_v = {}, {}, {}, {}

    def update(n, w2, parts, m2, v2, shape):
        g2, d2, nm2, nv2 = _adamw(w2, parts, m2, v2, "adamw_" + n)
        out_g[n], out_d[n], out_m[n], out_v[n] = (t.reshape(shape) for t in (g2, d2, nm2, nv2))

    chip_arr = (2 * lax.axis_index("x") + lax.axis_index("y")).astype(jnp.int32).reshape(1)
    for n in big_names:
        own, landed = chip_parts[n]
        out_g[n], out_d[n], out_m[n], out_v[n] = _adamw_shard(w[n], own, landed, mom[n], var[n], chip_arr, "adamw_" + n)
    n_gain = len(gain_names)
    stack = lambda src: jnp.stack([src[n] for n in gain_names])
    g2, d2, nm2, nv2 = _adamw(stack(w), small_all[:, :n_gain], stack(mom), stack(var), "adamw_gains")
    for k, n in enumerate(gain_names):
        out_g[n], out_d[n], out_m[n], out_v[n] = g2[k], d2[k], nm2[k], nv2[k]
    wc = bdim // N_DEV
    for k, n in enumerate(conv_names):
        rows_k = small_all[:, n_gain + CONV_K * k:n_gain + CONV_K * (k + 1)]
        parts = lax.dynamic_slice_in_dim(rows_k, place * wc, wc, axis=2)
        update(n, w[n], parts, mom[n], var[n], w[n].shape)

    grad_x = dh.reshape(x.shape)
    return (loss, grad_x, *[out_g[n] for n in names], *[out_d[n] for n in names],
            *[out_m[n] for n in names], *[out_v[n] for n in names])
```

```python
import functools
import math

import jax
import jax.numpy as jnp
from jax import lax
from jax.experimental import pallas as pl
from jax.experimental.pallas import tpu as pltpu

F32 = jnp.float32
BF16 = jnp.bfloat16
MESH = pl.DeviceIdType.MESH
ANY = pl.BlockSpec(memory_space=pl.ANY)

N_DEV = 8
N_CHIP = 4
DEPTH = 4
HEAD_DIM = 128
CONV_K = 3
RMS_EPS = 1e-6
ADAM_LR = 0.001
ADAM_B1 = 0.9
ADAM_B2 = 0.999
ADAM_EPS = 1e-08
ADAM_WD = 0.01
ADAM_STEP = 10

V7X_VMEM_BYTES = 64 * 1024 * 1024
VMEM_LIMIT = V7X_VMEM_BYTES * 3 // 4
LANES = 128
SUBLANES = 8
HEADS_PER_STEP = 2
DEAD_CARRY = -128.0
UNVISITED = -1e30


def _params(*sem):
    return pltpu.CompilerParams(dimension_semantics=sem, vmem_limit_bytes=VMEM_LIMIT)


def _silu_parts(z):
    sig = jax.nn.sigmoid(z)
    return z * sig, sig


NN = (((1,), (0,)), ((), ()))
NT = (((1,), (1,)), ((), ()))
TN = (((0,), (0,)), ((), ()))


def _gridded_call(body, operands, *, grid, in_specs, out_specs, out_shape, scratch_shapes, semantics, name, hosted=None):
    if hosted is None:
        return pl.pallas_call(
            body, grid=grid, in_specs=in_specs, out_specs=out_specs, out_shape=out_shape,
            scratch_shapes=scratch_shapes, compiler_params=_params(*semantics), name=name)(*operands)
    arrays, stage = hosted
    scatter = stage == CHIPS_SCATTER
    n_in, n_out, n_ex, n_scr = len(in_specs), len(out_specs), len(arrays), len(scratch_shapes)

    def hosting_body(*refs):
        ins, refs = refs[:n_in], refs[n_in:]
        ex_in, refs = refs[:n_ex], refs[n_ex:]
        outs, refs = refs[:n_out], refs[n_out:]
        ex_out, refs = refs[:n_ex], refs[n_ex:]
        scratch, sems = refs[:n_scr], refs[n_scr:]
        first = last = None
        for axis, size in enumerate(grid):
            at_start, at_end = pl.program_id(axis) == 0, pl.program_id(axis) == size - 1
            first = at_start if first is None else first & at_start
            last = at_end if last is None else last & at_end

        @pl.when(first)
        def _():
            for cp in _stage_copies(stage, ex_in, ex_out, *sems):
                cp.start()

        body(*ins, *outs, *scratch)

        @pl.when(last)
        def _():
            _wait_all(_stage_copies(stage, ex_in, ex_out, *sems))

    if scatter:
        ex_shapes, aliases = [jax.ShapeDtypeStruct((N_CHIP - 1,) + a.shape[1:], a.dtype) for a in arrays], {}
    else:
        ex_shapes, aliases = [jax.ShapeDtypeStruct(a.shape, a.dtype) for a in arrays], {n_in + a: n_out + a for a in range(n_ex)}
    out = pl.pallas_call(
        hosting_body, grid=grid, in_specs=list(in_specs) + [ANY] * n_ex, out_specs=list(out_specs) + [ANY] * n_ex,
        out_shape=list(out_shape) + ex_shapes, input_output_aliases=aliases,
        scratch_shapes=list(scratch_shapes) + [pltpu.SemaphoreType.DMA((N_CHIP * n_ex,))] * 2,
        compiler_params=_params(*["arbitrary"] * len(grid)), name=name)(*operands, *arrays)
    return out[:n_out], out[n_out:]


def _mm(a, b, *, dims, grid, a_spec, b_spec, o_spec, out_shape, acc_shape, name, hosted=None):
    nk = grid[2]

    def body(a_ref, b_ref, o_ref, *scratch):
        p = lax.dot_general(a_ref[...], b_ref[...], dims, preferred_element_type=F32)
        if nk == 1:
            o_ref[...] = p.astype(o_ref.dtype)
        else:
            acc_ref = scratch[0]
            k = pl.program_id(2)

            @pl.when(k == 0)
            def _():
                acc_ref[...] = p

            @pl.when(k > 0)
            def _():
                acc_ref[...] += p

            @pl.when(k == nk - 1)
            def _():
                o_ref[...] = acc_ref[...].astype(o_ref.dtype)

    scratch = [] if nk == 1 else [pltpu.VMEM(acc_shape, F32)]
    res = _gridded_call(
        body, (a, b), grid=grid, in_specs=[a_spec, b_spec], out_specs=[o_spec], out_shape=[out_shape],
        scratch_shapes=scratch, semantics=("parallel", "parallel", "arbitrary"), name=name, hosted=hosted)
    return res[0] if hosted is None else (res[0][0], res[1])


def _proj(u, w_in, shard0, n_shard, out_dtype, name, hosted=None):
    s, d = u.shape
    ws = w_in.shape[-1]
    tm, tn = min(s, 512), min(ws, 1024)
    nj = ws // tn
    return _mm(
        u, w_in, dims=NN, grid=(s // tm, n_shard * nj, 1),
        a_spec=pl.BlockSpec((tm, d), lambda i, j, k: (i, 0)),
        b_spec=pl.BlockSpec((None, d, tn), lambda i, j, k: (shard0 + j // nj, 0, j % nj)),
        o_spec=pl.BlockSpec((tm, tn), lambda i, j, k: (i, j)),
        out_shape=jax.ShapeDtypeStruct((s, n_shard * ws), out_dtype), acc_shape=(tm, tn), name=name, hosted=hosted,
    )


def _out_proj(a, w_out, name, hosted=None):
    s, bdim = a.shape
    d = w_out.shape[-1]
    tm, tn = min(s, 512), min(d, 1024)
    return _mm(
        a, w_out, dims=NN, grid=(s // tm, d // tn, 1),
        a_spec=pl.BlockSpec((tm, bdim), lambda i, j, k: (i, 0)),
        b_spec=pl.BlockSpec((bdim, tn), lambda i, j, k: (0, j)),
        o_spec=pl.BlockSpec((tm, tn), lambda i, j, k: (i, j)),
        out_shape=jax.ShapeDtypeStruct((s, d), F32), acc_shape=(tm, tn), name=name, hosted=hosted,
    )


def _out_proj_bwd_act(dm, w_out, name, hosted=None):
    s, d = dm.shape
    bdim = w_out.shape[-2]
    tm, tn = min(s, 512), min(bdim, 1024)
    return _mm(
        dm, w_out, dims=NT, grid=(s // tm, bdim // tn, 1),
        a_spec=pl.BlockSpec((tm, d), lambda i, j, k: (i, 0)),
        b_spec=pl.BlockSpec((tn, d), lambda i, j, k: (j, 0)),
        o_spec=pl.BlockSpec((tm, tn), lambda i, j, k: (i, j)),
        out_shape=jax.ShapeDtypeStruct((s, bdim), F32), acc_shape=(tm, tn), name=name, hosted=hosted,
    )


def _weight_grad(act_t, dout, n_blocks, name):
    din, s = act_t.shape
    w = dout.shape[1] // n_blocks
    tm, tn = min(din, 512), min(w, 1024)
    nj = w // tn
    return _mm(
        act_t, dout, dims=NN, grid=(din // tm, n_blocks * nj, 1),
        a_spec=pl.BlockSpec((tm, s), lambda i, j, k: (i, 0)),
        b_spec=pl.BlockSpec((s, tn), lambda i, j, k: (0, j)),
        o_spec=pl.BlockSpec((None, tm, tn), lambda i, j, k: (j // nj, i, j % nj)),
        out_shape=jax.ShapeDtypeStruct((n_blocks, din, w), BF16), acc_shape=(tm, tn), name=name,
    )


def _proj_bwd_act(dproj, w_in, name, hosted=None):
    s = dproj.shape[0]
    n_shards, d, ws = w_in.shape
    tm, tn = min(s, 512), min(d, 512)

    def body(a_ref, b_ref, o_ref):
        acc = None
        for k in range(n_shards):
            p = lax.dot_general(a_ref[:, k * ws:(k + 1) * ws], b_ref[k], NT, preferred_element_type=F32)
            acc = p if acc is None else acc + p
        o_ref[...] = acc

    res = _gridded_call(
        body, (dproj, w_in), grid=(s // tm, d // tn),
        in_specs=[pl.BlockSpec((tm, n_shards * ws), lambda i, j: (i, 0)),
                  pl.BlockSpec((n_shards, tn, ws), lambda i, j: (0, j, 0))],
        out_specs=[pl.BlockSpec((tm, tn), lambda i, j: (i, j))], out_shape=[jax.ShapeDtypeStruct((s, d), F32)],
        scratch_shapes=[], semantics=("parallel", "parallel"), name=name, hosted=hosted)
    return res[0] if hosted is None else (res[0][0], res[1])


def _row_spec(tm, d):
    return pl.BlockSpec((tm, d), lambda i: (i, 0))


def _gain_spec(d):
    return pl.BlockSpec((1, d), lambda i: (0, 0))


def _rstd(x):
    return lax.rsqrt(jnp.mean(x * x, axis=-1, keepdims=True) + RMS_EPS)


def _rmsnorm_fwd(h, gain, name):
    s, d = h.shape
    tm = min(s, 512)

    def body(h_ref, g_ref, u_ref, ut_ref):
        x = h_ref[...]
        u = (x * _rstd(x) * g_ref[...]).astype(u_ref.dtype)
        u_ref[...] = u
        ut_ref[...] = u.T

    return pl.pallas_call(
        body, grid=(s // tm,), in_specs=[_row_spec(tm, d), _gain_spec(d)],
        out_specs=[_row_spec(tm, d), pl.BlockSpec((d, tm), lambda i: (0, i))],
        out_shape=[jax.ShapeDtypeStruct((s, d), BF16), jax.ShapeDtypeStruct((d, s), BF16)],
        compiler_params=_params("parallel"), name=name,
    )(h, gain)


def _cast(block, dtype, name):
    r, c = block.shape
    tr = min(r, 256)

    def body(b_ref, o_ref):
        o_ref[...] = b_ref[...].astype(o_ref.dtype)

    spec = pl.BlockSpec((tr, c), lambda i: (i, 0))
    return pl.pallas_call(
        body, grid=(r // tr,), in_specs=[spec], out_specs=spec, out_shape=jax.ShapeDtypeStruct((r, c), dtype),
        compiler_params=_params("parallel"), name=name,
    )(block)


def _post_norm_residual(h, m, gain, next_gain, name, hosted):
    s, d = h.shape
    tm = min(s, 512)

    def body(h_ref, m_ref, g_ref, gn_ref, o_ref, u_ref, ut_ref):
        x = m_ref[...]
        y = h_ref[...] + x * _rstd(x) * g_ref[...]
        o_ref[...] = y
        u = (y * _rstd(y) * gn_ref[...]).astype(u_ref.dtype)
        u_ref[...] = u
        ut_ref[...] = u.T

    return _gridded_call(
        body, (h, m, gain, next_gain), grid=(s // tm,),
        in_specs=[_row_spec(tm, d), _row_spec(tm, d), _gain_spec(d), _gain_spec(d)],
        out_specs=[_row_spec(tm, d), _row_spec(tm, d), pl.BlockSpec((d, tm), lambda i: (0, i))],
        out_shape=[jax.ShapeDtypeStruct((s, d), F32), jax.ShapeDtypeStruct((s, d), BF16), jax.ShapeDtypeStruct((d, s), BF16)],
        scratch_shapes=[], semantics=("parallel",), name=name, hosted=hosted)


def _last_norm_and_loss(h, m, gain, target, name):
    s, d = h.shape
    tm = min(s, 256)
    n_steps = s // tm

    def body(h_ref, m_ref, g_ref, t_ref, loss_ref, dy_ref, dm_ref, dg_ref, loss_acc, dg_acc):
        i = pl.program_id(0)

        @pl.when(i == 0)
        def _():
            loss_acc[...] = jnp.zeros_like(loss_acc)
            dg_acc[...] = jnp.zeros_like(dg_acc)

        x = m_ref[...]
        rstd = _rstd(x)
        n = x * rstd
        err = h_ref[...] + n * g_ref[...] - t_ref[...]
        dy = err / d
        dy_ref[...] = dy
        dn = dy * g_ref[...]
        dm_ref[...] = (rstd * (dn - n * jnp.mean(dn * n, axis=-1, keepdims=True))).astype(dm_ref.dtype)
        _sum_rows_into(loss_acc, err * err)
        _sum_rows_into(dg_acc, dy * n)

        @pl.when(i == n_steps - 1)
        def _():
            total = jnp.sum(jnp.sum(loss_acc[...], axis=0, keepdims=True), axis=1, keepdims=True)
            loss_ref[...] = 0.5 * total / d
            dg_ref[...] = jnp.sum(dg_acc[...], axis=0, keepdims=True)

    row = _row_spec(tm, d)
    return pl.pallas_call(
        body, grid=(n_steps,), in_specs=[row, row, _gain_spec(d), row],
        out_specs=[pl.BlockSpec((1, 1), lambda i: (0, 0)), row, row, _gain_spec(d)],
        out_shape=[jax.ShapeDtypeStruct((1, 1), F32), jax.ShapeDtypeStruct((s, d), F32),
                   jax.ShapeDtypeStruct((s, d), BF16), jax.ShapeDtypeStruct((1, d), F32)],
        scratch_shapes=[pltpu.VMEM((SUBLANES, d), F32), pltpu.VMEM((SUBLANES, d), F32)],
        compiler_params=_params("arbitrary"), name=name,
    )(h, m, gain, target)


def _sum_rows_into(acc_ref, x):
    tm, d = x.shape
    acc_ref[...] += jnp.sum(x.reshape(tm // SUBLANES, SUBLANES, d), axis=0)


def _norm_bwd_body(n_steps, with_residual):
    def body(*refs):
        dy_ref, x_ref, g_ref = refs[:3]
        dres_ref = refs[3] if with_residual else None
        dx_ref, dg_ref, acc_ref = refs[-3:]
        i = pl.program_id(0)

        @pl.when(i == 0)
        def _():
            acc_ref[...] = jnp.zeros_like(acc_ref)

        x = x_ref[...]
        dy = dy_ref[...]
        rstd = _rstd(x)
        n = x * rstd
        dn = dy * g_ref[...]
        dx = rstd * (dn - n * jnp.mean(dn * n, axis=-1, keepdims=True))
        if with_residual:
            dx = dres_ref[...] + dx
        dx_ref[...] = dx.astype(dx_ref.dtype)
        _sum_rows_into(acc_ref, dy * n)

        @pl.when(i == n_steps - 1)
        def _():
            dg_ref[...] = jnp.sum(acc_ref[...], axis=0, keepdims=True)

    return body


def _post_norm_bwd(dh, m, gain, name):
    s, d = m.shape
    tm = min(s, 512)
    n_steps = s // tm
    return pl.pallas_call(
        _norm_bwd_body(n_steps, False), grid=(n_steps,),
        in_specs=[_row_spec(tm, d), _row_spec(tm, d), _gain_spec(d)],
        out_specs=[_row_spec(tm, d), _gain_spec(d)],
        out_shape=[jax.ShapeDtypeStruct((s, d), BF16), jax.ShapeDtypeStruct((1, d), F32)],
        scratch_shapes=[pltpu.VMEM((SUBLANES, d), F32)], compiler_params=_params("arbitrary"), name=name,
    )(dh, m, gain)


def _pre_norm_bwd(du, h, gain, dh, name):
    s, d = h.shape
    tm = min(s, 512)
    n_steps = s // tm
    return pl.pallas_call(
        _norm_bwd_body(n_steps, True), grid=(n_steps,),
        in_specs=[_row_spec(tm, d), _row_spec(tm, d), _gain_spec(d), _row_spec(tm, d)],
        out_specs=[_row_spec(tm, d), _gain_spec(d)],
        out_shape=[jax.ShapeDtypeStruct((s, d), F32), jax.ShapeDtypeStruct((1, d), F32)],
        scratch_shapes=[pltpu.VMEM((SUBLANES, d), F32)], compiler_params=_params("arbitrary"), name=name,
    )(du, h, gain, dh)


def _shift_down(p, halo, row, n):
    out = jnp.where(row == 0, halo[SUBLANES - n:SUBLANES - n + 1], pltpu.roll(p, n, 0))
    if n == 2:
        out = jnp.where(row == 1, halo[SUBLANES - 1:SUBLANES], out)
    return out


def _shift_up(p, halo, row, n):
    tm = p.shape[0]
    out = jnp.where(row == tm - 1, halo[n - 1:n], pltpu.roll(p, tm - n, 0))
    if n == 2:
        out = jnp.where(row == tm - 2, halo[0:1], out)
    return out


def _conv_specs(tm, tc, nb, n_row_blocks):
    hb = tm // SUBLANES
    cur = lambda part: pl.BlockSpec((tm, tc), lambda i, j: (i, part * nb + j))
    prev = lambda part: pl.BlockSpec((SUBLANES, tc), lambda i, j: (jnp.maximum(i * hb - 1, 0), part * nb + j))
    nxt = lambda part: pl.BlockSpec(
        (SUBLANES, tc), lambda i, j: (jnp.minimum((i + 1) * hb, n_row_blocks * hb - 1), part * nb + j))
    return cur, prev, nxt


def _conv_gate_fwd(proj, conv_w, name):
    s, b4 = proj.shape
    bdim = b4 // 4
    tm, tc = min(s, 512), min(bdim, 512)
    nb = bdim // tc
    cur, prev, _ = _conv_specs(tm, tc, nb, s // tm)

    def body(b_ref, c_ref, x_ref, z_ref, cp_ref, xp_ref, w_ref, a_ref, at_ref):
        i = pl.program_id(0)
        row = lax.broadcasted_iota(jnp.int32, (tm, tc), 0)
        p = c_ref[...] * x_ref[...]
        halo = jnp.where(i > 0, cp_ref[...] * xp_ref[...], 0.0)
        w = w_ref[...]
        cv = w[0:1] * _shift_down(p, halo, row, 2) + w[1:2] * _shift_down(p, halo, row, 1) + w[2:3] * p
        silu, _ = _silu_parts(z_ref[...])
        a = (silu * (b_ref[...] * cv)).astype(a_ref.dtype)
        a_ref[...] = a
        at_ref[...] = a.T

    return pl.pallas_call(
        body, grid=(s // tm, nb),
        in_specs=[cur(0), cur(1), cur(2), cur(3), prev(1), prev(2), pl.BlockSpec((CONV_K, tc), lambda i, j: (0, j))],
        out_specs=[pl.BlockSpec((tm, tc), lambda i, j: (i, j)), pl.BlockSpec((tc, tm), lambda i, j: (j, i))],
        out_shape=[jax.ShapeDtypeStruct((s, bdim), BF16), jax.ShapeDtypeStruct((bdim, s), BF16)],
        compiler_params=_params("parallel", "parallel"), name=name,
    )(proj, proj, proj, proj, proj, proj, conv_w)


def _conv_gate_bwd(proj, da, conv_w, name):
    s, b4 = proj.shape
    bdim = b4 // 4
    tm, tc = min(s, 128), bdim
    nb = bdim // tc
    n_rows = s // tm
    cur, prev, nxt = _conv_specs(tm, tc, nb, n_rows)
    da_cur = pl.BlockSpec((tm, tc), lambda j, i: (i, j))
    hb = tm // SUBLANES
    da_nxt = pl.BlockSpec((SUBLANES, tc), lambda j, i: (jnp.minimum((i + 1) * hb, n_rows * hb - 1), j))
    swap = lambda spec: pl.BlockSpec(spec.block_shape, lambda j, i, f=spec.index_map: f(i, j))

    def body(b_ref, c_ref, x_ref, z_ref, cp_ref, xp_ref, bn_ref, zn_ref, da_ref, dan_ref, w_ref,
             dproj_ref, dw_ref, acc_ref):
        i = pl.program_id(1)

        @pl.when(i == 0)
        def _():
            acc_ref[...] = jnp.zeros_like(acc_ref)

        row = lax.broadcasted_iota(jnp.int32, (tm, tc), 0)
        w = w_ref[...]
        b, c, x = b_ref[...], c_ref[...], x_ref[...]
        p = c * x
        halo_p = jnp.where(i > 0, cp_ref[...] * xp_ref[...], 0.0)
        p1, p2 = _shift_down(p, halo_p, row, 1), _shift_down(p, halo_p, row, 2)
        cv = w[0:1] * p2 + w[1:2] * p1 + w[2:3] * p
        z = z_ref[...]
        silu, sig = _silu_parts(z)
        da = da_ref[...]
        dy = da * silu
        dcv = dy * b
        silu_n, _ = _silu_parts(zn_ref[...])
        halo_d = jnp.where(i < n_rows - 1, dan_ref[...] * silu_n * bn_ref[...], 0.0)
        dp = w[2:3] * dcv + w[1:2] * _shift_up(dcv, halo_d, row, 1) + w[0:1] * _shift_up(dcv, halo_d, row, 2)
        gates = (dy * cv, dp * x, dp * c, da * (b * cv) * (sig * (1.0 + z * (1.0 - sig))))
        for part, dgate in enumerate(gates):
            dproj_ref[:, part * bdim:(part + 1) * bdim] = dgate.astype(dproj_ref.dtype)
        for k, pk in enumerate((p2, p1, p)):
            _sum_rows_into(acc_ref.at[k], dcv * pk)

        @pl.when(i == n_rows - 1)
        def _():
            for k in range(CONV_K):
                dw_ref[k:k + 1, :] = jnp.sum(acc_ref[k], axis=0, keepdims=True)

    out = pl.BlockSpec((tm, b4), lambda j, i: (i, 0))
    return pl.pallas_call(
        body, grid=(nb, n_rows),
        in_specs=[swap(cur(0)), swap(cur(1)), swap(cur(2)), swap(cur(3)), swap(prev(1)), swap(prev(2)),
                  swap(nxt(0)), swap(nxt(3)), da_cur, da_nxt, pl.BlockSpec((CONV_K, tc), lambda j, i: (0, j))],
        out_specs=[out, pl.BlockSpec((CONV_K, tc), lambda j, i: (0, j))],
        out_shape=[jax.ShapeDtypeStruct((s, b4), BF16), jax.ShapeDtypeStruct((CONV_K, bdim), F32)],
        scratch_shapes=[pltpu.VMEM((CONV_K, SUBLANES, tc), F32)],
        compiler_params=_params("parallel", "arbitrary"), name=name,
    )(proj, proj, proj, proj, proj, proj, proj, proj, da, da, conv_w)


def _split(x):
    hi = x.astype(BF16)
    lo = (x - hi.astype(F32)).astype(BF16)
    return jnp.concatenate([hi, lo], axis=1)


def _row_total(x, column):
    return jnp.broadcast_to(x[:, column:column + 1], (x.shape[0], LANES))


def _sb_tiles(qs, ks, carries, suffix_ones, masks, chain=0):
    items = range(len(qs))
    bk = ks[0].shape[0]
    scale = 1.0 / math.sqrt(HEAD_DIM)
    logits = [lax.dot_general(qs[n], ks[n], NT, preferred_element_type=F32) * scale for n in items]
    es = [jnp.exp(-jnp.abs(logits[n])) for n in items]
    keeps = []
    for n in items:
        log_keep = -(jnp.maximum(logits[n], 0.0) + jnp.log(1.0 + es[n]))
        if masks[n] is not None:
            log_keep = jnp.where(masks[n], log_keep, 0.0)
        keeps.append(_split(log_keep))
    tails = [lax.dot_general(keeps[n], suffix_ones, NN, preferred_element_type=F32) for n in items]
    ws, used = [], []
    for n in items:
        carry = carries[n] if n < len(carries) else used[n - chain] + _row_total(tails[n - chain], 0)
        used.append(carry)
        w = jnp.exp(logits[n] + tails[n] + (carry if carry.shape[1] == 1 else _lane_tile(carry, bk)))
        if masks[n] is not None:
            w = jnp.where(masks[n], w, 0.0)
        ws.append(w)
    return logits, es, tails, ws, used


def _tri_twice(n, upper):
    r = lax.broadcasted_iota(jnp.int32, (2 * n, n), 0)
    r = jnp.where(r >= n, r - n, r)
    c = lax.broadcasted_iota(jnp.int32, (2 * n, n), 1)
    return jnp.where(r <= c if upper else r >= c, 1.0, 0.0).astype(BF16)


def _group_spec(s, width, part, n_groups):
    return pl.BlockSpec((s, width), lambda h: (0, part * n_groups + h))


def _head_cols(g):
    return slice(g * HEAD_DIM, (g + 1) * HEAD_DIM)


def _lane_tile(x, n):
    return x if n == LANES else jnp.concatenate([x] * (n // LANES), axis=1)


def _sb_attn_fwd(qkv, name):
    s, b3 = qkv.shape
    bdim = b3 // 3
    hps = min(HEADS_PER_STEP, bdim // HEAD_DIM)
    width = hps * HEAD_DIM
    n_groups = bdim // width
    blk = min(s, 256)
    n_blk = s // blk

    def body(q_ref, k_ref, v_ref, o_ref, car_ref, carry_ref):
        suffix_ones = _tri_twice(blk, upper=False)
        r = lax.broadcasted_iota(jnp.int32, (blk, blk), 0)
        c = lax.broadcasted_iota(jnp.int32, (blk, blk), 1)
        diag_mask = c < r
        lane = lax.broadcasted_iota(jnp.int32, (blk, LANES), 1)

        def q_block(qi, _):
            q0 = pl.multiple_of(qi * blk, blk)
            rows = pl.ds(q0, blk)
            qs = [q_ref[rows, _head_cols(g)] for g in range(hps)]
            o_ref[rows, :] = jnp.zeros((blk, width), F32)
            car_ref[rows, :] = jnp.full((blk, width), UNVISITED, F32)
            carry_ref[...] = jnp.zeros_like(carry_ref)

            def step(js, tile_masks):
                k0s = [pl.multiple_of(j * blk, blk) for j in js]
                items = [(t, g) for t in range(len(js)) for g in range(hps)]
                ks = [k_ref[pl.ds(k0s[t], blk), _head_cols(g)] for t, g in items]
                first = [carry_ref[g] for g in range(hps)]
                _, _, tails, ws, carries = _sb_tiles([qs[g] for _, g in items], ks, first, suffix_ones,
                                                     [tile_masks[t] for t, _ in items], chain=hps)
                for g in range(hps):
                    mine = [n for n, (_, h) in enumerate(items) if h == g]
                    acc, saved = None, car_ref[rows, _head_cols(g)]
                    for n in mine:
                        v = v_ref[pl.ds(k0s[items[n][0]], blk), _head_cols(g)]
                        p = lax.dot_general(ws[n].astype(BF16), v, NN, preferred_element_type=F32)
                        acc = p if acc is None else acc + p
                        saved = jnp.where(lane == js[items[n][0]], carries[n], saved)
                    o_ref[rows, _head_cols(g)] += acc
                    car_ref[rows, _head_cols(g)] = saved
                    carry_ref[g] = carries[mine[-1]] + _row_total(tails[mine[-1]], 0)

            @pl.when(qi == 0)
            def _():
                step([0], [diag_mask])

            @pl.when(qi > 0)
            def _():
                step([qi, qi - 1], [diag_mask, None])

            def alive():
                top = jnp.max(jnp.max(carry_ref[...], axis=0), axis=0, keepdims=True)
                return (jnp.max(top, axis=1, keepdims=True)[0, 0] >= DEAD_CARRY).astype(jnp.int32)

            left = jnp.maximum(qi - 1, 0)

            def pair(state):
                p, _ = state
                j = qi - 2 - 2 * p
                step([j, j - 1], [None, None])
                return p + 1, alive()

            p, live = lax.while_loop(lambda state: (state[0] < left // 2) & (state[1] > 0), pair, (0, alive()))

            @pl.when((left % 2 == 1) & (p == left // 2) & (live > 0))
            def _():
                step([0], [None])

            return 0

        lax.fori_loop(0, n_blk, q_block, 0)

    out = pl.BlockSpec((s, width), lambda h: (0, h))
    shape = jax.ShapeDtypeStruct((s, bdim), F32)
    return pl.pallas_call(
        body, grid=(n_groups,),
        in_specs=[_group_spec(s, width, part, n_groups) for part in range(3)],
        out_specs=[out, out], out_shape=[shape, shape], scratch_shapes=[pltpu.VMEM((hps, blk, LANES), F32)],
        compiler_params=_params("parallel"), name=name,
    )(qkv, qkv, qkv)


def _sb_attn_bwd(qkv, do, carries, name):
    s, b3 = qkv.shape
    bdim = b3 // 3
    hps = min(HEADS_PER_STEP, bdim // HEAD_DIM)
    width = hps * HEAD_DIM
    n_groups = bdim // width
    blk = min(s, 256)
    n_blk = s // blk
    scale = 1.0 / math.sqrt(HEAD_DIM)

    def body(q_ref, k_ref, v_ref, do_ref, car_ref, dq_ref, dk_ref, dv_ref, dk_acc, dv_acc, dq_acc, before_ref):
        suffix_ones = _tri_twice(blk, upper=False)
        prefix_ones = _tri_twice(blk, upper=True)
        r = lax.broadcasted_iota(jnp.int32, (blk, blk), 0)
        c = lax.broadcasted_iota(jnp.int32, (blk, blk), 1)
        diag_mask = c < r
        lane = lax.broadcasted_iota(jnp.int32, (blk, LANES), 1)
        dk_acc[...] = jnp.zeros_like(dk_acc)
        dv_acc[...] = jnp.zeros_like(dv_acc)

        def q_block(qi, _):
            q0 = pl.multiple_of(qi * blk, blk)
            rows = pl.ds(q0, blk)
            qs = [q_ref[rows, _head_cols(g)] for g in range(hps)]
            dos = [do_ref[rows, _head_cols(g)] for g in range(hps)]
            dq_acc[...] = jnp.zeros_like(dq_acc)
            before_ref[...] = jnp.zeros_like(before_ref)

            def step(js, tile_masks):
                masks = [tile_masks[t] for t in range(len(js)) for _ in range(hps)]
                k0s = [pl.multiple_of(j * blk, blk) for j in js]
                items = [(t, g) for t in range(len(js)) for g in range(hps)]
                every = range(len(items))
                ks = [k_ref[pl.ds(k0s[t], blk), _head_cols(g)] for t, g in items]
                dws = [lax.dot_general(dos[g], v_ref[pl.ds(k0s[t], blk), _head_cols(g)], NT, preferred_element_type=F32)
                       for t, g in items]
                carries = [jnp.sum(jnp.where(lane == js[t], car_ref[rows, _head_cols(g)], 0.0), axis=1, keepdims=True)
                           for t, g in items]
                logits, es, _, ws, _ = _sb_tiles([qs[g] for _, g in items], ks, carries, suffix_ones, masks)
                gws = [dws[n] * ws[n] for n in every]
                g_upto = [lax.dot_general(_split(gws[n]), prefix_ones, NN, preferred_element_type=F32) for n in every]
                dss, befores = [], []
                for n, (t, g) in enumerate(items):
                    before = before_ref[g] if t == 0 else befores[n - hps] + _row_total(g_upto[n - hps], blk - 1)
                    befores.append(before)
                    sig = jnp.where(logits[n] >= 0.0, 1.0, es[n]) / (1.0 + es[n])
                    dlogits = gws[n] - sig * (_lane_tile(before, blk) + g_upto[n])
                    if masks[n] is not None:
                        dlogits = jnp.where(masks[n], dlogits, 0.0)
                    dss.append((dlogits * scale).astype(BF16))
                for g in range(hps):
                    mine = [n for n in every if items[n][1] == g]
                    dq = None
                    for n in mine:
                        k0 = k0s[items[n][0]]
                        p = lax.dot_general(dss[n], ks[n], NN, preferred_element_type=F32)
                        dq = p if dq is None else dq + p
                        dk_acc[pl.ds(k0, blk), _head_cols(g)] += lax.dot_general(
                            dss[n], qs[g], TN, preferred_element_type=F32)
                        dv_acc[pl.ds(k0, blk), _head_cols(g)] += lax.dot_general(
                            ws[n].astype(BF16), dos[g], TN, preferred_element_type=F32)
                    dq_acc[:, _head_cols(g)] += dq
                    before_ref[g] = befores[mine[-1]] + _row_total(g_upto[mine[-1]], blk - 1)

            top = car_ref[rows, _head_cols(0)]
            for g in range(1, hps):
                top = jnp.maximum(top, car_ref[rows, _head_cols(g)])
            top = jnp.max(top, axis=0, keepdims=True)
            lane_row = lax.broadcasted_iota(jnp.int32, (1, LANES), 1)
            counted = jnp.where((top >= DEAD_CARRY) & (lane_row < qi), 1.0, 0.0)
            n_alive = jnp.sum(counted, axis=1, keepdims=True)[0, 0].astype(jnp.int32)
            left = jnp.maximum(n_alive - 1, 0)
            start = qi - 1 - left

            @pl.when(left % 2 == 1)
            def _():
                step([start], [None])

            def pair(p, _):
                j = start + left % 2 + 2 * p
                step([j, j + 1], [None, None])
                return 0

            lax.fori_loop(0, left // 2, pair, 0)

            @pl.when(qi == 0)
            def _():
                step([0], [diag_mask])

            @pl.when(qi > 0)
            def _():
                step([qi - 1, qi], [None, diag_mask])
            dq_ref[rows, :] = dq_acc[...].astype(dq_ref.dtype)
            return 0

        lax.fori_loop(0, n_blk, q_block, 0)
        dk_ref[...] = dk_acc[...].astype(dk_ref.dtype)
        dv_ref[...] = dv_acc[...].astype(dv_ref.dtype)

    group = pl.BlockSpec((s, width), lambda h: (0, h))
    once = pl.BlockSpec((s, width), lambda h: (0, h), pipeline_mode=pl.Buffered(1))
    shape = jax.ShapeDtypeStruct((s, bdim), BF16)
    return pl.pallas_call(
        body, grid=(n_groups,),
        in_specs=[_group_spec(s, width, part, n_groups) for part in range(3)] + [once, once],
        out_specs=[group, group, group], out_shape=[shape, shape, shape],
        scratch_shapes=[pltpu.VMEM((s, width), F32), pltpu.VMEM((s, width), F32), pltpu.VMEM((blk, width), F32),
                        pltpu.VMEM((hps, blk, LANES), F32)],
        compiler_params=_params("parallel"), name=name,
    )(qkv, qkv, qkv, do, carries)


def _sb_gate_fwd(z, o, name):
    s, bdim = z.shape
    tm = min(s, 512)

    def body(z_ref, o_ref, a_ref, at_ref):
        silu, _ = _silu_parts(z_ref[...])
        a = (silu * o_ref[...]).astype(a_ref.dtype)
        a_ref[...] = a
        at_ref[...] = a.T

    return pl.pallas_call(
        body, grid=(s // tm,), in_specs=[_row_spec(tm, bdim), _row_spec(tm, bdim)],
        out_specs=[_row_spec(tm, bdim), pl.BlockSpec((bdim, tm), lambda i: (0, i))],
        out_shape=[jax.ShapeDtypeStruct((s, bdim), BF16), jax.ShapeDtypeStruct((bdim, s), BF16)],
        compiler_params=_params("parallel"), name=name,
    )(z, o)


def _sb_gate_bwd(da, z, o, name):
    s, bdim = z.shape
    tm = min(s, 512)

    def body(da_ref, z_ref, o_ref, do_ref, dz_ref):
        z = z_ref[...]
        da = da_ref[...]
        silu, sig = _silu_parts(z)
        do_ref[...] = (da * silu).astype(do_ref.dtype)
        dz_ref[...] = (da * o_ref[...] * (sig * (1.0 + z * (1.0 - sig)))).astype(dz_ref.dtype)

    spec = _row_spec(tm, bdim)
    shape = jax.ShapeDtypeStruct((s, bdim), BF16)
    return pl.pallas_call(
        body, grid=(s // tm,), in_specs=[spec, spec, spec], out_specs=[spec, spec], out_shape=[shape, shape],
        compiler_params=_params("parallel"), name=name,
    )(da, z, o)


def _into_slot(block, place, dtype, name):
    r, c = block.shape
    tr = min(r, 256)

    def body(place_ref, b_ref, o_ref):
        o_ref[...] = b_ref[...].astype(o_ref.dtype)

    grid_spec = pltpu.PrefetchScalarGridSpec(
        num_scalar_prefetch=1, grid=(r // tr,),
        in_specs=[pl.BlockSpec((tr, c), lambda i, place_ref: (i, 0))],
        out_specs=pl.BlockSpec((None, tr, c), lambda i, place_ref: (place_ref[0], i, 0)),
    )
    return pl.pallas_call(
        body, grid_spec=grid_spec, out_shape=jax.ShapeDtypeStruct((N_DEV, r, c), dtype),
        compiler_params=_params("parallel"), name=name,
    )(place, block)


def _add_core_pair(grads, received, core, name):
    _, _, r, c = grads.shape
    tr = min(r, 1024)

    def body(core_ref, g_ref, r_ref, o_ref):
        o_ref[...] = (g_ref[...].astype(F32) + r_ref[...].astype(F32)).astype(o_ref.dtype)

    grid_spec = pltpu.PrefetchScalarGridSpec(
        num_scalar_prefetch=1, grid=(N_CHIP, r // tr),
        in_specs=[pl.BlockSpec((None, None, tr, c), lambda q, i, core_ref: (q, core_ref[0], i, 0)),
                  pl.BlockSpec((None, tr, c), lambda q, i, core_ref: (q, i, 0))],
        out_specs=pl.BlockSpec((None, tr, c), lambda q, i, core_ref: (q, i, 0)),
    )
    return pl.pallas_call(
        body, grid_spec=grid_spec, out_shape=jax.ShapeDtypeStruct((N_CHIP, r, c), BF16),
        compiler_params=_params("parallel", "parallel"), name=name,
    )(core, grads, received)


def _adamw_step(g, w, m, v, g_ref, d_ref, nm_ref, nv_ref):
    new_m = ADAM_B1 * m + (1.0 - ADAM_B1) * g
    new_v = ADAM_B2 * v + (1.0 - ADAM_B2) * (g * g)
    m_hat = new_m / (1.0 - ADAM_B1 ** ADAM_STEP)
    v_hat = new_v / (1.0 - ADAM_B2 ** ADAM_STEP)
    g_ref[...] = g
    d_ref[...] = -ADAM_LR * (m_hat / (jnp.sqrt(v_hat) + ADAM_EPS) + ADAM_WD * w)
    nm_ref[...] = new_m
    nv_ref[...] = new_v


def _adamw(w, parts, m, v, name):
    r, c = w.shape
    n_parts = parts.shape[0]
    tr = min(r, 256)

    def body(w_ref, p_ref, m_ref, v_ref, *out_refs):
        g = p_ref[0].astype(F32)
        for k in range(1, n_parts):
            g = g + p_ref[k].astype(F32)
        _adamw_step(g, w_ref[...], m_ref[...], v_ref[...], *out_refs)

    spec = pl.BlockSpec((tr, c), lambda i: (i, 0))
    shape = jax.ShapeDtypeStruct((r, c), F32)
    return pl.pallas_call(
        body, grid=(r // tr,), in_specs=[spec, pl.BlockSpec((n_parts, tr, c), lambda i: (0, i, 0)), spec, spec],
        out_specs=[spec] * 4, out_shape=[shape] * 4, compiler_params=_params("parallel"), name=name,
    )(w, parts, m, v)


def _adamw_shard(w, grads, landed, m, v, place, name):
    r, c = w.shape
    n_landed = landed.shape[0]
    tr = min(r, 512)

    def body(place_ref, w_ref, own_ref, l_ref, m_ref, v_ref, *out_refs):
        g = own_ref[...].astype(F32)
        for k in range(n_landed):
            g = g + l_ref[k].astype(F32)
        _adamw_step(g, w_ref[...], m_ref[...], v_ref[...], *out_refs)

    spec = pl.BlockSpec((tr, c), lambda i, place_ref: (i, 0))
    grid_spec = pltpu.PrefetchScalarGridSpec(
        num_scalar_prefetch=1, grid=(r // tr,),
        in_specs=[spec, pl.BlockSpec((None, tr, c), lambda i, place_ref: (place_ref[0], i, 0)),
                  pl.BlockSpec((n_landed, tr, c), lambda i, place_ref: (0, i, 0)), spec, spec],
        out_specs=[spec] * 4,
    )
    return pl.pallas_call(
        body, grid_spec=grid_spec, out_shape=[jax.ShapeDtypeStruct((r, c), F32)] * 4,
        compiler_params=_params("parallel"), name=name,
    )(place, w, grads, landed, m, v)


def _place():
    x, y, c = lax.axis_index("x"), lax.axis_index("y"), lax.axis_index("c")
    other_chips = [(1 - x, y), (x, 1 - y), (1 - x, 1 - y)]
    return x, y, c, other_chips


def _all_gather(blocks, name):
    n_arr = len(blocks)
    items = [(a, i) for a, blk in enumerate(blocks) for i in range(blk.shape[0])]
    n_items = len(items)

    def body(*refs):
        srcs, outs = refs[:n_arr], refs[n_arr:2 * n_arr]
        send_sems, recv_sems, local_sems = refs[2 * n_arr:]
        x, y, c, other_chips = _place()
        me, sibling = (x, y, c), (x, y, 1 - c)

        def slot(it, dev):
            a, i = items[it]
            return outs[a].at[i, 4 * dev[0] + 2 * dev[1] + dev[2]]

        def copy(it, k, block_of, to, from_src=False):
            a, i = items[it]
            return pltpu.make_async_remote_copy(
                src_ref=srcs[a].at[i] if from_src else slot(it, block_of), dst_ref=slot(it, block_of),
                send_sem=send_sems.at[it * 7 + k], recv_sem=recv_sems.at[it * 7 + k],
                device_id=to, device_id_type=MESH)

        own = [pltpu.make_async_copy(srcs[items[it][0]].at[items[it][1]], slot(it, me), local_sems.at[it])
               for it in range(n_items)]
        for cp in own:
            cp.start()
        first = []
        for it in range(n_items):
            first.append(copy(it, 0, me, sibling, from_src=True))
            first += [copy(it, 1 + j, me, (*chip, c), from_src=True) for j, chip in enumerate(other_chips)]
        for cp in first:
            cp.start()
        passed = []
        for it in range(n_items):
            for j, chip in enumerate(other_chips):
                copy(it, 1 + j, (*chip, c), me).wait_recv()
                passed.append(copy(it, 4 + j, (*chip, c), sibling))
                passed[-1].start()
        for it in range(n_items):
            copy(it, 0, sibling, me).wait_recv()
            for j, chip in enumerate(other_chips):
                copy(it, 4 + j, (*chip, 1 - c), me).wait_recv()
        for cp in first + passed:
            cp.wait_send()
        for cp in own:
            cp.wait()

    return pl.pallas_call(
        body, in_specs=[ANY] * n_arr, out_specs=[ANY] * n_arr,
        out_shape=[jax.ShapeDtypeStruct((b.shape[0], N_DEV) + b.shape[1:], b.dtype) for b in blocks],
        scratch_shapes=[pltpu.SemaphoreType.DMA((7 * n_items,)), pltpu.SemaphoreType.DMA((7 * n_items,)),
                        pltpu.SemaphoreType.DMA((n_items,))],
        name=name,
    )(*blocks)


def _exchange_core_pair(grads, name):
    n_arr = len(grads)

    def body(*refs):
        srcs, outs = refs[:n_arr], refs[n_arr:2 * n_arr]
        send_sems, recv_sems = refs[2 * n_arr:]
        x, y, c, _ = _place()
        copies = [
            pltpu.make_async_remote_copy(
                src_ref=srcs[a].at[q, 1 - c], dst_ref=outs[a].at[q],
                send_sem=send_sems.at[a * N_CHIP + q], recv_sem=recv_sems.at[a * N_CHIP + q],
                device_id=(x, y, 1 - c), device_id_type=MESH)
            for a in range(n_arr) for q in range(N_CHIP)]
        for cp in copies:
            cp.start()
        for cp in copies:
            cp.wait_recv()
        for cp in copies:
            cp.wait_send()

    return pl.pallas_call(
        body, in_specs=[ANY] * n_arr, out_specs=[ANY] * n_arr,
        out_shape=[jax.ShapeDtypeStruct((N_CHIP,) + g.shape[2:], g.dtype) for g in grads],
        scratch_shapes=[pltpu.SemaphoreType.DMA((N_CHIP * n_arr,)), pltpu.SemaphoreType.DMA((N_CHIP * n_arr,))],
        name=name,
    )(*grads)


CHIPS_SCATTER, CHIPS_GATHER, PAIR_GATHER = "chips_scatter", "chips_gather", "pair_gather"


def _stage_copies(stage, srcs, outs, send_sems, recv_sems):
    x, y, c, other_chips = _place()
    my_chip = 2 * x + y
    copies = []
    for a in range(len(outs)):
        if stage == PAIR_GATHER:
            moves = [(outs[a].at[q, c], outs[a].at[q, c], (x, y, 1 - c)) for q in range(N_CHIP)]
        elif stage == CHIPS_GATHER:
            moves = [(outs[a].at[my_chip, c], outs[a].at[my_chip, c], (*chip, c)) for chip in other_chips]
        else:
            moves = [(srcs[a].at[2 * chip[0] + chip[1]], outs[a].at[j], (*chip, c)) for j, chip in enumerate(other_chips)]
        for k, (src, dst, peer) in enumerate(moves):
            copies.append(pltpu.make_async_remote_copy(
                src_ref=src, dst_ref=dst, send_sem=send_sems.at[a * N_CHIP + k], recv_sem=recv_sems.at[a * N_CHIP + k],
                device_id=peer, device_id_type=MESH))
    return copies


def _wait_all(copies):
    for cp in copies:
        cp.wait_recv()
    for cp in copies:
        cp.wait_send()


def kernel(x, ln_pre_0, conv_w_in_0, conv_w_0, conv_w_out_0, ln_post_0, ln_pre_1, sb_w_in_1, sb_w_out_1, ln_post_1, ln_pre_2, conv_w_in_2, conv_w_2, conv_w_out_2, ln_post_2, ln_pre_3, sb_w_in_3, sb_w_out_3, ln_post_3, loss_target, m_ln_pre_0, m_conv_w_in_0, m_conv_w_0, m_conv_w_out_0, m_ln_post_0, m_ln_pre_1, m_sb_w_in_1, m_sb_w_out_1, m_ln_post_1, m_ln_pre_2, m_conv_w_in_2, m_conv_w_2, m_conv_w_out_2, m_ln_post_2, m_ln_pre_3, m_sb_w_in_3, m_sb_w_out_3, m_ln_post_3, v_ln_pre_0, v_conv_w_in_0, v_conv_w_0, v_conv_w_out_0, v_ln_post_0, v_ln_pre_1, v_sb_w_in_1, v_sb_w_out_1, v_ln_post_1, v_ln_pre_2, v_conv_w_in_2, v_conv_w_2, v_conv_w_out_2, v_ln_post_2, v_ln_pre_3, v_sb_w_in_3, v_sb_w_out_3, v_ln_post_3):
    names = ['ln_pre_0', 'conv_w_in_0', 'conv_w_0', 'conv_w_out_0', 'ln_post_0', 'ln_pre_1', 'sb_w_in_1', 'sb_w_out_1',
             'ln_post_1', 'ln_pre_2', 'conv_w_in_2', 'conv_w_2', 'conv_w_out_2', 'ln_post_2', 'ln_pre_3', 'sb_w_in_3',
             'sb_w_out_3', 'ln_post_3']
    given = dict(locals())
    w = {n: given[n] for n in names}
    mom = {n: given["m_" + n] for n in names}
    var = {n: given["v_" + n] for n in names}
    conv_layers = [i for i in range(DEPTH) if i % 2 == 0]
    w_in_names = [("conv_w_in_%d" if i % 2 == 0 else "sb_w_in_%d") % i for i in range(DEPTH)]
    w_out_names = [("conv_w_out_%d" if i % 2 == 0 else "sb_w_out_%d") % i for i in range(DEPTH)]

    s, d = x.shape[1:]
    h = x.reshape(s, d)
    target = loss_target.reshape(s, d)
    gains = {n: w[n].reshape(1, d) for n in names if n.startswith("ln_")}
    place = 4 * lax.axis_index("x") + 2 * lax.axis_index("y") + lax.axis_index("c")
    place_arr = place.astype(jnp.int32).reshape(1)
    bdim = w[w_out_names[0]].shape[0] * N_DEV
    wc = bdim // N_DEV

    conv_rows = jnp.concatenate([w["conv_w_%d" % i] for i in conv_layers], axis=0)
    first = _all_gather([_cast(w[n], BF16, "cast_" + n)[None] for n in (w_in_names[0], w_out_names[0])] + [conv_rows[None]],
                        "gather_first_layer")
    slots = {n: _into_slot(w[n], place_arr, BF16, "slot_" + n) for n in w_in_names[1:] + w_out_names[1:]}
    slots = {n: a.reshape((N_CHIP, 2) + a.shape[1:]) for n, a in slots.items()}
    conv_all = first[2][0].reshape(N_DEV, len(conv_layers), CONV_K, wc)
    conv_all = jnp.transpose(conv_all, (1, 2, 0, 3)).reshape(len(conv_layers), CONV_K, bdim)
    conv_full = {layer: conv_all[n] for n, layer in enumerate(conv_layers)}
    weights = [(first[0][0], first[1][0].reshape(bdim, d))]

    saved = []
    u, u_t = _rmsnorm_fwd(h, gains["ln_pre_0"], "pre_norm_0")
    for i in range(DEPTH):
        w_in, w_out = weights[i]
        more = i + 1 < DEPTH
        nxt_in = ([slots[w_in_names[i + 1]]], CHIPS_GATHER) if more else None
        nxt_out = ([slots[w_out_names[i + 1]]], CHIPS_GATHER) if more else None
        if i % 2 == 0:
            proj = _proj(u, w_in, 0, N_DEV, F32, "proj_%d" % i, hosted=nxt_in)
            if more:
                proj, crossed_in = proj
            a, a_t = _conv_gate_fwd(proj, conv_full[i], "conv_gate_%d" % i)
            extra = (proj,)
        else:
            qkv = _proj(u, w_in, 0, 6, BF16, "proj_qkv_%d" % i, hosted=nxt_in)
            if more:
                qkv, crossed_in = qkv
            z = _proj(u, w_in, 6, 2, F32, "proj_z_%d" % i)
            o, carries = _sb_attn_fwd(qkv, "sb_attn_%d" % i)
            a, a_t = _sb_gate_fwd(z, o, "sb_gate_%d" % i)
            extra = (qkv, z, o, carries)
        m = _out_proj(a, w_out, "out_proj_%d" % i, hosted=nxt_out)
        if more:
            m, crossed_out = m
        saved.append((h, u_t, a_t, m, extra))
        if more:
            (h, u, u_t), both = _post_norm_residual(
                h, m, gains["ln_post_%d" % i], gains["ln_pre_%d" % (i + 1)], "post_norm_%d" % i,
                hosted=([crossed_in[0], crossed_out[0]], PAIR_GATHER))
            weights.append((both[0].reshape((N_DEV,) + both[0].shape[2:]), both[1].reshape(bdim, d)))

    last = DEPTH - 1
    loss, dh, dm_last, dgain_last = _last_norm_and_loss(h, m, gains["ln_post_%d" % last], target, "last_norm_and_loss")
    loss = lax.psum(loss[0, 0], ("x", "y", "c"))

    small = {}
    chip_parts = {}
    core = lax.axis_index("c").astype(jnp.int32).reshape(1)
    for i in reversed(range(DEPTH)):
        h_in, u_t, a_t, m, extra = saved[i]
        w_in, w_out = weights[i]
        if i == last:
            dm, small["ln_post_%d" % i] = dm_last, dgain_last
        else:
            dm, small["ln_post_%d" % i] = _post_norm_bwd(dh, m, gains["ln_post_%d" % i], "post_norm_bwd_%d" % i)
        g_out = _weight_grad(a_t, dm, 1, "grad_w_out_%d" % i).reshape(N_CHIP, 2, wc, d)

        def pair_sum(g, kind):
            (from_sibling,) = _exchange_core_pair([g], "reduce_core_pair_%s_%d" % (kind, i))
            return _add_core_pair(g, from_sibling, core, "add_core_pair_%s_%d" % (kind, i))

        pair_out = pair_sum(g_out, "out")
        da, landed = _out_proj_bwd_act(dm, w_out, "out_proj_bwd_%d" % i, hosted=([pair_out], CHIPS_SCATTER))
        chip_parts[w_out_names[i]] = (pair_out, landed[0])
        if i % 2 == 0:
            (proj,) = extra
            dproj, small["conv_w_%d" % i] = _conv_gate_bwd(proj, da, conv_full[i], "conv_gate_bwd_%d" % i)
        else:
            qkv, z, o, carries = extra
            do, dz = _sb_gate_bwd(da, z, o, "sb_gate_bwd_%d" % i)
            dq, dk, dv = _sb_attn_bwd(qkv, do, carries, "sb_attn_bwd_%d" % i)
            dproj = jnp.concatenate([dq, dk, dv, dz], axis=1)
        g_in = _weight_grad(u_t, dproj, N_DEV, "grad_w_in_%d" % i)
        pair_in = pair_sum(g_in.reshape((N_CHIP, 2) + g_in.shape[1:]), "in")
        du, landed = _proj_bwd_act(dproj, w_in, "proj_bwd_%d" % i, hosted=([pair_in], CHIPS_SCATTER))
        chip_parts[w_in_names[i]] = (pair_in, landed[0])
        dh, small["ln_pre_%d" % i] = _pre_norm_bwd(du, h_in, gains["ln_pre_%d" % i], dh, "pre_norm_bwd_%d" % i)
    big_names = w_in_names + w_out_names

    gain_names = [n for n in names if n.startswith("ln_")]
    conv_names = ["conv_w_%d" % i for i in conv_layers]
    rows = [small[n] for n in gain_names] + [small[n] for n in conv_names]
    n_rows = len(gain_names) + CONV_K * len(conv_names)
    pad = -n_rows % SUBLANES
    stacked = jnp.concatenate(rows + [jnp.zeros((pad, d), F32)], axis=0)
    (small_all,) = _all_gather([stacked[None]], "gather_small_grads")
    small_all = small_all[0]

    out_g, out_d, out_m, out_v = {}, {}, {}, {}

    def update(n, w2, parts, m2, v2, shape):
        g2, d2, nm2, nv2 = _adamw(w2, parts, m2, v2, "adamw_" + n)
        out_g[n], out_d[n], out_m[n], out_v[n] = (t.reshape(shape) for t in (g2, d2, nm2, nv2))

    chip_arr = (2 * lax.axis_index("x") + lax.axis_index("y")).astype(jnp.int32).reshape(1)
    for n in big_names:
        own, landed = chip_parts[n]
        out_g[n], out_d[n], out_m[n], out_v[n] = _adamw_shard(w[n], own, landed, mom[n], var[n], chip_arr, "adamw_" + n)
    n_gain = len(gain_names)
    stack = lambda src: jnp.stack([src[n] for n in gain_names])
    g2, d2, nm2, nv2 = _adamw(stack(w), small_all[:, :n_gain], stack(mom), stack(var), "adamw_gains")
    for k, n in enumerate(gain_names):
        out_g[n], out_d[n], out_m[n], out_v[n] = g2[k], d2[k], nm2[k], nv2[k]
    wc = bdim // N_DEV
    for k, n in enumerate(conv_names):
        rows_k = small_all[:, n_gain + CONV_K * k:n_gain + CONV_K * (k + 1)]
        parts = lax.dynamic_slice_in_dim(rows_k, place * wc, wc, axis=2)
        update(n, w[n], parts, mom[n], var[n], w[n].shape)

    grad_x = dh.reshape(x.shape)
    return (loss, grad_x, *[out_g[n] for n in names], *[out_d[n] for n in names],
            *[out_m[n] for n in names], *[out_v[n] for n in names])
```

```python
import functools
import math

import jax
import jax.numpy as jnp
from jax import lax
from jax.experimental import pallas as pl
from jax.experimental.pallas import tpu as pltpu

F32 = jnp.float32
BF16 = jnp.bfloat16
MESH = pl.DeviceIdType.MESH
ANY = pl.BlockSpec(memory_space=pl.ANY)

N_DEV = 8
N_CHIP = 4
DEPTH = 4
HEAD_DIM = 128
CONV_K = 3
RMS_EPS = 1e-6
ADAM_LR = 0.001
ADAM_B1 = 0.9
ADAM_B2 = 0.999
ADAM_EPS = 1e-08
ADAM_WD = 0.01
ADAM_STEP = 10

V7X_VMEM_BYTES = 64 * 1024 * 1024
VMEM_LIMIT = V7X_VMEM_BYTES * 3 // 4
LANES = 128
SUBLANES = 8
HEADS_PER_STEP = 2
DEAD_CARRY = -128.0
UNVISITED = -1e30


def _params(*sem):
    return pltpu.CompilerParams(dimension_semantics=sem, vmem_limit_bytes=VMEM_LIMIT)


def _silu_parts(z):
    sig = jax.nn.sigmoid(z)
    return z * sig, sig


NN = (((1,), (0,)), ((), ()))
NT = (((1,), (1,)), ((), ()))
TN = (((0,), (0,)), ((), ()))


def _gridded_call(body, operands, *, grid, in_specs, out_specs, out_shape, scratch_shapes, semantics, name, hosted=None):
    if hosted is None:
        return pl.pallas_call(
            body, grid=grid, in_specs=in_specs, out_specs=out_specs, out_shape=out_shape,
            scratch_shapes=scratch_shapes, compiler_params=_params(*semantics), name=name)(*operands)
    arrays, stage = hosted
    scatter = stage == CHIPS_SCATTER
    n_in, n_out, n_ex, n_scr = len(in_specs), len(out_specs), len(arrays), len(scratch_shapes)

    def hosting_body(*refs):
        ins, refs = refs[:n_in], refs[n_in:]
        ex_in, refs = refs[:n_ex], refs[n_ex:]
        outs, refs = refs[:n_out], refs[n_out:]
        ex_out, refs = refs[:n_ex], refs[n_ex:]
        scratch, sems = refs[:n_scr], refs[n_scr:]
        first = last = None
        for axis, size in enumerate(grid):
            at_start, at_end = pl.program_id(axis) == 0, pl.program_id(axis) == size - 1
            first = at_start if first is None else first & at_start
            last = at_end if last is None else last & at_end

        @pl.when(first)
        def _():
            for cp in _stage_copies(stage, ex_in, ex_out, *sems):
                cp.start()

        body(*ins, *outs, *scratch)

        @pl.when(last)
        def _():
            _wait_all(_stage_copies(stage, ex_in, ex_out, *sems))

    if scatter:
        ex_shapes, aliases = [jax.ShapeDtypeStruct((N_CHIP - 1,) + a.shape[1:], a.dtype) for a in arrays], {}
    else:
        ex_shapes, aliases = [jax.ShapeDtypeStruct(a.shape, a.dtype) for a in arrays], {n_in + a: n_out + a for a in range(n_ex)}
    out = pl.pallas_call(
        hosting_body, grid=grid, in_specs=list(in_specs) + [ANY] * n_ex, out_specs=list(out_specs) + [ANY] * n_ex,
        out_shape=list(out_shape) + ex_shapes, input_output_aliases=aliases,
        scratch_shapes=list(scratch_shapes) + [pltpu.SemaphoreType.DMA((N_CHIP * n_ex,))] * 2,
        compiler_params=_params(*["arbitrary"] * len(grid)), name=name)(*operands, *arrays)
    return out[:n_out], out[n_out:]


def _mm(a, b, *, dims, grid, a_spec, b_spec, o_spec, out_shape, acc_shape, name, hosted=None):
    nk = grid[2]

    def body(a_ref, b_ref, o_ref, *scratch):
        p = lax.dot_general(a_ref[...], b_ref[...], dims, preferred_element_type=F32)
        if nk == 1:
            o_ref[...] = p.astype(o_ref.dtype)
        else:
            acc_ref = scratch[0]
            k = pl.program_id(2)

            @pl.when(k == 0)
            def _():
                acc_ref[...] = p

            @pl.when(k > 0)
            def _():
                acc_ref[...] += p

            @pl.when(k == nk - 1)
            def _():
                o_ref[...] = acc_ref[...].astype(o_ref.dtype)

    scratch = [] if nk == 1 else [pltpu.VMEM(acc_shape, F32)]
    res = _gridded_call(
        body, (a, b), grid=grid, in_specs=[a_spec, b_spec], out_specs=[o_spec], out_shape=[out_shape],
        scratch_shapes=scratch, semantics=("parallel", "parallel", "arbitrary"), name=name, hosted=hosted)
    return res[0] if hosted is None else (res[0][0], res[1])


def _proj(u, w_in, shard0, n_shard, out_dtype, name, hosted=None):
    s, d = u.shape
    ws = w_in.shape[-1]
    tm, tn = min(s, 512), min(ws, 1024)
    nj = ws // tn
    return _mm(
        u, w_in, dims=NN, grid=(s // tm, n_shard * nj, 1),
        a_spec=pl.BlockSpec((tm, d), lambda i, j, k: (i, 0)),
        b_spec=pl.BlockSpec((None, d, tn), lambda i, j, k: (shard0 + j // nj, 0, j % nj)),
        o_spec=pl.BlockSpec((tm, tn), lambda i, j, k: (i, j)),
        out_shape=jax.ShapeDtypeStruct((s, n_shard * ws), out_dtype), acc_shape=(tm, tn), name=name, hosted=hosted,
    )


def _out_proj(a, w_out, name, hosted=None):
    s, bdim = a.shape
    d = w_out.shape[-1]
    tm, tn = min(s, 512), min(d, 1024)
    return _mm(
        a, w_out, dims=NN, grid=(s // tm, d // tn, 1),
        a_spec=pl.BlockSpec((tm, bdim), lambda i, j, k: (i, 0)),
        b_spec=pl.BlockSpec((bdim, tn), lambda i, j, k: (0, j)),
        o_spec=pl.BlockSpec((tm, tn), lambda i, j, k: (i, j)),
        out_shape=jax.ShapeDtypeStruct((s, d), F32), acc_shape=(tm, tn), name=name, hosted=hosted,
    )


def _out_proj_bwd_act(dm, w_out, name, hosted=None):
    s, d = dm.shape
    bdim = w_out.shape[-2]
    tm, tn = min(s, 512), min(bdim, 1024)
    return _mm(
        dm, w_out, dims=NT, grid=(s // tm, bdim // tn, 1),
        a_spec=pl.BlockSpec((tm, d), lambda i, j, k: (i, 0)),
        b_spec=pl.BlockSpec((tn, d), lambda i, j, k: (j, 0)),
        o_spec=pl.BlockSpec((tm, tn), lambda i, j, k: (i, j)),
        out_shape=jax.ShapeDtypeStruct((s, bdim), F32), acc_shape=(tm, tn), name=name, hosted=hosted,
    )


def _weight_grad(act_t, dout, n_blocks, name):
    din, s = act_t.shape
    w = dout.shape[1] // n_blocks
    tm, tn = min(din, 512), min(w, 1024)
    nj = w // tn
    return _mm(
        act_t, dout, dims=NN, grid=(din // tm, n_blocks * nj, 1),
        a_spec=pl.BlockSpec((tm, s), lambda i, j, k: (i, 0)),
        b_spec=pl.BlockSpec((s, tn), lambda i, j, k: (0, j)),
        o_spec=pl.BlockSpec((None, tm, tn), lambda i, j, k: (j // nj, i, j % nj)),
        out_shape=jax.ShapeDtypeStruct((n_blocks, din, w), BF16), acc_shape=(tm, tn), name=name,
    )


def _proj_bwd_act(dproj, w_in, name, hosted=None):
    s = dproj.shape[0]
    n_shards, d, ws = w_in.shape
    tm, tn = min(s, 512), min(d, 512)

    def body(a_ref, b_ref, o_ref):
        acc = None
        for k in range(n_shards):
            p = lax.dot_general(a_ref[:, k * ws:(k + 1) * ws], b_ref[k], NT, preferred_element_type=F32)
            acc = p if acc is None else acc + p
        o_ref[...] = acc

    res = _gridded_call(
        body, (dproj, w_in), grid=(s // tm, d // tn),
        in_specs=[pl.BlockSpec((tm, n_shards * ws), lambda i, j: (i, 0)),
                  pl.BlockSpec((n_shards, tn, ws), lambda i, j: (0, j, 0))],
        out_specs=[pl.BlockSpec((tm, tn), lambda i, j: (i, j))], out_shape=[jax.ShapeDtypeStruct((s, d), F32)],
        scratch_shapes=[], semantics=("parallel", "parallel"), name=name, hosted=hosted)
    return res[0] if hosted is None else (res[0][0], res[1])


def _row_spec(tm, d):
    return pl.BlockSpec((tm, d), lambda i: (i, 0))


def _gain_spec(d):
    return pl.BlockSpec((1, d), lambda i: (0, 0))


def _rstd(x):
    return lax.rsqrt(jnp.mean(x * x, axis=-1, keepdims=True) + RMS_EPS)


def _rmsnorm_fwd(h, gain, name):
    s, d = h.shape
    tm = min(s, 512)

    def body(h_ref, g_ref, u_ref, ut_ref):
        x = h_ref[...]
        u = (x * _rstd(x) * g_ref[...]).astype(u_ref.dtype)
        u_ref[...] = u
        ut_ref[...] = u.T

    return pl.pallas_call(
        body, grid=(s // tm,), in_specs=[_row_spec(tm, d), _gain_spec(d)],
        out_specs=[_row_spec(tm, d), pl.BlockSpec((d, tm), lambda i: (0, i))],
        out_shape=[jax.ShapeDtypeStruct((s, d), BF16), jax.ShapeDtypeStruct((d, s), BF16)],
        compiler_params=_params("parallel"), name=name,
    )(h, gain)


def _cast(block, dtype, name):
    r, c = block.shape
    tr = min(r, 256)

    def body(b_ref, o_ref):
        o_ref[...] = b_ref[...].astype(o_ref.dtype)

    spec = pl.BlockSpec((tr, c), lambda i: (i, 0))
    return pl.pallas_call(
        body, grid=(r // tr,), in_specs=[spec], out_specs=spec, out_shape=jax.ShapeDtypeStruct((r, c), dtype),
        compiler_params=_params("parallel"), name=name,
    )(block)


def _post_norm_residual(h, m, gain, next_gain, name, hosted):
    s, d = h.shape
    tm = min(s, 512)

    def body(h_ref, m_ref, g_ref, gn_ref, o_ref, u_ref, ut_ref):
        x = m_ref[...]
        y = h_ref[...] + x * _rstd(x) * g_ref[...]
        o_ref[...] = y
        u = (y * _rstd(y) * gn_ref[...]).astype(u_ref.dtype)
        u_ref[...] = u
        ut_ref[...] = u.T

    return _gridded_call(
        body, (h, m, gain, next_gain), grid=(s // tm,),
        in_specs=[_row_spec(tm, d), _row_spec(tm, d), _gain_spec(d), _gain_spec(d)],
        out_specs=[_row_spec(tm, d), _row_spec(tm, d), pl.BlockSpec((d, tm), lambda i: (0, i))],
        out_shape=[jax.ShapeDtypeStruct((s, d), F32), jax.ShapeDtypeStruct((s, d), BF16), jax.ShapeDtypeStruct((d, s), BF16)],
        scratch_shapes=[], semantics=("parallel",), name=name, hosted=hosted)


def _last_norm_and_loss(h, m, gain, target, name):
    s, d = h.shape
    tm = min(s, 256)
    n_steps = s // tm

    def body(h_ref, m_ref, g_ref, t_ref, loss_ref, dy_ref, dm_ref, dg_ref, loss_acc, dg_acc):
        i = pl.program_id(0)

        @pl.when(i == 0)
        def _():
            loss_acc[...] = jnp.zeros_like(loss_acc)
            dg_acc[...] = jnp.zeros_like(dg_acc)

        x = m_ref[...]
        rstd = _rstd(x)
        n = x * rstd
        err = h_ref[...] + n * g_ref[...] - t_ref[...]
        dy = err / d
        dy_ref[...] = dy
        dn = dy * g_ref[...]
        dm_ref[...] = (rstd * (dn - n * jnp.mean(dn * n, axis=-1, keepdims=True))).astype(dm_ref.dtype)
        _sum_rows_into(loss_acc, err * err)
        _sum_rows_into(dg_acc, dy * n)

        @pl.when(i == n_steps - 1)
        def _():
            total = jnp.sum(jnp.sum(loss_acc[...], axis=0, keepdims=True), axis=1, keepdims=True)
            loss_ref[...] = 0.5 * total / d
            dg_ref[...] = jnp.sum(dg_acc[...], axis=0, keepdims=True)

    row = _row_spec(tm, d)
    return pl.pallas_call(
        body, grid=(n_steps,), in_specs=[row, row, _gain_spec(d), row],
        out_specs=[pl.BlockSpec((1, 1), lambda i: (0, 0)), row, row, _gain_spec(d)],
        out_shape=[jax.ShapeDtypeStruct((1, 1), F32), jax.ShapeDtypeStruct((s, d), F32),
                   jax.ShapeDtypeStruct((s, d), BF16), jax.ShapeDtypeStruct((1, d), F32)],
        scratch_shapes=[pltpu.VMEM((SUBLANES, d), F32), pltpu.VMEM((SUBLANES, d), F32)],
        compiler_params=_params("arbitrary"), name=name,
    )(h, m, gain, target)


def _sum_rows_into(acc_ref, x):
    tm, d = x.shape
    acc_ref[...] += jnp.sum(x.reshape(tm // SUBLANES, SUBLANES, d), axis=0)


def _norm_bwd_body(n_steps, with_residual):
    def body(*refs):
        dy_ref, x_ref, g_ref = refs[:3]
        dres_ref = refs[3] if with_residual else None
        dx_ref, dg_ref, acc_ref = refs[-3:]
        i = pl.program_id(0)

        @pl.when(i == 0)
        def _():
            acc_ref[...] = jnp.zeros_like(acc_ref)

        x = x_ref[...]
        dy = dy_ref[...]
        rstd = _rstd(x)
        n = x * rstd
        dn = dy * g_ref[...]
        dx = rstd * (dn - n * jnp.mean(dn * n, axis=-1, keepdims=True))
        if with_residual:
            dx = dres_ref[...] + dx
        dx_ref[...] = dx.astype(dx_ref.dtype)
        _sum_rows_into(acc_ref, dy * n)

        @pl.when(i == n_steps - 1)
        def _():
            dg_ref[...] = jnp.sum(acc_ref[...], axis=0, keepdims=True)

    return body


def _post_norm_bwd(dh, m, gain, name):
    s, d = m.shape
    tm = min(s, 512)
    n_steps = s // tm
    return pl.pallas_call(
        _norm_bwd_body(n_steps, False), grid=(n_steps,),
        in_specs=[_row_spec(tm, d), _row_spec(tm, d), _gain_spec(d)],
        out_specs=[_row_spec(tm, d), _gain_spec(d)],
        out_shape=[jax.ShapeDtypeStruct((s, d), BF16), jax.ShapeDtypeStruct((1, d), F32)],
        scratch_shapes=[pltpu.VMEM((SUBLANES, d), F32)], compiler_params=_params("arbitrary"), name=name,
    )(dh, m, gain)


def _pre_norm_bwd(du, h, gain, dh, name):
    s, d = h.shape
    tm = min(s, 512)
    n_steps = s // tm
    return pl.pallas_call(
        _norm_bwd_body(n_steps, True), grid=(n_steps,),
        in_specs=[_row_spec(tm, d), _row_spec(tm, d), _gain_spec(d), _row_spec(tm, d)],
        out_specs=[_row_spec(tm, d), _gain_spec(d)],
        out_shape=[jax.ShapeDtypeStruct((s, d), F32), jax.ShapeDtypeStruct((1, d), F32)],
        scratch_shapes=[pltpu.VMEM((SUBLANES, d), F32)], compiler_params=_params("arbitrary"), name=name,
    )(du, h, gain, dh)


def _shift_down(p, halo, row, n):
    out = jnp.where(row == 0, halo[SUBLANES - n:SUBLANES - n + 1], pltpu.roll(p, n, 0))
    if n == 2:
        out = jnp.where(row == 1, halo[SUBLANES - 1:SUBLANES], out)
    return out


def _shift_up(p, halo, row, n):
    tm = p.shape[0]
    out = jnp.where(row == tm - 1, halo[n - 1:n], pltpu.roll(p, tm - n, 0))
    if n == 2:
        out = jnp.where(row == tm - 2, halo[0:1], out)
    return out


def _conv_specs(tm, tc, nb, n_row_blocks):
    hb = tm // SUBLANES
    cur = lambda part: pl.BlockSpec((tm, tc), lambda i, j: (i, part * nb + j))
    prev = lambda part: pl.BlockSpec((SUBLANES, tc), lambda i, j: (jnp.maximum(i * hb - 1, 0), part * nb + j))
    nxt = lambda part: pl.BlockSpec(
        (SUBLANES, tc), lambda i, j: (jnp.minimum((i + 1) * hb, n_row_blocks * hb - 1), part * nb + j))
    return cur, prev, nxt


def _conv_gate_fwd(proj, conv_w, name):
    s, b4 = proj.shape
    bdim = b4 // 4
    tm, tc = min(s, 512), min(bdim, 512)
    nb = bdim // tc
    cur, prev, _ = _conv_specs(tm, tc, nb, s // tm)

    def body(b_ref, c_ref, x_ref, z_ref, cp_ref, xp_ref, w_ref, a_ref, at_ref):
        i = pl.program_id(0)
        row = lax.broadcasted_iota(jnp.int32, (tm, tc), 0)
        p = c_ref[...] * x_ref[...]
        halo = jnp.where(i > 0, cp_ref[...] * xp_ref[...], 0.0)
        w = w_ref[...]
        cv = w[0:1] * _shift_down(p, halo, row, 2) + w[1:2] * _shift_down(p, halo, row, 1) + w[2:3] * p
        silu, _ = _silu_parts(z_ref[...])
        a = (silu * (b_ref[...] * cv)).astype(a_ref.dtype)
        a_ref[...] = a
        at_ref[...] = a.T

    return pl.pallas_call(
        body, grid=(s // tm, nb),
        in_specs=[cur(0), cur(1), cur(2), cur(3), prev(1), prev(2), pl.BlockSpec((CONV_K, tc), lambda i, j: (0, j))],
        out_specs=[pl.BlockSpec((tm, tc), lambda i, j: (i, j)), pl.BlockSpec((tc, tm), lambda i, j: (j, i))],
        out_shape=[jax.ShapeDtypeStruct((s, bdim), BF16), jax.ShapeDtypeStruct((bdim, s), BF16)],
        compiler_params=_params("parallel", "parallel"), name=name,
    )(proj, proj, proj, proj, proj, proj, conv_w)


def _conv_gate_bwd(proj, da, conv_w, name):
    s, b4 = proj.shape
    bdim = b4 // 4
    tm, tc = min(s, 128), bdim
    nb = bdim // tc
    n_rows = s // tm
    cur, prev, nxt = _conv_specs(tm, tc, nb, n_rows)
    da_cur = pl.BlockSpec((tm, tc), lambda j, i: (i, j))
    hb = tm // SUBLANES
    da_nxt = pl.BlockSpec((SUBLANES, tc), lambda j, i: (jnp.minimum((i + 1) * hb, n_rows * hb - 1), j))
    swap = lambda spec: pl.BlockSpec(spec.block_shape, lambda j, i, f=spec.index_map: f(i, j))

    def body(b_ref, c_ref, x_ref, z_ref, cp_ref, xp_ref, bn_ref, zn_ref, da_ref, dan_ref, w_ref,
             dproj_ref, dw_ref, acc_ref):
        i = pl.program_id(1)

        @pl.when(i == 0)
        def _():
            acc_ref[...] = jnp.zeros_like(acc_ref)

        row = lax.broadcasted_iota(jnp.int32, (tm, tc), 0)
        w = w_ref[...]
        b, c, x = b_ref[...], c_ref[...], x_ref[...]
        p = c * x
        halo_p = jnp.where(i > 0, cp_ref[...] * xp_ref[...], 0.0)
        p1, p2 = _shift_down(p, halo_p, row, 1), _shift_down(p, halo_p, row, 2)
        cv = w[0:1] * p2 + w[1:2] * p1 + w[2:3] * p
        z = z_ref[...]
        silu, sig = _silu_parts(z)
        da = da_ref[...]
        dy = da * silu
        dcv = dy * b
        silu_n, _ = _silu_parts(zn_ref[...])
        halo_d = jnp.where(i < n_rows - 1, dan_ref[...] * silu_n * bn_ref[...], 0.0)
        dp = w[2:3] * dcv + w[1:2] * _shift_up(dcv, halo_d, row, 1) + w[0:1] * _shift_up(dcv, halo_d, row, 2)
        gates = (dy * cv, dp * x, dp * c, da * (b * cv) * (sig * (1.0 + z * (1.0 - sig))))
        for part, dgate in enumerate(gates):
            dproj_ref[:, part * bdim:(part + 1) * bdim] = dgate.astype(dproj_ref.dtype)
        for k, pk in enumerate((p2, p1, p)):
            _sum_rows_into(acc_ref.at[k], dcv * pk)

        @pl.when(i == n_rows - 1)
        def _():
            for k in range(CONV_K):
                dw_ref[k:k + 1, :] = jnp.sum(acc_ref[k], axis=0, keepdims=True)

    out = pl.BlockSpec((tm, b4), lambda j, i: (i, 0))
    return pl.pallas_call(
        body, grid=(nb, n_rows),
        in_specs=[swap(cur(0)), swap(cur(1)), swap(cur(2)), swap(cur(3)), swap(prev(1)), swap(prev(2)),
                  swap(nxt(0)), swap(nxt(3)), da_cur, da_nxt, pl.BlockSpec((CONV_K, tc), lambda j, i: (0, j))],
        out_specs=[out, pl.BlockSpec((CONV_K, tc), lambda j, i: (0, j))],
        out_shape=[jax.ShapeDtypeStruct((s, b4), BF16), jax.ShapeDtypeStruct((CONV_K, bdim), F32)],
        scratch_shapes=[pltpu.VMEM((CONV_K, SUBLANES, tc), F32)],
        compiler_params=_params("parallel", "arbitrary"), name=name,
    )(proj, proj, proj, proj, proj, proj, proj, proj, da, da, conv_w)


def _split(x):
    hi = x.astype(BF16)
    lo = (x - hi.astype(F32)).astype(BF16)
    return jnp.concatenate([hi, lo], axis=1)


def _row_total(x, column):
    return jnp.broadcast_to(x[:, column:column + 1], (x.shape[0], LANES))


def _sb_tiles(qs, ks, carries, suffix_ones, masks, chain=0):
    items = range(len(qs))
    bk = ks[0].shape[0]
    scale = 1.0 / math.sqrt(HEAD_DIM)
    logits = [lax.dot_general(qs[n], ks[n], NT, preferred_element_type=F32) * scale for n in items]
    es = [jnp.exp(-jnp.abs(logits[n])) for n in items]
    keeps = []
    for n in items:
        log_keep = -(jnp.maximum(logits[n], 0.0) + jnp.log(1.0 + es[n]))
        if masks[n] is not None:
            log_keep = jnp.where(masks[n], log_keep, 0.0)
        keeps.append(_split(log_keep))
    tails = [lax.dot_general(keeps[n], suffix_ones, NN, preferred_element_type=F32) for n in items]
    ws, used = [], []
    for n in items:
        carry = carries[n] if n < len(carries) else used[n - chain] + _row_total(tails[n - chain], 0)
        used.append(carry)
        w = jnp.exp(logits[n] + tails[n] + (carry if carry.shape[1] == 1 else _lane_tile(carry, bk)))
        if masks[n] is not None:
            w = jnp.where(masks[n], w, 0.0)
        ws.append(w)
    return logits, es, tails, ws, used


def _tri_twice(n, upper):
    r = lax.broadcasted_iota(jnp.int32, (2 * n, n), 0)
    r = jnp.where(r >= n, r - n, r)
    c = lax.broadcasted_iota(jnp.int32, (2 * n, n), 1)
    return jnp.where(r <= c if upper else r >= c, 1.0, 0.0).astype(BF16)


def _group_spec(s, width, part, n_groups):
    return pl.BlockSpec((s, width), lambda h: (0, part * n_groups + h))


def _head_cols(g):
    return slice(g * HEAD_DIM, (g + 1) * HEAD_DIM)


def _lane_tile(x, n):
    return x if n == LANES else jnp.concatenate([x] * (n // LANES), axis=1)


def _sb_attn_fwd(qkv, name):
    s, b3 = qkv.shape
    bdim = b3 // 3
    hps = min(HEADS_PER_STEP, bdim // HEAD_DIM)
    width = hps * HEAD_DIM
    n_groups = bdim // width
    blk = min(s, 256)
    n_blk = s // blk

    def body(q_ref, k_ref, v_ref, o_ref, car_ref, carry_ref):
        suffix_ones = _tri_twice(blk, upper=False)
        r = lax.broadcasted_iota(jnp.int32, (blk, blk), 0)
        c = lax.broadcasted_iota(jnp.int32, (blk, blk), 1)
        diag_mask = c < r
        lane = lax.broadcasted_iota(jnp.int32, (blk, LANES), 1)

        def q_block(qi, _):
            q0 = pl.multiple_of(qi * blk, blk)
            rows = pl.ds(q0, blk)
            qs = [q_ref[rows, _head_cols(g)] for g in range(hps)]
            o_ref[rows, :] = jnp.zeros((blk, width), F32)
            car_ref[rows, :] = jnp.full((blk, width), UNVISITED, F32)
            carry_ref[...] = jnp.zeros_like(carry_ref)

            def step(js, tile_masks):
                k0s = [pl.multiple_of(j * blk, blk) for j in js]
                items = [(t, g) for t in range(len(js)) for g in range(hps)]
                ks = [k_ref[pl.ds(k0s[t], blk), _head_cols(g)] for t, g in items]
                first = [carry_ref[g] for g in range(hps)]
                _, _, tails, ws, carries = _sb_tiles([qs[g] for _, g in items], ks, first, suffix_ones,
                                                     [tile_masks[t] for t, _ in items], chain=hps)
                for g in range(hps):
                    mine = [n for n, (_, h) in enumerate(items) if h == g]
                    acc, saved = None, car_ref[rows, _head_cols(g)]
                    for n in mine:
                        v = v_ref[pl.ds(k0s[items[n][0]], blk), _head_cols(g)]
                        p = lax.dot_general(ws[n].astype(BF16), v, NN, preferred_element_type=F32)
                        acc = p if acc is None else acc + p
                        saved = jnp.where(lane == js[items[n][0]], carries[n], saved)
                    o_ref[rows, _head_cols(g)] += acc
                    car_ref[rows, _head_cols(g)] = saved
                    carry_ref[g] = carries[mine[-1]] + _row_total(tails[mine[-1]], 0)

            @pl.when(qi == 0)
            def _():
                step([0], [diag_mask])

            @pl.when(qi > 0)
            def _():
                step([qi, qi - 1], [diag_mask, None])

            def alive():
                top = jnp.max(jnp.max(carry_ref[...], axis=0), axis=0, keepdims=True)
                return (jnp.max(top, axis=1, keepdims=True)[0, 0] >= DEAD_CARRY).astype(jnp.int32)

            left = jnp.maximum(qi - 1, 0)

            def pair(state):
                p, _ = state
                j = qi - 2 - 2 * p
                step([j, j - 1], [None, None])
                return p + 1, alive()

            p, live = lax.while_loop(lambda state: (state[0] < left // 2) & (state[1] > 0), pair, (0, alive()))

            @pl.when((left % 2 == 1) & (p == left // 2) & (live > 0))
            def _():
                step([0], [None])

            return 0

        lax.fori_loop(0, n_blk, q_block, 0)

    out = pl.BlockSpec((s, width), lambda h: (0, h))
    shape = jax.ShapeDtypeStruct((s, bdim), F32)
    return pl.pallas_call(
        body, grid=(n_groups,),
        in_specs=[_group_spec(s, width, part, n_groups) for part in range(3)],
        out_specs=[out, out], out_shape=[shape, shape], scratch_shapes=[pltpu.VMEM((hps, blk, LANES), F32)],
        compiler_params=_params("parallel"), name=name,
    )(qkv, qkv, qkv)


def _sb_attn_bwd(qkv, do, carries, dproj, name):
    s, b3 = qkv.shape
    bdim = b3 // 3
    hps = min(HEADS_PER_STEP, bdim // HEAD_DIM)
    width = hps * HEAD_DIM
    n_groups = bdim // width
    blk = min(s, 256)
    n_blk = s // blk
    scale = 1.0 / math.sqrt(HEAD_DIM)

    def sweep(q_ref, k_ref, v_ref, do_ref, car_ref, dq_ref, dk_acc, dv_acc, dq_acc, before_ref):
        suffix_ones = _tri_twice(blk, upper=False)
        prefix_ones = _tri_twice(blk, upper=True)
        r = lax.broadcasted_iota(jnp.int32, (blk, blk), 0)
        c = lax.broadcasted_iota(jnp.int32, (blk, blk), 1)
        diag_mask = c < r
        lane = lax.broadcasted_iota(jnp.int32, (blk, LANES), 1)
        dk_acc[...] = jnp.zeros_like(dk_acc)
        dv_acc[...] = jnp.zeros_like(dv_acc)

        def q_block(qi, _):
            q0 = pl.multiple_of(qi * blk, blk)
            rows = pl.ds(q0, blk)
            qs = [q_ref[rows, _head_cols(g)] for g in range(hps)]
            dos = [do_ref[rows, _head_cols(g)] for g in range(hps)]
            dq_acc[...] = jnp.zeros_like(dq_acc)
            before_ref[...] = jnp.zeros_like(before_ref)

            def step(js, tile_masks):
                masks = [tile_masks[t] for t in range(len(js)) for _ in range(hps)]
                k0s = [pl.multiple_of(j * blk, blk) for j in js]
                items = [(t, g) for t in range(len(js)) for g in range(hps)]
                every = range(len(items))
                ks = [k_ref[pl.ds(k0s[t], blk), _head_cols(g)] for t, g in items]
                dws = [lax.dot_general(dos[g], v_ref[pl.ds(k0s[t], blk), _head_cols(g)], NT, preferred_element_type=F32)
                       for t, g in items]
                carries = [jnp.sum(jnp.where(lane == js[t], car_ref[rows, _head_cols(g)], 0.0), axis=1, keepdims=True)
                           for t, g in items]
                logits, es, _, ws, _ = _sb_tiles([qs[g] for _, g in items], ks, carries, suffix_ones, masks)
                gws = [dws[n] * ws[n] for n in every]
                g_upto = [lax.dot_general(_split(gws[n]), prefix_ones, NN, preferred_element_type=F32) for n in every]
                dss, befores = [], []
                for n, (t, g) in enumerate(items):
                    before = before_ref[g] if t == 0 else befores[n - hps] + _row_total(g_upto[n - hps], blk - 1)
                    befores.append(before)
                    sig = jnp.where(logits[n] >= 0.0, 1.0, es[n]) / (1.0 + es[n])
                    dlogits = gws[n] - sig * (_lane_tile(before, blk) + g_upto[n])
                    if masks[n] is not None:
                        dlogits = jnp.where(masks[n], dlogits, 0.0)
                    dss.append((dlogits * scale).astype(BF16))
                for g in range(hps):
                    mine = [n for n in every if items[n][1] == g]
                    dq = None
                    for n in mine:
                        k0 = k0s[items[n][0]]
                        p = lax.dot_general(dss[n], ks[n], NN, preferred_element_type=F32)
                        dq = p if dq is None else dq + p
                        dk_acc[pl.ds(k0, blk), _head_cols(g)] += lax.dot_general(
                            dss[n], qs[g], TN, preferred_element_type=F32)
                        dv_acc[pl.ds(k0, blk), _head_cols(g)] += lax.dot_general(
                            ws[n].astype(BF16), dos[g], TN, preferred_element_type=F32)
                    dq_acc[:, _head_cols(g)] += dq
                    before_ref[g] = befores[mine[-1]] + _row_total(g_upto[mine[-1]], blk - 1)

            top = car_ref[rows, _head_cols(0)]
            for g in range(1, hps):
                top = jnp.maximum(top, car_ref[rows, _head_cols(g)])
            top = jnp.max(top, axis=0, keepdims=True)
            lane_row = lax.broadcasted_iota(jnp.int32, (1, LANES), 1)
            counted = jnp.where((top >= DEAD_CARRY) & (lane_row < qi), 1.0, 0.0)
            n_alive = jnp.sum(counted, axis=1, keepdims=True)[0, 0].astype(jnp.int32)
            left = jnp.maximum(n_alive - 1, 0)
            start = qi - 1 - left

            @pl.when(left % 2 == 1)
            def _():
                step([start], [None])

            def pair(p, _):
                j = start + left % 2 + 2 * p
                step([j, j + 1], [None, None])
                return 0

            lax.fori_loop(0, left // 2, pair, 0)

            @pl.when(qi == 0)
            def _():
                step([0], [diag_mask])

            @pl.when(qi > 0)
            def _():
                step([qi - 1, qi], [None, diag_mask])
            dq_ref[rows, :] = dq_acc[...].astype(dq_ref.dtype)
            return 0

        lax.fori_loop(0, n_blk, q_block, 0)

    def body(q_ref, k_ref, v_ref, do_ref, car_ref, dproj_ref, out_ref, dk_acc, dv_acc, dq_acc, before_ref):
        part = pl.program_id(1)

        @pl.when(part == 0)
        def _():
            sweep(q_ref, k_ref, v_ref, do_ref, car_ref, out_ref, dk_acc, dv_acc, dq_acc, before_ref)

        @pl.when(part == 1)
        def _():
            out_ref[...] = dk_acc[...].astype(out_ref.dtype)

        @pl.when(part == 2)
        def _():
            out_ref[...] = dv_acc[...].astype(out_ref.dtype)

    qkv_spec = lambda which: pl.BlockSpec((s, width), lambda g, p: (0, which * n_groups + g))
    once = pl.BlockSpec((s, width), lambda g, p: (0, g), pipeline_mode=pl.Buffered(1))
    return pl.pallas_call(
        body, grid=(n_groups, 3),
        in_specs=[qkv_spec(0), qkv_spec(1), qkv_spec(2), once, once, ANY],
        out_specs=pl.BlockSpec((s, width), lambda g, p: (0, p * n_groups + g)),
        out_shape=jax.ShapeDtypeStruct(dproj.shape, dproj.dtype), input_output_aliases={5: 0},
        scratch_shapes=[pltpu.VMEM((s, width), F32), pltpu.VMEM((s, width), F32), pltpu.VMEM((blk, width), F32),
                        pltpu.VMEM((hps, blk, LANES), F32)],
        compiler_params=_params("parallel", "arbitrary"), name=name,
    )(qkv, qkv, qkv, do, carries, dproj)


def _sb_gate_fwd(z, o, name):
    s, bdim = z.shape
    tm = min(s, 512)

    def body(z_ref, o_ref, a_ref, at_ref):
        silu, _ = _silu_parts(z_ref[...])
        a = (silu * o_ref[...]).astype(a_ref.dtype)
        a_ref[...] = a
        at_ref[...] = a.T

    return pl.pallas_call(
        body, grid=(s // tm,), in_specs=[_row_spec(tm, bdim), _row_spec(tm, bdim)],
        out_specs=[_row_spec(tm, bdim), pl.BlockSpec((bdim, tm), lambda i: (0, i))],
        out_shape=[jax.ShapeDtypeStruct((s, bdim), BF16), jax.ShapeDtypeStruct((bdim, s), BF16)],
        compiler_params=_params("parallel"), name=name,
    )(z, o)


def _sb_gate_bwd(da, z, o, name):
    s, bdim = z.shape
    tm = min(s, 512)

    def body(da_ref, z_ref, o_ref, do_ref, dz_ref):
        z = z_ref[...]
        da = da_ref[...]
        silu, sig = _silu_parts(z)
        do_ref[...] = (da * silu).astype(do_ref.dtype)
        dz_ref[...] = (da * o_ref[...] * (sig * (1.0 + z * (1.0 - sig)))).astype(dz_ref.dtype)

    spec = _row_spec(tm, bdim)
    return pl.pallas_call(
        body, grid=(s // tm,), in_specs=[spec, spec, spec],
        out_specs=[spec, pl.BlockSpec((tm, bdim), lambda i: (i, 3))],
        out_shape=[jax.ShapeDtypeStruct((s, bdim), BF16), jax.ShapeDtypeStruct((s, 4 * bdim), BF16)],
        compiler_params=_params("parallel"), name=name,
    )(da, z, o)


def _into_slot(block, place, dtype, name):
    r, c = block.shape
    tr = min(r, 256)

    def body(place_ref, b_ref, o_ref):
        o_ref[...] = b_ref[...].astype(o_ref.dtype)

    grid_spec = pltpu.PrefetchScalarGridSpec(
        num_scalar_prefetch=1, grid=(r // tr,),
        in_specs=[pl.BlockSpec((tr, c), lambda i, place_ref: (i, 0))],
        out_specs=pl.BlockSpec((None, tr, c), lambda i, place_ref: (place_ref[0], i, 0)),
    )
    return pl.pallas_call(
        body, grid_spec=grid_spec, out_shape=jax.ShapeDtypeStruct((N_DEV, r, c), dtype),
        compiler_params=_params("parallel"), name=name,
    )(place, block)


def _add_core_pair(grads, received, core, name):
    _, _, r, c = grads.shape
    tr = min(r, 1024)

    def body(core_ref, g_ref, r_ref, o_ref):
        o_ref[...] = (g_ref[...].astype(F32) + r_ref[...].astype(F32)).astype(o_ref.dtype)

    grid_spec = pltpu.PrefetchScalarGridSpec(
        num_scalar_prefetch=1, grid=(N_CHIP, r // tr),
        in_specs=[pl.BlockSpec((None, None, tr, c), lambda q, i, core_ref: (q, core_ref[0], i, 0)),
                  pl.BlockSpec((None, tr, c), lambda q, i, core_ref: (q, i, 0))],
        out_specs=pl.BlockSpec((None, tr, c), lambda q, i, core_ref: (q, i, 0)),
    )
    return pl.pallas_call(
        body, grid_spec=grid_spec, out_shape=jax.ShapeDtypeStruct((N_CHIP, r, c), BF16),
        compiler_params=_params("parallel", "parallel"), name=name,
    )(core, grads, received)


def _adamw_step(g, w, m, v, g_ref, d_ref, nm_ref, nv_ref):
    new_m = ADAM_B1 * m + (1.0 - ADAM_B1) * g
    new_v = ADAM_B2 * v + (1.0 - ADAM_B2) * (g * g)
    m_hat = new_m / (1.0 - ADAM_B1 ** ADAM_STEP)
    v_hat = new_v / (1.0 - ADAM_B2 ** ADAM_STEP)
    g_ref[...] = g
    d_ref[...] = -ADAM_LR * (m_hat / (jnp.sqrt(v_hat) + ADAM_EPS) + ADAM_WD * w)
    nm_ref[...] = new_m
    nv_ref[...] = new_v


def _adamw(w, parts, m, v, name):
    r, c = w.shape
    n_parts = parts.shape[0]
    tr = min(r, 256)

    def body(w_ref, p_ref, m_ref, v_ref, *out_refs):
        g = p_ref[0].astype(F32)
        for k in range(1, n_parts):
            g = g + p_ref[k].astype(F32)
        _adamw_step(g, w_ref[...], m_ref[...], v_ref[...], *out_refs)

    spec = pl.BlockSpec((tr, c), lambda i: (i, 0))
    shape = jax.ShapeDtypeStruct((r, c), F32)
    return pl.pallas_call(
        body, grid=(r // tr,), in_specs=[spec, pl.BlockSpec((n_parts, tr, c), lambda i: (0, i, 0)), spec, spec],
        out_specs=[spec] * 4, out_shape=[shape] * 4, compiler_params=_params("parallel"), name=name,
    )(w, parts, m, v)


def _adamw_shard(w, grads, landed, m, v, place, name):
    r, c = w.shape
    n_landed = landed.shape[0]
    tr = min(r, 512)

    def body(place_ref, w_ref, own_ref, l_ref, m_ref, v_ref, *out_refs):
        g = own_ref[...].astype(F32)
        for k in range(n_landed):
            g = g + l_ref[k].astype(F32)
        _adamw_step(g, w_ref[...], m_ref[...], v_ref[...], *out_refs)

    spec = pl.BlockSpec((tr, c), lambda i, place_ref: (i, 0))
    grid_spec = pltpu.PrefetchScalarGridSpec(
        num_scalar_prefetch=1, grid=(r // tr,),
        in_specs=[spec, pl.BlockSpec((None, tr, c), lambda i, place_ref: (place_ref[0], i, 0)),
                  pl.BlockSpec((n_landed, tr, c), lambda i, place_ref: (0, i, 0)), spec, spec],
        out_specs=[spec] * 4,
    )
    return pl.pallas_call(
        body, grid_spec=grid_spec, out_shape=[jax.ShapeDtypeStruct((r, c), F32)] * 4,
        compiler_params=_params("parallel"), name=name,
    )(place, w, grads, landed, m, v)


def _place():
    x, y, c = lax.axis_index("x"), lax.axis_index("y"), lax.axis_index("c")
    other_chips = [(1 - x, y), (x, 1 - y), (1 - x, 1 - y)]
    return x, y, c, other_chips


def _all_gather(blocks, name):
    n_arr = len(blocks)
    items = [(a, i) for a, blk in enumerate(blocks) for i in range(blk.shape[0])]
    n_items = len(items)

    def body(*refs):
        srcs, outs = refs[:n_arr], refs[n_arr:2 * n_arr]
        send_sems, recv_sems, local_sems = refs[2 * n_arr:]
        x, y, c, other_chips = _place()
        me, sibling = (x, y, c), (x, y, 1 - c)

        def slot(it, dev):
            a, i = items[it]
            return outs[a].at[i, 4 * dev[0] + 2 * dev[1] + dev[2]]

        def copy(it, k, block_of, to, from_src=False):
            a, i = items[it]
            return pltpu.make_async_remote_copy(
                src_ref=srcs[a].at[i] if from_src else slot(it, block_of), dst_ref=slot(it, block_of),
                send_sem=send_sems.at[it * 7 + k], recv_sem=recv_sems.at[it * 7 + k],
                device_id=to, device_id_type=MESH)

        own = [pltpu.make_async_copy(srcs[items[it][0]].at[items[it][1]], slot(it, me), local_sems.at[it])
               for it in range(n_items)]
        for cp in own:
            cp.start()
        first = []
        for it in range(n_items):
            first.append(copy(it, 0, me, sibling, from_src=True))
            first += [copy(it, 1 + j, me, (*chip, c), from_src=True) for j, chip in enumerate(other_chips)]
        for cp in first:
            cp.start()
        passed = []
        for it in range(n_items):
            for j, chip in enumerate(other_chips):
                copy(it, 1 + j, (*chip, c), me).wait_recv()
                passed.append(copy(it, 4 + j, (*chip, c), sibling))
                passed[-1].start()
        for it in range(n_items):
            copy(it, 0, sibling, me).wait_recv()
            for j, chip in enumerate(other_chips):
                copy(it, 4 + j, (*chip, 1 - c), me).wait_recv()
        for cp in first + passed:
            cp.wait_send()
        for cp in own:
            cp.wait()

    return pl.pallas_call(
        body, in_specs=[ANY] * n_arr, out_specs=[ANY] * n_arr,
        out_shape=[jax.ShapeDtypeStruct((b.shape[0], N_DEV) + b.shape[1:], b.dtype) for b in blocks],
        scratch_shapes=[pltpu.SemaphoreType.DMA((7 * n_items,)), pltpu.SemaphoreType.DMA((7 * n_items,)),
                        pltpu.SemaphoreType.DMA((n_items,))],
        name=name,
    )(*blocks)


def _exchange_core_pair(grads, name):
    n_arr = len(grads)

    def body(*refs):
        srcs, outs = refs[:n_arr], refs[n_arr:2 * n_arr]
        send_sems, recv_sems = refs[2 * n_arr:]
        x, y, c, _ = _place()
        copies = [
            pltpu.make_async_remote_copy(
                src_ref=srcs[a].at[q, 1 - c], dst_ref=outs[a].at[q],
                send_sem=send_sems.at[a * N_CHIP + q], recv_sem=recv_sems.at[a * N_CHIP + q],
                device_id=(x, y, 1 - c), device_id_type=MESH)
            for a in range(n_arr) for q in range(N_CHIP)]
        for cp in copies:
            cp.start()
        for cp in copies:
            cp.wait_recv()
        for cp in copies:
            cp.wait_send()

    return pl.pallas_call(
        body, in_specs=[ANY] * n_arr, out_specs=[ANY] * n_arr,
        out_shape=[jax.ShapeDtypeStruct((N_CHIP,) + g.shape[2:], g.dtype) for g in grads],
        scratch_shapes=[pltpu.SemaphoreType.DMA((N_CHIP * n_arr,)), pltpu.SemaphoreType.DMA((N_CHIP * n_arr,))],
        name=name,
    )(*grads)


CHIPS_SCATTER, CHIPS_GATHER, PAIR_GATHER = "chips_scatter", "chips_gather", "pair_gather"


def _stage_copies(stage, srcs, outs, send_sems, recv_sems):
    x, y, c, other_chips = _place()
    my_chip = 2 * x + y
    copies = []
    for a in range(len(outs)):
        if stage == PAIR_GATHER:
            moves = [(outs[a].at[q, c], outs[a].at[q, c], (x, y, 1 - c)) for q in range(N_CHIP)]
        elif stage == CHIPS_GATHER:
            moves = [(outs[a].at[my_chip, c], outs[a].at[my_chip, c], (*chip, c)) for chip in other_chips]
        else:
            moves = [(srcs[a].at[2 * chip[0] + chip[1]], outs[a].at[j], (*chip, c)) for j, chip in enumerate(other_chips)]
        for k, (src, dst, peer) in enumerate(moves):
            copies.append(pltpu.make_async_remote_copy(
                src_ref=src, dst_ref=dst, send_sem=send_sems.at[a * N_CHIP + k], recv_sem=recv_sems.at[a * N_CHIP + k],
                device_id=peer, device_id_type=MESH))
    return copies


def _wait_all(copies):
    for cp in copies:
        cp.wait_recv()
    for cp in copies:
        cp.wait_send()


def kernel(x, ln_pre_0, conv_w_in_0, conv_w_0, conv_w_out_0, ln_post_0, ln_pre_1, sb_w_in_1, sb_w_out_1, ln_post_1, ln_pre_2, conv_w_in_2, conv_w_2, conv_w_out_2, ln_post_2, ln_pre_3, sb_w_in_3, sb_w_out_3, ln_post_3, loss_target, m_ln_pre_0, m_conv_w_in_0, m_conv_w_0, m_conv_w_out_0, m_ln_post_0, m_ln_pre_1, m_sb_w_in_1, m_sb_w_out_1, m_ln_post_1, m_ln_pre_2, m_conv_w_in_2, m_conv_w_2, m_conv_w_out_2, m_ln_post_2, m_ln_pre_3, m_sb_w_in_3, m_sb_w_out_3, m_ln_post_3, v_ln_pre_0, v_conv_w_in_0, v_conv_w_0, v_conv_w_out_0, v_ln_post_0, v_ln_pre_1, v_sb_w_in_1, v_sb_w_out_1, v_ln_post_1, v_ln_pre_2, v_conv_w_in_2, v_conv_w_2, v_conv_w_out_2, v_ln_post_2, v_ln_pre_3, v_sb_w_in_3, v_sb_w_out_3, v_ln_post_3):
    names = ['ln_pre_0', 'conv_w_in_0', 'conv_w_0', 'conv_w_out_0', 'ln_post_0', 'ln_pre_1', 'sb_w_in_1', 'sb_w_out_1',
             'ln_post_1', 'ln_pre_2', 'conv_w_in_2', 'conv_w_2', 'conv_w_out_2', 'ln_post_2', 'ln_pre_3', 'sb_w_in_3',
             'sb_w_out_3', 'ln_post_3']
    given = dict(locals())
    w = {n: given[n] for n in names}
    mom = {n: given["m_" + n] for n in names}
    var = {n: given["v_" + n] for n in names}
    conv_layers = [i for i in range(DEPTH) if i % 2 == 0]
    w_in_names = [("conv_w_in_%d" if i % 2 == 0 else "sb_w_in_%d") % i for i in range(DEPTH)]
    w_out_names = [("conv_w_out_%d" if i % 2 == 0 else "sb_w_out_%d") % i for i in range(DEPTH)]

    s, d = x.shape[1:]
    h = x.reshape(s, d)
    target = loss_target.reshape(s, d)
    gains = {n: w[n].reshape(1, d) for n in names if n.startswith("ln_")}
    place = 4 * lax.axis_index("x") + 2 * lax.axis_index("y") + lax.axis_index("c")
    place_arr = place.astype(jnp.int32).reshape(1)
    bdim = w[w_out_names[0]].shape[0] * N_DEV
    wc = bdim // N_DEV

    conv_rows = jnp.concatenate([w["conv_w_%d" % i] for i in conv_layers], axis=0)
    first = _all_gather([_cast(w[n], BF16, "cast_" + n)[None] for n in (w_in_names[0], w_out_names[0])] + [conv_rows[None]],
                        "gather_first_layer")
    slots = {n: _into_slot(w[n], place_arr, BF16, "slot_" + n) for n in w_in_names[1:] + w_out_names[1:]}
    slots = {n: a.reshape((N_CHIP, 2) + a.shape[1:]) for n, a in slots.items()}
    conv_all = first[2][0].reshape(N_DEV, len(conv_layers), CONV_K, wc)
    conv_all = jnp.transpose(conv_all, (1, 2, 0, 3)).reshape(len(conv_layers), CONV_K, bdim)
    conv_full = {layer: conv_all[n] for n, layer in enumerate(conv_layers)}
    weights = [(first[0][0], first[1][0].reshape(bdim, d))]

    saved = []
    u, u_t = _rmsnorm_fwd(h, gains["ln_pre_0"], "pre_norm_0")
    for i in range(DEPTH):
        w_in, w_out = weights[i]
        more = i + 1 < DEPTH
        nxt_in = ([slots[w_in_names[i + 1]]], CHIPS_GATHER) if more else None
        nxt_out = ([slots[w_out_names[i + 1]]], CHIPS_GATHER) if more else None
        if i % 2 == 0:
            proj = _proj(u, w_in, 0, N_DEV, F32, "proj_%d" % i, hosted=nxt_in)
            if more:
                proj, crossed_in = proj
            a, a_t = _conv_gate_fwd(proj, conv_full[i], "conv_gate_%d" % i)
            extra = (proj,)
        else:
            qkv = _proj(u, w_in, 0, 6, BF16, "proj_qkv_%d" % i, hosted=nxt_in)
            if more:
                qkv, crossed_in = qkv
            z = _proj(u, w_in, 6, 2, F32, "proj_z_%d" % i)
            o, carries = _sb_attn_fwd(qkv, "sb_attn_%d" % i)
            a, a_t = _sb_gate_fwd(z, o, "sb_gate_%d" % i)
            extra = (qkv, z, o, carries)
        m = _out_proj(a, w_out, "out_proj_%d" % i, hosted=nxt_out)
        if more:
            m, crossed_out = m
        saved.append((h, u_t, a_t, m, extra))
        if more:
            (h, u, u_t), both = _post_norm_residual(
                h, m, gains["ln_post_%d" % i], gains["ln_pre_%d" % (i + 1)], "post_norm_%d" % i,
                hosted=([crossed_in[0], crossed_out[0]], PAIR_GATHER))
            weights.append((both[0].reshape((N_DEV,) + both[0].shape[2:]), both[1].reshape(bdim, d)))

    last = DEPTH - 1
    loss, dh, dm_last, dgain_last = _last_norm_and_loss(h, m, gains["ln_post_%d" % last], target, "last_norm_and_loss")
    loss = lax.psum(loss[0, 0], ("x", "y", "c"))

    small = {}
    chip_parts = {}
    core = lax.axis_index("c").astype(jnp.int32).reshape(1)
    for i in reversed(range(DEPTH)):
        h_in, u_t, a_t, m, extra = saved[i]
        w_in, w_out = weights[i]
        if i == last:
            dm, small["ln_post_%d" % i] = dm_last, dgain_last
        else:
            dm, small["ln_post_%d" % i] = _post_norm_bwd(dh, m, gains["ln_post_%d" % i], "post_norm_bwd_%d" % i)
        g_out = _weight_grad(a_t, dm, 1, "grad_w_out_%d" % i).reshape(N_CHIP, 2, wc, d)

        def pair_sum(g, kind):
            (from_sibling,) = _exchange_core_pair([g], "reduce_core_pair_%s_%d" % (kind, i))
            return _add_core_pair(g, from_sibling, core, "add_core_pair_%s_%d" % (kind, i))

        pair_out = pair_sum(g_out, "out")
        da, landed = _out_proj_bwd_act(dm, w_out, "out_proj_bwd_%d" % i, hosted=([pair_out], CHIPS_SCATTER))
        chip_parts[w_out_names[i]] = (pair_out, landed[0])
        if i % 2 == 0:
            (proj,) = extra
            dproj, small["conv_w_%d" % i] = _conv_gate_bwd(proj, da, conv_full[i], "conv_gate_bwd_%d" % i)
        else:
            qkv, z, o, carries = extra
            do, dproj = _sb_gate_bwd(da, z, o, "sb_gate_bwd_%d" % i)
            dproj = _sb_attn_bwd(qkv, do, carries, dproj, "sb_attn_bwd_%d" % i)
        g_in = _weight_grad(u_t, dproj, N_DEV, "grad_w_in_%d" % i)
        pair_in = pair_sum(g_in.reshape((N_CHIP, 2) + g_in.shape[1:]), "in")
        du, landed = _proj_bwd_act(dproj, w_in, "proj_bwd_%d" % i, hosted=([pair_in], CHIPS_SCATTER))
        chip_parts[w_in_names[i]] = (pair_in, landed[0])
        dh, small["ln_pre_%d" % i] = _pre_norm_bwd(du, h_in, gains["ln_pre_%d" % i], dh, "pre_norm_bwd_%d" % i)
    big_names = w_in_names + w_out_names

    gain_names = [n for n in names if n.startswith("ln_")]
    conv_names = ["conv_w_%d" % i for i in conv_layers]
    rows = [small[n] for n in gain_names] + [small[n] for n in conv_names]
    n_rows = len(gain_names) + CONV_K * len(conv_names)
    pad = -n_rows % SUBLANES
    stacked = jnp.concatenate(rows + [jnp.zeros((pad, d), F32)], axis=0)
    (small_all,) = _all_gather([stacked[None]], "gather_small_grads")
    small_all = small_all[0]

    out_g, out_d, out_m, out_v = {}, {}, {}, {}

    def update(n, w2, parts, m2, v2, shape):
        g2, d2, nm2, nv2 = _adamw(w2, parts, m2, v2, "adamw_" + n)
        out_g[n], out_d[n], out_m[n], out_v[n] = (t.reshape(shape) for t in (g2, d2, nm2, nv2))

    chip_arr = (2 * lax.axis_index("x") + lax.axis_index("y")).astype(jnp.int32).reshape(1)
    for n in big_names:
        own, landed = chip_parts[n]
        out_g[n], out_d[n], out_m[n], out_v[n] = _adamw_shard(w[n], own, landed, mom[n], var[n], chip_arr, "adamw_" + n)
    n_gain = len(gain_names)
    stack = lambda src: jnp.stack([src[n] for n in gain_names])
    g2, d2, nm2, nv2 = _adamw(stack(w), small_all[:, :n_gain], stack(mom), stack(var), "adamw_gains")
    for k, n in enumerate(gain_names):
        out_g[n], out_d[n], out_m[n], out_v[n] = g2[k], d2[k], nm2[k], nv2[k]
    wc = bdim // N_DEV
    for k, n in enumerate(conv_names):
        rows_k = small_all[:, n_gain + CONV_K * k:n_gain + CONV_K * (k + 1)]
        parts = lax.dynamic_slice_in_dim(rows_k, place * wc, wc, axis=2)
        update(n, w[n], parts, mom[n], var[n], w[n].shape)

    grad_x = dh.reshape(x.shape)
    return (loss, grad_x, *[out_g[n] for n in names], *[out_d[n] for n in names],
            *[out_m[n] for n in names], *[out_v[n] for n in names])
```

```python
import functools
import math

import jax
import jax.numpy as jnp
from jax import lax
from jax.experimental import pallas as pl
from jax.experimental.pallas import tpu as pltpu

F32 = jnp.float32
BF16 = jnp.bfloat16
MESH = pl.DeviceIdType.MESH
ANY = pl.BlockSpec(memory_space=pl.ANY)

N_DEV = 8
N_CHIP = 4
DEPTH = 4
HEAD_DIM = 128
CONV_K = 3
RMS_EPS = 1e-6
ADAM_LR = 0.001
ADAM_B1 = 0.9
ADAM_B2 = 0.999
ADAM_EPS = 1e-08
ADAM_WD = 0.01
ADAM_STEP = 10

V7X_VMEM_BYTES = 64 * 1024 * 1024
VMEM_LIMIT = V7X_VMEM_BYTES * 3 // 4
LANES = 128
SUBLANES = 8
HEADS_PER_STEP = 2
DEAD_CARRY = -128.0
UNVISITED = -1e30


def _params(*sem):
    return pltpu.CompilerParams(dimension_semantics=sem, vmem_limit_bytes=VMEM_LIMIT)


def _silu_parts(z):
    sig = jax.nn.sigmoid(z)
    return z * sig, sig


NN = (((1,), (0,)), ((), ()))
NT = (((1,), (1,)), ((), ()))
TN = (((0,), (0,)), ((), ()))


def _gridded_call(body, operands, *, grid, in_specs, out_specs, out_shape, scratch_shapes, semantics, name, hosted=None):
    if hosted is None:
        return pl.pallas_call(
            body, grid=grid, in_specs=in_specs, out_specs=out_specs, out_shape=out_shape,
            scratch_shapes=scratch_shapes, compiler_params=_params(*semantics), name=name)(*operands)
    arrays, stage = hosted
    scatter = stage == CHIPS_SCATTER
    n_in, n_out, n_ex, n_scr = len(in_specs), len(out_specs), len(arrays), len(scratch_shapes)

    def hosting_body(*refs):
        ins, refs = refs[:n_in], refs[n_in:]
        ex_in, refs = refs[:n_ex], refs[n_ex:]
        outs, refs = refs[:n_out], refs[n_out:]
        ex_out, refs = refs[:n_ex], refs[n_ex:]
        scratch, sems = refs[:n_scr], refs[n_scr:]
        first = last = None
        for axis, size in enumerate(grid):
            at_start, at_end = pl.program_id(axis) == 0, pl.program_id(axis) == size - 1
            first = at_start if first is None else first & at_start
            last = at_end if last is None else last & at_end

        @pl.when(first)
        def _():
            for cp in _stage_copies(stage, ex_in, ex_out, *sems):
                cp.start()

        body(*ins, *outs, *scratch)

        @pl.when(last)
        def _():
            _wait_all(_stage_copies(stage, ex_in, ex_out, *sems))

    if scatter:
        ex_shapes, aliases = [jax.ShapeDtypeStruct((N_CHIP - 1,) + a.shape[1:], a.dtype) for a in arrays], {}
    else:
        ex_shapes, aliases = [jax.ShapeDtypeStruct(a.shape, a.dtype) for a in arrays], {n_in + a: n_out + a for a in range(n_ex)}
    out = pl.pallas_call(
        hosting_body, grid=grid, in_specs=list(in_specs) + [ANY] * n_ex, out_specs=list(out_specs) + [ANY] * n_ex,
        out_shape=list(out_shape) + ex_shapes, input_output_aliases=aliases,
        scratch_shapes=list(scratch_shapes) + [pltpu.SemaphoreType.DMA((N_CHIP * n_ex,))] * 2,
        compiler_params=_params(*["arbitrary"] * len(grid)), name=name)(*operands, *arrays)
    return out[:n_out], out[n_out:]


def _mm(a, b, *, dims, grid, a_spec, b_spec, o_spec, out_shape, acc_shape, name, hosted=None):
    nk = grid[2]

    def body(a_ref, b_ref, o_ref, *scratch):
        p = lax.dot_general(a_ref[...], b_ref[...], dims, preferred_element_type=F32)
        if nk == 1:
            o_ref[...] = p.astype(o_ref.dtype)
        else:
            acc_ref = scratch[0]
            k = pl.program_id(2)

            @pl.when(k == 0)
            def _():
                acc_ref[...] = p

            @pl.when(k > 0)
            def _():
                acc_ref[...] += p

            @pl.when(k == nk - 1)
            def _():
                o_ref[...] = acc_ref[...].astype(o_ref.dtype)

    scratch = [] if nk == 1 else [pltpu.VMEM(acc_shape, F32)]
    res = _gridded_call(
        body, (a, b), grid=grid, in_specs=[a_spec, b_spec], out_specs=[o_spec], out_shape=[out_shape],
        scratch_shapes=scratch, semantics=("parallel", "parallel", "arbitrary"), name=name, hosted=hosted)
    return res[0] if hosted is None else (res[0][0], res[1])


def _proj(u, w_in, shard0, n_shard, out_dtype, name, hosted=None):
    s, d = u.shape
    ws = w_in.shape[-1]
    tm, tn = min(s, 512), min(ws, 1024)
    nj = ws // tn
    return _mm(
        u, w_in, dims=NN, grid=(s // tm, n_shard * nj, 1),
        a_spec=pl.BlockSpec((tm, d), lambda i, j, k: (i, 0)),
        b_spec=pl.BlockSpec((None, d, tn), lambda i, j, k: (shard0 + j // nj, 0, j % nj)),
        o_spec=pl.BlockSpec((tm, tn), lambda i, j, k: (i, j)),
        out_shape=jax.ShapeDtypeStruct((s, n_shard * ws), out_dtype), acc_shape=(tm, tn), name=name, hosted=hosted,
    )


def _out_proj(a, w_out, name, hosted=None):
    s, bdim = a.shape
    d = w_out.shape[-1]
    tm, tn = min(s, 512), min(d, 1024)
    return _mm(
        a, w_out, dims=NN, grid=(s // tm, d // tn, 1),
        a_spec=pl.BlockSpec((tm, bdim), lambda i, j, k: (i, 0)),
        b_spec=pl.BlockSpec((bdim, tn), lambda i, j, k: (0, j)),
        o_spec=pl.BlockSpec((tm, tn), lambda i, j, k: (i, j)),
        out_shape=jax.ShapeDtypeStruct((s, d), F32), acc_shape=(tm, tn), name=name, hosted=hosted,
    )


def _out_proj_bwd_act(dm, w_out, name, hosted=None):
    s, d = dm.shape
    bdim = w_out.shape[-2]
    tm, tn = min(s, 512), min(bdim, 1024)
    return _mm(
        dm, w_out, dims=NT, grid=(s // tm, bdim // tn, 1),
        a_spec=pl.BlockSpec((tm, d), lambda i, j, k: (i, 0)),
        b_spec=pl.BlockSpec((tn, d), lambda i, j, k: (j, 0)),
        o_spec=pl.BlockSpec((tm, tn), lambda i, j, k: (i, j)),
        out_shape=jax.ShapeDtypeStruct((s, bdim), F32), acc_shape=(tm, tn), name=name, hosted=hosted,
    )


def _weight_grad(act_t, dout, n_blocks, name):
    din, s = act_t.shape
    w = dout.shape[1] // n_blocks
    tm, tn = min(din, 512), min(w, 1024)
    nj = w // tn
    return _mm(
        act_t, dout, dims=NN, grid=(din // tm, n_blocks * nj, 1),
        a_spec=pl.BlockSpec((tm, s), lambda i, j, k: (i, 0)),
        b_spec=pl.BlockSpec((s, tn), lambda i, j, k: (0, j)),
        o_spec=pl.BlockSpec((None, tm, tn), lambda i, j, k: (j // nj, i, j % nj)),
        out_shape=jax.ShapeDtypeStruct((n_blocks, din, w), BF16), acc_shape=(tm, tn), name=name,
    )


def _proj_bwd_act(dproj, w_in, name, hosted=None):
    s = dproj.shape[0]
    n_shards, d, ws = w_in.shape
    tm, tn = min(s, 512), min(d, 512)

    def body(a_ref, b_ref, o_ref):
        acc = None
        for k in range(n_shards):
            p = lax.dot_general(a_ref[:, k * ws:(k + 1) * ws], b_ref[k], NT, preferred_element_type=F32)
            acc = p if acc is None else acc + p
        o_ref[...] = acc

    res = _gridded_call(
        body, (dproj, w_in), grid=(s // tm, d // tn),
        in_specs=[pl.BlockSpec((tm, n_shards * ws), lambda i, j: (i, 0)),
                  pl.BlockSpec((n_shards, tn, ws), lambda i, j: (0, j, 0))],
        out_specs=[pl.BlockSpec((tm, tn), lambda i, j: (i, j))], out_shape=[jax.ShapeDtypeStruct((s, d), F32)],
        scratch_shapes=[], semantics=("parallel", "parallel"), name=name, hosted=hosted)
    return res[0] if hosted is None else (res[0][0], res[1])


def _row_spec(tm, d):
    return pl.BlockSpec((tm, d), lambda i: (i, 0))


def _gain_spec(d):
    return pl.BlockSpec((1, d), lambda i: (0, 0))


def _rstd(x):
    return lax.rsqrt(jnp.mean(x * x, axis=-1, keepdims=True) + RMS_EPS)


def _rmsnorm_fwd(h, gain, name):
    s, d = h.shape
    tm = min(s, 512)

    def body(h_ref, g_ref, u_ref, ut_ref):
        x = h_ref[...]
        u = (x * _rstd(x) * g_ref[...]).astype(u_ref.dtype)
        u_ref[...] = u
        ut_ref[...] = u.T

    return pl.pallas_call(
        body, grid=(s // tm,), in_specs=[_row_spec(tm, d), _gain_spec(d)],
        out_specs=[_row_spec(tm, d), pl.BlockSpec((d, tm), lambda i: (0, i))],
        out_shape=[jax.ShapeDtypeStruct((s, d), BF16), jax.ShapeDtypeStruct((d, s), BF16)],
        compiler_params=_params("parallel"), name=name,
    )(h, gain)


def _cast(block, dtype, name):
    r, c = block.shape
    tr = min(r, 256)

    def body(b_ref, o_ref):
        o_ref[...] = b_ref[...].astype(o_ref.dtype)

    spec = pl.BlockSpec((tr, c), lambda i: (i, 0))
    return pl.pallas_call(
        body, grid=(r // tr,), in_specs=[spec], out_specs=spec, out_shape=jax.ShapeDtypeStruct((r, c), dtype),
        compiler_params=_params("parallel"), name=name,
    )(block)


def _post_norm_residual(h, m, gain, next_gain, name, hosted):
    s, d = h.shape
    tm = min(s, 512)

    def body(h_ref, m_ref, g_ref, gn_ref, o_ref, u_ref, ut_ref):
        x = m_ref[...]
        y = h_ref[...] + x * _rstd(x) * g_ref[...]
        o_ref[...] = y
        u = (y * _rstd(y) * gn_ref[...]).astype(u_ref.dtype)
        u_ref[...] = u
        ut_ref[...] = u.T

    return _gridded_call(
        body, (h, m, gain, next_gain), grid=(s // tm,),
        in_specs=[_row_spec(tm, d), _row_spec(tm, d), _gain_spec(d), _gain_spec(d)],
        out_specs=[_row_spec(tm, d), _row_spec(tm, d), pl.BlockSpec((d, tm), lambda i: (0, i))],
        out_shape=[jax.ShapeDtypeStruct((s, d), F32), jax.ShapeDtypeStruct((s, d), BF16), jax.ShapeDtypeStruct((d, s), BF16)],
        scratch_shapes=[], semantics=("parallel",), name=name, hosted=hosted)


def _last_norm_and_loss(h, m, gain, target, name):
    s, d = h.shape
    tm = min(s, 256)
    n_steps = s // tm

    def body(h_ref, m_ref, g_ref, t_ref, loss_ref, dy_ref, dm_ref, dg_ref, loss_acc, dg_acc):
        i = pl.program_id(0)

        @pl.when(i == 0)
        def _():
            loss_acc[...] = jnp.zeros_like(loss_acc)
            dg_acc[...] = jnp.zeros_like(dg_acc)

        x = m_ref[...]
        rstd = _rstd(x)
        n = x * rstd
        err = h_ref[...] + n * g_ref[...] - t_ref[...]
        dy = err / d
        dy_ref[...] = dy
        dn = dy * g_ref[...]
        dm_ref[...] = (rstd * (dn - n * jnp.mean(dn * n, axis=-1, keepdims=True))).astype(dm_ref.dtype)
        _sum_rows_into(loss_acc, err * err)
        _sum_rows_into(dg_acc, dy * n)

        @pl.when(i == n_steps - 1)
        def _():
            total = jnp.sum(jnp.sum(loss_acc[...], axis=0, keepdims=True), axis=1, keepdims=True)
            loss_ref[...] = 0.5 * total / d
            dg_ref[...] = jnp.sum(dg_acc[...], axis=0, keepdims=True)

    row = _row_spec(tm, d)
    return pl.pallas_call(
        body, grid=(n_steps,), in_specs=[row, row, _gain_spec(d), row],
        out_specs=[pl.BlockSpec((1, 1), lambda i: (0, 0)), row, row, _gain_spec(d)],
        out_shape=[jax.ShapeDtypeStruct((1, 1), F32), jax.ShapeDtypeStruct((s, d), F32),
                   jax.ShapeDtypeStruct((s, d), BF16), jax.ShapeDtypeStruct((1, d), F32)],
        scratch_shapes=[pltpu.VMEM((SUBLANES, d), F32), pltpu.VMEM((SUBLANES, d), F32)],
        compiler_params=_params("arbitrary"), name=name,
    )(h, m, gain, target)


def _sum_rows_into(acc_ref, x):
    tm, d = x.shape
    acc_ref[...] += jnp.sum(x.reshape(tm // SUBLANES, SUBLANES, d), axis=0)


def _norm_bwd_body(n_steps, with_residual):
    def body(*refs):
        dy_ref, x_ref, g_ref = refs[:3]
        dres_ref = refs[3] if with_residual else None
        dx_ref, dg_ref, acc_ref = refs[-3:]
        i = pl.program_id(0)

        @pl.when(i == 0)
        def _():
            acc_ref[...] = jnp.zeros_like(acc_ref)

        x = x_ref[...]
        dy = dy_ref[...]
        rstd = _rstd(x)
        n = x * rstd
        dn = dy * g_ref[...]
        dx = rstd * (dn - n * jnp.mean(dn * n, axis=-1, keepdims=True))
        if with_residual:
            dx = dres_ref[...] + dx
        dx_ref[...] = dx.astype(dx_ref.dtype)
        _sum_rows_into(acc_ref, dy * n)

        @pl.when(i == n_steps - 1)
        def _():
            dg_ref[...] = jnp.sum(acc_ref[...], axis=0, keepdims=True)

    return body


def _norms_bwd(du, h, gain, dh, m_below, gain_below, name):
    s, d = h.shape
    tm = min(s, 256)
    n_steps = s // tm

    def body(du_ref, h_ref, g_ref, dh_ref, m_ref, gb_ref, dx_ref, dg_ref, dm_ref, dgb_ref, acc, acc_below):
        i = pl.program_id(0)

        @pl.when(i == 0)
        def _():
            acc[...] = jnp.zeros_like(acc)
            acc_below[...] = jnp.zeros_like(acc_below)

        def through(x, dy, gain_row):
            rstd = _rstd(x)
            n = x * rstd
            dn = dy * gain_row
            return rstd * (dn - n * jnp.mean(dn * n, axis=-1, keepdims=True)), dy * n

        du = du_ref[...]
        dx, dgain_rows = through(h_ref[...], du, g_ref[...])
        dx = dh_ref[...] + dx
        dx_ref[...] = dx
        _sum_rows_into(acc, dgain_rows)
        dm, dgain_rows = through(m_ref[...], dx, gb_ref[...])
        dm_ref[...] = dm.astype(dm_ref.dtype)
        _sum_rows_into(acc_below, dgain_rows)

        @pl.when(i == n_steps - 1)
        def _():
            dg_ref[...] = jnp.sum(acc[...], axis=0, keepdims=True)
            dgb_ref[...] = jnp.sum(acc_below[...], axis=0, keepdims=True)

    row, gain_spec = _row_spec(tm, d), _gain_spec(d)
    return pl.pallas_call(
        body, grid=(n_steps,), in_specs=[row, row, gain_spec, row, row, gain_spec],
        out_specs=[row, gain_spec, row, gain_spec],
        out_shape=[jax.ShapeDtypeStruct((s, d), F32), jax.ShapeDtypeStruct((1, d), F32),
                   jax.ShapeDtypeStruct((s, d), BF16), jax.ShapeDtypeStruct((1, d), F32)],
        scratch_shapes=[pltpu.VMEM((SUBLANES, d), F32), pltpu.VMEM((SUBLANES, d), F32)],
        compiler_params=_params("arbitrary"), name=name,
    )(du, h, gain, dh, m_below, gain_below)


def _pre_norm_bwd(du, h, gain, dh, name):
    s, d = h.shape
    tm = min(s, 512)
    n_steps = s // tm
    return pl.pallas_call(
        _norm_bwd_body(n_steps, True), grid=(n_steps,),
        in_specs=[_row_spec(tm, d), _row_spec(tm, d), _gain_spec(d), _row_spec(tm, d)],
        out_specs=[_row_spec(tm, d), _gain_spec(d)],
        out_shape=[jax.ShapeDtypeStruct((s, d), F32), jax.ShapeDtypeStruct((1, d), F32)],
        scratch_shapes=[pltpu.VMEM((SUBLANES, d), F32)], compiler_params=_params("arbitrary"), name=name,
    )(du, h, gain, dh)


def _shift_down(p, halo, row, n):
    out = jnp.where(row == 0, halo[SUBLANES - n:SUBLANES - n + 1], pltpu.roll(p, n, 0))
    if n == 2:
        out = jnp.where(row == 1, halo[SUBLANES - 1:SUBLANES], out)
    return out


def _shift_up(p, halo, row, n):
    tm = p.shape[0]
    out = jnp.where(row == tm - 1, halo[n - 1:n], pltpu.roll(p, tm - n, 0))
    if n == 2:
        out = jnp.where(row == tm - 2, halo[0:1], out)
    return out


def _conv_specs(tm, tc, nb, n_row_blocks):
    hb = tm // SUBLANES
    cur = lambda part: pl.BlockSpec((tm, tc), lambda i, j: (i, part * nb + j))
    prev = lambda part: pl.BlockSpec((SUBLANES, tc), lambda i, j: (jnp.maximum(i * hb - 1, 0), part * nb + j))
    nxt = lambda part: pl.BlockSpec(
        (SUBLANES, tc), lambda i, j: (jnp.minimum((i + 1) * hb, n_row_blocks * hb - 1), part * nb + j))
    return cur, prev, nxt


def _conv_gate_fwd(proj, conv_w, name):
    s, b4 = proj.shape
    bdim = b4 // 4
    tm, tc = min(s, 512), min(bdim, 512)
    nb = bdim // tc
    cur, prev, _ = _conv_specs(tm, tc, nb, s // tm)

    def body(b_ref, c_ref, x_ref, z_ref, cp_ref, xp_ref, w_ref, a_ref, at_ref):
        i = pl.program_id(0)
        row = lax.broadcasted_iota(jnp.int32, (tm, tc), 0)
        p = c_ref[...] * x_ref[...]
        halo = jnp.where(i > 0, cp_ref[...] * xp_ref[...], 0.0)
        w = w_ref[...]
        cv = w[0:1] * _shift_down(p, halo, row, 2) + w[1:2] * _shift_down(p, halo, row, 1) + w[2:3] * p
        silu, _ = _silu_parts(z_ref[...])
        a = (silu * (b_ref[...] * cv)).astype(a_ref.dtype)
        a_ref[...] = a
        at_ref[...] = a.T

    return pl.pallas_call(
        body, grid=(s // tm, nb),
        in_specs=[cur(0), cur(1), cur(2), cur(3), prev(1), prev(2), pl.BlockSpec((CONV_K, tc), lambda i, j: (0, j))],
        out_specs=[pl.BlockSpec((tm, tc), lambda i, j: (i, j)), pl.BlockSpec((tc, tm), lambda i, j: (j, i))],
        out_shape=[jax.ShapeDtypeStruct((s, bdim), BF16), jax.ShapeDtypeStruct((bdim, s), BF16)],
        compiler_params=_params("parallel", "parallel"), name=name,
    )(proj, proj, proj, proj, proj, proj, conv_w)


def _conv_gate_bwd(proj, da, conv_w, name):
    s, b4 = proj.shape
    bdim = b4 // 4
    tm, tc = min(s, 128), bdim
    nb = bdim // tc
    n_rows = s // tm
    cur, prev, nxt = _conv_specs(tm, tc, nb, n_rows)
    da_cur = pl.BlockSpec((tm, tc), lambda j, i: (i, j))
    hb = tm // SUBLANES
    da_nxt = pl.BlockSpec((SUBLANES, tc), lambda j, i: (jnp.minimum((i + 1) * hb, n_rows * hb - 1), j))
    swap = lambda spec: pl.BlockSpec(spec.block_shape, lambda j, i, f=spec.index_map: f(i, j))

    def body(b_ref, c_ref, x_ref, z_ref, cp_ref, xp_ref, bn_ref, zn_ref, da_ref, dan_ref, w_ref,
             dproj_ref, dw_ref, acc_ref):
        i = pl.program_id(1)

        @pl.when(i == 0)
        def _():
            acc_ref[...] = jnp.zeros_like(acc_ref)

        row = lax.broadcasted_iota(jnp.int32, (tm, tc), 0)
        w = w_ref[...]
        b, c, x = b_ref[...], c_ref[...], x_ref[...]
        p = c * x
        halo_p = jnp.where(i > 0, cp_ref[...] * xp_ref[...], 0.0)
        p1, p2 = _shift_down(p, halo_p, row, 1), _shift_down(p, halo_p, row, 2)
        cv = w[0:1] * p2 + w[1:2] * p1 + w[2:3] * p
        z = z_ref[...]
        silu, sig = _silu_parts(z)
        da = da_ref[...]
        dy = da * silu
        dcv = dy * b
        silu_n, _ = _silu_parts(zn_ref[...])
        halo_d = jnp.where(i < n_rows - 1, dan_ref[...] * silu_n * bn_ref[...], 0.0)
        dp = w[2:3] * dcv + w[1:2] * _shift_up(dcv, halo_d, row, 1) + w[0:1] * _shift_up(dcv, halo_d, row, 2)
        gates = (dy * cv, dp * x, dp * c, da * (b * cv) * (sig * (1.0 + z * (1.0 - sig))))
        for part, dgate in enumerate(gates):
            dproj_ref[:, part * bdim:(part + 1) * bdim] = dgate.astype(dproj_ref.dtype)
        for k, pk in enumerate((p2, p1, p)):
            _sum_rows_into(acc_ref.at[k], dcv * pk)

        @pl.when(i == n_rows - 1)
        def _():
            for k in range(CONV_K):
                dw_ref[k:k + 1, :] = jnp.sum(acc_ref[k], axis=0, keepdims=True)

    out = pl.BlockSpec((tm, b4), lambda j, i: (i, 0))
    return pl.pallas_call(
        body, grid=(nb, n_rows),
        in_specs=[swap(cur(0)), swap(cur(1)), swap(cur(2)), swap(cur(3)), swap(prev(1)), swap(prev(2)),
                  swap(nxt(0)), swap(nxt(3)), da_cur, da_nxt, pl.BlockSpec((CONV_K, tc), lambda j, i: (0, j))],
        out_specs=[out, pl.BlockSpec((CONV_K, tc), lambda j, i: (0, j))],
        out_shape=[jax.ShapeDtypeStruct((s, b4), BF16), jax.ShapeDtypeStruct((CONV_K, bdim), F32)],
        scratch_shapes=[pltpu.VMEM((CONV_K, SUBLANES, tc), F32)],
        compiler_params=_params("parallel", "arbitrary"), name=name,
    )(proj, proj, proj, proj, proj, proj, proj, proj, da, da, conv_w)


def _split(x):
    hi = x.astype(BF16)
    lo = (x - hi.astype(F32)).astype(BF16)
    return jnp.concatenate([hi, lo], axis=1)


def _row_total(x, column):
    return jnp.broadcast_to(x[:, column:column + 1], (x.shape[0], LANES))


def _sb_tiles(qs, ks, carries, suffix_ones, masks, chain=0):
    items = range(len(qs))
    bk = ks[0].shape[0]
    scale = 1.0 / math.sqrt(HEAD_DIM)
    logits = [lax.dot_general(qs[n], ks[n], NT, preferred_element_type=F32) * scale for n in items]
    es = [jnp.exp(-jnp.abs(logits[n])) for n in items]
    keeps = []
    for n in items:
        log_keep = -(jnp.maximum(logits[n], 0.0) + jnp.log(1.0 + es[n]))
        if masks[n] is not None:
            log_keep = jnp.where(masks[n], log_keep, 0.0)
        keeps.append(_split(log_keep))
    tails = [lax.dot_general(keeps[n], suffix_ones, NN, preferred_element_type=F32) for n in items]
    ws, used = [], []
    for n in items:
        carry = carries[n] if n < len(carries) else used[n - chain] + _row_total(tails[n - chain], 0)
        used.append(carry)
        w = jnp.exp(logits[n] + tails[n] + (carry if carry.shape[1] == 1 else _lane_tile(carry, bk)))
        if masks[n] is not None:
            w = jnp.where(masks[n], w, 0.0)
        ws.append(w)
    return logits, es, tails, ws, used


def _tri_twice(n, upper):
    r = lax.broadcasted_iota(jnp.int32, (2 * n, n), 0)
    r = jnp.where(r >= n, r - n, r)
    c = lax.broadcasted_iota(jnp.int32, (2 * n, n), 1)
    return jnp.where(r <= c if upper else r >= c, 1.0, 0.0).astype(BF16)


def _group_spec(s, width, part, n_groups):
    return pl.BlockSpec((s, width), lambda h: (0, part * n_groups + h))


def _head_cols(g):
    return slice(g * HEAD_DIM, (g + 1) * HEAD_DIM)


def _lane_tile(x, n):
    return x if n == LANES else jnp.concatenate([x] * (n // LANES), axis=1)


def _sb_attn_fwd(qkv, name):
    s, b3 = qkv.shape
    bdim = b3 // 3
    hps = min(HEADS_PER_STEP, bdim // HEAD_DIM)
    width = hps * HEAD_DIM
    n_groups = bdim // width
    blk = min(s, 256)
    n_blk = s // blk

    def body(q_ref, k_ref, v_ref, o_ref, car_ref, carry_ref):
        suffix_ones = _tri_twice(blk, upper=False)
        r = lax.broadcasted_iota(jnp.int32, (blk, blk), 0)
        c = lax.broadcasted_iota(jnp.int32, (blk, blk), 1)
        diag_mask = c < r
        lane = lax.broadcasted_iota(jnp.int32, (blk, LANES), 1)

        def q_block(qi, _):
            q0 = pl.multiple_of(qi * blk, blk)
            rows = pl.ds(q0, blk)
            qs = [q_ref[rows, _head_cols(g)] for g in range(hps)]
            o_ref[rows, :] = jnp.zeros((blk, width), F32)
            car_ref[rows, :] = jnp.full((blk, width), UNVISITED, F32)
            carry_ref[...] = jnp.zeros_like(carry_ref)

            def step(js, tile_masks):
                k0s = [pl.multiple_of(j * blk, blk) for j in js]
                items = [(t, g) for t in range(len(js)) for g in range(hps)]
                ks = [k_ref[pl.ds(k0s[t], blk), _head_cols(g)] for t, g in items]
                first = [carry_ref[g] for g in range(hps)]
                _, _, tails, ws, carries = _sb_tiles([qs[g] for _, g in items], ks, first, suffix_ones,
                                                     [tile_masks[t] for t, _ in items], chain=hps)
                for g in range(hps):
                    mine = [n for n, (_, h) in enumerate(items) if h == g]
                    acc, saved = None, car_ref[rows, _head_cols(g)]
                    for n in mine:
                        v = v_ref[pl.ds(k0s[items[n][0]], blk), _head_cols(g)]
                        p = lax.dot_general(ws[n].astype(BF16), v, NN, preferred_element_type=F32)
                        acc = p if acc is None else acc + p
                        saved = jnp.where(lane == js[items[n][0]], carries[n], saved)
                    o_ref[rows, _head_cols(g)] += acc
                    car_ref[rows, _head_cols(g)] = saved
                    carry_ref[g] = carries[mine[-1]] + _row_total(tails[mine[-1]], 0)

            @pl.when(qi == 0)
            def _():
                step([0], [diag_mask])

            @pl.when(qi > 0)
            def _():
                step([qi, qi - 1], [diag_mask, None])

            def alive():
                top = jnp.max(jnp.max(carry_ref[...], axis=0), axis=0, keepdims=True)
                return (jnp.max(top, axis=1, keepdims=True)[0, 0] >= DEAD_CARRY).astype(jnp.int32)

            left = jnp.maximum(qi - 1, 0)

            def pair(state):
                p, _ = state
                j = qi - 2 - 2 * p
                step([j, j - 1], [None, None])
                return p + 1, alive()

            p, live = lax.while_loop(lambda state: (state[0] < left // 2) & (state[1] > 0), pair, (0, alive()))

            @pl.when((left % 2 == 1) & (p == left // 2) & (live > 0))
            def _():
                step([0], [None])

            return 0

        lax.fori_loop(0, n_blk, q_block, 0)

    out = pl.BlockSpec((s, width), lambda h: (0, h))
    shape = jax.ShapeDtypeStruct((s, bdim), F32)
    return pl.pallas_call(
        body, grid=(n_groups,),
        in_specs=[_group_spec(s, width, part, n_groups) for part in range(3)],
        out_specs=[out, out], out_shape=[shape, shape], scratch_shapes=[pltpu.VMEM((hps, blk, LANES), F32)],
        compiler_params=_params("parallel"), name=name,
    )(qkv, qkv, qkv)


def _sb_attn_bwd(qkv, do, carries, dproj, name):
    s, b3 = qkv.shape
    bdim = b3 // 3
    hps = min(HEADS_PER_STEP, bdim // HEAD_DIM)
    width = hps * HEAD_DIM
    n_groups = bdim // width
    blk = min(s, 256)
    n_blk = s // blk
    scale = 1.0 / math.sqrt(HEAD_DIM)

    def sweep(q_ref, k_ref, v_ref, do_ref, car_ref, dq_ref, dk_acc, dv_acc, dq_acc, before_ref):
        suffix_ones = _tri_twice(blk, upper=False)
        prefix_ones = _tri_twice(blk, upper=True)
        r = lax.broadcasted_iota(jnp.int32, (blk, blk), 0)
        c = lax.broadcasted_iota(jnp.int32, (blk, blk), 1)
        diag_mask = c < r
        lane = lax.broadcasted_iota(jnp.int32, (blk, LANES), 1)
        dk_acc[...] = jnp.zeros_like(dk_acc)
        dv_acc[...] = jnp.zeros_like(dv_acc)

        def q_block(qi, _):
            q0 = pl.multiple_of(qi * blk, blk)
            rows = pl.ds(q0, blk)
            qs = [q_ref[rows, _head_cols(g)] for g in range(hps)]
            dos = [do_ref[rows, _head_cols(g)] for g in range(hps)]
            dq_acc[...] = jnp.zeros_like(dq_acc)
            before_ref[...] = jnp.zeros_like(before_ref)

            def step(js, tile_masks):
                masks = [tile_masks[t] for t in range(len(js)) for _ in range(hps)]
                k0s = [pl.multiple_of(j * blk, blk) for j in js]
                items = [(t, g) for t in range(len(js)) for g in range(hps)]
                every = range(len(items))
                ks = [k_ref[pl.ds(k0s[t], blk), _head_cols(g)] for t, g in items]
                dws = [lax.dot_general(dos[g], v_ref[pl.ds(k0s[t], blk), _head_cols(g)], NT, preferred_element_type=F32)
                       for t, g in items]
                carries = [jnp.sum(jnp.where(lane == js[t], car_ref[rows, _head_cols(g)], 0.0), axis=1, keepdims=True)
                           for t, g in items]
                logits, es, _, ws, _ = _sb_tiles([qs[g] for _, g in items], ks, carries, suffix_ones, masks)
                gws = [dws[n] * ws[n] for n in every]
                g_upto = [lax.dot_general(_split(gws[n]), prefix_ones, NN, preferred_element_type=F32) for n in every]
                dss, befores = [], []
                for n, (t, g) in enumerate(items):
                    before = before_ref[g] if t == 0 else befores[n - hps] + _row_total(g_upto[n - hps], blk - 1)
                    befores.append(before)
                    sig = jnp.where(logits[n] >= 0.0, 1.0, es[n]) / (1.0 + es[n])
                    dlogits = gws[n] - sig * (_lane_tile(before, blk) + g_upto[n])
                    if masks[n] is not None:
                        dlogits = jnp.where(masks[n], dlogits, 0.0)
                    dss.append((dlogits * scale).astype(BF16))
                for g in range(hps):
                    mine = [n for n in every if items[n][1] == g]
                    dq = None
                    for n in mine:
                        k0 = k0s[items[n][0]]
                        p = lax.dot_general(dss[n], ks[n], NN, preferred_element_type=F32)
                        dq = p if dq is None else dq + p
                        dk_acc[pl.ds(k0, blk), _head_cols(g)] += lax.dot_general(
                            dss[n], qs[g], TN, preferred_element_type=F32)
                        dv_acc[pl.ds(k0, blk), _head_cols(g)] += lax.dot_general(
                            ws[n].astype(BF16), dos[g], TN, preferred_element_type=F32)
                    dq_acc[:, _head_cols(g)] += dq
                    before_ref[g] = befores[mine[-1]] + _row_total(g_upto[mine[-1]], blk - 1)

            top = car_ref[rows, _head_cols(0)]
            for g in range(1, hps):
                top = jnp.maximum(top, car_ref[rows, _head_cols(g)])
            top = jnp.max(top, axis=0, keepdims=True)
            lane_row = lax.broadcasted_iota(jnp.int32, (1, LANES), 1)
            counted = jnp.where((top >= DEAD_CARRY) & (lane_row < qi), 1.0, 0.0)
            n_alive = jnp.sum(counted, axis=1, keepdims=True)[0, 0].astype(jnp.int32)
            left = jnp.maximum(n_alive - 1, 0)
            start = qi - 1 - left

            @pl.when(left % 2 == 1)
            def _():
                step([start], [None])

            def pair(p, _):
                j = start + left % 2 + 2 * p
                step([j, j + 1], [None, None])
                return 0

            lax.fori_loop(0, left // 2, pair, 0)

            @pl.when(qi == 0)
            def _():
                step([0], [diag_mask])

            @pl.when(qi > 0)
            def _():
                step([qi - 1, qi], [None, diag_mask])
            dq_ref[rows, :] = dq_acc[...].astype(dq_ref.dtype)
            return 0

        lax.fori_loop(0, n_blk, q_block, 0)

    def body(q_ref, k_ref, v_ref, do_ref, car_ref, dproj_ref, out_ref, dk_acc, dv_acc, dq_acc, before_ref):
        part = pl.program_id(1)

        @pl.when(part == 0)
        def _():
            sweep(q_ref, k_ref, v_ref, do_ref, car_ref, out_ref, dk_acc, dv_acc, dq_acc, before_ref)

        @pl.when(part == 1)
        def _():
            out_ref[...] = dk_acc[...].astype(out_ref.dtype)

        @pl.when(part == 2)
        def _():
            out_ref[...] = dv_acc[...].astype(out_ref.dtype)

    qkv_spec = lambda which: pl.BlockSpec((s, width), lambda g, p: (0, which * n_groups + g))
    once = pl.BlockSpec((s, width), lambda g, p: (0, g), pipeline_mode=pl.Buffered(1))
    return pl.pallas_call(
        body, grid=(n_groups, 3),
        in_specs=[qkv_spec(0), qkv_spec(1), qkv_spec(2), once, once, ANY],
        out_specs=pl.BlockSpec((s, width), lambda g, p: (0, p * n_groups + g)),
        out_shape=jax.ShapeDtypeStruct(dproj.shape, dproj.dtype), input_output_aliases={5: 0},
        scratch_shapes=[pltpu.VMEM((s, width), F32), pltpu.VMEM((s, width), F32), pltpu.VMEM((blk, width), F32),
                        pltpu.VMEM((hps, blk, LANES), F32)],
        compiler_params=_params("parallel", "arbitrary"), name=name,
    )(qkv, qkv, qkv, do, carries, dproj)


def _sb_gate_fwd(z, o, name):
    s, bdim = z.shape
    tm = min(s, 512)

    def body(z_ref, o_ref, a_ref, at_ref):
        silu, _ = _silu_parts(z_ref[...])
        a = (silu * o_ref[...]).astype(a_ref.dtype)
        a_ref[...] = a
        at_ref[...] = a.T

    return pl.pallas_call(
        body, grid=(s // tm,), in_specs=[_row_spec(tm, bdim), _row_spec(tm, bdim)],
        out_specs=[_row_spec(tm, bdim), pl.BlockSpec((bdim, tm), lambda i: (0, i))],
        out_shape=[jax.ShapeDtypeStruct((s, bdim), BF16), jax.ShapeDtypeStruct((bdim, s), BF16)],
        compiler_params=_params("parallel"), name=name,
    )(z, o)


def _sb_gate_bwd(da, z, o, name):
    s, bdim = z.shape
    tm = min(s, 512)

    def body(da_ref, z_ref, o_ref, do_ref, dz_ref):
        z = z_ref[...]
        da = da_ref[...]
        silu, sig = _silu_parts(z)
        do_ref[...] = (da * silu).astype(do_ref.dtype)
        dz_ref[...] = (da * o_ref[...] * (sig * (1.0 + z * (1.0 - sig)))).astype(dz_ref.dtype)

    spec = _row_spec(tm, bdim)
    return pl.pallas_call(
        body, grid=(s // tm,), in_specs=[spec, spec, spec],
        out_specs=[spec, pl.BlockSpec((tm, bdim), lambda i: (i, 3))],
        out_shape=[jax.ShapeDtypeStruct((s, bdim), BF16), jax.ShapeDtypeStruct((s, 4 * bdim), BF16)],
        compiler_params=_params("parallel"), name=name,
    )(da, z, o)


def _into_slot(block, place, dtype, name):
    r, c = block.shape
    tr = min(r, 256)

    def body(place_ref, b_ref, o_ref):
        o_ref[...] = b_ref[...].astype(o_ref.dtype)

    grid_spec = pltpu.PrefetchScalarGridSpec(
        num_scalar_prefetch=1, grid=(r // tr,),
        in_specs=[pl.BlockSpec((tr, c), lambda i, place_ref: (i, 0))],
        out_specs=pl.BlockSpec((None, tr, c), lambda i, place_ref: (place_ref[0], i, 0)),
    )
    return pl.pallas_call(
        body, grid_spec=grid_spec, out_shape=jax.ShapeDtypeStruct((N_DEV, r, c), dtype),
        compiler_params=_params("parallel"), name=name,
    )(place, block)


def _add_core_pair(grads, received, core, name):
    _, _, r, c = grads.shape
    tr = min(r, 1024)

    def body(core_ref, g_ref, r_ref, o_ref):
        o_ref[...] = (g_ref[...].astype(F32) + r_ref[...].astype(F32)).astype(o_ref.dtype)

    grid_spec = pltpu.PrefetchScalarGridSpec(
        num_scalar_prefetch=1, grid=(N_CHIP, r // tr),
        in_specs=[pl.BlockSpec((None, None, tr, c), lambda q, i, core_ref: (q, core_ref[0], i, 0)),
                  pl.BlockSpec((None, tr, c), lambda q, i, core_ref: (q, i, 0))],
        out_specs=pl.BlockSpec((None, tr, c), lambda q, i, core_ref: (q, i, 0)),
    )
    return pl.pallas_call(
        body, grid_spec=grid_spec, out_shape=jax.ShapeDtypeStruct((N_CHIP, r, c), BF16),
        compiler_params=_params("parallel", "parallel"), name=name,
    )(core, grads, received)


def _adamw_step(g, w, m, v, g_ref, d_ref, nm_ref, nv_ref):
    new_m = ADAM_B1 * m + (1.0 - ADAM_B1) * g
    new_v = ADAM_B2 * v + (1.0 - ADAM_B2) * (g * g)
    m_hat = new_m / (1.0 - ADAM_B1 ** ADAM_STEP)
    v_hat = new_v / (1.0 - ADAM_B2 ** ADAM_STEP)
    g_ref[...] = g
    d_ref[...] = -ADAM_LR * (m_hat / (jnp.sqrt(v_hat) + ADAM_EPS) + ADAM_WD * w)
    nm_ref[...] = new_m
    nv_ref[...] = new_v


def _adamw(w, parts, m, v, name):
    r, c = w.shape
    n_parts = parts.shape[0]
    tr = min(r, 256)

    def body(w_ref, p_ref, m_ref, v_ref, *out_refs):
        g = p_ref[0].astype(F32)
        for k in range(1, n_parts):
            g = g + p_ref[k].astype(F32)
        _adamw_step(g, w_ref[...], m_ref[...], v_ref[...], *out_refs)

    spec = pl.BlockSpec((tr, c), lambda i: (i, 0))
    shape = jax.ShapeDtypeStruct((r, c), F32)
    return pl.pallas_call(
        body, grid=(r // tr,), in_specs=[spec, pl.BlockSpec((n_parts, tr, c), lambda i: (0, i, 0)), spec, spec],
        out_specs=[spec] * 4, out_shape=[shape] * 4, compiler_params=_params("parallel"), name=name,
    )(w, parts, m, v)


def _adamw_shard(w, grads, landed, m, v, place, name):
    r, c = w.shape
    n_landed = landed.shape[0]
    tr = min(r, 512)

    def body(place_ref, w_ref, own_ref, l_ref, m_ref, v_ref, *out_refs):
        g = own_ref[...].astype(F32)
        for k in range(n_landed):
            g = g + l_ref[k].astype(F32)
        _adamw_step(g, w_ref[...], m_ref[...], v_ref[...], *out_refs)

    spec = pl.BlockSpec((tr, c), lambda i, place_ref: (i, 0))
    grid_spec = pltpu.PrefetchScalarGridSpec(
        num_scalar_prefetch=1, grid=(r // tr,),
        in_specs=[spec, pl.BlockSpec((None, tr, c), lambda i, place_ref: (place_ref[0], i, 0)),
                  pl.BlockSpec((n_landed, tr, c), lambda i, place_ref: (0, i, 0)), spec, spec],
        out_specs=[spec] * 4,
    )
    return pl.pallas_call(
        body, grid_spec=grid_spec, out_shape=[jax.ShapeDtypeStruct((r, c), F32)] * 4,
        compiler_params=_params("parallel"), name=name,
    )(place, w, grads, landed, m, v)


def _place():
    x, y, c = lax.axis_index("x"), lax.axis_index("y"), lax.axis_index("c")
    other_chips = [(1 - x, y), (x, 1 - y), (1 - x, 1 - y)]
    return x, y, c, other_chips


def _all_gather(blocks, name):
    n_arr = len(blocks)
    items = [(a, i) for a, blk in enumerate(blocks) for i in range(blk.shape[0])]
    n_items = len(items)

    def body(*refs):
        srcs, outs = refs[:n_arr], refs[n_arr:2 * n_arr]
        send_sems, recv_sems, local_sems = refs[2 * n_arr:]
        x, y, c, other_chips = _place()
        me, sibling = (x, y, c), (x, y, 1 - c)

        def slot(it, dev):
            a, i = items[it]
            return outs[a].at[i, 4 * dev[0] + 2 * dev[1] + dev[2]]

        def copy(it, k, block_of, to, from_src=False):
            a, i = items[it]
            return pltpu.make_async_remote_copy(
                src_ref=srcs[a].at[i] if from_src else slot(it, block_of), dst_ref=slot(it, block_of),
                send_sem=send_sems.at[it * 7 + k], recv_sem=recv_sems.at[it * 7 + k],
                device_id=to, device_id_type=MESH)

        own = [pltpu.make_async_copy(srcs[items[it][0]].at[items[it][1]], slot(it, me), local_sems.at[it])
               for it in range(n_items)]
        for cp in own:
            cp.start()
        first = []
        for it in range(n_items):
            first.append(copy(it, 0, me, sibling, from_src=True))
            first += [copy(it, 1 + j, me, (*chip, c), from_src=True) for j, chip in enumerate(other_chips)]
        for cp in first:
            cp.start()
        passed = []
        for it in range(n_items):
            for j, chip in enumerate(other_chips):
                copy(it, 1 + j, (*chip, c), me).wait_recv()
                passed.append(copy(it, 4 + j, (*chip, c), sibling))
                passed[-1].start()
        for it in range(n_items):
            copy(it, 0, sibling, me).wait_recv()
            for j, chip in enumerate(other_chips):
                copy(it, 4 + j, (*chip, 1 - c), me).wait_recv()
        for cp in first + passed:
            cp.wait_send()
        for cp in own:
            cp.wait()

    return pl.pallas_call(
        body, in_specs=[ANY] * n_arr, out_specs=[ANY] * n_arr,
        out_shape=[jax.ShapeDtypeStruct((b.shape[0], N_DEV) + b.shape[1:], b.dtype) for b in blocks],
        scratch_shapes=[pltpu.SemaphoreType.DMA((7 * n_items,)), pltpu.SemaphoreType.DMA((7 * n_items,)),
                        pltpu.SemaphoreType.DMA((n_items,))],
        name=name,
    )(*blocks)


def _exchange_core_pair(grads, name):
    n_arr = len(grads)

    def body(*refs):
        srcs, outs = refs[:n_arr], refs[n_arr:2 * n_arr]
        send_sems, recv_sems = refs[2 * n_arr:]
        x, y, c, _ = _place()
        copies = [
            pltpu.make_async_remote_copy(
                src_ref=srcs[a].at[q, 1 - c], dst_ref=outs[a].at[q],
                send_sem=send_sems.at[a * N_CHIP + q], recv_sem=recv_sems.at[a * N_CHIP + q],
                device_id=(x, y, 1 - c), device_id_type=MESH)
            for a in range(n_arr) for q in range(N_CHIP)]
        for cp in copies:
            cp.start()
        for cp in copies:
            cp.wait_recv()
        for cp in copies:
            cp.wait_send()

    return pl.pallas_call(
        body, in_specs=[ANY] * n_arr, out_specs=[ANY] * n_arr,
        out_shape=[jax.ShapeDtypeStruct((N_CHIP,) + g.shape[2:], g.dtype) for g in grads],
        scratch_shapes=[pltpu.SemaphoreType.DMA((N_CHIP * n_arr,)), pltpu.SemaphoreType.DMA((N_CHIP * n_arr,))],
        name=name,
    )(*grads)


CHIPS_SCATTER, CHIPS_GATHER, PAIR_GATHER = "chips_scatter", "chips_gather", "pair_gather"


def _stage_copies(stage, srcs, outs, send_sems, recv_sems):
    x, y, c, other_chips = _place()
    my_chip = 2 * x + y
    copies = []
    for a in range(len(outs)):
        if stage == PAIR_GATHER:
            moves = [(outs[a].at[q, c], outs[a].at[q, c], (x, y, 1 - c)) for q in range(N_CHIP)]
        elif stage == CHIPS_GATHER:
            moves = [(outs[a].at[my_chip, c], outs[a].at[my_chip, c], (*chip, c)) for chip in other_chips]
        else:
            moves = [(srcs[a].at[2 * chip[0] + chip[1]], outs[a].at[j], (*chip, c)) for j, chip in enumerate(other_chips)]
        for k, (src, dst, peer) in enumerate(moves):
            copies.append(pltpu.make_async_remote_copy(
                src_ref=src, dst_ref=dst, send_sem=send_sems.at[a * N_CHIP + k], recv_sem=recv_sems.at[a * N_CHIP + k],
                device_id=peer, device_id_type=MESH))
    return copies


def _wait_all(copies):
    for cp in copies:
        cp.wait_recv()
    for cp in copies:
        cp.wait_send()


def kernel(x, ln_pre_0, conv_w_in_0, conv_w_0, conv_w_out_0, ln_post_0, ln_pre_1, sb_w_in_1, sb_w_out_1, ln_post_1, ln_pre_2, conv_w_in_2, conv_w_2, conv_w_out_2, ln_post_2, ln_pre_3, sb_w_in_3, sb_w_out_3, ln_post_3, loss_target, m_ln_pre_0, m_conv_w_in_0, m_conv_w_0, m_conv_w_out_0, m_ln_post_0, m_ln_pre_1, m_sb_w_in_1, m_sb_w_out_1, m_ln_post_1, m_ln_pre_2, m_conv_w_in_2, m_conv_w_2, m_conv_w_out_2, m_ln_post_2, m_ln_pre_3, m_sb_w_in_3, m_sb_w_out_3, m_ln_post_3, v_ln_pre_0, v_conv_w_in_0, v_conv_w_0, v_conv_w_out_0, v_ln_post_0, v_ln_pre_1, v_sb_w_in_1, v_sb_w_out_1, v_ln_post_1, v_ln_pre_2, v_conv_w_in_2, v_conv_w_2, v_conv_w_out_2, v_ln_post_2, v_ln_pre_3, v_sb_w_in_3, v_sb_w_out_3, v_ln_post_3):
    names = ['ln_pre_0', 'conv_w_in_0', 'conv_w_0', 'conv_w_out_0', 'ln_post_0', 'ln_pre_1', 'sb_w_in_1', 'sb_w_out_1',
             'ln_post_1', 'ln_pre_2', 'conv_w_in_2', 'conv_w_2', 'conv_w_out_2', 'ln_post_2', 'ln_pre_3', 'sb_w_in_3',
             'sb_w_out_3', 'ln_post_3']
    given = dict(locals())
    w = {n: given[n] for n in names}
    mom = {n: given["m_" + n] for n in names}
    var = {n: given["v_" + n] for n in names}
    conv_layers = [i for i in range(DEPTH) if i % 2 == 0]
    w_in_names = [("conv_w_in_%d" if i % 2 == 0 else "sb_w_in_%d") % i for i in range(DEPTH)]
    w_out_names = [("conv_w_out_%d" if i % 2 == 0 else "sb_w_out_%d") % i for i in range(DEPTH)]

    s, d = x.shape[1:]
    h = x.reshape(s, d)
    target = loss_target.reshape(s, d)
    gains = {n: w[n].reshape(1, d) for n in names if n.startswith("ln_")}
    place = 4 * lax.axis_index("x") + 2 * lax.axis_index("y") + lax.axis_index("c")
    place_arr = place.astype(jnp.int32).reshape(1)
    bdim = w[w_out_names[0]].shape[0] * N_DEV
    wc = bdim // N_DEV

    conv_rows = jnp.concatenate([w["conv_w_%d" % i] for i in conv_layers], axis=0)
    first = _all_gather([_cast(w[n], BF16, "cast_" + n)[None] for n in (w_in_names[0], w_out_names[0])] + [conv_rows[None]],
                        "gather_first_layer")
    slots = {n: _into_slot(w[n], place_arr, BF16, "slot_" + n) for n in w_in_names[1:] + w_out_names[1:]}
    slots = {n: a.reshape((N_CHIP, 2) + a.shape[1:]) for n, a in slots.items()}
    conv_all = first[2][0].reshape(N_DEV, len(conv_layers), CONV_K, wc)
    conv_all = jnp.transpose(conv_all, (1, 2, 0, 3)).reshape(len(conv_layers), CONV_K, bdim)
    conv_full = {layer: conv_all[n] for n, layer in enumerate(conv_layers)}
    weights = [(first[0][0], first[1][0].reshape(bdim, d))]

    saved = []
    u, u_t = _rmsnorm_fwd(h, gains["ln_pre_0"], "pre_norm_0")
    for i in range(DEPTH):
        w_in, w_out = weights[i]
        more = i + 1 < DEPTH
        nxt_in = ([slots[w_in_names[i + 1]]], CHIPS_GATHER) if more else None
        nxt_out = ([slots[w_out_names[i + 1]]], CHIPS_GATHER) if more else None
        if i % 2 == 0:
            proj = _proj(u, w_in, 0, N_DEV, F32, "proj_%d" % i, hosted=nxt_in)
            if more:
                proj, crossed_in = proj
            a, a_t = _conv_gate_fwd(proj, conv_full[i], "conv_gate_%d" % i)
            extra = (proj,)
        else:
            qkv = _proj(u, w_in, 0, 6, BF16, "proj_qkv_%d" % i, hosted=nxt_in)
            if more:
                qkv, crossed_in = qkv
            z = _proj(u, w_in, 6, 2, F32, "proj_z_%d" % i)
            o, carries = _sb_attn_fwd(qkv, "sb_attn_%d" % i)
            a, a_t = _sb_gate_fwd(z, o, "sb_gate_%d" % i)
            extra = (qkv, z, o, carries)
        m = _out_proj(a, w_out, "out_proj_%d" % i, hosted=nxt_out)
        if more:
            m, crossed_out = m
        saved.append((h, u_t, a_t, m, extra))
        if more:
            (h, u, u_t), both = _post_norm_residual(
                h, m, gains["ln_post_%d" % i], gains["ln_pre_%d" % (i + 1)], "post_norm_%d" % i,
                hosted=([crossed_in[0], crossed_out[0]], PAIR_GATHER))
            weights.append((both[0].reshape((N_DEV,) + both[0].shape[2:]), both[1].reshape(bdim, d)))

    last = DEPTH - 1
    small = {}
    loss, dh, dm, small["ln_post_%d" % last] = _last_norm_and_loss(
        h, m, gains["ln_post_%d" % last], target, "last_norm_and_loss")
    loss = lax.psum(loss[0, 0], ("x", "y", "c"))

    chip_parts = {}
    core = lax.axis_index("c").astype(jnp.int32).reshape(1)
    for i in reversed(range(DEPTH)):
        h_in, u_t, a_t, m, extra = saved[i]
        w_in, w_out = weights[i]
        g_out = _weight_grad(a_t, dm, 1, "grad_w_out_%d" % i).reshape(N_CHIP, 2, wc, d)

        def pair_sum(g, kind):
            (from_sibling,) = _exchange_core_pair([g], "reduce_core_pair_%s_%d" % (kind, i))
            return _add_core_pair(g, from_sibling, core, "add_core_pair_%s_%d" % (kind, i))

        pair_out = pair_sum(g_out, "out")
        da, landed = _out_proj_bwd_act(dm, w_out, "out_proj_bwd_%d" % i, hosted=([pair_out], CHIPS_SCATTER))
        chip_parts[w_out_names[i]] = (pair_out, landed[0])
        if i % 2 == 0:
            (proj,) = extra
            dproj, small["conv_w_%d" % i] = _conv_gate_bwd(proj, da, conv_full[i], "conv_gate_bwd_%d" % i)
        else:
            qkv, z, o, carries = extra
            do, dproj = _sb_gate_bwd(da, z, o, "sb_gate_bwd_%d" % i)
            dproj = _sb_attn_bwd(qkv, do, carries, dproj, "sb_attn_bwd_%d" % i)
        g_in = _weight_grad(u_t, dproj, N_DEV, "grad_w_in_%d" % i)
        pair_in = pair_sum(g_in.reshape((N_CHIP, 2) + g_in.shape[1:]), "in")
        du, landed = _proj_bwd_act(dproj, w_in, "proj_bwd_%d" % i, hosted=([pair_in], CHIPS_SCATTER))
        chip_parts[w_in_names[i]] = (pair_in, landed[0])
        if i > 0:
            dh, small["ln_pre_%d" % i], dm, small["ln_post_%d" % (i - 1)] = _norms_bwd(
                du, h_in, gains["ln_pre_%d" % i], dh, saved[i - 1][3], gains["ln_post_%d" % (i - 1)], "norms_bwd_%d" % i)
        else:
            dh, small["ln_pre_%d" % i] = _pre_norm_bwd(du, h_in, gains["ln_pre_%d" % i], dh, "pre_norm_bwd_%d" % i)
    big_names = w_in_names + w_out_names

    gain_names = [n for n in names if n.startswith("ln_")]
    conv_names = ["conv_w_%d" % i for i in conv_layers]
    rows = [small[n] for n in gain_names] + [small[n] for n in conv_names]
    n_rows = len(gain_names) + CONV_K * len(conv_names)
    pad = -n_rows % SUBLANES
    stacked = jnp.concatenate(rows + [jnp.zeros((pad, d), F32)], axis=0)
    (small_all,) = _all_gather([stacked[None]], "gather_small_grads")
    small_all = small_all[0]

    out_g, out_d, out_m, out_v = {}, {}, {}, {}

    def update(n, w2, parts, m2, v2, shape):
        g2, d2, nm2, nv2 = _adamw(w2, parts, m2, v2, "adamw_" + n)
        out_g[n], out_d[n], out_m[n], out_v[n] = (t.reshape(shape) for t in (g2, d2, nm2, nv2))

    chip_arr = (2 * lax.axis_index("x") + lax.axis_index("y")).astype(jnp.int32).reshape(1)
    for n in big_names:
        own, landed = chip_parts[n]
        out_g[n], out_d[n], out_m[n], out_v[n] = _adamw_shard(w[n], own, landed, mom[n], var[n], chip_arr, "adamw_" + n)
    n_gain = len(gain_names)
    stack = lambda src: jnp.stack([src[n] for n in gain_names])
    g2, d2, nm2, nv2 = _adamw(stack(w), small_all[:, :n_gain], stack(mom), stack(var), "adamw_gains")
    for k, n in enumerate(gain_names):
        out_g[n], out_d[n], out_m[n], out_v[n] = g2[k], d2[k], nm2[k], nv2[k]
    wc = bdim // N_DEV
    for k, n in enumerate(conv_names):
        rows_k = small_all[:, n_gain + CONV_K * k:n_gain + CONV_K * (k + 1)]
        parts = lax.dynamic_slice_in_dim(rows_k, place * wc, wc, axis=2)
        update(n, w[n], parts, mom[n], var[n], w[n].shape)

    grad_x = dh.reshape(x.shape)
    return (loss, grad_x, *[out_g[n] for n in names], *[out_d[n] for n in names],
            *[out_m[n] for n in names], *[out_v[n] for n in names])
```

```python
import functools
import math

import jax
import jax.numpy as jnp
from jax import lax
from jax.experimental import pallas as pl
from jax.experimental.pallas import tpu as pltpu

F32 = jnp.float32
BF16 = jnp.bfloat16
MESH = pl.DeviceIdType.MESH
ANY = pl.BlockSpec(memory_space=pl.ANY)

N_DEV = 8
N_CHIP = 4
DEPTH = 4
HEAD_DIM = 128
CONV_K = 3
RMS_EPS = 1e-6
ADAM_LR = 0.001
ADAM_B1 = 0.9
ADAM_B2 = 0.999
ADAM_EPS = 1e-08
ADAM_WD = 0.01
ADAM_STEP = 10

V7X_VMEM_BYTES = 64 * 1024 * 1024
VMEM_LIMIT = V7X_VMEM_BYTES * 3 // 4
LANES = 128
SUBLANES = 8
HEADS_PER_STEP = 2
DEAD_CARRY = -128.0
UNVISITED = -1e30


def _params(*sem):
    return pltpu.CompilerParams(dimension_semantics=sem, vmem_limit_bytes=VMEM_LIMIT)


def _silu_parts(z):
    sig = jax.nn.sigmoid(z)
    return z * sig, sig


NN = (((1,), (0,)), ((), ()))
NT = (((1,), (1,)), ((), ()))
TN = (((0,), (0,)), ((), ()))


def _gridded_call(body, operands, *, grid, in_specs, out_specs, out_shape, scratch_shapes, semantics, name, hosted=None):
    if hosted is None:
        return pl.pallas_call(
            body, grid=grid, in_specs=in_specs, out_specs=out_specs, out_shape=out_shape,
            scratch_shapes=scratch_shapes, compiler_params=_params(*semantics), name=name)(*operands)
    arrays, stage = hosted
    scatter = stage == CHIPS_SCATTER
    n_in, n_out, n_ex, n_scr = len(in_specs), len(out_specs), len(arrays), len(scratch_shapes)

    def hosting_body(*refs):
        ins, refs = refs[:n_in], refs[n_in:]
        ex_in, refs = refs[:n_ex], refs[n_ex:]
        outs, refs = refs[:n_out], refs[n_out:]
        ex_out, refs = refs[:n_ex], refs[n_ex:]
        scratch, sems = refs[:n_scr], refs[n_scr:]
        first = last = None
        for axis, size in enumerate(grid):
            at_start, at_end = pl.program_id(axis) == 0, pl.program_id(axis) == size - 1
            first = at_start if first is None else first & at_start
            last = at_end if last is None else last & at_end

        @pl.when(first)
        def _():
            for cp in _stage_copies(stage, ex_in, ex_out, *sems):
                cp.start()

        body(*ins, *outs, *scratch)

        @pl.when(last)
        def _():
            _wait_all(_stage_copies(stage, ex_in, ex_out, *sems))

    if scatter:
        ex_shapes, aliases = [jax.ShapeDtypeStruct((N_CHIP - 1,) + a.shape[1:], a.dtype) for a in arrays], {}
    else:
        ex_shapes, aliases = [jax.ShapeDtypeStruct(a.shape, a.dtype) for a in arrays], {n_in + a: n_out + a for a in range(n_ex)}
    out = pl.pallas_call(
        hosting_body, grid=grid, in_specs=list(in_specs) + [ANY] * n_ex, out_specs=list(out_specs) + [ANY] * n_ex,
        out_shape=list(out_shape) + ex_shapes, input_output_aliases=aliases,
        scratch_shapes=list(scratch_shapes) + [pltpu.SemaphoreType.DMA((N_CHIP * n_ex,))] * 2,
        compiler_params=_params(*["arbitrary"] * len(grid)), name=name)(*operands, *arrays)
    return out[:n_out], out[n_out:]


def _mm(a, b, *, dims, grid, a_spec, b_spec, o_spec, out_shape, acc_shape, name, hosted=None):
    nk = grid[2]

    def body(a_ref, b_ref, o_ref, *scratch):
        p = lax.dot_general(a_ref[...], b_ref[...], dims, preferred_element_type=F32)
        if nk == 1:
            o_ref[...] = p.astype(o_ref.dtype)
        else:
            acc_ref = scratch[0]
            k = pl.program_id(2)

            @pl.when(k == 0)
            def _():
                acc_ref[...] = p

            @pl.when(k > 0)
            def _():
                acc_ref[...] += p

            @pl.when(k == nk - 1)
            def _():
                o_ref[...] = acc_ref[...].astype(o_ref.dtype)

    scratch = [] if nk == 1 else [pltpu.VMEM(acc_shape, F32)]
    res = _gridded_call(
        body, (a, b), grid=grid, in_specs=[a_spec, b_spec], out_specs=[o_spec], out_shape=[out_shape],
        scratch_shapes=scratch, semantics=("parallel", "parallel", "arbitrary"), name=name, hosted=hosted)
    return res[0] if hosted is None else (res[0][0], res[1])


def _proj(u, w_in, shard0, n_shard, out_dtype, name, hosted=None):
    s, d = u.shape
    ws = w_in.shape[-1]
    tm, tn = min(s, 1024), min(ws, 1024)
    nj = ws // tn
    return _mm(
        u, w_in, dims=NN, grid=(s // tm, n_shard * nj, 1),
        a_spec=pl.BlockSpec((tm, d), lambda i, j, k: (i, 0)),
        b_spec=pl.BlockSpec((None, d, tn), lambda i, j, k: (shard0 + j // nj, 0, j % nj)),
        o_spec=pl.BlockSpec((tm, tn), lambda i, j, k: (i, j)),
        out_shape=jax.ShapeDtypeStruct((s, n_shard * ws), out_dtype), acc_shape=(tm, tn), name=name, hosted=hosted,
    )


def _out_proj(a, w_out, name, hosted=None):
    s, bdim = a.shape
    d = w_out.shape[-1]
    tm, tn = min(s, 512), min(d, 1024)
    return _mm(
        a, w_out, dims=NN, grid=(s // tm, d // tn, 1),
        a_spec=pl.BlockSpec((tm, bdim), lambda i, j, k: (i, 0)),
        b_spec=pl.BlockSpec((bdim, tn), lambda i, j, k: (0, j)),
        o_spec=pl.BlockSpec((tm, tn), lambda i, j, k: (i, j)),
        out_shape=jax.ShapeDtypeStruct((s, d), F32), acc_shape=(tm, tn), name=name, hosted=hosted,
    )


def _out_proj_bwd_act(dm, w_out, name, hosted=None):
    s, d = dm.shape
    bdim = w_out.shape[-2]
    tm, tn = min(s, 512), min(bdim, 1024)
    return _mm(
        dm, w_out, dims=NT, grid=(s // tm, bdim // tn, 1),
        a_spec=pl.BlockSpec((tm, d), lambda i, j, k: (i, 0)),
        b_spec=pl.BlockSpec((tn, d), lambda i, j, k: (j, 0)),
        o_spec=pl.BlockSpec((tm, tn), lambda i, j, k: (i, j)),
        out_shape=jax.ShapeDtypeStruct((s, bdim), F32), acc_shape=(tm, tn), name=name, hosted=hosted,
    )


def _weight_grad(act_t, dout, n_blocks, name):
    din, s = act_t.shape
    w = dout.shape[1] // n_blocks
    tm, tn = min(din, 512), min(w, 1024)
    nj = w // tn
    return _mm(
        act_t, dout, dims=NN, grid=(din // tm, n_blocks * nj, 1),
        a_spec=pl.BlockSpec((tm, s), lambda i, j, k: (i, 0)),
        b_spec=pl.BlockSpec((s, tn), lambda i, j, k: (0, j)),
        o_spec=pl.BlockSpec((None, tm, tn), lambda i, j, k: (j // nj, i, j % nj)),
        out_shape=jax.ShapeDtypeStruct((n_blocks, din, w), BF16), acc_shape=(tm, tn), name=name,
    )


def _proj_bwd_act(dproj, w_in, name, hosted=None):
    s = dproj.shape[0]
    n_shards, d, ws = w_in.shape
    tm, tn = min(s, 512), min(d, 512)

    def body(a_ref, b_ref, o_ref):
        acc = None
        for k in range(n_shards):
            p = lax.dot_general(a_ref[:, k * ws:(k + 1) * ws], b_ref[k], NT, preferred_element_type=F32)
            acc = p if acc is None else acc + p
        o_ref[...] = acc

    res = _gridded_call(
        body, (dproj, w_in), grid=(s // tm, d // tn),
        in_specs=[pl.BlockSpec((tm, n_shards * ws), lambda i, j: (i, 0)),
                  pl.BlockSpec((n_shards, tn, ws), lambda i, j: (0, j, 0))],
        out_specs=[pl.BlockSpec((tm, tn), lambda i, j: (i, j))], out_shape=[jax.ShapeDtypeStruct((s, d), F32)],
        scratch_shapes=[], semantics=("parallel", "parallel"), name=name, hosted=hosted)
    return res[0] if hosted is None else (res[0][0], res[1])


def _row_spec(tm, d):
    return pl.BlockSpec((tm, d), lambda i: (i, 0))


def _gain_spec(d):
    return pl.BlockSpec((1, d), lambda i: (0, 0))


def _rstd(x):
    return lax.rsqrt(jnp.mean(x * x, axis=-1, keepdims=True) + RMS_EPS)


def _rmsnorm_fwd(h, gain, name):
    s, d = h.shape
    tm = min(s, 512)

    def body(h_ref, g_ref, u_ref, ut_ref):
        x = h_ref[...]
        u = (x * _rstd(x) * g_ref[...]).astype(u_ref.dtype)
        u_ref[...] = u
        ut_ref[...] = u.T

    return pl.pallas_call(
        body, grid=(s // tm,), in_specs=[_row_spec(tm, d), _gain_spec(d)],
        out_specs=[_row_spec(tm, d), pl.BlockSpec((d, tm), lambda i: (0, i))],
        out_shape=[jax.ShapeDtypeStruct((s, d), BF16), jax.ShapeDtypeStruct((d, s), BF16)],
        compiler_params=_params("parallel"), name=name,
    )(h, gain)


def _cast(block, dtype, name):
    r, c = block.shape
    tr = min(r, 256)

    def body(b_ref, o_ref):
        o_ref[...] = b_ref[...].astype(o_ref.dtype)

    spec = pl.BlockSpec((tr, c), lambda i: (i, 0))
    return pl.pallas_call(
        body, grid=(r // tr,), in_specs=[spec], out_specs=spec, out_shape=jax.ShapeDtypeStruct((r, c), dtype),
        compiler_params=_params("parallel"), name=name,
    )(block)


def _post_norm_residual(h, m, gain, next_gain, name, hosted):
    s, d = h.shape
    tm = min(s, 512)

    def body(h_ref, m_ref, g_ref, gn_ref, o_ref, u_ref, ut_ref):
        x = m_ref[...]
        y = h_ref[...] + x * _rstd(x) * g_ref[...]
        o_ref[...] = y
        u = (y * _rstd(y) * gn_ref[...]).astype(u_ref.dtype)
        u_ref[...] = u
        ut_ref[...] = u.T

    return _gridded_call(
        body, (h, m, gain, next_gain), grid=(s // tm,),
        in_specs=[_row_spec(tm, d), _row_spec(tm, d), _gain_spec(d), _gain_spec(d)],
        out_specs=[_row_spec(tm, d), _row_spec(tm, d), pl.BlockSpec((d, tm), lambda i: (0, i))],
        out_shape=[jax.ShapeDtypeStruct((s, d), F32), jax.ShapeDtypeStruct((s, d), BF16), jax.ShapeDtypeStruct((d, s), BF16)],
        scratch_shapes=[], semantics=("parallel",), name=name, hosted=hosted)


def _last_norm_and_loss(h, m, gain, target, name):
    s, d = h.shape
    tm = min(s, 256)
    n_steps = s // tm

    def body(h_ref, m_ref, g_ref, t_ref, loss_ref, dy_ref, dm_ref, dg_ref, loss_acc, dg_acc):
        i = pl.program_id(0)

        @pl.when(i == 0)
        def _():
            loss_acc[...] = jnp.zeros_like(loss_acc)
            dg_acc[...] = jnp.zeros_like(dg_acc)

        x = m_ref[...]
        rstd = _rstd(x)
        n = x * rstd
        err = h_ref[...] + n * g_ref[...] - t_ref[...]
        dy = err / d
        dy_ref[...] = dy
        dn = dy * g_ref[...]
        dm_ref[...] = (rstd * (dn - n * jnp.mean(dn * n, axis=-1, keepdims=True))).astype(dm_ref.dtype)
        _sum_rows_into(loss_acc, err * err)
        _sum_rows_into(dg_acc, dy * n)

        @pl.when(i == n_steps - 1)
        def _():
            total = jnp.sum(jnp.sum(loss_acc[...], axis=0, keepdims=True), axis=1, keepdims=True)
            loss_ref[...] = 0.5 * total / d
            dg_ref[...] = jnp.sum(dg_acc[...], axis=0, keepdims=True)

    row = _row_spec(tm, d)
    return pl.pallas_call(
        body, grid=(n_steps,), in_specs=[row, row, _gain_spec(d), row],
        out_specs=[pl.BlockSpec((1, 1), lambda i: (0, 0)), row, row, _gain_spec(d)],
        out_shape=[jax.ShapeDtypeStruct((1, 1), F32), jax.ShapeDtypeStruct((s, d), F32),
                   jax.ShapeDtypeStruct((s, d), BF16), jax.ShapeDtypeStruct((1, d), F32)],
        scratch_shapes=[pltpu.VMEM((SUBLANES, d), F32), pltpu.VMEM((SUBLANES, d), F32)],
        compiler_params=_params("arbitrary"), name=name,
    )(h, m, gain, target)


def _sum_rows_into(acc_ref, x):
    tm, d = x.shape
    acc_ref[...] += jnp.sum(x.reshape(tm // SUBLANES, SUBLANES, d), axis=0)


def _norm_bwd_body(n_steps, with_residual):
    def body(*refs):
        dy_ref, x_ref, g_ref = refs[:3]
        dres_ref = refs[3] if with_residual else None
        dx_ref, dg_ref, acc_ref = refs[-3:]
        i = pl.program_id(0)

        @pl.when(i == 0)
        def _():
            acc_ref[...] = jnp.zeros_like(acc_ref)

        x = x_ref[...]
        dy = dy_ref[...]
        rstd = _rstd(x)
        n = x * rstd
        dn = dy * g_ref[...]
        dx = rstd * (dn - n * jnp.mean(dn * n, axis=-1, keepdims=True))
        if with_residual:
            dx = dres_ref[...] + dx
        dx_ref[...] = dx.astype(dx_ref.dtype)
        _sum_rows_into(acc_ref, dy * n)

        @pl.when(i == n_steps - 1)
        def _():
            dg_ref[...] = jnp.sum(acc_ref[...], axis=0, keepdims=True)

    return body


def _norms_bwd(du, h, gain, dh, m_below, gain_below, name):
    s, d = h.shape
    tm = min(s, 256)
    n_steps = s // tm

    def body(du_ref, h_ref, g_ref, dh_ref, m_ref, gb_ref, dx_ref, dg_ref, dm_ref, dgb_ref, acc, acc_below):
        i = pl.program_id(0)

        @pl.when(i == 0)
        def _():
            acc[...] = jnp.zeros_like(acc)
            acc_below[...] = jnp.zeros_like(acc_below)

        def through(x, dy, gain_row):
            rstd = _rstd(x)
            n = x * rstd
            dn = dy * gain_row
            return rstd * (dn - n * jnp.mean(dn * n, axis=-1, keepdims=True)), dy * n

        du = du_ref[...]
        dx, dgain_rows = through(h_ref[...], du, g_ref[...])
        dx = dh_ref[...] + dx
        dx_ref[...] = dx
        _sum_rows_into(acc, dgain_rows)
        dm, dgain_rows = through(m_ref[...], dx, gb_ref[...])
        dm_ref[...] = dm.astype(dm_ref.dtype)
        _sum_rows_into(acc_below, dgain_rows)

        @pl.when(i == n_steps - 1)
        def _():
            dg_ref[...] = jnp.sum(acc[...], axis=0, keepdims=True)
            dgb_ref[...] = jnp.sum(acc_below[...], axis=0, keepdims=True)

    row, gain_spec = _row_spec(tm, d), _gain_spec(d)
    return pl.pallas_call(
        body, grid=(n_steps,), in_specs=[row, row, gain_spec, row, row, gain_spec],
        out_specs=[row, gain_spec, row, gain_spec],
        out_shape=[jax.ShapeDtypeStruct((s, d), F32), jax.ShapeDtypeStruct((1, d), F32),
                   jax.ShapeDtypeStruct((s, d), BF16), jax.ShapeDtypeStruct((1, d), F32)],
        scratch_shapes=[pltpu.VMEM((SUBLANES, d), F32), pltpu.VMEM((SUBLANES, d), F32)],
        compiler_params=_params("arbitrary"), name=name,
    )(du, h, gain, dh, m_below, gain_below)


def _pre_norm_bwd(du, h, gain, dh, name):
    s, d = h.shape
    tm = min(s, 512)
    n_steps = s // tm
    return pl.pallas_call(
        _norm_bwd_body(n_steps, True), grid=(n_steps,),
        in_specs=[_row_spec(tm, d), _row_spec(tm, d), _gain_spec(d), _row_spec(tm, d)],
        out_specs=[_row_spec(tm, d), _gain_spec(d)],
        out_shape=[jax.ShapeDtypeStruct((s, d), F32), jax.ShapeDtypeStruct((1, d), F32)],
        scratch_shapes=[pltpu.VMEM((SUBLANES, d), F32)], compiler_params=_params("arbitrary"), name=name,
    )(du, h, gain, dh)


def _shift_down(p, halo, row, n):
    out = jnp.where(row == 0, halo[SUBLANES - n:SUBLANES - n + 1], pltpu.roll(p, n, 0))
    if n == 2:
        out = jnp.where(row == 1, halo[SUBLANES - 1:SUBLANES], out)
    return out


def _shift_up(p, halo, row, n):
    tm = p.shape[0]
    out = jnp.where(row == tm - 1, halo[n - 1:n], pltpu.roll(p, tm - n, 0))
    if n == 2:
        out = jnp.where(row == tm - 2, halo[0:1], out)
    return out


def _conv_specs(tm, tc, nb, n_row_blocks):
    hb = tm // SUBLANES
    cur = lambda part: pl.BlockSpec((tm, tc), lambda i, j: (i, part * nb + j))
    prev = lambda part: pl.BlockSpec((SUBLANES, tc), lambda i, j: (jnp.maximum(i * hb - 1, 0), part * nb + j))
    nxt = lambda part: pl.BlockSpec(
        (SUBLANES, tc), lambda i, j: (jnp.minimum((i + 1) * hb, n_row_blocks * hb - 1), part * nb + j))
    return cur, prev, nxt


def _conv_gate_fwd(proj, conv_w, name):
    s, b4 = proj.shape
    bdim = b4 // 4
    tm, tc = min(s, 512), min(bdim, 512)
    nb = bdim // tc
    cur, prev, _ = _conv_specs(tm, tc, nb, s // tm)

    def body(b_ref, c_ref, x_ref, z_ref, cp_ref, xp_ref, w_ref, a_ref, at_ref):
        i = pl.program_id(0)
        row = lax.broadcasted_iota(jnp.int32, (tm, tc), 0)
        p = c_ref[...] * x_ref[...]
        halo = jnp.where(i > 0, cp_ref[...] * xp_ref[...], 0.0)
        w = w_ref[...]
        cv = w[0:1] * _shift_down(p, halo, row, 2) + w[1:2] * _shift_down(p, halo, row, 1) + w[2:3] * p
        silu, _ = _silu_parts(z_ref[...])
        a = (silu * (b_ref[...] * cv)).astype(a_ref.dtype)
        a_ref[...] = a
        at_ref[...] = a.T

    return pl.pallas_call(
        body, grid=(s // tm, nb),
        in_specs=[cur(0), cur(1), cur(2), cur(3), prev(1), prev(2), pl.BlockSpec((CONV_K, tc), lambda i, j: (0, j))],
        out_specs=[pl.BlockSpec((tm, tc), lambda i, j: (i, j)), pl.BlockSpec((tc, tm), lambda i, j: (j, i))],
        out_shape=[jax.ShapeDtypeStruct((s, bdim), BF16), jax.ShapeDtypeStruct((bdim, s), BF16)],
        compiler_params=_params("parallel", "parallel"), name=name,
    )(proj, proj, proj, proj, proj, proj, conv_w)


def _conv_gate_bwd(proj, da, conv_w, name):
    s, b4 = proj.shape
    bdim = b4 // 4
    tm, tc = min(s, 128), bdim
    nb = bdim // tc
    n_rows = s // tm
    cur, prev, nxt = _conv_specs(tm, tc, nb, n_rows)
    da_cur = pl.BlockSpec((tm, tc), lambda j, i: (i, j))
    hb = tm // SUBLANES
    da_nxt = pl.BlockSpec((SUBLANES, tc), lambda j, i: (jnp.minimum((i + 1) * hb, n_rows * hb - 1), j))
    swap = lambda spec: pl.BlockSpec(spec.block_shape, lambda j, i, f=spec.index_map: f(i, j))

    def body(b_ref, c_ref, x_ref, z_ref, cp_ref, xp_ref, bn_ref, zn_ref, da_ref, dan_ref, w_ref,
             dproj_ref, dw_ref, acc_ref):
        i = pl.program_id(1)

        @pl.when(i == 0)
        def _():
            acc_ref[...] = jnp.zeros_like(acc_ref)

        row = lax.broadcasted_iota(jnp.int32, (tm, tc), 0)
        w = w_ref[...]
        b, c, x = b_ref[...], c_ref[...], x_ref[...]
        p = c * x
        halo_p = jnp.where(i > 0, cp_ref[...] * xp_ref[...], 0.0)
        p1, p2 = _shift_down(p, halo_p, row, 1), _shift_down(p, halo_p, row, 2)
        cv = w[0:1] * p2 + w[1:2] * p1 + w[2:3] * p
        z = z_ref[...]
        silu, sig = _silu_parts(z)
        da = da_ref[...]
        dy = da * silu
        dcv = dy * b
        silu_n, _ = _silu_parts(zn_ref[...])
        halo_d = jnp.where(i < n_rows - 1, dan_ref[...] * silu_n * bn_ref[...], 0.0)
        dp = w[2:3] * dcv + w[1:2] * _shift_up(dcv, halo_d, row, 1) + w[0:1] * _shift_up(dcv, halo_d, row, 2)
        gates = (dy * cv, dp * x, dp * c, da * (b * cv) * (sig * (1.0 + z * (1.0 - sig))))
        for part, dgate in enumerate(gates):
            dproj_ref[:, part * bdim:(part + 1) * bdim] = dgate.astype(dproj_ref.dtype)
        for k, pk in enumerate((p2, p1, p)):
            _sum_rows_into(acc_ref.at[k], dcv * pk)

        @pl.when(i == n_rows - 1)
        def _():
            for k in range(CONV_K):
                dw_ref[k:k + 1, :] = jnp.sum(acc_ref[k], axis=0, keepdims=True)

    out = pl.BlockSpec((tm, b4), lambda j, i: (i, 0))
    return pl.pallas_call(
        body, grid=(nb, n_rows),
        in_specs=[swap(cur(0)), swap(cur(1)), swap(cur(2)), swap(cur(3)), swap(prev(1)), swap(prev(2)),
                  swap(nxt(0)), swap(nxt(3)), da_cur, da_nxt, pl.BlockSpec((CONV_K, tc), lambda j, i: (0, j))],
        out_specs=[out, pl.BlockSpec((CONV_K, tc), lambda j, i: (0, j))],
        out_shape=[jax.ShapeDtypeStruct((s, b4), BF16), jax.ShapeDtypeStruct((CONV_K, bdim), F32)],
        scratch_shapes=[pltpu.VMEM((CONV_K, SUBLANES, tc), F32)],
        compiler_params=_params("parallel", "arbitrary"), name=name,
    )(proj, proj, proj, proj, proj, proj, proj, proj, da, da, conv_w)


def _split(x):
    hi = x.astype(BF16)
    lo = (x - hi.astype(F32)).astype(BF16)
    return jnp.concatenate([hi, lo], axis=1)


def _row_total(x, column):
    return jnp.broadcast_to(x[:, column:column + 1], (x.shape[0], LANES))


def _sb_tiles(qs, ks, carries, suffix_ones, masks, chain=0):
    items = range(len(qs))
    bk = ks[0].shape[0]
    scale = 1.0 / math.sqrt(HEAD_DIM)
    logits = [lax.dot_general(qs[n], ks[n], NT, preferred_element_type=F32) * scale for n in items]
    es = [jnp.exp(-jnp.abs(logits[n])) for n in items]
    keeps = []
    for n in items:
        log_keep = -(jnp.maximum(logits[n], 0.0) + jnp.log(1.0 + es[n]))
        if masks[n] is not None:
            log_keep = jnp.where(masks[n], log_keep, 0.0)
        keeps.append(_split(log_keep))
    tails = [lax.dot_general(keeps[n], suffix_ones, NN, preferred_element_type=F32) for n in items]
    ws, used = [], []
    for n in items:
        carry = carries[n] if n < len(carries) else used[n - chain] + _row_total(tails[n - chain], 0)
        used.append(carry)
        w = jnp.exp(logits[n] + tails[n] + (carry if carry.shape[1] == 1 else _lane_tile(carry, bk)))
        if masks[n] is not None:
            w = jnp.where(masks[n], w, 0.0)
        ws.append(w)
    return logits, es, tails, ws, used


def _tri_twice(n, upper):
    r = lax.broadcasted_iota(jnp.int32, (2 * n, n), 0)
    r = jnp.where(r >= n, r - n, r)
    c = lax.broadcasted_iota(jnp.int32, (2 * n, n), 1)
    return jnp.where(r <= c if upper else r >= c, 1.0, 0.0).astype(BF16)


def _group_spec(s, width, part, n_groups):
    return pl.BlockSpec((s, width), lambda h: (0, part * n_groups + h))


def _head_cols(g):
    return slice(g * HEAD_DIM, (g + 1) * HEAD_DIM)


def _lane_tile(x, n):
    return x if n == LANES else jnp.concatenate([x] * (n // LANES), axis=1)


def _sb_attn_fwd(qkv, name):
    s, b3 = qkv.shape
    bdim = b3 // 3
    hps = min(HEADS_PER_STEP, bdim // HEAD_DIM)
    width = hps * HEAD_DIM
    n_groups = bdim // width
    blk = min(s, 256)
    n_blk = s // blk

    def body(q_ref, k_ref, v_ref, o_ref, car_ref, carry_ref):
        suffix_ones = _tri_twice(blk, upper=False)
        r = lax.broadcasted_iota(jnp.int32, (blk, blk), 0)
        c = lax.broadcasted_iota(jnp.int32, (blk, blk), 1)
        diag_mask = c < r
        lane = lax.broadcasted_iota(jnp.int32, (blk, LANES), 1)

        def q_block(qi, _):
            q0 = pl.multiple_of(qi * blk, blk)
            rows = pl.ds(q0, blk)
            qs = [q_ref[rows, _head_cols(g)] for g in range(hps)]
            o_ref[rows, :] = jnp.zeros((blk, width), F32)
            car_ref[rows, :] = jnp.full((blk, width), UNVISITED, F32)
            carry_ref[...] = jnp.zeros_like(carry_ref)

            def step(js, tile_masks):
                k0s = [pl.multiple_of(j * blk, blk) for j in js]
                items = [(t, g) for t in range(len(js)) for g in range(hps)]
                ks = [k_ref[pl.ds(k0s[t], blk), _head_cols(g)] for t, g in items]
                first = [carry_ref[g] for g in range(hps)]
                _, _, tails, ws, carries = _sb_tiles([qs[g] for _, g in items], ks, first, suffix_ones,
                                                     [tile_masks[t] for t, _ in items], chain=hps)
                for g in range(hps):
                    mine = [n for n, (_, h) in enumerate(items) if h == g]
                    acc, saved = None, car_ref[rows, _head_cols(g)]
                    for n in mine:
                        v = v_ref[pl.ds(k0s[items[n][0]], blk), _head_cols(g)]
                        p = lax.dot_general(ws[n].astype(BF16), v, NN, preferred_element_type=F32)
                        acc = p if acc is None else acc + p
                        saved = jnp.where(lane == js[items[n][0]], carries[n], saved)
                    o_ref[rows, _head_cols(g)] += acc
                    car_ref[rows, _head_cols(g)] = saved
                    carry_ref[g] = carries[mine[-1]] + _row_total(tails[mine[-1]], 0)

            @pl.when(qi == 0)
            def _():
                step([0], [diag_mask])

            @pl.when(qi > 0)
            def _():
                step([qi, qi - 1], [diag_mask, None])

            def alive():
                top = jnp.max(jnp.max(carry_ref[...], axis=0), axis=0, keepdims=True)
                return (jnp.max(top, axis=1, keepdims=True)[0, 0] >= DEAD_CARRY).astype(jnp.int32)

            left = jnp.maximum(qi - 1, 0)

            def pair(state):
                p, _ = state
                j = qi - 2 - 2 * p
                step([j, j - 1], [None, None])
                return p + 1, alive()

            p, live = lax.while_loop(lambda state: (state[0] < left // 2) & (state[1] > 0), pair, (0, alive()))

            @pl.when((left % 2 == 1) & (p == left // 2) & (live > 0))
            def _():
                step([0], [None])

            return 0

        lax.fori_loop(0, n_blk, q_block, 0)

    out = pl.BlockSpec((s, width), lambda h: (0, h))
    shape = jax.ShapeDtypeStruct((s, bdim), F32)
    return pl.pallas_call(
        body, grid=(n_groups,),
        in_specs=[_group_spec(s, width, part, n_groups) for part in range(3)],
        out_specs=[out, out], out_shape=[shape, shape], scratch_shapes=[pltpu.VMEM((hps, blk, LANES), F32)],
        compiler_params=_params("parallel"), name=name,
    )(qkv, qkv, qkv)


def _sb_attn_bwd(qkv, do, carries, dproj, name):
    s, b3 = qkv.shape
    bdim = b3 // 3
    hps = min(HEADS_PER_STEP, bdim // HEAD_DIM)
    width = hps * HEAD_DIM
    n_groups = bdim // width
    blk = min(s, 256)
    n_blk = s // blk
    scale = 1.0 / math.sqrt(HEAD_DIM)

    def sweep(q_ref, k_ref, v_ref, do_ref, car_ref, dq_ref, dk_acc, dv_acc, dq_acc, before_ref):
        suffix_ones = _tri_twice(blk, upper=False)
        prefix_ones = _tri_twice(blk, upper=True)
        r = lax.broadcasted_iota(jnp.int32, (blk, blk), 0)
        c = lax.broadcasted_iota(jnp.int32, (blk, blk), 1)
        diag_mask = c < r
        lane = lax.broadcasted_iota(jnp.int32, (blk, LANES), 1)
        dk_acc[...] = jnp.zeros_like(dk_acc)
        dv_acc[...] = jnp.zeros_like(dv_acc)

        def q_block(qi, _):
            q0 = pl.multiple_of(qi * blk, blk)
            rows = pl.ds(q0, blk)
            qs = [q_ref[rows, _head_cols(g)] for g in range(hps)]
            dos = [do_ref[rows, _head_cols(g)] for g in range(hps)]
            dq_acc[...] = jnp.zeros_like(dq_acc)
            before_ref[...] = jnp.zeros_like(before_ref)

            def step(js, tile_masks):
                masks = [tile_masks[t] for t in range(len(js)) for _ in range(hps)]
                k0s = [pl.multiple_of(j * blk, blk) for j in js]
                items = [(t, g) for t in range(len(js)) for g in range(hps)]
                every = range(len(items))
                ks = [k_ref[pl.ds(k0s[t], blk), _head_cols(g)] for t, g in items]
                dws = [lax.dot_general(dos[g], v_ref[pl.ds(k0s[t], blk), _head_cols(g)], NT, preferred_element_type=F32)
                       for t, g in items]
                carries = [jnp.sum(jnp.where(lane == js[t], car_ref[rows, _head_cols(g)], 0.0), axis=1, keepdims=True)
                           for t, g in items]
                logits, es, _, ws, _ = _sb_tiles([qs[g] for _, g in items], ks, carries, suffix_ones, masks)
                gws = [dws[n] * ws[n] for n in every]
                g_upto = [lax.dot_general(_split(gws[n]), prefix_ones, NN, preferred_element_type=F32) for n in every]
                dss, befores = [], []
                for n, (t, g) in enumerate(items):
                    before = before_ref[g] if t == 0 else befores[n - hps] + _row_total(g_upto[n - hps], blk - 1)
                    befores.append(before)
                    sig = jnp.where(logits[n] >= 0.0, 1.0, es[n]) / (1.0 + es[n])
                    dlogits = gws[n] - sig * (_lane_tile(before, blk) + g_upto[n])
                    if masks[n] is not None:
                        dlogits = jnp.where(masks[n], dlogits, 0.0)
                    dss.append((dlogits * scale).astype(BF16))
                for g in range(hps):
                    mine = [n for n in every if items[n][1] == g]
                    dq = None
                    for n in mine:
                        k0 = k0s[items[n][0]]
                        p = lax.dot_general(dss[n], ks[n], NN, preferred_element_type=F32)
                        dq = p if dq is None else dq + p
                        dk_acc[pl.ds(k0, blk), _head_cols(g)] += lax.dot_general(
                            dss[n], qs[g], TN, preferred_element_type=F32)
                        dv_acc[pl.ds(k0, blk), _head_cols(g)] += lax.dot_general(
                            ws[n].astype(BF16), dos[g], TN, preferred_element_type=F32)
                    dq_acc[:, _head_cols(g)] += dq
                    before_ref[g] = befores[mine[-1]] + _row_total(g_upto[mine[-1]], blk - 1)

            top = car_ref[rows, _head_cols(0)]
            for g in range(1, hps):
                top = jnp.maximum(top, car_ref[rows, _head_cols(g)])
            top = jnp.max(top, axis=0, keepdims=True)
            lane_row = lax.broadcasted_iota(jnp.int32, (1, LANES), 1)
            counted = jnp.where((top >= DEAD_CARRY) & (lane_row < qi), 1.0, 0.0)
            n_alive = jnp.sum(counted, axis=1, keepdims=True)[0, 0].astype(jnp.int32)
            left = jnp.maximum(n_alive - 1, 0)
            start = qi - 1 - left

            @pl.when(left % 2 == 1)
            def _():
                step([start], [None])

            def pair(p, _):
                j = start + left % 2 + 2 * p
                step([j, j + 1], [None, None])
                return 0

            lax.fori_loop(0, left // 2, pair, 0)

            @pl.when(qi == 0)
            def _():
                step([0], [diag_mask])

            @pl.when(qi > 0)
            def _():
                step([qi - 1, qi], [None, diag_mask])
            dq_ref[rows, :] = dq_acc[...].astype(dq_ref.dtype)
            return 0

        lax.fori_loop(0, n_blk, q_block, 0)

    def body(q_ref, k_ref, v_ref, do_ref, car_ref, dproj_ref, out_ref, dk_acc, dv_acc, dq_acc, before_ref):
        part = pl.program_id(1)

        @pl.when(part == 0)
        def _():
            sweep(q_ref, k_ref, v_ref, do_ref, car_ref, out_ref, dk_acc, dv_acc, dq_acc, before_ref)

        @pl.when(part == 1)
        def _():
            out_ref[...] = dk_acc[...].astype(out_ref.dtype)

        @pl.when(part == 2)
        def _():
            out_ref[...] = dv_acc[...].astype(out_ref.dtype)

    qkv_spec = lambda which: pl.BlockSpec((s, width), lambda g, p: (0, which * n_groups + g))
    once = pl.BlockSpec((s, width), lambda g, p: (0, g), pipeline_mode=pl.Buffered(1))
    return pl.pallas_call(
        body, grid=(n_groups, 3),
        in_specs=[qkv_spec(0), qkv_spec(1), qkv_spec(2), once, once, ANY],
        out_specs=pl.BlockSpec((s, width), lambda g, p: (0, p * n_groups + g)),
        out_shape=jax.ShapeDtypeStruct(dproj.shape, dproj.dtype), input_output_aliases={5: 0},
        scratch_shapes=[pltpu.VMEM((s, width), F32), pltpu.VMEM((s, width), F32), pltpu.VMEM((blk, width), F32),
                        pltpu.VMEM((hps, blk, LANES), F32)],
        compiler_params=_params("parallel", "arbitrary"), name=name,
    )(qkv, qkv, qkv, do, carries, dproj)


def _sb_gate_fwd(z, o, name):
    s, bdim = z.shape
    tm = min(s, 512)

    def body(z_ref, o_ref, a_ref, at_ref):
        silu, _ = _silu_parts(z_ref[...])
        a = (silu * o_ref[...]).astype(a_ref.dtype)
        a_ref[...] = a
        at_ref[...] = a.T

    return pl.pallas_call(
        body, grid=(s // tm,), in_specs=[_row_spec(tm, bdim), _row_spec(tm, bdim)],
        out_specs=[_row_spec(tm, bdim), pl.BlockSpec((bdim, tm), lambda i: (0, i))],
        out_shape=[jax.ShapeDtypeStruct((s, bdim), BF16), jax.ShapeDtypeStruct((bdim, s), BF16)],
        compiler_params=_params("parallel"), name=name,
    )(z, o)


def _sb_gate_bwd(da, z, o, name):
    s, bdim = z.shape
    tm = min(s, 512)

    def body(da_ref, z_ref, o_ref, do_ref, dz_ref):
        z = z_ref[...]
        da = da_ref[...]
        silu, sig = _silu_parts(z)
        do_ref[...] = (da * silu).astype(do_ref.dtype)
        dz_ref[...] = (da * o_ref[...] * (sig * (1.0 + z * (1.0 - sig)))).astype(dz_ref.dtype)

    spec = _row_spec(tm, bdim)
    return pl.pallas_call(
        body, grid=(s // tm,), in_specs=[spec, spec, spec],
        out_specs=[spec, pl.BlockSpec((tm, bdim), lambda i: (i, 3))],
        out_shape=[jax.ShapeDtypeStruct((s, bdim), BF16), jax.ShapeDtypeStruct((s, 4 * bdim), BF16)],
        compiler_params=_params("parallel"), name=name,
    )(da, z, o)


def _into_slot(block, place, dtype, name):
    r, c = block.shape
    tr = min(r, 256)

    def body(place_ref, b_ref, o_ref):
        o_ref[...] = b_ref[...].astype(o_ref.dtype)

    grid_spec = pltpu.PrefetchScalarGridSpec(
        num_scalar_prefetch=1, grid=(r // tr,),
        in_specs=[pl.BlockSpec((tr, c), lambda i, place_ref: (i, 0))],
        out_specs=pl.BlockSpec((None, tr, c), lambda i, place_ref: (place_ref[0], i, 0)),
    )
    return pl.pallas_call(
        body, grid_spec=grid_spec, out_shape=jax.ShapeDtypeStruct((N_DEV, r, c), dtype),
        compiler_params=_params("parallel"), name=name,
    )(place, block)


def _add_core_pair(grads, received, core, name):
    _, _, r, c = grads.shape
    tr = min(r, 1024)

    def body(core_ref, g_ref, r_ref, o_ref):
        o_ref[...] = (g_ref[...].astype(F32) + r_ref[...].astype(F32)).astype(o_ref.dtype)

    grid_spec = pltpu.PrefetchScalarGridSpec(
        num_scalar_prefetch=1, grid=(N_CHIP, r // tr),
        in_specs=[pl.BlockSpec((None, None, tr, c), lambda q, i, core_ref: (q, core_ref[0], i, 0)),
                  pl.BlockSpec((None, tr, c), lambda q, i, core_ref: (q, i, 0))],
        out_specs=pl.BlockSpec((None, tr, c), lambda q, i, core_ref: (q, i, 0)),
    )
    return pl.pallas_call(
        body, grid_spec=grid_spec, out_shape=jax.ShapeDtypeStruct((N_CHIP, r, c), BF16),
        compiler_params=_params("parallel", "parallel"), name=name,
    )(core, grads, received)


def _adamw_step(g, w, m, v, g_ref, d_ref, nm_ref, nv_ref):
    new_m = ADAM_B1 * m + (1.0 - ADAM_B1) * g
    new_v = ADAM_B2 * v + (1.0 - ADAM_B2) * (g * g)
    m_hat = new_m / (1.0 - ADAM_B1 ** ADAM_STEP)
    v_hat = new_v / (1.0 - ADAM_B2 ** ADAM_STEP)
    g_ref[...] = g
    d_ref[...] = -ADAM_LR * (m_hat / (jnp.sqrt(v_hat) + ADAM_EPS) + ADAM_WD * w)
    nm_ref[...] = new_m
    nv_ref[...] = new_v


def _adamw(w, parts, m, v, name):
    r, c = w.shape
    n_parts = parts.shape[0]
    tr = min(r, 256)

    def body(w_ref, p_ref, m_ref, v_ref, *out_refs):
        g = p_ref[0].astype(F32)
        for k in range(1, n_parts):
            g = g + p_ref[k].astype(F32)
        _adamw_step(g, w_ref[...], m_ref[...], v_ref[...], *out_refs)

    spec = pl.BlockSpec((tr, c), lambda i: (i, 0))
    shape = jax.ShapeDtypeStruct((r, c), F32)
    return pl.pallas_call(
        body, grid=(r // tr,), in_specs=[spec, pl.BlockSpec((n_parts, tr, c), lambda i: (0, i, 0)), spec, spec],
        out_specs=[spec] * 4, out_shape=[shape] * 4, compiler_params=_params("parallel"), name=name,
    )(w, parts, m, v)


def _adamw_shard(w, grads, landed, m, v, place, name):
    r, c = w.shape
    n_landed = landed.shape[0]
    tr = min(r, 512)

    def body(place_ref, w_ref, own_ref, l_ref, m_ref, v_ref, *out_refs):
        g = own_ref[...].astype(F32)
        for k in range(n_landed):
            g = g + l_ref[k].astype(F32)
        _adamw_step(g, w_ref[...], m_ref[...], v_ref[...], *out_refs)

    spec = pl.BlockSpec((tr, c), lambda i, place_ref: (i, 0))
    grid_spec = pltpu.PrefetchScalarGridSpec(
        num_scalar_prefetch=1, grid=(r // tr,),
        in_specs=[spec, pl.BlockSpec((None, tr, c), lambda i, place_ref: (place_ref[0], i, 0)),
                  pl.BlockSpec((n_landed, tr, c), lambda i, place_ref: (0, i, 0)), spec, spec],
        out_specs=[spec] * 4,
    )
    return pl.pallas_call(
        body, grid_spec=grid_spec, out_shape=[jax.ShapeDtypeStruct((r, c), F32)] * 4,
        compiler_params=_params("parallel"), name=name,
    )(place, w, grads, landed, m, v)


def _place():
    x, y, c = lax.axis_index("x"), lax.axis_index("y"), lax.axis_index("c")
    other_chips = [(1 - x, y), (x, 1 - y), (1 - x, 1 - y)]
    return x, y, c, other_chips


def _all_gather(blocks, name):
    n_arr = len(blocks)
    items = [(a, i) for a, blk in enumerate(blocks) for i in range(blk.shape[0])]
    n_items = len(items)

    def body(*refs):
        srcs, outs = refs[:n_arr], refs[n_arr:2 * n_arr]
        send_sems, recv_sems, local_sems = refs[2 * n_arr:]
        x, y, c, other_chips = _place()
        me, sibling = (x, y, c), (x, y, 1 - c)

        def slot(it, dev):
            a, i = items[it]
            return outs[a].at[i, 4 * dev[0] + 2 * dev[1] + dev[2]]

        def copy(it, k, block_of, to, from_src=False):
            a, i = items[it]
            return pltpu.make_async_remote_copy(
                src_ref=srcs[a].at[i] if from_src else slot(it, block_of), dst_ref=slot(it, block_of),
                send_sem=send_sems.at[it * 7 + k], recv_sem=recv_sems.at[it * 7 + k],
                device_id=to, device_id_type=MESH)

        own = [pltpu.make_async_copy(srcs[items[it][0]].at[items[it][1]], slot(it, me), local_sems.at[it])
               for it in range(n_items)]
        for cp in own:
            cp.start()
        first = []
        for it in range(n_items):
            first.append(copy(it, 0, me, sibling, from_src=True))
            first += [copy(it, 1 + j, me, (*chip, c), from_src=True) for j, chip in enumerate(other_chips)]
        for cp in first:
            cp.start()
        passed = []
        for it in range(n_items):
            for j, chip in enumerate(other_chips):
                copy(it, 1 + j, (*chip, c), me).wait_recv()
                passed.append(copy(it, 4 + j, (*chip, c), sibling))
                passed[-1].start()
        for it in range(n_items):
            copy(it, 0, sibling, me).wait_recv()
            for j, chip in enumerate(other_chips):
                copy(it, 4 + j, (*chip, 1 - c), me).wait_recv()
        for cp in first + passed:
            cp.wait_send()
        for cp in own:
            cp.wait()

    return pl.pallas_call(
        body, in_specs=[ANY] * n_arr, out_specs=[ANY] * n_arr,
        out_shape=[jax.ShapeDtypeStruct((b.shape[0], N_DEV) + b.shape[1:], b.dtype) for b in blocks],
        scratch_shapes=[pltpu.SemaphoreType.DMA((7 * n_items,)), pltpu.SemaphoreType.DMA((7 * n_items,)),
                        pltpu.SemaphoreType.DMA((n_items,))],
        name=name,
    )(*blocks)


def _exchange_core_pair(grads, name):
    n_arr = len(grads)

    def body(*refs):
        srcs, outs = refs[:n_arr], refs[n_arr:2 * n_arr]
        send_sems, recv_sems = refs[2 * n_arr:]
        x, y, c, _ = _place()
        copies = [
            pltpu.make_async_remote_copy(
                src_ref=srcs[a].at[q, 1 - c], dst_ref=outs[a].at[q],
                send_sem=send_sems.at[a * N_CHIP + q], recv_sem=recv_sems.at[a * N_CHIP + q],
                device_id=(x, y, 1 - c), device_id_type=MESH)
            for a in range(n_arr) for q in range(N_CHIP)]
        for cp in copies:
            cp.start()
        for cp in copies:
            cp.wait_recv()
        for cp in copies:
            cp.wait_send()

    return pl.pallas_call(
        body, in_specs=[ANY] * n_arr, out_specs=[ANY] * n_arr,
        out_shape=[jax.ShapeDtypeStruct((N_CHIP,) + g.shape[2:], g.dtype) for g in grads],
        scratch_shapes=[pltpu.SemaphoreType.DMA((N_CHIP * n_arr,)), pltpu.SemaphoreType.DMA((N_CHIP * n_arr,))],
        name=name,
    )(*grads)


CHIPS_SCATTER, CHIPS_GATHER, PAIR_GATHER = "chips_scatter", "chips_gather", "pair_gather"


def _stage_copies(stage, srcs, outs, send_sems, recv_sems):
    x, y, c, other_chips = _place()
    my_chip = 2 * x + y
    copies = []
    for a in range(len(outs)):
        if stage == PAIR_GATHER:
            moves = [(outs[a].at[q, c], outs[a].at[q, c], (x, y, 1 - c)) for q in range(N_CHIP)]
        elif stage == CHIPS_GATHER:
            moves = [(outs[a].at[my_chip, c], outs[a].at[my_chip, c], (*chip, c)) for chip in other_chips]
        else:
            moves = [(srcs[a].at[2 * chip[0] + chip[1]], outs[a].at[j], (*chip, c)) for j, chip in enumerate(other_chips)]
        for k, (src, dst, peer) in enumerate(moves):
            copies.append(pltpu.make_async_remote_copy(
                src_ref=src, dst_ref=dst, send_sem=send_sems.at[a * N_CHIP + k], recv_sem=recv_sems.at[a * N_CHIP + k],
                device_id=peer, device_id_type=MESH))
    return copies


def _wait_all(copies):
    for cp in copies:
        cp.wait_recv()
    for cp in copies:
        cp.wait_send()


def kernel(x, ln_pre_0, conv_w_in_0, conv_w_0, conv_w_out_0, ln_post_0, ln_pre_1, sb_w_in_1, sb_w_out_1, ln_post_1, ln_pre_2, conv_w_in_2, conv_w_2, conv_w_out_2, ln_post_2, ln_pre_3, sb_w_in_3, sb_w_out_3, ln_post_3, loss_target, m_ln_pre_0, m_conv_w_in_0, m_conv_w_0, m_conv_w_out_0, m_ln_post_0, m_ln_pre_1, m_sb_w_in_1, m_sb_w_out_1, m_ln_post_1, m_ln_pre_2, m_conv_w_in_2, m_conv_w_2, m_conv_w_out_2, m_ln_post_2, m_ln_pre_3, m_sb_w_in_3, m_sb_w_out_3, m_ln_post_3, v_ln_pre_0, v_conv_w_in_0, v_conv_w_0, v_conv_w_out_0, v_ln_post_0, v_ln_pre_1, v_sb_w_in_1, v_sb_w_out_1, v_ln_post_1, v_ln_pre_2, v_conv_w_in_2, v_conv_w_2, v_conv_w_out_2, v_ln_post_2, v_ln_pre_3, v_sb_w_in_3, v_sb_w_out_3, v_ln_post_3):
    names = ['ln_pre_0', 'conv_w_in_0', 'conv_w_0', 'conv_w_out_0', 'ln_post_0', 'ln_pre_1', 'sb_w_in_1', 'sb_w_out_1',
             'ln_post_1', 'ln_pre_2', 'conv_w_in_2', 'conv_w_2', 'conv_w_out_2', 'ln_post_2', 'ln_pre_3', 'sb_w_in_3',
             'sb_w_out_3', 'ln_post_3']
    given = dict(locals())
    w = {n: given[n] for n in names}
    mom = {n: given["m_" + n] for n in names}
    var = {n: given["v_" + n] for n in names}
    conv_layers = [i for i in range(DEPTH) if i % 2 == 0]
    w_in_names = [("conv_w_in_%d" if i % 2 == 0 else "sb_w_in_%d") % i for i in range(DEPTH)]
    w_out_names = [("conv_w_out_%d" if i % 2 == 0 else "sb_w_out_%d") % i for i in range(DEPTH)]

    s, d = x.shape[1:]
    h = x.reshape(s, d)
    target = loss_target.reshape(s, d)
    gains = {n: w[n].reshape(1, d) for n in names if n.startswith("ln_")}
    place = 4 * lax.axis_index("x") + 2 * lax.axis_index("y") + lax.axis_index("c")
    place_arr = place.astype(jnp.int32).reshape(1)
    bdim = w[w_out_names[0]].shape[0] * N_DEV
    wc = bdim // N_DEV

    conv_rows = jnp.concatenate([w["conv_w_%d" % i] for i in conv_layers], axis=0)
    first = _all_gather([_cast(w[n], BF16, "cast_" + n)[None] for n in (w_in_names[0], w_out_names[0])] + [conv_rows[None]],
                        "gather_first_layer")
    slots = {n: _into_slot(w[n], place_arr, BF16, "slot_" + n) for n in w_in_names[1:] + w_out_names[1:]}
    slots = {n: a.reshape((N_CHIP, 2) + a.shape[1:]) for n, a in slots.items()}
    conv_all = first[2][0].reshape(N_DEV, len(conv_layers), CONV_K, wc)
    conv_all = jnp.transpose(conv_all, (1, 2, 0, 3)).reshape(len(conv_layers), CONV_K, bdim)
    conv_full = {layer: conv_all[n] for n, layer in enumerate(conv_layers)}
    weights = [(first[0][0], first[1][0].reshape(bdim, d))]

    saved = []
    u, u_t = _rmsnorm_fwd(h, gains["ln_pre_0"], "pre_norm_0")
    for i in range(DEPTH):
        w_in, w_out = weights[i]
        more = i + 1 < DEPTH
        nxt_in = ([slots[w_in_names[i + 1]]], CHIPS_GATHER) if more else None
        nxt_out = ([slots[w_out_names[i + 1]]], CHIPS_GATHER) if more else None
        if i % 2 == 0:
            proj = _proj(u, w_in, 0, N_DEV, F32, "proj_%d" % i, hosted=nxt_in)
            if more:
                proj, crossed_in = proj
            a, a_t = _conv_gate_fwd(proj, conv_full[i], "conv_gate_%d" % i)
            extra = (proj,)
        else:
            qkv = _proj(u, w_in, 0, 6, BF16, "proj_qkv_%d" % i, hosted=nxt_in)
            if more:
                qkv, crossed_in = qkv
            z = _proj(u, w_in, 6, 2, F32, "proj_z_%d" % i)
            o, carries = _sb_attn_fwd(qkv, "sb_attn_%d" % i)
            a, a_t = _sb_gate_fwd(z, o, "sb_gate_%d" % i)
            extra = (qkv, z, o, carries)
        m = _out_proj(a, w_out, "out_proj_%d" % i, hosted=nxt_out)
        if more:
            m, crossed_out = m
        saved.append((h, u_t, a_t, m, extra))
        if more:
            (h, u, u_t), both = _post_norm_residual(
                h, m, gains["ln_post_%d" % i], gains["ln_pre_%d" % (i + 1)], "post_norm_%d" % i,
                hosted=([crossed_in[0], crossed_out[0]], PAIR_GATHER))
            weights.append((both[0].reshape((N_DEV,) + both[0].shape[2:]), both[1].reshape(bdim, d)))

    last = DEPTH - 1
    small = {}
    loss, dh, dm, small["ln_post_%d" % last] = _last_norm_and_loss(
        h, m, gains["ln_post_%d" % last], target, "last_norm_and_loss")
    loss = lax.psum(loss[0, 0], ("x", "y", "c"))

    chip_parts = {}
    core = lax.axis_index("c").astype(jnp.int32).reshape(1)
    for i in reversed(range(DEPTH)):
        h_in, u_t, a_t, m, extra = saved[i]
        w_in, w_out = weights[i]
        g_out = _weight_grad(a_t, dm, 1, "grad_w_out_%d" % i).reshape(N_CHIP, 2, wc, d)

        def pair_sum(g, kind):
            (from_sibling,) = _exchange_core_pair([g], "reduce_core_pair_%s_%d" % (kind, i))
            return _add_core_pair(g, from_sibling, core, "add_core_pair_%s_%d" % (kind, i))

        pair_out = pair_sum(g_out, "out")
        da, landed = _out_proj_bwd_act(dm, w_out, "out_proj_bwd_%d" % i, hosted=([pair_out], CHIPS_SCATTER))
        chip_parts[w_out_names[i]] = (pair_out, landed[0])
        if i % 2 == 0:
            (proj,) = extra
            dproj, small["conv_w_%d" % i] = _conv_gate_bwd(proj, da, conv_full[i], "conv_gate_bwd_%d" % i)
        else:
            qkv, z, o, carries = extra
            do, dproj = _sb_gate_bwd(da, z, o, "sb_gate_bwd_%d" % i)
            dproj = _sb_attn_bwd(qkv, do, carries, dproj, "sb_attn_bwd_%d" % i)
        g_in = _weight_grad(u_t, dproj, N_DEV, "grad_w_in_%d" % i)
        pair_in = pair_sum(g_in.reshape((N_CHIP, 2) + g_in.shape[1:]), "in")
        du, landed = _proj_bwd_act(dproj, w_in, "proj_bwd_%d" % i, hosted=([pair_in], CHIPS_SCATTER))
        chip_parts[w_in_names[i]] = (pair_in, landed[0])
        if i > 0:
            dh, small["ln_pre_%d" % i], dm, small["ln_post_%d" % (i - 1)] = _norms_bwd(
                du, h_in, gains["ln_pre_%d" % i], dh, saved[i - 1][3], gains["ln_post_%d" % (i - 1)], "norms_bwd_%d" % i)
        else:
            dh, small["ln_pre_%d" % i] = _pre_norm_bwd(du, h_in, gains["ln_pre_%d" % i], dh, "pre_norm_bwd_%d" % i)
    big_names = w_in_names + w_out_names

    gain_names = [n for n in names if n.startswith("ln_")]
    conv_names = ["conv_w_%d" % i for i in conv_layers]
    rows = [small[n] for n in gain_names] + [small[n] for n in conv_names]
    n_rows = len(gain_names) + CONV_K * len(conv_names)
    pad = -n_rows % SUBLANES
    stacked = jnp.concatenate(rows + [jnp.zeros((pad, d), F32)], axis=0)
    (small_all,) = _all_gather([stacked[None]], "gather_small_grads")
    small_all = small_all[0]

    out_g, out_d, out_m, out_v = {}, {}, {}, {}

    def update(n, w2, parts, m2, v2, shape):
        g2, d2, nm2, nv2 = _adamw(w2, parts, m2, v2, "adamw_" + n)
        out_g[n], out_d[n], out_m[n], out_v[n] = (t.reshape(shape) for t in (g2, d2, nm2, nv2))

    chip_arr = (2 * lax.axis_index("x") + lax.axis_index("y")).astype(jnp.int32).reshape(1)
    for n in big_names:
        own, landed = chip_parts[n]
        out_g[n], out_d[n], out_m[n], out_v[n] = _adamw_shard(w[n], own, landed, mom[n], var[n], chip_arr, "adamw_" + n)
    n_gain = len(gain_names)
    stack = lambda src: jnp.stack([src[n] for n in gain_names])
    g2, d2, nm2, nv2 = _adamw(stack(w), small_all[:, :n_gain], stack(mom), stack(var), "adamw_gains")
    for k, n in enumerate(gain_names):
        out_g[n], out_d[n], out_m[n], out_v[n] = g2[k], d2[k], nm2[k], nv2[k]
    wc = bdim // N_DEV
    for k, n in enumerate(conv_names):
        rows_k = small_all[:, n_gain + CONV_K * k:n_gain + CONV_K * (k + 1)]
        parts = lax.dynamic_slice_in_dim(rows_k, place * wc, wc, axis=2)
        update(n, w[n], parts, mom[n], var[n], w[n].shape)

    grad_x = dh.reshape(x.shape)
    return (loss, grad_x, *[out_g[n] for n in names], *[out_d[n] for n in names],
            *[out_m[n] for n in names], *[out_v[n] for n in names])
```

```python
import functools
import math

import jax
import jax.numpy as jnp
from jax import lax
from jax.experimental import pallas as pl
from jax.experimental.pallas import tpu as pltpu

F32 = jnp.float32
BF16 = jnp.bfloat16
MESH = pl.DeviceIdType.MESH
ANY = pl.BlockSpec(memory_space=pl.ANY)

N_DEV = 8
N_CHIP = 4
DEPTH = 4
HEAD_DIM = 128
CONV_K = 3
RMS_EPS = 1e-6
ADAM_LR = 0.001
ADAM_B1 = 0.9
ADAM_B2 = 0.999
ADAM_EPS = 1e-08
ADAM_WD = 0.01
ADAM_STEP = 10

V7X_VMEM_BYTES = 64 * 1024 * 1024
VMEM_LIMIT = V7X_VMEM_BYTES * 3 // 4
LANES = 128
SUBLANES = 8
HEADS_PER_STEP = 2
DEAD_CARRY = -128.0
UNVISITED = -1e30


def _params(*sem):
    return pltpu.CompilerParams(dimension_semantics=sem, vmem_limit_bytes=VMEM_LIMIT)


def _silu_parts(z):
    sig = jax.nn.sigmoid(z)
    return z * sig, sig


NN = (((1,), (0,)), ((), ()))
NT = (((1,), (1,)), ((), ()))
TN = (((0,), (0,)), ((), ()))


def _gridded_call(body, operands, *, grid, in_specs, out_specs, out_shape, scratch_shapes, semantics, name, hosted=None):
    if hosted is None:
        return pl.pallas_call(
            body, grid=grid, in_specs=in_specs, out_specs=out_specs, out_shape=out_shape,
            scratch_shapes=scratch_shapes, compiler_params=_params(*semantics), name=name)(*operands)
    arrays, stage = hosted
    scatter = stage == CHIPS_SCATTER
    n_in, n_out, n_ex, n_scr = len(in_specs), len(out_specs), len(arrays), len(scratch_shapes)

    def hosting_body(*refs):
        ins, refs = refs[:n_in], refs[n_in:]
        ex_in, refs = refs[:n_ex], refs[n_ex:]
        outs, refs = refs[:n_out], refs[n_out:]
        ex_out, refs = refs[:n_ex], refs[n_ex:]
        scratch, sems = refs[:n_scr], refs[n_scr:]
        first = last = None
        for axis, size in enumerate(grid):
            at_start, at_end = pl.program_id(axis) == 0, pl.program_id(axis) == size - 1
            first = at_start if first is None else first & at_start
            last = at_end if last is None else last & at_end

        @pl.when(first)
        def _():
            for cp in _stage_copies(stage, ex_in, ex_out, *sems):
                cp.start()

        body(*ins, *outs, *scratch)

        @pl.when(last)
        def _():
            _wait_all(_stage_copies(stage, ex_in, ex_out, *sems))

    if stage == PAIR_SCATTER:
        ex_shapes, aliases = [jax.ShapeDtypeStruct((N_CHIP,) + a.shape[2:], a.dtype) for a in arrays], {}
    elif scatter:
        ex_shapes, aliases = [jax.ShapeDtypeStruct((N_CHIP - 1,) + a.shape[1:], a.dtype) for a in arrays], {}
    else:
        ex_shapes, aliases = [jax.ShapeDtypeStruct(a.shape, a.dtype) for a in arrays], {n_in + a: n_out + a for a in range(n_ex)}
    out = pl.pallas_call(
        hosting_body, grid=grid, in_specs=list(in_specs) + [ANY] * n_ex, out_specs=list(out_specs) + [ANY] * n_ex,
        out_shape=list(out_shape) + ex_shapes, input_output_aliases=aliases,
        scratch_shapes=list(scratch_shapes) + [pltpu.SemaphoreType.DMA((N_CHIP * n_ex,))] * 2,
        compiler_params=_params(*["arbitrary"] * len(grid)), name=name)(*operands, *arrays)
    return out[:n_out], out[n_out:]


def _mm(a, b, *, dims, grid, a_spec, b_spec, o_spec, out_shape, acc_shape, name, hosted=None):
    nk = grid[2]

    def body(a_ref, b_ref, o_ref, *scratch):
        p = lax.dot_general(a_ref[...], b_ref[...], dims, preferred_element_type=F32)
        if nk == 1:
            o_ref[...] = p.astype(o_ref.dtype)
        else:
            acc_ref = scratch[0]
            k = pl.program_id(2)

            @pl.when(k == 0)
            def _():
                acc_ref[...] = p

            @pl.when(k > 0)
            def _():
                acc_ref[...] += p

            @pl.when(k == nk - 1)
            def _():
                o_ref[...] = acc_ref[...].astype(o_ref.dtype)

    scratch = [] if nk == 1 else [pltpu.VMEM(acc_shape, F32)]
    res = _gridded_call(
        body, (a, b), grid=grid, in_specs=[a_spec, b_spec], out_specs=[o_spec], out_shape=[out_shape],
        scratch_shapes=scratch, semantics=("parallel", "parallel", "arbitrary"), name=name, hosted=hosted)
    return res[0] if hosted is None else (res[0][0], res[1])


def _proj(u, w_in, shard0, n_shard, out_dtype, name, hosted=None):
    s, d = u.shape
    ws = w_in.shape[-1]
    tm, tn = min(s, 1024), min(ws, 1024)
    nj = ws // tn
    return _mm(
        u, w_in, dims=NN, grid=(s // tm, n_shard * nj, 1),
        a_spec=pl.BlockSpec((tm, d), lambda i, j, k: (i, 0)),
        b_spec=pl.BlockSpec((None, d, tn), lambda i, j, k: (shard0 + j // nj, 0, j % nj)),
        o_spec=pl.BlockSpec((tm, tn), lambda i, j, k: (i, j)),
        out_shape=jax.ShapeDtypeStruct((s, n_shard * ws), out_dtype), acc_shape=(tm, tn), name=name, hosted=hosted,
    )


def _out_proj(a, w_out, name, hosted=None):
    s, bdim = a.shape
    d = w_out.shape[-1]
    tm, tn = min(s, 512), min(d, 1024)
    return _mm(
        a, w_out, dims=NN, grid=(s // tm, d // tn, 1),
        a_spec=pl.BlockSpec((tm, bdim), lambda i, j, k: (i, 0)),
        b_spec=pl.BlockSpec((bdim, tn), lambda i, j, k: (0, j)),
        o_spec=pl.BlockSpec((tm, tn), lambda i, j, k: (i, j)),
        out_shape=jax.ShapeDtypeStruct((s, d), F32), acc_shape=(tm, tn), name=name, hosted=hosted,
    )


def _out_proj_bwd_act(dm, w_out, name, hosted=None):
    s, d = dm.shape
    bdim = w_out.shape[-2]
    tm, tn = min(s, 512), min(bdim, 1024)
    return _mm(
        dm, w_out, dims=NT, grid=(s // tm, bdim // tn, 1),
        a_spec=pl.BlockSpec((tm, d), lambda i, j, k: (i, 0)),
        b_spec=pl.BlockSpec((tn, d), lambda i, j, k: (j, 0)),
        o_spec=pl.BlockSpec((tm, tn), lambda i, j, k: (i, j)),
        out_shape=jax.ShapeDtypeStruct((s, bdim), F32), acc_shape=(tm, tn), name=name, hosted=hosted,
    )


def _weight_grad(act_t, dout, n_blocks, name, hosted=None):
    din, s = act_t.shape
    w = dout.shape[1] // n_blocks
    tm, tn = min(din, 512), min(w, 1024)
    nj = w // tn
    return _mm(
        act_t, dout, dims=NN, grid=(din // tm, n_blocks * nj, 1),
        a_spec=pl.BlockSpec((tm, s), lambda i, j, k: (i, 0)),
        b_spec=pl.BlockSpec((s, tn), lambda i, j, k: (0, j)),
        o_spec=pl.BlockSpec((None, tm, tn), lambda i, j, k: (j // nj, i, j % nj)),
        out_shape=jax.ShapeDtypeStruct((n_blocks, din, w), BF16), acc_shape=(tm, tn), name=name, hosted=hosted,
    )


def _proj_bwd_act(dproj, w_in, name, hosted=None):
    s = dproj.shape[0]
    n_shards, d, ws = w_in.shape
    tm, tn = min(s, 512), min(d, 512)

    def body(a_ref, b_ref, o_ref):
        acc = None
        for k in range(n_shards):
            p = lax.dot_general(a_ref[:, k * ws:(k + 1) * ws], b_ref[k], NT, preferred_element_type=F32)
            acc = p if acc is None else acc + p
        o_ref[...] = acc

    res = _gridded_call(
        body, (dproj, w_in), grid=(s // tm, d // tn),
        in_specs=[pl.BlockSpec((tm, n_shards * ws), lambda i, j: (i, 0)),
                  pl.BlockSpec((n_shards, tn, ws), lambda i, j: (0, j, 0))],
        out_specs=[pl.BlockSpec((tm, tn), lambda i, j: (i, j))], out_shape=[jax.ShapeDtypeStruct((s, d), F32)],
        scratch_shapes=[], semantics=("parallel", "parallel"), name=name, hosted=hosted)
    return res[0] if hosted is None else (res[0][0], res[1])


def _row_spec(tm, d):
    return pl.BlockSpec((tm, d), lambda i: (i, 0))


def _gain_spec(d):
    return pl.BlockSpec((1, d), lambda i: (0, 0))


def _rstd(x):
    return lax.rsqrt(jnp.mean(x * x, axis=-1, keepdims=True) + RMS_EPS)


def _rmsnorm_fwd(h, gain, name):
    s, d = h.shape
    tm = min(s, 512)

    def body(h_ref, g_ref, u_ref, ut_ref):
        x = h_ref[...]
        u = (x * _rstd(x) * g_ref[...]).astype(u_ref.dtype)
        u_ref[...] = u
        ut_ref[...] = u.T

    return pl.pallas_call(
        body, grid=(s // tm,), in_specs=[_row_spec(tm, d), _gain_spec(d)],
        out_specs=[_row_spec(tm, d), pl.BlockSpec((d, tm), lambda i: (0, i))],
        out_shape=[jax.ShapeDtypeStruct((s, d), BF16), jax.ShapeDtypeStruct((d, s), BF16)],
        compiler_params=_params("parallel"), name=name,
    )(h, gain)


def _cast(block, dtype, name):
    r, c = block.shape
    tr = min(r, 256)

    def body(b_ref, o_ref):
        o_ref[...] = b_ref[...].astype(o_ref.dtype)

    spec = pl.BlockSpec((tr, c), lambda i: (i, 0))
    return pl.pallas_call(
        body, grid=(r // tr,), in_specs=[spec], out_specs=spec, out_shape=jax.ShapeDtypeStruct((r, c), dtype),
        compiler_params=_params("parallel"), name=name,
    )(block)


def _post_norm_residual(h, m, gain, next_gain, name, hosted):
    s, d = h.shape
    tm = min(s, 512)

    def body(h_ref, m_ref, g_ref, gn_ref, o_ref, u_ref, ut_ref):
        x = m_ref[...]
        y = h_ref[...] + x * _rstd(x) * g_ref[...]
        o_ref[...] = y
        u = (y * _rstd(y) * gn_ref[...]).astype(u_ref.dtype)
        u_ref[...] = u
        ut_ref[...] = u.T

    return _gridded_call(
        body, (h, m, gain, next_gain), grid=(s // tm,),
        in_specs=[_row_spec(tm, d), _row_spec(tm, d), _gain_spec(d), _gain_spec(d)],
        out_specs=[_row_spec(tm, d), _row_spec(tm, d), pl.BlockSpec((d, tm), lambda i: (0, i))],
        out_shape=[jax.ShapeDtypeStruct((s, d), F32), jax.ShapeDtypeStruct((s, d), BF16), jax.ShapeDtypeStruct((d, s), BF16)],
        scratch_shapes=[], semantics=("parallel",), name=name, hosted=hosted)


def _last_norm_and_loss(h, m, gain, target, name):
    s, d = h.shape
    tm = min(s, 256)
    n_steps = s // tm

    def body(h_ref, m_ref, g_ref, t_ref, loss_ref, dy_ref, dm_ref, dg_ref, loss_acc, dg_acc):
        i = pl.program_id(0)

        @pl.when(i == 0)
        def _():
            loss_acc[...] = jnp.zeros_like(loss_acc)
            dg_acc[...] = jnp.zeros_like(dg_acc)

        x = m_ref[...]
        rstd = _rstd(x)
        n = x * rstd
        err = h_ref[...] + n * g_ref[...] - t_ref[...]
        dy = err / d
        dy_ref[...] = dy
        dn = dy * g_ref[...]
        dm_ref[...] = (rstd * (dn - n * jnp.mean(dn * n, axis=-1, keepdims=True))).astype(dm_ref.dtype)
        _sum_rows_into(loss_acc, err * err)
        _sum_rows_into(dg_acc, dy * n)

        @pl.when(i == n_steps - 1)
        def _():
            total = jnp.sum(jnp.sum(loss_acc[...], axis=0, keepdims=True), axis=1, keepdims=True)
            loss_ref[...] = 0.5 * total / d
            dg_ref[...] = jnp.sum(dg_acc[...], axis=0, keepdims=True)

    row = _row_spec(tm, d)
    return pl.pallas_call(
        body, grid=(n_steps,), in_specs=[row, row, _gain_spec(d), row],
        out_specs=[pl.BlockSpec((1, 1), lambda i: (0, 0)), row, row, _gain_spec(d)],
        out_shape=[jax.ShapeDtypeStruct((1, 1), F32), jax.ShapeDtypeStruct((s, d), F32),
                   jax.ShapeDtypeStruct((s, d), BF16), jax.ShapeDtypeStruct((1, d), F32)],
        scratch_shapes=[pltpu.VMEM((SUBLANES, d), F32), pltpu.VMEM((SUBLANES, d), F32)],
        compiler_params=_params("arbitrary"), name=name,
    )(h, m, gain, target)


def _sum_rows_into(acc_ref, x):
    tm, d = x.shape
    acc_ref[...] += jnp.sum(x.reshape(tm // SUBLANES, SUBLANES, d), axis=0)


def _norm_bwd_body(n_steps, with_residual):
    def body(*refs):
        dy_ref, x_ref, g_ref = refs[:3]
        dres_ref = refs[3] if with_residual else None
        dx_ref, dg_ref, acc_ref = refs[-3:]
        i = pl.program_id(0)

        @pl.when(i == 0)
        def _():
            acc_ref[...] = jnp.zeros_like(acc_ref)

        x = x_ref[...]
        dy = dy_ref[...]
        rstd = _rstd(x)
        n = x * rstd
        dn = dy * g_ref[...]
        dx = rstd * (dn - n * jnp.mean(dn * n, axis=-1, keepdims=True))
        if with_residual:
            dx = dres_ref[...] + dx
        dx_ref[...] = dx.astype(dx_ref.dtype)
        _sum_rows_into(acc_ref, dy * n)

        @pl.when(i == n_steps - 1)
        def _():
            dg_ref[...] = jnp.sum(acc_ref[...], axis=0, keepdims=True)

    return body


def _norms_bwd(du, h, gain, dh, m_below, gain_below, name):
    s, d = h.shape
    tm = min(s, 256)
    n_steps = s // tm

    def body(du_ref, h_ref, g_ref, dh_ref, m_ref, gb_ref, dx_ref, dg_ref, dm_ref, dgb_ref, acc, acc_below):
        i = pl.program_id(0)

        @pl.when(i == 0)
        def _():
            acc[...] = jnp.zeros_like(acc)
            acc_below[...] = jnp.zeros_like(acc_below)

        def through(x, dy, gain_row):
            rstd = _rstd(x)
            n = x * rstd
            dn = dy * gain_row
            return rstd * (dn - n * jnp.mean(dn * n, axis=-1, keepdims=True)), dy * n

        du = du_ref[...]
        dx, dgain_rows = through(h_ref[...], du, g_ref[...])
        dx = dh_ref[...] + dx
        dx_ref[...] = dx
        _sum_rows_into(acc, dgain_rows)
        dm, dgain_rows = through(m_ref[...], dx, gb_ref[...])
        dm_ref[...] = dm.astype(dm_ref.dtype)
        _sum_rows_into(acc_below, dgain_rows)

        @pl.when(i == n_steps - 1)
        def _():
            dg_ref[...] = jnp.sum(acc[...], axis=0, keepdims=True)
            dgb_ref[...] = jnp.sum(acc_below[...], axis=0, keepdims=True)

    row, gain_spec = _row_spec(tm, d), _gain_spec(d)
    return pl.pallas_call(
        body, grid=(n_steps,), in_specs=[row, row, gain_spec, row, row, gain_spec],
        out_specs=[row, gain_spec, row, gain_spec],
        out_shape=[jax.ShapeDtypeStruct((s, d), F32), jax.ShapeDtypeStruct((1, d), F32),
                   jax.ShapeDtypeStruct((s, d), BF16), jax.ShapeDtypeStruct((1, d), F32)],
        scratch_shapes=[pltpu.VMEM((SUBLANES, d), F32), pltpu.VMEM((SUBLANES, d), F32)],
        compiler_params=_params("arbitrary"), name=name,
    )(du, h, gain, dh, m_below, gain_below)


def _pre_norm_bwd(du, h, gain, dh, name):
    s, d = h.shape
    tm = min(s, 512)
    n_steps = s // tm
    return pl.pallas_call(
        _norm_bwd_body(n_steps, True), grid=(n_steps,),
        in_specs=[_row_spec(tm, d), _row_spec(tm, d), _gain_spec(d), _row_spec(tm, d)],
        out_specs=[_row_spec(tm, d), _gain_spec(d)],
        out_shape=[jax.ShapeDtypeStruct((s, d), F32), jax.ShapeDtypeStruct((1, d), F32)],
        scratch_shapes=[pltpu.VMEM((SUBLANES, d), F32)], compiler_params=_params("arbitrary"), name=name,
    )(du, h, gain, dh)


def _shift_down(p, halo, row, n):
    out = jnp.where(row == 0, halo[SUBLANES - n:SUBLANES - n + 1], pltpu.roll(p, n, 0))
    if n == 2:
        out = jnp.where(row == 1, halo[SUBLANES - 1:SUBLANES], out)
    return out


def _shift_up(p, halo, row, n):
    tm = p.shape[0]
    out = jnp.where(row == tm - 1, halo[n - 1:n], pltpu.roll(p, tm - n, 0))
    if n == 2:
        out = jnp.where(row == tm - 2, halo[0:1], out)
    return out


def _conv_specs(tm, tc, nb, n_row_blocks):
    hb = tm // SUBLANES
    cur = lambda part: pl.BlockSpec((tm, tc), lambda i, j: (i, part * nb + j))
    prev = lambda part: pl.BlockSpec((SUBLANES, tc), lambda i, j: (jnp.maximum(i * hb - 1, 0), part * nb + j))
    nxt = lambda part: pl.BlockSpec(
        (SUBLANES, tc), lambda i, j: (jnp.minimum((i + 1) * hb, n_row_blocks * hb - 1), part * nb + j))
    return cur, prev, nxt


def _conv_gate_fwd(proj, conv_w, name):
    s, b4 = proj.shape
    bdim = b4 // 4
    tm, tc = min(s, 512), min(bdim, 512)
    nb = bdim // tc
    cur, prev, _ = _conv_specs(tm, tc, nb, s // tm)

    def body(b_ref, c_ref, x_ref, z_ref, cp_ref, xp_ref, w_ref, a_ref, at_ref):
        i = pl.program_id(0)
        row = lax.broadcasted_iota(jnp.int32, (tm, tc), 0)
        p = c_ref[...] * x_ref[...]
        halo = jnp.where(i > 0, cp_ref[...] * xp_ref[...], 0.0)
        w = w_ref[...]
        cv = w[0:1] * _shift_down(p, halo, row, 2) + w[1:2] * _shift_down(p, halo, row, 1) + w[2:3] * p
        silu, _ = _silu_parts(z_ref[...])
        a = (silu * (b_ref[...] * cv)).astype(a_ref.dtype)
        a_ref[...] = a
        at_ref[...] = a.T

    return pl.pallas_call(
        body, grid=(s // tm, nb),
        in_specs=[cur(0), cur(1), cur(2), cur(3), prev(1), prev(2), pl.BlockSpec((CONV_K, tc), lambda i, j: (0, j))],
        out_specs=[pl.BlockSpec((tm, tc), lambda i, j: (i, j)), pl.BlockSpec((tc, tm), lambda i, j: (j, i))],
        out_shape=[jax.ShapeDtypeStruct((s, bdim), BF16), jax.ShapeDtypeStruct((bdim, s), BF16)],
        compiler_params=_params("parallel", "parallel"), name=name,
    )(proj, proj, proj, proj, proj, proj, conv_w)


def _conv_gate_bwd(proj, da, conv_w, name):
    s, b4 = proj.shape
    bdim = b4 // 4
    tm, tc = min(s, 128), bdim
    nb = bdim // tc
    n_rows = s // tm
    cur, prev, nxt = _conv_specs(tm, tc, nb, n_rows)
    da_cur = pl.BlockSpec((tm, tc), lambda j, i: (i, j))
    hb = tm // SUBLANES
    da_nxt = pl.BlockSpec((SUBLANES, tc), lambda j, i: (jnp.minimum((i + 1) * hb, n_rows * hb - 1), j))
    swap = lambda spec: pl.BlockSpec(spec.block_shape, lambda j, i, f=spec.index_map: f(i, j))

    def body(b_ref, c_ref, x_ref, z_ref, cp_ref, xp_ref, bn_ref, zn_ref, da_ref, dan_ref, w_ref,
             dproj_ref, dw_ref, acc_ref):
        i = pl.program_id(1)

        @pl.when(i == 0)
        def _():
            acc_ref[...] = jnp.zeros_like(acc_ref)

        row = lax.broadcasted_iota(jnp.int32, (tm, tc), 0)
        w = w_ref[...]
        b, c, x = b_ref[...], c_ref[...], x_ref[...]
        p = c * x
        halo_p = jnp.where(i > 0, cp_ref[...] * xp_ref[...], 0.0)
        p1, p2 = _shift_down(p, halo_p, row, 1), _shift_down(p, halo_p, row, 2)
        cv = w[0:1] * p2 + w[1:2] * p1 + w[2:3] * p
        z = z_ref[...]
        silu, sig = _silu_parts(z)
        da = da_ref[...]
        dy = da * silu
        dcv = dy * b
        silu_n, _ = _silu_parts(zn_ref[...])
        halo_d = jnp.where(i < n_rows - 1, dan_ref[...] * silu_n * bn_ref[...], 0.0)
        dp = w[2:3] * dcv + w[1:2] * _shift_up(dcv, halo_d, row, 1) + w[0:1] * _shift_up(dcv, halo_d, row, 2)
        gates = (dy * cv, dp * x, dp * c, da * (b * cv) * (sig * (1.0 + z * (1.0 - sig))))
        for part, dgate in enumerate(gates):
            dproj_ref[:, part * bdim:(part + 1) * bdim] = dgate.astype(dproj_ref.dtype)
        for k, pk in enumerate((p2, p1, p)):
            _sum_rows_into(acc_ref.at[k], dcv * pk)

        @pl.when(i == n_rows - 1)
        def _():
            for k in range(CONV_K):
                dw_ref[k:k + 1, :] = jnp.sum(acc_ref[k], axis=0, keepdims=True)

    out = pl.BlockSpec((tm, b4), lambda j, i: (i, 0))
    return pl.pallas_call(
        body, grid=(nb, n_rows),
        in_specs=[swap(cur(0)), swap(cur(1)), swap(cur(2)), swap(cur(3)), swap(prev(1)), swap(prev(2)),
                  swap(nxt(0)), swap(nxt(3)), da_cur, da_nxt, pl.BlockSpec((CONV_K, tc), lambda j, i: (0, j))],
        out_specs=[out, pl.BlockSpec((CONV_K, tc), lambda j, i: (0, j))],
        out_shape=[jax.ShapeDtypeStruct((s, b4), BF16), jax.ShapeDtypeStruct((CONV_K, bdim), F32)],
        scratch_shapes=[pltpu.VMEM((CONV_K, SUBLANES, tc), F32)],
        compiler_params=_params("parallel", "arbitrary"), name=name,
    )(proj, proj, proj, proj, proj, proj, proj, proj, da, da, conv_w)


def _split(x):
    hi = x.astype(BF16)
    lo = (x - hi.astype(F32)).astype(BF16)
    return jnp.concatenate([hi, lo], axis=1)


def _row_total(x, column):
    return jnp.broadcast_to(x[:, column:column + 1], (x.shape[0], LANES))


def _sb_tiles(qs, ks, carries, suffix_ones, masks, chain=0):
    items = range(len(qs))
    bk = ks[0].shape[0]
    scale = 1.0 / math.sqrt(HEAD_DIM)
    logits = [lax.dot_general(qs[n], ks[n], NT, preferred_element_type=F32) * scale for n in items]
    es = [jnp.exp(-jnp.abs(logits[n])) for n in items]
    keeps = []
    for n in items:
        log_keep = -(jnp.maximum(logits[n], 0.0) + jnp.log(1.0 + es[n]))
        if masks[n] is not None:
            log_keep = jnp.where(masks[n], log_keep, 0.0)
        keeps.append(_split(log_keep))
    tails = [lax.dot_general(keeps[n], suffix_ones, NN, preferred_element_type=F32) for n in items]
    ws, used = [], []
    for n in items:
        carry = carries[n] if n < len(carries) else used[n - chain] + _row_total(tails[n - chain], 0)
        used.append(carry)
        w = jnp.exp(logits[n] + tails[n] + (carry if carry.shape[1] == 1 else _lane_tile(carry, bk)))
        if masks[n] is not None:
            w = jnp.where(masks[n], w, 0.0)
        ws.append(w)
    return logits, es, tails, ws, used


def _tri_twice(n, upper):
    r = lax.broadcasted_iota(jnp.int32, (2 * n, n), 0)
    r = jnp.where(r >= n, r - n, r)
    c = lax.broadcasted_iota(jnp.int32, (2 * n, n), 1)
    return jnp.where(r <= c if upper else r >= c, 1.0, 0.0).astype(BF16)


def _group_spec(s, width, part, n_groups):
    return pl.BlockSpec((s, width), lambda h: (0, part * n_groups + h))


def _head_cols(g):
    return slice(g * HEAD_DIM, (g + 1) * HEAD_DIM)


def _lane_tile(x, n):
    return x if n == LANES else jnp.concatenate([x] * (n // LANES), axis=1)


def _sb_attn_fwd(qkv, name):
    s, b3 = qkv.shape
    bdim = b3 // 3
    hps = min(HEADS_PER_STEP, bdim // HEAD_DIM)
    width = hps * HEAD_DIM
    n_groups = bdim // width
    blk = min(s, 256)
    n_blk = s // blk

    def body(q_ref, k_ref, v_ref, o_ref, car_ref, carry_ref):
        suffix_ones = _tri_twice(blk, upper=False)
        r = lax.broadcasted_iota(jnp.int32, (blk, blk), 0)
        c = lax.broadcasted_iota(jnp.int32, (blk, blk), 1)
        diag_mask = c < r
        lane = lax.broadcasted_iota(jnp.int32, (blk, LANES), 1)

        def q_block(qi, _):
            q0 = pl.multiple_of(qi * blk, blk)
            rows = pl.ds(q0, blk)
            qs = [q_ref[rows, _head_cols(g)] for g in range(hps)]
            o_ref[rows, :] = jnp.zeros((blk, width), F32)
            car_ref[rows, :] = jnp.full((blk, width), UNVISITED, F32)
            carry_ref[...] = jnp.zeros_like(carry_ref)

            def step(js, tile_masks):
                k0s = [pl.multiple_of(j * blk, blk) for j in js]
                items = [(t, g) for t in range(len(js)) for g in range(hps)]
                ks = [k_ref[pl.ds(k0s[t], blk), _head_cols(g)] for t, g in items]
                first = [carry_ref[g] for g in range(hps)]
                _, _, tails, ws, carries = _sb_tiles([qs[g] for _, g in items], ks, first, suffix_ones,
                                                     [tile_masks[t] for t, _ in items], chain=hps)
                for g in range(hps):
                    mine = [n for n, (_, h) in enumerate(items) if h == g]
                    acc, saved = None, car_ref[rows, _head_cols(g)]
                    for n in mine:
                        v = v_ref[pl.ds(k0s[items[n][0]], blk), _head_cols(g)]
                        p = lax.dot_general(ws[n].astype(BF16), v, NN, preferred_element_type=F32)
                        acc = p if acc is None else acc + p
                        saved = jnp.where(lane == js[items[n][0]], carries[n], saved)
                    o_ref[rows, _head_cols(g)] += acc
                    car_ref[rows, _head_cols(g)] = saved
                    carry_ref[g] = carries[mine[-1]] + _row_total(tails[mine[-1]], 0)

            @pl.when(qi == 0)
            def _():
                step([0], [diag_mask])

            @pl.when(qi > 0)
            def _():
                step([qi, qi - 1], [diag_mask, None])

            def alive():
                top = jnp.max(jnp.max(carry_ref[...], axis=0), axis=0, keepdims=True)
                return (jnp.max(top, axis=1, keepdims=True)[0, 0] >= DEAD_CARRY).astype(jnp.int32)

            left = jnp.maximum(qi - 1, 0)

            def pair(state):
                p, _ = state
                j = qi - 2 - 2 * p
                step([j, j - 1], [None, None])
                return p + 1, alive()

            p, live = lax.while_loop(lambda state: (state[0] < left // 2) & (state[1] > 0), pair, (0, alive()))

            @pl.when((left % 2 == 1) & (p == left // 2) & (live > 0))
            def _():
                step([0], [None])

            return 0

        lax.fori_loop(0, n_blk, q_block, 0)

    out = pl.BlockSpec((s, width), lambda h: (0, h))
    shape = jax.ShapeDtypeStruct((s, bdim), F32)
    return pl.pallas_call(
        body, grid=(n_groups,),
        in_specs=[_group_spec(s, width, part, n_groups) for part in range(3)],
        out_specs=[out, out], out_shape=[shape, shape], scratch_shapes=[pltpu.VMEM((hps, blk, LANES), F32)],
        compiler_params=_params("parallel"), name=name,
    )(qkv, qkv, qkv)


def _sb_attn_bwd(qkv, do, carries, dproj, name):
    s, b3 = qkv.shape
    bdim = b3 // 3
    hps = min(HEADS_PER_STEP, bdim // HEAD_DIM)
    width = hps * HEAD_DIM
    n_groups = bdim // width
    blk = min(s, 256)
    n_blk = s // blk
    scale = 1.0 / math.sqrt(HEAD_DIM)

    def sweep(q_ref, k_ref, v_ref, do_ref, car_ref, dq_ref, dk_acc, dv_acc, dq_acc, before_ref):
        suffix_ones = _tri_twice(blk, upper=False)
        prefix_ones = _tri_twice(blk, upper=True)
        r = lax.broadcasted_iota(jnp.int32, (blk, blk), 0)
        c = lax.broadcasted_iota(jnp.int32, (blk, blk), 1)
        diag_mask = c < r
        lane = lax.broadcasted_iota(jnp.int32, (blk, LANES), 1)
        dk_acc[...] = jnp.zeros_like(dk_acc)
        dv_acc[...] = jnp.zeros_like(dv_acc)

        def q_block(qi, _):
            q0 = pl.multiple_of(qi * blk, blk)
            rows = pl.ds(q0, blk)
            qs = [q_ref[rows, _head_cols(g)] for g in range(hps)]
            dos = [do_ref[rows, _head_cols(g)] for g in range(hps)]
            dq_acc[...] = jnp.zeros_like(dq_acc)
            before_ref[...] = jnp.zeros_like(before_ref)

            def step(js, tile_masks):
                masks = [tile_masks[t] for t in range(len(js)) for _ in range(hps)]
                k0s = [pl.multiple_of(j * blk, blk) for j in js]
                items = [(t, g) for t in range(len(js)) for g in range(hps)]
                every = range(len(items))
                ks = [k_ref[pl.ds(k0s[t], blk), _head_cols(g)] for t, g in items]
                dws = [lax.dot_general(dos[g], v_ref[pl.ds(k0s[t], blk), _head_cols(g)], NT, preferred_element_type=F32)
                       for t, g in items]
                carries = [jnp.sum(jnp.where(lane == js[t], car_ref[rows, _head_cols(g)], 0.0), axis=1, keepdims=True)
                           for t, g in items]
                logits, es, _, ws, _ = _sb_tiles([qs[g] for _, g in items], ks, carries, suffix_ones, masks)
                gws = [dws[n] * ws[n] for n in every]
                g_upto = [lax.dot_general(_split(gws[n]), prefix_ones, NN, preferred_element_type=F32) for n in every]
                dss, befores = [], []
                for n, (t, g) in enumerate(items):
                    before = before_ref[g] if t == 0 else befores[n - hps] + _row_total(g_upto[n - hps], blk - 1)
                    befores.append(before)
                    sig = jnp.where(logits[n] >= 0.0, 1.0, es[n]) / (1.0 + es[n])
                    dlogits = gws[n] - sig * (_lane_tile(before, blk) + g_upto[n])
                    if masks[n] is not None:
                        dlogits = jnp.where(masks[n], dlogits, 0.0)
                    dss.append((dlogits * scale).astype(BF16))
                for g in range(hps):
                    mine = [n for n in every if items[n][1] == g]
                    dq = None
                    for n in mine:
                        k0 = k0s[items[n][0]]
                        p = lax.dot_general(dss[n], ks[n], NN, preferred_element_type=F32)
                        dq = p if dq is None else dq + p
                        dk_acc[pl.ds(k0, blk), _head_cols(g)] += lax.dot_general(
                            dss[n], qs[g], TN, preferred_element_type=F32)
                        dv_acc[pl.ds(k0, blk), _head_cols(g)] += lax.dot_general(
                            ws[n].astype(BF16), dos[g], TN, preferred_element_type=F32)
                    dq_acc[:, _head_cols(g)] += dq
                    before_ref[g] = befores[mine[-1]] + _row_total(g_upto[mine[-1]], blk - 1)

            top = car_ref[rows, _head_cols(0)]
            for g in range(1, hps):
                top = jnp.maximum(top, car_ref[rows, _head_cols(g)])
            top = jnp.max(top, axis=0, keepdims=True)
            lane_row = lax.broadcasted_iota(jnp.int32, (1, LANES), 1)
            counted = jnp.where((top >= DEAD_CARRY) & (lane_row < qi), 1.0, 0.0)
            n_alive = jnp.sum(counted, axis=1, keepdims=True)[0, 0].astype(jnp.int32)
            left = jnp.maximum(n_alive - 1, 0)
            start = qi - 1 - left

            @pl.when(left % 2 == 1)
            def _():
                step([start], [None])

            def pair(p, _):
                j = start + left % 2 + 2 * p
                step([j, j + 1], [None, None])
                return 0

            lax.fori_loop(0, left // 2, pair, 0)

            @pl.when(qi == 0)
            def _():
                step([0], [diag_mask])

            @pl.when(qi > 0)
            def _():
                step([qi - 1, qi], [None, diag_mask])
            dq_ref[rows, :] = dq_acc[...].astype(dq_ref.dtype)
            return 0

        lax.fori_loop(0, n_blk, q_block, 0)

    def body(q_ref, k_ref, v_ref, do_ref, car_ref, dproj_ref, out_ref, dk_acc, dv_acc, dq_acc, before_ref):
        part = pl.program_id(1)

        @pl.when(part == 0)
        def _():
            sweep(q_ref, k_ref, v_ref, do_ref, car_ref, out_ref, dk_acc, dv_acc, dq_acc, before_ref)

        @pl.when(part == 1)
        def _():
            out_ref[...] = dk_acc[...].astype(out_ref.dtype)

        @pl.when(part == 2)
        def _():
            out_ref[...] = dv_acc[...].astype(out_ref.dtype)

    qkv_spec = lambda which: pl.BlockSpec((s, width), lambda g, p: (0, which * n_groups + g))
    once = pl.BlockSpec((s, width), lambda g, p: (0, g), pipeline_mode=pl.Buffered(1))
    return pl.pallas_call(
        body, grid=(n_groups, 3),
        in_specs=[qkv_spec(0), qkv_spec(1), qkv_spec(2), once, once, ANY],
        out_specs=pl.BlockSpec((s, width), lambda g, p: (0, p * n_groups + g)),
        out_shape=jax.ShapeDtypeStruct(dproj.shape, dproj.dtype), input_output_aliases={5: 0},
        scratch_shapes=[pltpu.VMEM((s, width), F32), pltpu.VMEM((s, width), F32), pltpu.VMEM((blk, width), F32),
                        pltpu.VMEM((hps, blk, LANES), F32)],
        compiler_params=_params("parallel", "arbitrary"), name=name,
    )(qkv, qkv, qkv, do, carries, dproj)


def _sb_gate_fwd(z, o, name):
    s, bdim = z.shape
    tm = min(s, 512)

    def body(z_ref, o_ref, a_ref, at_ref):
        silu, _ = _silu_parts(z_ref[...])
        a = (silu * o_ref[...]).astype(a_ref.dtype)
        a_ref[...] = a
        at_ref[...] = a.T

    return pl.pallas_call(
        body, grid=(s // tm,), in_specs=[_row_spec(tm, bdim), _row_spec(tm, bdim)],
        out_specs=[_row_spec(tm, bdim), pl.BlockSpec((bdim, tm), lambda i: (0, i))],
        out_shape=[jax.ShapeDtypeStruct((s, bdim), BF16), jax.ShapeDtypeStruct((bdim, s), BF16)],
        compiler_params=_params("parallel"), name=name,
    )(z, o)


def _sb_gate_bwd(da, z, o, name):
    s, bdim = z.shape
    tm = min(s, 512)

    def body(da_ref, z_ref, o_ref, do_ref, dz_ref):
        z = z_ref[...]
        da = da_ref[...]
        silu, sig = _silu_parts(z)
        do_ref[...] = (da * silu).astype(do_ref.dtype)
        dz_ref[...] = (da * o_ref[...] * (sig * (1.0 + z * (1.0 - sig)))).astype(dz_ref.dtype)

    spec = _row_spec(tm, bdim)
    return pl.pallas_call(
        body, grid=(s // tm,), in_specs=[spec, spec, spec],
        out_specs=[spec, pl.BlockSpec((tm, bdim), lambda i: (i, 3))],
        out_shape=[jax.ShapeDtypeStruct((s, bdim), BF16), jax.ShapeDtypeStruct((s, 4 * bdim), BF16)],
        compiler_params=_params("parallel"), name=name,
    )(da, z, o)


def _into_slot(block, place, dtype, name):
    r, c = block.shape
    tr = min(r, 256)

    def body(place_ref, b_ref, o_ref):
        o_ref[...] = b_ref[...].astype(o_ref.dtype)

    grid_spec = pltpu.PrefetchScalarGridSpec(
        num_scalar_prefetch=1, grid=(r // tr,),
        in_specs=[pl.BlockSpec((tr, c), lambda i, place_ref: (i, 0))],
        out_specs=pl.BlockSpec((None, tr, c), lambda i, place_ref: (place_ref[0], i, 0)),
    )
    return pl.pallas_call(
        body, grid_spec=grid_spec, out_shape=jax.ShapeDtypeStruct((N_DEV, r, c), dtype),
        compiler_params=_params("parallel"), name=name,
    )(place, block)


def _add_core_pair(grads, received, core, name):
    _, _, r, c = grads.shape
    tr = min(r, 1024)

    def body(core_ref, g_ref, r_ref, o_ref):
        o_ref[...] = (g_ref[...].astype(F32) + r_ref[...].astype(F32)).astype(o_ref.dtype)

    grid_spec = pltpu.PrefetchScalarGridSpec(
        num_scalar_prefetch=1, grid=(N_CHIP, r // tr),
        in_specs=[pl.BlockSpec((None, None, tr, c), lambda q, i, core_ref: (q, core_ref[0], i, 0)),
                  pl.BlockSpec((None, tr, c), lambda q, i, core_ref: (q, i, 0))],
        out_specs=pl.BlockSpec((None, tr, c), lambda q, i, core_ref: (q, i, 0)),
    )
    return pl.pallas_call(
        body, grid_spec=grid_spec, out_shape=jax.ShapeDtypeStruct((N_CHIP, r, c), BF16),
        compiler_params=_params("parallel", "parallel"), name=name,
    )(core, grads, received)


def _adamw_step(g, w, m, v, g_ref, d_ref, nm_ref, nv_ref):
    new_m = ADAM_B1 * m + (1.0 - ADAM_B1) * g
    new_v = ADAM_B2 * v + (1.0 - ADAM_B2) * (g * g)
    m_hat = new_m / (1.0 - ADAM_B1 ** ADAM_STEP)
    v_hat = new_v / (1.0 - ADAM_B2 ** ADAM_STEP)
    g_ref[...] = g
    d_ref[...] = -ADAM_LR * (m_hat / (jnp.sqrt(v_hat) + ADAM_EPS) + ADAM_WD * w)
    nm_ref[...] = new_m
    nv_ref[...] = new_v


def _adamw(w, parts, m, v, name):
    r, c = w.shape
    n_parts = parts.shape[0]
    tr = min(r, 256)

    def body(w_ref, p_ref, m_ref, v_ref, *out_refs):
        g = p_ref[0].astype(F32)
        for k in range(1, n_parts):
            g = g + p_ref[k].astype(F32)
        _adamw_step(g, w_ref[...], m_ref[...], v_ref[...], *out_refs)

    spec = pl.BlockSpec((tr, c), lambda i: (i, 0))
    shape = jax.ShapeDtypeStruct((r, c), F32)
    return pl.pallas_call(
        body, grid=(r // tr,), in_specs=[spec, pl.BlockSpec((n_parts, tr, c), lambda i: (0, i, 0)), spec, spec],
        out_specs=[spec] * 4, out_shape=[shape] * 4, compiler_params=_params("parallel"), name=name,
    )(w, parts, m, v)


def _adamw_shard(w, grads, landed, m, v, place, name):
    r, c = w.shape
    n_landed = landed.shape[0]
    tr = min(r, 512)

    def body(place_ref, w_ref, own_ref, l_ref, m_ref, v_ref, *out_refs):
        g = own_ref[...].astype(F32)
        for k in range(n_landed):
            g = g + l_ref[k].astype(F32)
        _adamw_step(g, w_ref[...], m_ref[...], v_ref[...], *out_refs)

    spec = pl.BlockSpec((tr, c), lambda i, place_ref: (i, 0))
    grid_spec = pltpu.PrefetchScalarGridSpec(
        num_scalar_prefetch=1, grid=(r // tr,),
        in_specs=[spec, pl.BlockSpec((None, tr, c), lambda i, place_ref: (place_ref[0], i, 0)),
                  pl.BlockSpec((n_landed, tr, c), lambda i, place_ref: (0, i, 0)), spec, spec],
        out_specs=[spec] * 4,
    )
    return pl.pallas_call(
        body, grid_spec=grid_spec, out_shape=[jax.ShapeDtypeStruct((r, c), F32)] * 4,
        compiler_params=_params("parallel"), name=name,
    )(place, w, grads, landed, m, v)


def _place():
    x, y, c = lax.axis_index("x"), lax.axis_index("y"), lax.axis_index("c")
    other_chips = [(1 - x, y), (x, 1 - y), (1 - x, 1 - y)]
    return x, y, c, other_chips


def _all_gather(blocks, name):
    n_arr = len(blocks)
    items = [(a, i) for a, blk in enumerate(blocks) for i in range(blk.shape[0])]
    n_items = len(items)

    def body(*refs):
        srcs, outs = refs[:n_arr], refs[n_arr:2 * n_arr]
        send_sems, recv_sems, local_sems = refs[2 * n_arr:]
        x, y, c, other_chips = _place()
        me, sibling = (x, y, c), (x, y, 1 - c)

        def slot(it, dev):
            a, i = items[it]
            return outs[a].at[i, 4 * dev[0] + 2 * dev[1] + dev[2]]

        def copy(it, k, block_of, to, from_src=False):
            a, i = items[it]
            return pltpu.make_async_remote_copy(
                src_ref=srcs[a].at[i] if from_src else slot(it, block_of), dst_ref=slot(it, block_of),
                send_sem=send_sems.at[it * 7 + k], recv_sem=recv_sems.at[it * 7 + k],
                device_id=to, device_id_type=MESH)

        own = [pltpu.make_async_copy(srcs[items[it][0]].at[items[it][1]], slot(it, me), local_sems.at[it])
               for it in range(n_items)]
        for cp in own:
            cp.start()
        first = []
        for it in range(n_items):
            first.append(copy(it, 0, me, sibling, from_src=True))
            first += [copy(it, 1 + j, me, (*chip, c), from_src=True) for j, chip in enumerate(other_chips)]
        for cp in first:
            cp.start()
        passed = []
        for it in range(n_items):
            for j, chip in enumerate(other_chips):
                copy(it, 1 + j, (*chip, c), me).wait_recv()
                passed.append(copy(it, 4 + j, (*chip, c), sibling))
                passed[-1].start()
        for it in range(n_items):
            copy(it, 0, sibling, me).wait_recv()
            for j, chip in enumerate(other_chips):
                copy(it, 4 + j, (*chip, 1 - c), me).wait_recv()
        for cp in first + passed:
            cp.wait_send()
        for cp in own:
            cp.wait()

    return pl.pallas_call(
        body, in_specs=[ANY] * n_arr, out_specs=[ANY] * n_arr,
        out_shape=[jax.ShapeDtypeStruct((b.shape[0], N_DEV) + b.shape[1:], b.dtype) for b in blocks],
        scratch_shapes=[pltpu.SemaphoreType.DMA((7 * n_items,)), pltpu.SemaphoreType.DMA((7 * n_items,)),
                        pltpu.SemaphoreType.DMA((n_items,))],
        name=name,
    )(*blocks)


def _exchange_core_pair(grads, name):
    n_arr = len(grads)

    def body(*refs):
        srcs, outs = refs[:n_arr], refs[n_arr:2 * n_arr]
        send_sems, recv_sems = refs[2 * n_arr:]
        x, y, c, _ = _place()
        copies = [
            pltpu.make_async_remote_copy(
                src_ref=srcs[a].at[q, 1 - c], dst_ref=outs[a].at[q],
                send_sem=send_sems.at[a * N_CHIP + q], recv_sem=recv_sems.at[a * N_CHIP + q],
                device_id=(x, y, 1 - c), device_id_type=MESH)
            for a in range(n_arr) for q in range(N_CHIP)]
        for cp in copies:
            cp.start()
        for cp in copies:
            cp.wait_recv()
        for cp in copies:
            cp.wait_send()

    return pl.pallas_call(
        body, in_specs=[ANY] * n_arr, out_specs=[ANY] * n_arr,
        out_shape=[jax.ShapeDtypeStruct((N_CHIP,) + g.shape[2:], g.dtype) for g in grads],
        scratch_shapes=[pltpu.SemaphoreType.DMA((N_CHIP * n_arr,)), pltpu.SemaphoreType.DMA((N_CHIP * n_arr,))],
        name=name,
    )(*grads)


CHIPS_SCATTER, CHIPS_GATHER, PAIR_GATHER, PAIR_SCATTER = "chips_scatter", "chips_gather", "pair_gather", "pair_scatter"


def _stage_copies(stage, srcs, outs, send_sems, recv_sems):
    x, y, c, other_chips = _place()
    my_chip = 2 * x + y
    copies = []
    for a in range(len(outs)):
        if stage == PAIR_SCATTER:
            moves = [(srcs[a].at[q, 1 - c], outs[a].at[q], (x, y, 1 - c)) for q in range(N_CHIP)]
        elif stage == PAIR_GATHER:
            moves = [(outs[a].at[q, c], outs[a].at[q, c], (x, y, 1 - c)) for q in range(N_CHIP)]
        elif stage == CHIPS_GATHER:
            moves = [(outs[a].at[my_chip, c], outs[a].at[my_chip, c], (*chip, c)) for chip in other_chips]
        else:
            moves = [(srcs[a].at[2 * chip[0] + chip[1]], outs[a].at[j], (*chip, c)) for j, chip in enumerate(other_chips)]
        for k, (src, dst, peer) in enumerate(moves):
            copies.append(pltpu.make_async_remote_copy(
                src_ref=src, dst_ref=dst, send_sem=send_sems.at[a * N_CHIP + k], recv_sem=recv_sems.at[a * N_CHIP + k],
                device_id=peer, device_id_type=MESH))
    return copies


def _wait_all(copies):
    for cp in copies:
        cp.wait_recv()
    for cp in copies:
        cp.wait_send()


def kernel(x, ln_pre_0, conv_w_in_0, conv_w_0, conv_w_out_0, ln_post_0, ln_pre_1, sb_w_in_1, sb_w_out_1, ln_post_1, ln_pre_2, conv_w_in_2, conv_w_2, conv_w_out_2, ln_post_2, ln_pre_3, sb_w_in_3, sb_w_out_3, ln_post_3, loss_target, m_ln_pre_0, m_conv_w_in_0, m_conv_w_0, m_conv_w_out_0, m_ln_post_0, m_ln_pre_1, m_sb_w_in_1, m_sb_w_out_1, m_ln_post_1, m_ln_pre_2, m_conv_w_in_2, m_conv_w_2, m_conv_w_out_2, m_ln_post_2, m_ln_pre_3, m_sb_w_in_3, m_sb_w_out_3, m_ln_post_3, v_ln_pre_0, v_conv_w_in_0, v_conv_w_0, v_conv_w_out_0, v_ln_post_0, v_ln_pre_1, v_sb_w_in_1, v_sb_w_out_1, v_ln_post_1, v_ln_pre_2, v_conv_w_in_2, v_conv_w_2, v_conv_w_out_2, v_ln_post_2, v_ln_pre_3, v_sb_w_in_3, v_sb_w_out_3, v_ln_post_3):
    names = ['ln_pre_0', 'conv_w_in_0', 'conv_w_0', 'conv_w_out_0', 'ln_post_0', 'ln_pre_1', 'sb_w_in_1', 'sb_w_out_1',
             'ln_post_1', 'ln_pre_2', 'conv_w_in_2', 'conv_w_2', 'conv_w_out_2', 'ln_post_2', 'ln_pre_3', 'sb_w_in_3',
             'sb_w_out_3', 'ln_post_3']
    given = dict(locals())
    w = {n: given[n] for n in names}
    mom = {n: given["m_" + n] for n in names}
    var = {n: given["v_" + n] for n in names}
    conv_layers = [i for i in range(DEPTH) if i % 2 == 0]
    w_in_names = [("conv_w_in_%d" if i % 2 == 0 else "sb_w_in_%d") % i for i in range(DEPTH)]
    w_out_names = [("conv_w_out_%d" if i % 2 == 0 else "sb_w_out_%d") % i for i in range(DEPTH)]

    s, d = x.shape[1:]
    h = x.reshape(s, d)
    target = loss_target.reshape(s, d)
    gains = {n: w[n].reshape(1, d) for n in names if n.startswith("ln_")}
    place = 4 * lax.axis_index("x") + 2 * lax.axis_index("y") + lax.axis_index("c")
    place_arr = place.astype(jnp.int32).reshape(1)
    bdim = w[w_out_names[0]].shape[0] * N_DEV
    wc = bdim // N_DEV

    conv_rows = jnp.concatenate([w["conv_w_%d" % i] for i in conv_layers], axis=0)
    first = _all_gather([_cast(w[n], BF16, "cast_" + n)[None] for n in (w_in_names[0], w_out_names[0])] + [conv_rows[None]],
                        "gather_first_layer")
    slots = {n: _into_slot(w[n], place_arr, BF16, "slot_" + n) for n in w_in_names[1:] + w_out_names[1:]}
    slots = {n: a.reshape((N_CHIP, 2) + a.shape[1:]) for n, a in slots.items()}
    conv_all = first[2][0].reshape(N_DEV, len(conv_layers), CONV_K, wc)
    conv_all = jnp.transpose(conv_all, (1, 2, 0, 3)).reshape(len(conv_layers), CONV_K, bdim)
    conv_full = {layer: conv_all[n] for n, layer in enumerate(conv_layers)}
    weights = [(first[0][0], first[1][0].reshape(bdim, d))]

    saved = []
    u, u_t = _rmsnorm_fwd(h, gains["ln_pre_0"], "pre_norm_0")
    for i in range(DEPTH):
        w_in, w_out = weights[i]
        more = i + 1 < DEPTH
        nxt_in = ([slots[w_in_names[i + 1]]], CHIPS_GATHER) if more else None
        nxt_out = ([slots[w_out_names[i + 1]]], CHIPS_GATHER) if more else None
        if i % 2 == 0:
            proj = _proj(u, w_in, 0, N_DEV, F32, "proj_%d" % i, hosted=nxt_in)
            if more:
                proj, crossed_in = proj
            a, a_t = _conv_gate_fwd(proj, conv_full[i], "conv_gate_%d" % i)
            extra = (proj,)
        else:
            qkv = _proj(u, w_in, 0, 6, BF16, "proj_qkv_%d" % i, hosted=nxt_in)
            if more:
                qkv, crossed_in = qkv
            z = _proj(u, w_in, 6, 2, F32, "proj_z_%d" % i)
            o, carries = _sb_attn_fwd(qkv, "sb_attn_%d" % i)
            a, a_t = _sb_gate_fwd(z, o, "sb_gate_%d" % i)
            extra = (qkv, z, o, carries)
        m = _out_proj(a, w_out, "out_proj_%d" % i, hosted=nxt_out)
        if more:
            m, crossed_out = m
        saved.append((h, u_t, a_t, m, extra))
        if more:
            (h, u, u_t), both = _post_norm_residual(
                h, m, gains["ln_post_%d" % i], gains["ln_pre_%d" % (i + 1)], "post_norm_%d" % i,
                hosted=([crossed_in[0], crossed_out[0]], PAIR_GATHER))
            weights.append((both[0].reshape((N_DEV,) + both[0].shape[2:]), both[1].reshape(bdim, d)))

    last = DEPTH - 1
    small = {}
    loss, dh, dm, small["ln_post_%d" % last] = _last_norm_and_loss(
        h, m, gains["ln_post_%d" % last], target, "last_norm_and_loss")
    loss = lax.psum(loss[0, 0], ("x", "y", "c"))

    chip_parts = {}
    core = lax.axis_index("c").astype(jnp.int32).reshape(1)
    for i in reversed(range(DEPTH)):
        h_in, u_t, a_t, m, extra = saved[i]
        w_in, w_out = weights[i]
        g_out = _weight_grad(a_t, dm, 1, "grad_w_out_%d" % i).reshape(N_CHIP, 2, wc, d)

        def pair_sum(g, kind):
            (from_sibling,) = _exchange_core_pair([g], "reduce_core_pair_%s_%d" % (kind, i))
            return _add_core_pair(g, from_sibling, core, "add_core_pair_%s_%d" % (kind, i))

        pair_out = pair_sum(g_out, "out")
        da, landed = _out_proj_bwd_act(dm, w_out, "out_proj_bwd_%d" % i, hosted=([pair_out], CHIPS_SCATTER))
        chip_parts[w_out_names[i]] = (pair_out, landed[0])
        if i % 2 == 0:
            (proj,) = extra
            dproj, small["conv_w_%d" % i] = _conv_gate_bwd(proj, da, conv_full[i], "conv_gate_bwd_%d" % i)
        else:
            qkv, z, o, carries = extra
            do, dproj = _sb_gate_bwd(da, z, o, "sb_gate_bwd_%d" % i)
            dproj = _sb_attn_bwd(qkv, do, carries, dproj, "sb_attn_bwd_%d" % i)
        if i < last:
            g_in, landed = _weight_grad(u_t, dproj, N_DEV, "grad_w_in_%d" % i, hosted=([pending], CHIPS_SCATTER))
            chip_parts[w_in_names[i + 1]] = (pending, landed[0])
        else:
            g_in = _weight_grad(u_t, dproj, N_DEV, "grad_w_in_%d" % i)
        g_in = g_in.reshape((N_CHIP, 2) + g_in.shape[1:])
        if i > 0:
            du, from_sibling = _proj_bwd_act(dproj, w_in, "proj_bwd_%d" % i, hosted=([g_in], PAIR_SCATTER))
            pending = _add_core_pair(g_in, from_sibling[0], core, "add_core_pair_in_%d" % i)
        else:
            pair_in = pair_sum(g_in, "in")
            du, landed = _proj_bwd_act(dproj, w_in, "proj_bwd_%d" % i, hosted=([pair_in], CHIPS_SCATTER))
            chip_parts[w_in_names[i]] = (pair_in, landed[0])
        if i > 0:
            dh, small["ln_pre_%d" % i], dm, small["ln_post_%d" % (i - 1)] = _norms_bwd(
                du, h_in, gains["ln_pre_%d" % i], dh, saved[i - 1][3], gains["ln_post_%d" % (i - 1)], "norms_bwd_%d" % i)
        else:
            dh, small["ln_pre_%d" % i] = _pre_norm_bwd(du, h_in, gains["ln_pre_%d" % i], dh, "pre_norm_bwd_%d" % i)
    big_names = w_in_names + w_out_names

    gain_names = [n for n in names if n.startswith("ln_")]
    conv_names = ["conv_w_%d" % i for i in conv_layers]
    rows = [small[n] for n in gain_names] + [small[n] for n in conv_names]
    n_rows = len(gain_names) + CONV_K * len(conv_names)
    pad = -n_rows % SUBLANES
    stacked = jnp.concatenate(rows + [jnp.zeros((pad, d), F32)], axis=0)
    (small_all,) = _all_gather([stacked[None]], "gather_small_grads")
    small_all = small_all[0]

    out_g, out_d, out_m, out_v = {}, {}, {}, {}

    def update(n, w2, parts, m2, v2, shape):
        g2, d2, nm2, nv2 = _adamw(w2, parts, m2, v2, "adamw_" + n)
        out_g[n], out_d[n], out_m[n], out_v[n] = (t.reshape(shape) for t in (g2, d2, nm2, nv2))

    chip_arr = (2 * lax.axis_index("x") + lax.axis_index("y")).astype(jnp.int32).reshape(1)
    for n in big_names:
        own, landed = chip_parts[n]
        out_g[n], out_d[n], out_m[n], out_v[n] = _adamw_shard(w[n], own, landed, mom[n], var[n], chip_arr, "adamw_" + n)
    n_gain = len(gain_names)
    stack = lambda src: jnp.stack([src[n] for n in gain_names])
    g2, d2, nm2, nv2 = _adamw(stack(w), small_all[:, :n_gain], stack(mom), stack(var), "adamw_gains")
    for k, n in enumerate(gain_names):
        out_g[n], out_d[n], out_m[n], out_v[n] = g2[k], d2[k], nm2[k], nv2[k]
    wc = bdim // N_DEV
    for k, n in enumerate(conv_names):
        rows_k = small_all[:, n_gain + CONV_K * k:n_gain + CONV_K * (k + 1)]
        parts = lax.dynamic_slice_in_dim(rows_k, place * wc, wc, axis=2)
        update(n, w[n], parts, mom[n], var[n], w[n].shape)

    grad_x = dh.reshape(x.shape)
    return (loss, grad_x, *[out_g[n] for n in names], *[out_d[n] for n in names],
            *[out_m[n] for n in names], *[out_v[n] for n in names])
```
